```python
import math
import jax, jax.numpy as jnp
from jax import lax
import numpy as np

D_MODEL = 1024
BATCH = 2
SEQ = 8192
DEPTH = 1
DEC_BATCH = 32
DEC_SEQ = 8
PAST_LEN = 16384
PAGE_SIZE = 128

GDN_HEADS = 4
GDN_DK = 128
GDN_DV = 128
GDN_CONV = 4
GDN_CHUNK = 64
GDN_CONV_CH = GDN_HEADS * (2 * GDN_DK + GDN_DV)
NSA_HEADS = 8
NSA_KV = 2
NSA_DH = 64
NSA_CMP = 32
NSA_SEL = 64
NSA_TOPN = 16
NSA_WIN = 512
NSA_QBLK = 128
NSA_FORCE = 1e9
MEM_LEN = 256
MEM_HEADS = 4
MEM_DH = 128
PEER_HEADS = 8
PEER_NKEYS = 128
PEER_N = PEER_NKEYS * PEER_NKEYS
PEER_DKEY = 256
PEER_TOPK = 16
PEER_TBLK = 128

N_BRANCH = 3
ROPE_THETA = 10000.0
EPS = 1e-6
IN_SIZES = (GDN_CONV_CH, GDN_HEADS * GDN_DV, GDN_HEADS, GDN_HEADS,
            NSA_HEADS * NSA_DH, 3 * 2 * NSA_KV * NSA_DH, 3 * NSA_HEADS,
            MEM_HEADS * MEM_DH, N_BRANCH * D_MODEL)
IN_WIDTH = sum(IN_SIZES)

kernel_name = 'hybrid_gdn_nsa_mem_peer_step'


def rmsnorm(x, g):
    xf = x.astype(jnp.float32)
    y = xf * lax.rsqrt(jnp.mean(xf * xf, axis=-1, keepdims=True) + EPS)
    return (y * g.astype(jnp.float32)).astype(x.dtype)


def l2norm(x):
    xf = x.astype(jnp.float32)
    return xf * lax.rsqrt(jnp.sum(xf * xf, axis=-1, keepdims=True) + EPS)


def split_cols(z, sizes):
    return jnp.split(z, np.cumsum(sizes)[:-1].tolist(), axis=-1)


def rope(x, pos):
    half = x.shape[-1] // 2
    inv = jnp.power(ROPE_THETA, -jnp.arange(half, dtype=jnp.float32) / half)
    ang = pos.astype(jnp.float32)[:, None] * inv[None, :]
    cos, sin = jnp.cos(ang)[:, None, :], jnp.sin(ang)[:, None, :]
    xf = x.astype(jnp.float32)
    x1, x2 = xf[..., :half], xf[..., half:]
    return jnp.concatenate([x1 * cos - x2 * sin, x2 * cos + x1 * sin], axis=-1).astype(x.dtype)


def masked_softmax(s, mask):
    s = jnp.where(mask, s, -jnp.inf)
    m = jnp.max(s, axis=-1, keepdims=True)
    m = jnp.where(jnp.isfinite(m), m, 0.0)
    e = jnp.where(mask, jnp.exp(s - m), 0.0)
    return e / jnp.maximum(jnp.sum(e, axis=-1, keepdims=True), 1e-30)


def causal_conv(u, buf, w):
    T = u.shape[1]
    full = jnp.concatenate([buf.astype(u.dtype), u], axis=1)
    out = sum(full[:, j:j + T] * w[j] for j in range(w.shape[0]))
    return jax.nn.silu(out), full[:, T:]


def gated_delta(q, k, v, beta, g, s0):
    B, T, H, _ = q.shape
    f32 = jnp.float32
    C = min(GDN_CHUNK, T)
    pad = (-T) % C
    n = (T + pad) // C

    def prep(t):
        t = jnp.pad(t.astype(f32), [(0, 0), (0, pad)] + [(0, 0)] * (t.ndim - 2))
        t = t.reshape((B, n, C) + t.shape[2:])
        return jnp.moveaxis(t, (1, 0, 3, 2), (0, 1, 2, 3))

    qc, kc, vc, bc = prep(q), prep(k), prep(v), prep(beta)
    gc = jnp.cumsum(prep(g), axis=-1)
    i = jnp.arange(C)
    tril = i[:, None] >= i[None, :]
    strict = i[:, None] > i[None, :]
    decay = jnp.exp(jnp.where(tril, gc[..., :, None] - gc[..., None, :], -jnp.inf))
    kb = kc * bc[..., None]
    lmat = jnp.where(strict, jnp.einsum('nbhik,nbhjk->nbhij', kb, kc) * decay, 0.0)
    eye = jnp.eye(C, dtype=f32)
    amat = eye + lmat
    tinv = lax.linalg.triangular_solve(amat, jnp.broadcast_to(eye, amat.shape),
                                       left_side=True, lower=True)
    uu = tinv @ (vc * bc[..., None])
    ww = tinv @ (kb * jnp.exp(gc)[..., None])
    aqk = jnp.einsum('nbhik,nbhjk->nbhij', qc, kc) * decay

    def step(s, xs):
        qi, ki, ui, wi, gi, ai = xs
        v_new = ui - wi @ s
        o = (qi * jnp.exp(gi)[..., None]) @ s + ai @ v_new
        gl = gi[..., -1:]
        s = s * jnp.exp(gl)[..., None] + jnp.einsum('bhck,bhcv->bhkv', ki * jnp.exp(gl - gi)[..., None], v_new)
        return s, o

    s, o = lax.scan(step, s0.astype(f32), (qc, kc, uu, ww, gc, aqk))
    o = jnp.moveaxis(o, (0, 1, 2, 3), (1, 0, 3, 2)).reshape(B, n * C, H, -1)[:, :T]
    return o, s


def gdn_branch(qkv, zg, a, b, conv_buf, s0, p):
    B, T, _ = qkv.shape
    u, new_buf = causal_conv(qkv, conv_buf, p['gdn_conv'])
    q, k, v = split_cols(u, (GDN_HEADS * GDN_DK, GDN_HEADS * GDN_DK, GDN_HEADS * GDN_DV))
    q = l2norm(q.reshape(B, T, GDN_HEADS, GDN_DK)) * (GDN_DK ** -0.5)
    k = l2norm(k.reshape(B, T, GDN_HEADS, GDN_DK))
    v = v.reshape(B, T, GDN_HEADS, GDN_DV)
    beta = jax.nn.sigmoid(b.astype(jnp.float32))
    g = -jnp.exp(p['gdn_a_log'].astype(jnp.float32)) * jax.nn.softplus(
        a.astype(jnp.float32) + p['gdn_dt_bias'].astype(jnp.float32))
    o, s = gated_delta(q, k, v, beta, g, s0)
    o = rmsnorm(o, p['gdn_norm']) * jax.nn.silu(zg.reshape(B, T, GDN_HEADS, GDN_DV).astype(jnp.float32))
    return o.reshape(B, T, GDN_HEADS * GDN_DV).astype(qkv.dtype), new_buf, s.astype(s0.dtype)


def compress(raw, p):
    B, L = raw.shape[:2]
    nb = L // NSA_CMP
    blk = raw[:, :nb * NSA_CMP].reshape(B, nb, NSA_CMP, 2, NSA_KV, NSA_DH)
    blk = blk + jnp.moveaxis(p['nsa_cmp_pe'], 0, 1)[:, :, None, :].astype(raw.dtype)
    flat = jnp.moveaxis(blk, 2, 4).reshape(B, nb, 2, NSA_KV, NSA_CMP * NSA_DH)
    hid = jax.nn.gelu(jnp.einsum('bnsgf,sfe->bnsge', flat, p['nsa_cmp_w1']))
    out = jnp.einsum('bnsge,sed->bnsgd', hid, p['nsa_cmp_w2'])
    return out[:, :, 0], out[:, :, 1]


def nsa_attend(q, q_rot, q_pos, kc, vc, gather, n_sel, kw, vw, kw_pos, gate):
    B, T, H, Dh = q.shape
    G = kc.shape[2]
    hp = H // G
    f32 = jnp.float32
    scale = Dh ** -0.5
    qg = q.reshape(B, T, G, hp, Dh).astype(f32) * scale
    qr = q_rot.reshape(B, T, G, hp, Dh).astype(f32) * scale
    nc = kc.shape[1]
    cmask = ((jnp.arange(nc) + 1) * NSA_CMP <= q_pos[:, None] + 1)[None, :, None, None, :]
    p_cmp = masked_softmax(jnp.einsum('btgpd,bngd->btgpn', qg, kc.astype(f32)), cmask)
    o_cmp = jnp.einsum('btgpn,bngd->btgpd', p_cmp, vc.astype(f32))
    r = NSA_SEL // NSA_CMP
    imp = jnp.pad(p_cmp.sum(3), ((0, 0), (0, 0), (0, 0), (0, n_sel * r - nc)))
    imp = imp.reshape(B, T, G, n_sel, r).sum(-1)
    blk = jnp.arange(n_sel)[None, :]
    cur = (q_pos // NSA_SEL)[:, None]
    forced = ((blk == 0) | (blk == cur) | (blk == cur - 1))[None, :, None, :]
    future = (blk > cur)[None, :, None, :]
    score = jnp.where(future, -jnp.inf, jnp.where(forced, NSA_FORCE, imp))
    _, idx = lax.top_k(score, min(NSA_TOPN, n_sel))
    ks, vs, kpos = gather(idx)
    smask = (kpos <= q_pos[None, :, None, None])[:, :, :, None, :]
    p_slc = masked_softmax(jnp.einsum('btgpd,btgld->btgpl', qr, ks.astype(f32)), smask)
    o_slc = jnp.einsum('btgpl,btgld->btgpd', p_slc, vs.astype(f32))
    rel = q_pos[:, None] - kw_pos[None, :]
    wmask = ((rel >= 0) & (rel < NSA_WIN) & (kw_pos >= 0)[None, :])[None, :, None, None, :]
    p_win = masked_softmax(jnp.einsum('btgpd,blgd->btgpl', qr, kw.astype(f32)), wmask)
    o_win = jnp.einsum('btgpl,blgd->btgpd', p_win, vw.astype(f32))
    gt = gate.reshape(B, T, G, hp, 3).astype(f32)
    o = gt[..., 0:1] * o_cmp + gt[..., 1:2] * o_slc + gt[..., 2:3] * o_win
    return o.reshape(B, T, H * Dh)


def nsa_prompt(nq, nkv, ng, p):
    B, S, _ = nq.shape
    pos = jnp.arange(S)
    q = nq.reshape(B, S, NSA_HEADS, NSA_DH)
    q_rot = rope(q, pos)
    kv = nkv.reshape(B, S, 3, 2, NSA_KV, NSA_DH)
    cmp_rows = kv[:, :, 0]
    k_slc, v_slc = rope(kv[:, :, 1, 0], pos), kv[:, :, 1, 1]
    k_win, v_win = rope(kv[:, :, 2, 0], pos), kv[:, :, 2, 1]
    kc, vc = compress(cmp_rows, p)
    gate = jax.nn.sigmoid(ng.astype(jnp.float32)).reshape(B, S, NSA_HEADS, 3)
    n_sel = S // NSA_SEL
    bi = jnp.arange(B)[:, None, None, None]
    gi = jnp.arange(NSA_KV)[None, None, :, None]

    def gather(idx):
        tok = (idx[..., None] * NSA_SEL + jnp.arange(NSA_SEL)).reshape(idx.shape[:3] + (-1,))
        return k_slc[bi, tok, gi], v_slc[bi, tok, gi], tok

    pw = ((0, 0), (NSA_WIN, 0), (0, 0), (0, 0))
    kw_pad, vw_pad = jnp.pad(k_win, pw), jnp.pad(v_win, pw)

    def block(i):
        q0 = i * NSA_QBLK
        take = lambda t, n: lax.dynamic_slice_in_dim(t, q0, n, axis=1)
        return nsa_attend(take(q, NSA_QBLK), take(q_rot, NSA_QBLK), q0 + jnp.arange(NSA_QBLK),
                          kc, vc, gather, n_sel,
                          take(kw_pad, NSA_WIN + NSA_QBLK), take(vw_pad, NSA_WIN + NSA_QBLK),
                          q0 - NSA_WIN + jnp.arange(NSA_WIN + NSA_QBLK), take(gate, NSA_QBLK))

    o = lax.map(block, jnp.arange(S // NSA_QBLK))
    o = jnp.moveaxis(o, 0, 1).reshape(B, S, NSA_HEADS * NSA_DH).astype(nq.dtype)
    slc_rows = jnp.stack([k_slc, v_slc], axis=2)
    win_rows = jnp.stack([k_win, v_win], axis=2)[:, S - min(NSA_WIN, S):]
    return o, cmp_rows, slc_rows, win_rows


def nsa_sample(nq, nkv, ng, cache_cmp, cache_slc, cache_win, page_table, p):
    B, T, _ = nq.shape
    P = page_table.shape[1] * PAGE_SIZE
    pos = P + jnp.arange(T)
    q = nq.reshape(B, T, NSA_HEADS, NSA_DH)
    q_rot = rope(q, pos)
    kv = nkv.reshape(B, T, 3, 2, NSA_KV, NSA_DH)
    cmp_new = kv[:, :, 0]
    slc_new = jnp.stack([rope(kv[:, :, 1, 0], pos), kv[:, :, 1, 1]], axis=2)
    win_new = jnp.stack([rope(kv[:, :, 2, 0], pos), kv[:, :, 2, 1]], axis=2)
    past_cmp = cache_cmp[page_table].reshape(B, P, 2, NSA_KV, NSA_DH)
    kc, vc = compress(jnp.concatenate([past_cmp, cmp_new.astype(past_cmp.dtype)], axis=1), p)
    gate = jax.nn.sigmoid(ng.astype(jnp.float32)).reshape(B, T, NSA_HEADS, 3)
    n_sel = -(-(P + T) // NSA_SEL)
    bi = jnp.arange(B)[:, None, None, None]
    gi = jnp.arange(NSA_KV)[None, None, :, None]

    def gather(idx):
        tok = (idx[..., None] * NSA_SEL + jnp.arange(NSA_SEL)).reshape(idx.shape[:3] + (-1,))
        tp = jnp.minimum(tok, P - 1)
        phys = page_table[bi, tp // PAGE_SIZE]
        past = cache_slc[phys, tp % PAGE_SIZE, :, gi]
        new = slc_new[bi, jnp.clip(tok - P, 0, T - 1), :, gi].astype(past.dtype)
        rows = jnp.where((tok < P)[..., None, None], past, new)
        return rows[..., 0, :], rows[..., 1, :], tok

    wb = cache_win.shape[1]
    win_all = jnp.concatenate([cache_win, win_new.astype(cache_win.dtype)], axis=1)
    o = nsa_attend(q, q_rot, pos, kc, vc, gather, n_sel, win_all[:, :, 0], win_all[:, :, 1],
                   P - wb + jnp.arange(wb + T), gate)
    new_win = win_all[:, win_all.shape[1] - min(NSA_WIN, P + T):]
    return o.astype(nq.dtype), cmp_new, slc_new, new_win


def mem_kv(mem, p):
    B, M, _ = mem.shape
    memn = rmsnorm(mem, p['norm_mem'])
    return (memn @ p['w_mem_kv']).reshape(B, M, 2, MEM_HEADS, MEM_DH)


def mem_attend(mq, kv):
    B, T, _ = mq.shape
    q = mq.reshape(B, T, MEM_HEADS, MEM_DH).astype(jnp.float32) * (MEM_DH ** -0.5)
    s = jnp.einsum('bthd,bmhd->bhtm', q, kv[:, :, 0].astype(jnp.float32))
    pr = jax.nn.softmax(s, axis=-1)
    o = jnp.einsum('bhtm,bmhd->bthd', pr, kv[:, :, 1].astype(jnp.float32))
    return o.reshape(B, T, MEM_HEADS * MEM_DH).astype(mq.dtype)


def peer(h, p):
    B, T, D = h.shape
    n = B * T
    nb = -(-n // PEER_TBLK)
    hf = jnp.pad(h.reshape(n, D), ((0, nb * PEER_TBLK - n), (0, 0))).reshape(nb, PEER_TBLK, D)

    def blk(xb):
        q = (xb @ p['peer_wq']).reshape(-1, PEER_HEADS, 2, PEER_DKEY // 2).astype(jnp.float32)
        s = jnp.einsum('thcd,cnd->thcn', q, p['peer_subkeys'].astype(jnp.float32))
        s1, i1 = lax.top_k(s[:, :, 0], PEER_TOPK)
        s2, i2 = lax.top_k(s[:, :, 1], PEER_TOPK)
        cand = (s1[..., :, None] + s2[..., None, :]).reshape(-1, PEER_HEADS, PEER_TOPK * PEER_TOPK)
        cidx = (i1[..., :, None] * PEER_NKEYS + i2[..., None, :]).reshape(-1, PEER_HEADS, PEER_TOPK * PEER_TOPK)
        sc, j = lax.top_k(cand, PEER_TOPK)
        e = jnp.take_along_axis(cidx, j, axis=-1)
        gw = jax.nn.softmax(sc, axis=-1)
        act = jax.nn.gelu(jnp.einsum('thkd,td->thk', p['peer_u'][e], xb).astype(jnp.float32))
        return jnp.einsum('thk,thkd->td', (gw * act).astype(xb.dtype), p['peer_v'][e])

    out = lax.map(blk, hf).reshape(nb * PEER_TBLK, D)[:n]
    return out.reshape(B, T, D)


def mixer_inputs(x, p):
    h = rmsnorm(x, p['norm_attn'])
    return split_cols(h @ p['w_in'], IN_SIZES)


def merge_and_ffn(x, o_gdn, o_nsa, o_mem, mg, p):
    B, T, D = x.shape
    gt = jax.nn.sigmoid(mg.astype(jnp.float32)).reshape(B, T, N_BRANCH, D)
    mix = (gt[:, :, 0] * (o_gdn @ p['w_gdn_out']) + gt[:, :, 1] * (o_nsa @ p['w_nsa_out'])
           + gt[:, :, 2] * (o_mem @ p['w_mem_out']))
    x = x + mix.astype(x.dtype) @ p['w_o']
    return x + peer(rmsnorm(x, p['norm_ffn']), p)


def layer_prompt(x, mem, p):
    B = x.shape[0]
    qkv, zg, a, b, nq, nkv, ng, mq, mg = mixer_inputs(x, p)
    o_gdn, conv_buf, s_gdn = gdn_branch(qkv, zg, a, b,
                                        jnp.zeros((B, GDN_CONV - 1, GDN_CONV_CH), x.dtype),
                                        jnp.zeros((B, GDN_HEADS, GDN_DK, GDN_DV), x.dtype), p)
    o_nsa, cmp_rows, slc_rows, win_rows = nsa_prompt(nq, nkv, ng, p)
    kvm = mem_kv(mem, p)
    o_mem = mem_attend(mq, kvm)
    x = merge_and_ffn(x, o_gdn, o_nsa, o_mem, mg, p)
    return x, (kvm, cmp_rows, slc_rows, win_rows, s_gdn, conv_buf)


def layer_sample(x, mem_kv_c, cmp_c, slc_c, win_c, s_gdn, s_conv, page_table, p):
    qkv, zg, a, b, nq, nkv, ng, mq, mg = mixer_inputs(x, p)
    o_gdn, conv_buf, s_new = gdn_branch(qkv, zg, a, b, s_conv, s_gdn, p)
    o_nsa, cmp_new, slc_new, win_new = nsa_sample(nq, nkv, ng, cmp_c, slc_c, win_c, page_table, p)
    o_mem = mem_attend(mq, mem_kv_c)
    x = merge_and_ffn(x, o_gdn, o_nsa, o_mem, mg, p)
    return x, (cmp_new, slc_new, win_new, s_new, conv_buf)


def setup_inputs(seed: int = 0) -> dict:
    key = jax.random.key(seed)
    ks = iter(jax.random.split(key, 40))
    f32 = jnp.float32

    def nrm(shape, scale=1.0):
        return jax.random.normal(next(ks), shape, f32) * scale

    def gain(shape):
        return 1.0 + nrm(shape, 0.02)

    n_pages = PAST_LEN // PAGE_SIZE
    n_pool = (5 * DEC_BATCH * n_pages + 3) // 4
    win_buf = min(NSA_WIN, PAST_LEN)
    page_table = jax.random.permutation(next(ks), n_pool)[:DEC_BATCH * n_pages].reshape(
        DEC_BATCH, n_pages).astype(jnp.int32)
    dt = jnp.exp(jax.random.uniform(next(ks), (DEPTH, GDN_HEADS), f32, math.log(1e-3), math.log(1e-1)))
    a_log = jnp.log(jax.random.uniform(next(ks), (DEPTH, GDN_HEADS), f32, 1.0, 16.0))
    return {
        'x_prompt': nrm((BATCH, SEQ, D_MODEL)),
        'x_sample': nrm((DEC_BATCH, DEC_SEQ, D_MODEL)),
        'cache_mem_kv': nrm((DEPTH, DEC_BATCH, MEM_LEN, 2, MEM_HEADS, MEM_DH)),
        'cache_cmp_kv': nrm((DEPTH, n_pool, PAGE_SIZE, 2, NSA_KV, NSA_DH)),
        'cache_slc_kv': nrm((DEPTH, n_pool, PAGE_SIZE, 2, NSA_KV, NSA_DH)),
        'cache_win_kv': nrm((DEPTH, DEC_BATCH, win_buf, 2, NSA_KV, NSA_DH)),
        'state_gdn': nrm((DEPTH, DEC_BATCH, GDN_HEADS, GDN_DK, GDN_DV), GDN_DK ** -0.5),
        'state_conv': nrm((DEPTH, DEC_BATCH, GDN_CONV - 1, GDN_CONV_CH)),
        'page_table': page_table,
        'mem_prompt': nrm((BATCH, MEM_LEN, D_MODEL)),
        'norm_attn': gain((DEPTH, D_MODEL)),
        'w_in': nrm((DEPTH, D_MODEL, IN_WIDTH), D_MODEL ** -0.5),
        'gdn_conv': nrm((DEPTH, GDN_CONV, GDN_CONV_CH), GDN_CONV ** -0.5),
        'gdn_a_log': a_log,
        'gdn_dt_bias': dt + jnp.log(-jnp.expm1(-dt)),
        'gdn_norm': gain((DEPTH, GDN_DV)),
        'nsa_cmp_pe': nrm((DEPTH, 2, NSA_CMP, NSA_DH), 0.1),
        'nsa_cmp_w1': nrm((DEPTH, 2, NSA_CMP * NSA_DH, NSA_DH), (NSA_CMP * NSA_DH) ** -0.5),
        'nsa_cmp_w2': nrm((DEPTH, 2, NSA_DH, NSA_DH), NSA_DH ** -0.5),
        'norm_mem': gain((DEPTH, D_MODEL)),
        'w_mem_kv': nrm((DEPTH, D_MODEL, 2 * MEM_HEADS * MEM_DH), D_MODEL ** -0.5),
        'w_gdn_out': nrm((DEPTH, GDN_HEADS * GDN_DV, D_MODEL), (GDN_HEADS * GDN_DV) ** -0.5),
        'w_nsa_out': nrm((DEPTH, NSA_HEADS * NSA_DH, D_MODEL), (NSA_HEADS * NSA_DH) ** -0.5),
        'w_mem_out': nrm((DEPTH, MEM_HEADS * MEM_DH, D_MODEL), (MEM_HEADS * MEM_DH) ** -0.5),
        'w_o': nrm((DEPTH, D_MODEL, D_MODEL), D_MODEL ** -0.5),
        'norm_ffn': gain((DEPTH, D_MODEL)),
        'peer_wq': nrm((DEPTH, D_MODEL, PEER_HEADS * PEER_DKEY), D_MODEL ** -0.5),
        'peer_subkeys': nrm((DEPTH, 2, PEER_NKEYS, PEER_DKEY // 2), (PEER_DKEY // 2) ** -0.5),
        'peer_u': nrm((DEPTH, PEER_N, D_MODEL), D_MODEL ** -0.5),
        'peer_v': nrm((DEPTH, PEER_N, D_MODEL), PEER_HEADS ** -0.5),
        'norm_final': gain((D_MODEL,)),
    }


def reference(x_prompt, x_sample, cache_mem_kv, cache_cmp_kv, cache_slc_kv, cache_win_kv,
              state_gdn, state_conv, page_table, mem_prompt, norm_attn, w_in, gdn_conv,
              gdn_a_log, gdn_dt_bias, gdn_norm, nsa_cmp_pe, nsa_cmp_w1, nsa_cmp_w2, norm_mem,
              w_mem_kv, w_gdn_out, w_nsa_out, w_mem_out, w_o, norm_ffn, peer_wq, peer_subkeys,
              peer_u, peer_v, norm_final):
    xp, xs = x_prompt, x_sample
    sp, ss = [], []
    for l in range(DEPTH):
        p = dict(norm_attn=norm_attn[l], w_in=w_in[l], gdn_conv=gdn_conv[l], gdn_a_log=gdn_a_log[l],
                 gdn_dt_bias=gdn_dt_bias[l], gdn_norm=gdn_norm[l], nsa_cmp_pe=nsa_cmp_pe[l],
                 nsa_cmp_w1=nsa_cmp_w1[l], nsa_cmp_w2=nsa_cmp_w2[l], norm_mem=norm_mem[l],
                 w_mem_kv=w_mem_kv[l], w_gdn_out=w_gdn_out[l], w_nsa_out=w_nsa_out[l],
                 w_mem_out=w_mem_out[l], w_o=w_o[l], norm_ffn=norm_ffn[l], peer_wq=peer_wq[l],
                 peer_subkeys=peer_subkeys[l], peer_u=peer_u[l], peer_v=peer_v[l])
        xp, st_p = layer_prompt(xp, mem_prompt, p)
        xs, st_s = layer_sample(xs, cache_mem_kv[l], cache_cmp_kv[l], cache_slc_kv[l], cache_win_kv[l],
                                state_gdn[l], state_conv[l], page_table, p)
        sp.append(st_p)
        ss.append(st_s)
    y_prompt = rmsnorm(xp, norm_final)
    y_sample = rmsnorm(xs, norm_final)
    new_mem_kv_p = jnp.stack([s[0] for s in sp])
    new_cmp_kv_p = jnp.stack([s[1] for s in sp])
    new_slc_kv_p = jnp.stack([s[2] for s in sp])
    new_win_kv_p = jnp.stack([s[3] for s in sp])
    new_gdn_p = jnp.stack([s[4] for s in sp])
    new_conv_p = jnp.stack([s[5] for s in sp])
    new_cmp_kv_s = jnp.stack([s[0] for s in ss])
    new_slc_kv_s = jnp.stack([s[1] for s in ss])
    new_win_kv_s = jnp.stack([s[2] for s in ss])
    new_gdn_s = jnp.stack([s[3] for s in ss])
    new_conv_s = jnp.stack([s[4] for s in ss])
    return (y_prompt, y_sample, new_mem_kv_p, new_cmp_kv_p, new_slc_kv_p, new_win_kv_p, new_gdn_p,
            new_conv_p, new_cmp_kv_s, new_slc_kv_s, new_win_kv_s, new_gdn_s, new_conv_s)
```

```python
import functools
import math

import jax
import jax.numpy as jnp
import numpy as np
from jax import lax
from jax.experimental import pallas as pl
from jax.experimental.pallas import tpu as pltpu

F32 = jnp.float32
BF16 = jnp.bfloat16
HIGHEST = lax.Precision.HIGHEST

D_MODEL = 1024
PAGE_SIZE = 128
GDN_HEADS = 4
GDN_DK = 128
GDN_DV = 128
GDN_CONV = 4
GDN_CHUNK = 64
GDN_CONV_CH = GDN_HEADS * (2 * GDN_DK + GDN_DV)
NSA_HEADS = 8
NSA_KV = 2
NSA_DH = 64
NSA_CMP = 32
NSA_SEL = 64
NSA_TOPN = 16
NSA_WIN = 512
NSA_FORCE = 1e9
MEM_HEADS = 4
MEM_DH = 128
PEER_HEADS = 8
PEER_NKEYS = 128
PEER_DKEY = 256
PEER_TOPK = 16
N_BRANCH = 3
ROPE_THETA = 10000.0
EPS = 1e-6

LANES = 128
SUBLANES = 8
VMEM_LIMIT = 56 * 1024 * 1024

Z_MG = 0
Z_QKV = Z_MG + N_BRANCH * D_MODEL
Z_ZG = Z_QKV + GDN_CONV_CH
Z_NQ = Z_ZG + GDN_HEADS * GDN_DV
Z_MQ = Z_NQ + NSA_HEADS * NSA_DH
Z_NKV = Z_MQ + MEM_HEADS * MEM_DH
Z_SMALL = Z_NKV + 3 * 2 * NSA_KV * NSA_DH
Z_WIDTH = Z_SMALL + LANES
SM_A = 0
SM_B = GDN_HEADS
SM_NG = 2 * GDN_HEADS
NKV_W = 3 * 2 * NSA_KV * NSA_DH
ROW_W = 2 * NSA_KV * NSA_DH


def _params(sem, vmem=VMEM_LIMIT):
    return pltpu.CompilerParams(dimension_semantics=sem, vmem_limit_bytes=vmem)


def _dot(a, b):
    return jnp.dot(a.astype(BF16), b.astype(BF16), preferred_element_type=F32)


def _dot_nt(a, b):
    return lax.dot_general(a.astype(BF16), b.astype(BF16), (((1,), (1,)), ((), ())),
                           preferred_element_type=F32)


def _dot_hi(a, b):
    return jnp.dot(a, b, precision=HIGHEST, preferred_element_type=F32)


def _rms(x, g):
    return x * lax.rsqrt(jnp.mean(x * x, axis=-1, keepdims=True) + EPS) * g


def _gelu(x):
    return jax.nn.gelu(x, approximate=True)


def _norm_matmul_kernel(x_ref, g_ref, w_ref, o_ref, *, col_chunk):
    yb = _rms(x_ref[...], g_ref[...]).astype(BF16)
    for c0 in range(0, o_ref.shape[1], col_chunk):
        o_ref[:, c0:c0 + col_chunk] = jnp.dot(yb, w_ref[:, c0:c0 + col_chunk],
                                              preferred_element_type=F32)


def _norm_matmul(x, g, w, tm, col_chunk):
    n, d = x.shape
    wc = w.shape[1]
    return pl.pallas_call(
        functools.partial(_norm_matmul_kernel, col_chunk=col_chunk),
        grid=(n // tm,),
        in_specs=[pl.BlockSpec((tm, d), lambda i: (i, 0)),
                  pl.BlockSpec((1, d), lambda i: (0, 0)),
                  pl.BlockSpec((d, wc), lambda i: (0, 0), pipeline_mode=pl.Buffered(1))],
        out_specs=pl.BlockSpec((tm, wc), lambda i: (i, 0)),
        out_shape=jax.ShapeDtypeStruct((n, wc), F32),
        compiler_params=_params(("parallel",)),
    )(x, g.reshape(1, d), w)


def _tri_inverse(lmat, c):
    row = lax.broadcasted_iota(jnp.int32, (c, c), 0)
    col = lax.broadcasted_iota(jnp.int32, (c, c), 1)
    x = jnp.where(row == col, 1.0, 0.0) - lmat
    p = _dot_hi(lmat, lmat)
    n = 2
    while n < c:
        x = x + _dot_hi(x, p)
        n *= 2
        if n < c:
            p = _dot_hi(p, p)
    return x


def _gdn_kernel(qkv_ref, zg_ref, sm_ref, buf_ref, s0_ref, cw_ref, alog_ref, dt_ref, gn_ref,
                o_ref, snew_ref, cnew_ref, ext_ref, s_ref, *, tb, n_chunks):
    c = GDN_CHUNK
    ci = pl.program_id(1)

    @pl.when(ci == 0)
    def _():
        ext_ref[0:SUBLANES, :] = buf_ref[0]
        s_ref[...] = s0_ref[0]

    if tb < c:
        ext_ref[SUBLANES + tb:, :] = jnp.zeros((c - tb, GDN_CONV_CH), F32)
    ext_ref[SUBLANES:SUBLANES + tb, :] = qkv_ref[...]
    cw = cw_ref[...]
    conv = cw[0:1] * ext_ref[SUBLANES - 3:SUBLANES - 3 + c, :]
    for j in range(1, GDN_CONV):
        conv = conv + cw[j:j + 1] * ext_ref[SUBLANES - 3 + j:SUBLANES - 3 + j + c, :]
    u = conv * jax.nn.sigmoid(conv)
    last_rows = ext_ref[tb:tb + SUBLANES, :]
    cnew_ref[0] = last_rows
    ext_ref[0:SUBLANES, :] = last_rows

    sm = sm_ref[...]
    if tb < c:
        sm = jnp.concatenate([sm, jnp.zeros((c - tb, LANES), F32)], axis=0)
    za = sm + dt_ref[...]
    softplus = jnp.maximum(za, 0.0) + jnp.log1p(jnp.exp(-jnp.abs(za)))
    g_all = -jnp.exp(alog_ref[...]) * softplus
    beta_all = jax.nn.sigmoid(sm)
    if tb < c:
        valid = lax.broadcasted_iota(jnp.int32, (c, 1), 0) < tb
        u = jnp.where(valid, u, 0.0)
        g_all = jnp.where(valid, g_all, 0.0)
        beta_all = jnp.where(valid, beta_all, 0.0)

    row = lax.broadcasted_iota(jnp.int32, (c, c), 0)
    col = lax.broadcasted_iota(jnp.int32, (c, c), 1)
    tril = row >= col
    eye = row == col
    gc_all = _dot_hi(jnp.where(tril, 1.0, 0.0), g_all)

    hk = GDN_HEADS * GDN_DK
    for h in range(GDN_HEADS):
        qh = u[:, h * GDN_DK:(h + 1) * GDN_DK]
        kh = u[:, hk + h * GDN_DK:hk + (h + 1) * GDN_DK]
        vh = u[:, 2 * hk + h * GDN_DV:2 * hk + (h + 1) * GDN_DV]
        qn = qh * lax.rsqrt(jnp.sum(qh * qh, axis=-1, keepdims=True) + EPS) * (GDN_DK ** -0.5)
        kn = kh * lax.rsqrt(jnp.sum(kh * kh, axis=-1, keepdims=True) + EPS)
        beta = beta_all[:, SM_B + h:SM_B + h + 1]
        gc = gc_all[:, SM_A + h:SM_A + h + 1]
        gl = gc_all[c - 1:c, SM_A + h:SM_A + h + 1]
        gc_row = jnp.sum(jnp.where(eye, gc, 0.0), axis=0, keepdims=True)
        decay = jnp.exp(jnp.where(tril, gc - gc_row, -jnp.inf))
        kb = kn * beta
        lmat = jnp.where(row > col, _dot_nt(kb, kn) * decay, 0.0)
        tinv = _tri_inverse(lmat, c)
        egc = jnp.exp(gc)
        uu = _dot(tinv, vh * beta)
        ww = _dot(tinv, kb * egc)
        aqk = _dot_nt(qn, kn) * decay
        s = s_ref[h]
        v_new = uu - _dot(ww, s)
        o = _dot(qn * egc, s) + _dot(aqk, v_new)
        kd = kn * jnp.exp(gl - gc)
        s_ref[h] = s * jnp.exp(gl) + _dot(kd.T, v_new)
        zh = zg_ref[:, h * GDN_DV:(h + 1) * GDN_DV]
        on = _rms(o[0:tb], gn_ref[...]) * (zh * jax.nn.sigmoid(zh))
        o_ref[:, h * GDN_DV:(h + 1) * GDN_DV] = on

    @pl.when(ci == n_chunks - 1)
    def _():
        snew_ref[0] = s_ref[...]


def _gdn(z, conv_buf, s0, conv_w, a_log, dt_bias, gnorm, batch, t):
    c = GDN_CHUNK
    tb = min(t, c)
    n_chunks = t // tb
    assert tb % SUBLANES == 0 and n_chunks * tb == t and (tb == c or n_chunks == 1)
    buf8 = jnp.pad(conv_buf, ((0, 0), (SUBLANES - (GDN_CONV - 1), 0), (0, 0)))
    alog_row = jnp.zeros((1, LANES), F32).at[0, SM_A:SM_A + GDN_HEADS].set(a_log)
    dt_row = jnp.zeros((1, LANES), F32).at[0, SM_A:SM_A + GDN_HEADS].set(dt_bias)
    rowblk = lambda b, ci: b * n_chunks + ci
    o, s_new, c_new = pl.pallas_call(
        functools.partial(_gdn_kernel, tb=tb, n_chunks=n_chunks),
        grid=(batch, n_chunks),
        in_specs=[
            pl.BlockSpec((tb, GDN_CONV_CH), lambda b, ci: (rowblk(b, ci), Z_QKV // GDN_CONV_CH)),
            pl.BlockSpec((tb, GDN_HEADS * GDN_DV), lambda b, ci: (rowblk(b, ci), Z_ZG // (GDN_HEADS * GDN_DV))),
            pl.BlockSpec((tb, LANES), lambda b, ci: (rowblk(b, ci), Z_SMALL // LANES)),
            pl.BlockSpec((1, SUBLANES, GDN_CONV_CH), lambda b, ci: (b, 0, 0)),
            pl.BlockSpec((1, GDN_HEADS, GDN_DK, GDN_DV), lambda b, ci: (b, 0, 0, 0)),
            pl.BlockSpec((GDN_CONV, GDN_CONV_CH), lambda b, ci: (0, 0)),
            pl.BlockSpec((1, LANES), lambda b, ci: (0, 0)),
            pl.BlockSpec((1, LANES), lambda b, ci: (0, 0)),
            pl.BlockSpec((1, GDN_DV), lambda b, ci: (0, 0)),
        ],
        out_specs=[
            pl.BlockSpec((tb, GDN_HEADS * GDN_DV), lambda b, ci: (rowblk(b, ci), 0)),
            pl.BlockSpec((1, GDN_HEADS, GDN_DK, GDN_DV), lambda b, ci: (b, 0, 0, 0)),
            pl.BlockSpec((1, SUBLANES, GDN_CONV_CH), lambda b, ci: (b, 0, 0)),
        ],
        out_shape=[
            jax.ShapeDtypeStruct((batch * t, GDN_HEADS * GDN_DV), F32),
            jax.ShapeDtypeStruct((batch, GDN_HEADS, GDN_DK, GDN_DV), F32),
            jax.ShapeDtypeStruct((batch, SUBLANES, GDN_CONV_CH), F32),
        ],
        scratch_shapes=[pltpu.VMEM((SUBLANES + c, GDN_CONV_CH), F32),
                        pltpu.VMEM((GDN_HEADS, GDN_DK, GDN_DV), F32)],
        compiler_params=_params(("parallel", "arbitrary")),
    )(z, z, z, buf8, s0, conv_w, alog_row, dt_row, gnorm.reshape(1, GDN_DV))
    return o, s_new, c_new[:, SUBLANES - (GDN_CONV - 1):]


def _rope_tables(pos):
    half = NSA_DH // 2
    inv = jnp.power(ROPE_THETA, -jnp.arange(half, dtype=F32) / half)
    ang = pos.astype(F32)[:, None] * inv[None, :]
    cos, sin = jnp.cos(ang), jnp.sin(ang)
    cos_t = jnp.concatenate([cos, cos, cos, cos], axis=-1)
    sin_t = jnp.concatenate([-sin, sin, -sin, sin], axis=-1)
    return cos_t, sin_t


def _nsa_prep_kernel(nq_ref, nkv_ref, cos_ref, sin_ref,
                     qc_ref, qr_ref, slc_ref, win_ref, ks_ref, vs_ref, kw_ref, vw_ref):
    cos = cos_ref[...]
    sin = sin_ref[...]
    lane = lax.broadcasted_iota(jnp.int32, cos.shape, 1)
    first_half = (lane % NSA_DH) < (NSA_DH // 2)

    def rope(x):
        swapped = jnp.where(first_half, pltpu.roll(x, LANES - NSA_DH // 2, 1),
                            pltpu.roll(x, NSA_DH // 2, 1))
        return x * cos + swapped * sin

    scale = NSA_DH ** -0.5
    for j in range(NSA_HEADS * NSA_DH // LANES):
        x = nq_ref[:, j * LANES:(j + 1) * LANES]
        qc_ref[:, j * LANES:(j + 1) * LANES] = x * scale
        qr_ref[:, j * LANES:(j + 1) * LANES] = rope(x) * scale

    kv_w = NSA_KV * NSA_DH
    for br, (row_ref, k_ref, v_ref) in enumerate(((slc_ref, ks_ref, vs_ref),
                                                   (win_ref, kw_ref, vw_ref))):
        base = (br + 1) * ROW_W
        kr = rope(nkv_ref[:, base:base + kv_w])
        v = nkv_ref[:, base + kv_w:base + 2 * kv_w]
        row_ref[:, 0:kv_w] = kr
        row_ref[:, kv_w:2 * kv_w] = v
        for g in range(NSA_KV):
            k_ref[0, g] = kr[:, g * NSA_DH:(g + 1) * NSA_DH].astype(k_ref.dtype)
            v_ref[0, g] = v[:, g * NSA_DH:(g + 1) * NSA_DH].astype(v_ref.dtype)


def _nsa_prep(z, pos, batch, t, tm, kv_dtype):
    cos_t, sin_t = _rope_tables(pos)
    nt = t // tm
    rowblk = lambda b, j: b * nt + j
    kv_shape = jax.ShapeDtypeStruct((batch, NSA_KV, t, NSA_DH), kv_dtype)
    kv_spec = pl.BlockSpec((1, NSA_KV, tm, NSA_DH), lambda b, j: (b, 0, j, 0))
    qw = NSA_HEADS * NSA_DH
    return pl.pallas_call(
        _nsa_prep_kernel,
        grid=(batch, nt),
        in_specs=[pl.BlockSpec((tm, qw), lambda b, j: (rowblk(b, j), Z_NQ // qw)),
                  pl.BlockSpec((tm, NKV_W), lambda b, j: (rowblk(b, j), Z_NKV // NKV_W)),
                  pl.BlockSpec((tm, LANES), lambda b, j: (j, 0)),
                  pl.BlockSpec((tm, LANES), lambda b, j: (j, 0))],
        out_specs=[pl.BlockSpec((tm, qw), lambda b, j: (rowblk(b, j), 0)),
                   pl.BlockSpec((tm, qw), lambda b, j: (rowblk(b, j), 0)),
                   pl.BlockSpec((tm, ROW_W), lambda b, j: (rowblk(b, j), 0)),
                   pl.BlockSpec((tm, ROW_W), lambda b, j: (rowblk(b, j), 0)),
                   kv_spec, kv_spec, kv_spec, kv_spec],
        out_shape=[jax.ShapeDtypeStruct((batch * t, qw), F32),
                   jax.ShapeDtypeStruct((batch * t, qw), F32),
                   jax.ShapeDtypeStruct((batch * t, ROW_W), F32),
                   jax.ShapeDtypeStruct((batch * t, ROW_W), F32),
                   kv_shape, kv_shape, kv_shape, kv_shape],
        compiler_params=_params(("parallel", "parallel")),
    )(z, z, cos_t, sin_t)


def _compress_weights(pe, w1, w2):
    eye = jnp.eye(2, dtype=F32)
    w1r = w1.reshape(2, NSA_CMP, NSA_DH, NSA_DH)
    w1big = jnp.einsum('srde,st,gh->rsgdthe', w1r, eye, eye).reshape(NSA_CMP * ROW_W, ROW_W)
    w2big = jnp.einsum('sed,st,gh->sgethd', w2, eye, eye).reshape(ROW_W, ROW_W)
    pe_big = jnp.broadcast_to(jnp.transpose(pe, (1, 0, 2))[:, :, None, :],
                              (NSA_CMP, 2, NSA_KV, NSA_DH)).reshape(1, NSA_CMP * ROW_W)
    return pe_big, w1big.astype(BF16), w2big.astype(BF16)


def _compress_kernel(x_ref, pe_ref, w1_ref, w2_ref, o_ref):
    x = (x_ref[...] + pe_ref[...]).astype(BF16)
    hid = _gelu(jnp.dot(x, w1_ref[...], preferred_element_type=F32))
    o_ref[...] = jnp.dot(hid.astype(BF16), w2_ref[...], preferred_element_type=F32)


def _compress(rows, pe_big, w1big, w2big, tm):
    n, kdim = rows.shape
    return pl.pallas_call(
        _compress_kernel,
        grid=(n // tm,),
        in_specs=[pl.BlockSpec((tm, kdim), lambda i: (i, 0)),
                  pl.BlockSpec((1, kdim), lambda i: (0, 0)),
                  pl.BlockSpec((kdim, ROW_W), lambda i: (0, 0)),
                  pl.BlockSpec((ROW_W, ROW_W), lambda i: (0, 0))],
        out_specs=pl.BlockSpec((tm, ROW_W), lambda i: (i, 0)),
        out_shape=jax.ShapeDtypeStruct((n, ROW_W), F32),
        compiler_params=_params(("parallel",)),
    )(rows, pe_big, w1big, w2big)


def _split_compressed(cmp_out, batch):
    nb = cmp_out.shape[0] // batch
    x = cmp_out.reshape(batch, nb // 2, 2, 2, NSA_KV, NSA_DH)
    x = jnp.transpose(x, (3, 0, 4, 2, 1, 5)).reshape(2, batch, NSA_KV, nb, NSA_DH)
    return x[0].astype(BF16), x[1].astype(BF16)


PAGES_PER_STEP = 8


def _gather_rows_kernel(pt_ref, *refs):
    o_ref = refs[-1]
    for j, p_ref in enumerate(refs[:-1]):
        o_ref[0, j * PAGE_SIZE:(j + 1) * PAGE_SIZE, :] = p_ref[0]


def _gather_split_kernel(pt_ref, *refs):
    k_ref, v_ref = refs[-2:]
    kv_w = NSA_KV * NSA_DH
    for j, p_ref in enumerate(refs[:-2]):
        page = p_ref[0]
        for g in range(NSA_KV):
            k_ref[0, g, j * PAGE_SIZE:(j + 1) * PAGE_SIZE, :] = (
                page[:, g * NSA_DH:(g + 1) * NSA_DH].astype(BF16))
            v_ref[0, g, j * PAGE_SIZE:(j + 1) * PAGE_SIZE, :] = (
                page[:, kv_w + g * NSA_DH:kv_w + (g + 1) * NSA_DH].astype(BF16))


def _page_specs(n_pages):
    def spec(j):
        return pl.BlockSpec((1, PAGE_SIZE, ROW_W),
                            lambda b, s, pt: (pt[b * n_pages + s * PAGES_PER_STEP + j], 0, 0))
    return [spec(j) for j in range(PAGES_PER_STEP)]


def _gather_rows(cache, page_table):
    batch, n_pages = page_table.shape
    rows = PAGES_PER_STEP * PAGE_SIZE
    return pl.pallas_call(
        _gather_rows_kernel,
        grid_spec=pltpu.PrefetchScalarGridSpec(
            num_scalar_prefetch=1, grid=(batch, n_pages // PAGES_PER_STEP),
            in_specs=_page_specs(n_pages),
            out_specs=pl.BlockSpec((1, rows, ROW_W), lambda b, s, pt: (b, s, 0))),
        out_shape=jax.ShapeDtypeStruct((batch, n_pages * PAGE_SIZE, ROW_W), F32),
        compiler_params=_params(("parallel", "parallel")),
    )(page_table.reshape(-1), *([cache] * PAGES_PER_STEP))


def _gather_split(cache, page_table):
    batch, n_pages = page_table.shape
    rows = PAGES_PER_STEP * PAGE_SIZE
    shape = jax.ShapeDtypeStruct((batch, NSA_KV, n_pages * PAGE_SIZE, NSA_DH), BF16)
    spec = pl.BlockSpec((1, NSA_KV, rows, NSA_DH), lambda b, s, pt: (b, 0, s, 0))
    return pl.pallas_call(
        _gather_split_kernel,
        grid_spec=pltpu.PrefetchScalarGridSpec(
            num_scalar_prefetch=1, grid=(batch, n_pages // PAGES_PER_STEP),
            in_specs=_page_specs(n_pages), out_specs=[spec, spec]),
        out_shape=[shape, shape],
        compiler_params=_params(("parallel", "parallel")),
    )(page_table.reshape(-1), *([cache] * PAGES_PER_STEP))


def _topk_mask(score, k):
    n = score.shape[-1]
    lane = lax.broadcasted_iota(jnp.int32, score.shape, 1).astype(F32)
    sel = jnp.zeros(score.shape, F32)
    for _ in range(k):
        m = jnp.max(score, axis=-1, keepdims=True)
        idx = jnp.min(jnp.where(score == m, lane, float(n)), axis=-1, keepdims=True)
        pick = lane == idx
        sel = jnp.where(pick, 1.0, sel)
        score = jnp.where(pick, -jnp.inf, score)
    return sel


def _softmax_step(carry, s, v):
    m, l, acc = carry
    m_new = jnp.maximum(m, jnp.max(s, axis=-1, keepdims=True))
    m_safe = jnp.where(m_new == -jnp.inf, 0.0, m_new)
    p = jnp.exp(s - m_safe)
    alpha = jnp.exp(m - m_safe)
    l = alpha * l + jnp.sum(p, axis=-1, keepdims=True)
    acc = alpha * acc + _dot(p, v)
    return m_new, l, acc


def _nsa_attn_kernel(*refs, tq, pos0, wrel0, n_keys, tk, has_tail):
    if has_tail:
        (qc_ref, qr_ref, kc_ref, vc_ref, ks_ref, vs_ref, kw_ref, vw_ref, kt_ref, vt_ref,
         sm_ref, o_ref) = refs
    else:
        (qc_ref, qr_ref, kc_ref, vc_ref, ks_ref, vs_ref, kw_ref, vw_ref, sm_ref, o_ref) = refs
    hp = NSA_HEADS // NSA_KV
    r = hp * tq
    g = pl.program_id(1)
    qi = pl.program_id(2)
    ncp = kc_ref.shape[2]
    nch = ncp // 2
    nsl = -(-(n_keys // NSA_SEL + (1 if has_tail else 0)) // LANES) * LANES

    def stack_heads(ref):
        x = ref[...]
        return jnp.concatenate([x[:, h * NSA_DH:(h + 1) * NSA_DH] for h in range(hp)],
                               axis=0).astype(BF16)

    qc4 = stack_heads(qc_ref)
    qr4 = stack_heads(qr_ref)
    t_q = pos0 + qi * tq + lax.broadcasted_iota(jnp.int32, (tq, 1), 0)
    t_row = jnp.concatenate([t_q] * hp, axis=0)

    s = _dot_nt(qc4, kc_ref[0, 0])
    ccol = lax.broadcasted_iota(jnp.int32, (r, ncp), 1)
    cblk = 2 * (ccol % nch) + ccol // nch
    s = jnp.where((cblk + 1) * NSA_CMP <= t_row + 1, s, -jnp.inf)
    m = jnp.max(s, axis=-1, keepdims=True)
    e = jnp.exp(s - jnp.where(m == -jnp.inf, 0.0, m))
    p_cmp = e / jnp.maximum(jnp.sum(e, axis=-1, keepdims=True), 1e-30)
    o_cmp = _dot(p_cmp, vc_ref[0, 0])

    imp = p_cmp[0:tq]
    for h in range(1, hp):
        imp = imp + p_cmp[h * tq:(h + 1) * tq]
    imp = imp[:, :nch] + imp[:, nch:]
    if nsl > nch:
        imp = jnp.concatenate([imp, jnp.zeros((tq, nsl - nch), F32)], axis=1)
    blk = lax.broadcasted_iota(jnp.int32, (tq, nsl), 1)
    cur = t_q // NSA_SEL
    forced = (blk == 0) | (blk == cur) | (blk == cur - 1)
    score = jnp.where(blk > cur, -jnp.inf, jnp.where(forced, NSA_FORCE, imp))
    sel = _topk_mask(score, NSA_TOPN).astype(BF16)

    init = (jnp.full((r, 1), -jnp.inf, F32), jnp.zeros((r, 1), F32), jnp.zeros((r, NSA_DH), F32))

    def slc_scores(k, kpos0, width):
        kpos = kpos0 + lax.broadcasted_iota(jnp.int32, (1, width), 1)
        expand = (lax.broadcasted_iota(jnp.int32, (nsl, width), 0) == kpos // NSA_SEL)
        picked = jnp.dot(sel, expand.astype(BF16), preferred_element_type=F32)
        picked = jnp.concatenate([picked] * hp, axis=0)
        return jnp.where((picked > 0.5) & (kpos <= t_row), _dot_nt(qr4, k), -jnp.inf)

    def slc_body(kt, carry):
        start = pl.multiple_of(kt * tk, tk)
        k = ks_ref[0, 0, pl.ds(start, tk), :]
        v = vs_ref[0, 0, pl.ds(start, tk), :]
        return _softmax_step(carry, slc_scores(k, kt * tk, tk), v)

    t_last = pos0 + qi * tq + tq - 1
    n_main = jnp.minimum(n_keys // tk, t_last // tk + 1)
    carry = lax.fori_loop(0, n_main, slc_body, init)
    if has_tail:
        carry = _softmax_step(carry, slc_scores(kt_ref[0, 0], n_keys, kt_ref.shape[2]), vt_ref[0, 0])
    o_slc = carry[2] / jnp.maximum(carry[1], 1e-30)

    wbase = pos0 - wrel0

    def win_body(kt, carry):
        start = pl.multiple_of(kt * tk, tk)
        k = kw_ref[0, 0, pl.ds(start, tk), :]
        v = vw_ref[0, 0, pl.ds(start, tk), :]
        kpos = wbase + kt * tk + lax.broadcasted_iota(jnp.int32, (1, tk), 1)
        rel = t_row - kpos
        ok = (rel >= 0) & (rel < NSA_WIN) & (kpos >= 0)
        return _softmax_step(carry, jnp.where(ok, _dot_nt(qr4, k), -jnp.inf), v)

    w_first = jnp.maximum(pos0 + qi * tq - (NSA_WIN - 1) - wbase, 0) // tk
    w_last = jnp.minimum((t_last - wbase) // tk, kw_ref.shape[2] // tk - 1)
    carry = lax.fori_loop(w_first, w_last + 1, win_body, init)
    o_win = carry[2] / jnp.maximum(carry[1], 1e-30)

    sig = jax.nn.sigmoid(sm_ref[...])
    gw = N_BRANCH * hp
    gates = jnp.where(g == 0, sig[:, SM_NG:SM_NG + gw], sig[:, SM_NG + gw:SM_NG + 2 * gw])
    outs = []
    for h in range(hp):
        rows = slice(h * tq, (h + 1) * tq)
        outs.append(gates[:, 3 * h:3 * h + 1] * o_cmp[rows]
                    + gates[:, 3 * h + 1:3 * h + 2] * o_slc[rows]
                    + gates[:, 3 * h + 2:3 * h + 3] * o_win[rows])
    o_ref[...] = jnp.concatenate(outs, axis=1)


def _nsa_attn(qc, qr, kc, vc, ks, vs, kw, vw, tails, z, batch, t, tq, pos0, wrel0, tk):
    nq = t // tq
    n_keys = ks.shape[2]
    gw = NSA_HEADS * NSA_DH // NSA_KV
    rowblk = lambda b, g, i: b * nq + i
    full = lambda a: pl.BlockSpec((1, 1) + a.shape[2:], lambda b, g, i: (b, g, 0, 0))
    q_spec = pl.BlockSpec((tq, gw), lambda b, g, i: (rowblk(b, g, i), g))
    operands = [qc, qr, kc, vc, ks, vs, kw, vw] + list(tails)
    in_specs = [q_spec, q_spec] + [full(a) for a in operands[2:]]
    in_specs.append(pl.BlockSpec((tq, LANES), lambda b, g, i: (rowblk(b, g, i), Z_SMALL // LANES)))
    return pl.pallas_call(
        functools.partial(_nsa_attn_kernel, tq=tq, pos0=pos0, wrel0=wrel0, n_keys=n_keys, tk=tk,
                          has_tail=bool(tails)),
        grid=(batch, NSA_KV, nq),
        in_specs=in_specs,
        out_specs=pl.BlockSpec((tq, gw), lambda b, g, i: (rowblk(b, g, i), g)),
        out_shape=jax.ShapeDtypeStruct((batch * t, NSA_HEADS * NSA_DH), F32),
        compiler_params=_params(("parallel", "parallel", "arbitrary")),
    )(*operands, z)


def _mem_attn_kernel(q_ref, kv_ref, o_ref):
    hw = MEM_HEADS * MEM_DH
    for h in range(MEM_HEADS):
        q = q_ref[:, h * MEM_DH:(h + 1) * MEM_DH] * (MEM_DH ** -0.5)
        k = kv_ref[0, :, h * MEM_DH:(h + 1) * MEM_DH]
        v = kv_ref[0, :, hw + h * MEM_DH:hw + (h + 1) * MEM_DH]
        s = _dot_nt(q, k)
        e = jnp.exp(s - jnp.max(s, axis=-1, keepdims=True))
        p = e / jnp.sum(e, axis=-1, keepdims=True)
        o_ref[:, h * MEM_DH:(h + 1) * MEM_DH] = _dot(p, v)


def _mem_attn(z, kv, batch, t, tm):
    nt = t // tm
    hw = MEM_HEADS * MEM_DH
    return pl.pallas_call(
        _mem_attn_kernel,
        grid=(batch, nt),
        in_specs=[pl.BlockSpec((tm, hw), lambda b, j: (b * nt + j, Z_MQ // hw)),
                  pl.BlockSpec((1,) + kv.shape[1:], lambda b, j: (b, 0, 0))],
        out_specs=pl.BlockSpec((tm, hw), lambda b, j: (b * nt + j, 0)),
        out_shape=jax.ShapeDtypeStruct((batch * t, hw), F32),
        compiler_params=_params(("parallel", "parallel")),
    )(z, kv)


def _merge_kernel(x_ref, mg_ref, og_ref, on_ref, om_ref, wg_ref, wn_ref, wm_ref, wo_ref,
                  gf_ref, wq_ref, x1_ref, h2_ref, qp_ref):
    d = D_MODEL
    mix = (jax.nn.sigmoid(mg_ref[:, 0:d]) * _dot(og_ref[...], wg_ref[...])
           + jax.nn.sigmoid(mg_ref[:, d:2 * d]) * _dot(on_ref[...], wn_ref[...])
           + jax.nn.sigmoid(mg_ref[:, 2 * d:3 * d]) * _dot(om_ref[...], wm_ref[...]))
    x1 = x_ref[...] + _dot(mix, wo_ref[...])
    x1_ref[...] = x1
    h2 = _rms(x1, gf_ref[...]).astype(BF16)
    h2_ref[...] = h2
    qp_ref[...] = jnp.dot(h2, wq_ref[...], preferred_element_type=F32)


def _merge(x, z, o_gdn, o_nsa, o_mem, wg, wn, wm, wo, norm_ffn, wq, tm):
    n, d = x.shape
    qw = wq.shape[1]
    row = lambda w: pl.BlockSpec((tm, w), lambda i: (i, 0))
    const = lambda a: pl.BlockSpec(a.shape, lambda i: (0, 0))
    return pl.pallas_call(
        _merge_kernel,
        grid=(n // tm,),
        in_specs=[row(d), pl.BlockSpec((tm, N_BRANCH * d), lambda i: (i, Z_MG)),
                  row(o_gdn.shape[1]), row(o_nsa.shape[1]), row(o_mem.shape[1]),
                  const(wg), const(wn), const(wm), const(wo),
                  pl.BlockSpec((1, d), lambda i: (0, 0)), const(wq)],
        out_specs=[row(d), row(d), row(qw)],
        out_shape=[jax.ShapeDtypeStruct((n, d), F32), jax.ShapeDtypeStruct((n, d), BF16),
                   jax.ShapeDtypeStruct((n, qw), F32)],
        compiler_params=_params(("parallel",)),
    )(x, z, o_gdn, o_nsa, o_mem, wg, wn, wm, wo, norm_ffn.reshape(1, d), wq)


PEER_RANKS = PEER_TOPK + 1


def _top_values(s, n):
    vals = []
    for _ in range(n):
        m = jnp.max(s, axis=0, keepdims=True)
        vals.append(m)
        s = jnp.where(s >= m, -jnp.inf, s)
    return vals


def _peer_route_kernel(qp_ref, sk_ref, s2_ref, e2_ref, th_ref, e1_ref):
    half = PEER_DKEY // 2
    nt = (((1,), (1,)), ((), ()))
    for h in range(PEER_HEADS):
        qa = qp_ref[:, h * PEER_DKEY:h * PEER_DKEY + half]
        qb = qp_ref[:, h * PEER_DKEY + half:(h + 1) * PEER_DKEY]
        s1 = lax.dot_general(sk_ref[0], qa, nt, precision=HIGHEST, preferred_element_type=F32)
        s2 = lax.dot_general(sk_ref[1], qb, nt, precision=HIGHEST, preferred_element_type=F32)
        a = _top_values(s1, PEER_RANKS)
        b = _top_values(s2, PEER_RANKS)
        b_rows = jnp.concatenate(b, axis=0)
        cands = [a[i - 1] + b_rows[0:PEER_RANKS // i] for i in range(1, PEER_RANKS + 1)]
        work = cands
        ranked = []
        for _ in range(PEER_RANKS):
            m = work[0][0:1]
            for cnd in work:
                m = jnp.maximum(m, jnp.max(cnd, axis=0, keepdims=True))
            ranked.append(m)
            work = [jnp.where(cnd >= m, -jnp.inf, cnd) for cnd in work]
        tau = 0.5 * (ranked[PEER_TOPK - 1] + ranked[PEER_TOPK])
        top = a[0] + b[0]
        zsum = jnp.zeros_like(tau)
        for cnd in cands:
            zsum = zsum + jnp.sum(jnp.where(cnd >= tau, jnp.exp(cnd - top), 0.0),
                                  axis=0, keepdims=True)
        s2_ref[h] = s2
        e2_ref[h] = jnp.exp(s2 - b[0])
        th_ref[h] = tau - s1
        e1_ref[h] = jnp.exp(s1 - a[0]) / zsum


def _peer_route(qp, subkeys, tt):
    n = qp.shape[0]
    shape = jax.ShapeDtypeStruct((PEER_HEADS, PEER_NKEYS, n), F32)
    spec = pl.BlockSpec((PEER_HEADS, PEER_NKEYS, tt), lambda i: (0, 0, i))
    return pl.pallas_call(
        _peer_route_kernel,
        grid=(n // tt,),
        in_specs=[pl.BlockSpec((tt, qp.shape[1]), lambda i: (i, 0)),
                  pl.BlockSpec(subkeys.shape, lambda i: (0, 0, 0))],
        out_specs=[spec, spec, spec, spec],
        out_shape=[shape, shape, shape, shape],
        compiler_params=_params(("parallel",)),
    )(qp, subkeys)


def _peer_dense_kernel(ht_ref, u_ref, vt_ref, s2_ref, e2_ref, th_ref, e1_ref, x1_ref, gf_ref,
                       y_ref, acc_ref, *, jb):
    j = pl.program_id(1)

    @pl.when(j == 0)
    def _():
        acc_ref[...] = jnp.zeros(acc_ref.shape, F32)

    act = _gelu(jnp.dot(u_ref[...], ht_ref[...], preferred_element_type=F32))
    ws = []
    for jj in range(jb):
        i1 = j * jb + jj
        w = None
        for h in range(PEER_HEADS):
            th = th_ref[h, pl.ds(i1, 1), :]
            e1 = e1_ref[h, pl.ds(i1, 1), :]
            term = jnp.where(s2_ref[h] >= th, e2_ref[h], 0.0) * e1
            w = term if w is None else w + term
        ws.append(w)
    wa = (jnp.concatenate(ws, axis=0) * act).astype(BF16)
    acc_ref[...] += jnp.dot(vt_ref[...], wa, preferred_element_type=F32)

    @pl.when(j == pl.num_programs(1) - 1)
    def _():
        y_ref[...] = _rms(x1_ref[...] + acc_ref[...].T, gf_ref[...])


def _peer_dense(ht, u, vt, s2, e2, th, e1, x1, norm_final, tt, jb):
    d, n = ht.shape
    n_exp = u.shape[0]
    eb = jb * PEER_NKEYS
    route = pl.BlockSpec((PEER_HEADS, PEER_NKEYS, tt), lambda t, j: (0, 0, t))
    return pl.pallas_call(
        functools.partial(_peer_dense_kernel, jb=jb),
        grid=(n // tt, n_exp // eb),
        in_specs=[pl.BlockSpec((d, tt), lambda t, j: (0, t)),
                  pl.BlockSpec((eb, d), lambda t, j: (j, 0)),
                  pl.BlockSpec((d, eb), lambda t, j: (0, j)),
                  route, route, route, route,
                  pl.BlockSpec((tt, d), lambda t, j: (t, 0)),
                  pl.BlockSpec((1, d), lambda t, j: (0, 0))],
        out_specs=pl.BlockSpec((tt, d), lambda t, j: (t, 0)),
        out_shape=jax.ShapeDtypeStruct((n, d), F32),
        scratch_shapes=[pltpu.VMEM((d, tt), F32)],
        compiler_params=_params(("parallel", "arbitrary")),
    )(ht, u, vt, s2, e2, th, e1, x1, norm_final.reshape(1, d))


def _permute_w_in(w_in):
    sizes = (GDN_CONV_CH, GDN_HEADS * GDN_DV, GDN_HEADS, GDN_HEADS, NSA_HEADS * NSA_DH, NKV_W,
             N_BRANCH * NSA_HEADS, MEM_HEADS * MEM_DH, N_BRANCH * D_MODEL)
    qkv, zg, a, b, nq, nkv, ng, mq, mg = jnp.split(w_in, np.cumsum(sizes)[:-1].tolist(), axis=1)
    pad = jnp.zeros((w_in.shape[0], LANES - a.shape[1] - b.shape[1] - ng.shape[1]), w_in.dtype)
    return jnp.concatenate([mg, qkv, zg, nq, mq, nkv, a, b, ng, pad], axis=1).astype(BF16)


def _tokens_tile(n, pref):
    return pref if n % pref == 0 else n


def _layer(x, pos0, kv_mem, nsa_keys, gdn_state, conv_buf, w, peer_tt):
    batch, t, d = x.shape
    n = batch * t
    xf = x.reshape(n, d)
    z = _norm_matmul(xf, w['norm_attn'], w['w_in'], _tokens_tile(n, 256), 640)

    o_gdn, s_new, conv_new = _gdn(z, conv_buf, gdn_state, w['gdn_conv'], w['gdn_a_log'],
                                  w['gdn_dt_bias'], w['gdn_norm'], batch, t)

    prep_tm = _tokens_tile(t, 512)
    kv_dtype = BF16 if prep_tm % 16 == 0 else F32
    qc, qr, slc_rows, win_rows, ks, vs, kw, vw = _nsa_prep(
        z, pos0 + jnp.arange(t), batch, t, prep_tm, kv_dtype)
    cmp_rows = z[:, Z_NKV:Z_NKV + ROW_W]
    o_nsa = nsa_keys(z, qc, qr, cmp_rows, ks, vs, kw, vw)

    o_mem = _mem_attn(z, kv_mem, batch, t, _tokens_tile(t, 512))

    x1, h2, qp = _merge(xf, z, o_gdn, o_nsa, o_mem, w['w_gdn_out'], w['w_nsa_out'],
                        w['w_mem_out'], w['w_o'], w['norm_ffn'], w['peer_wq'],
                        _tokens_tile(n, 256))
    s2, e2, th, e1 = _peer_route(qp, w['peer_subkeys'], 256)
    y = _peer_dense(h2.T, w['peer_u'], w['peer_vt'], s2, e2, th, e1, x1, w['norm_final'],
                    peer_tt, 4)
    row5 = lambda a: a.reshape(batch, t, 2, NSA_KV, NSA_DH)
    return (y.reshape(batch, t, d), row5(cmp_rows), row5(slc_rows), row5(win_rows), s_new,
            conv_new)


def kernel(x_prompt, x_sample, cache_mem_kv, cache_cmp_kv, cache_slc_kv, cache_win_kv, state_gdn, state_conv, page_table, mem_prompt, norm_attn, w_in, gdn_conv, gdn_a_log, gdn_dt_bias, gdn_norm, nsa_cmp_pe, nsa_cmp_w1, nsa_cmp_w2, norm_mem, w_mem_kv, w_gdn_out, w_nsa_out, w_mem_out, w_o, norm_ffn, peer_wq, peer_subkeys, peer_u, peer_v, norm_final):
    depth = w_in.shape[0]
    assert depth == 1
    l = 0
    bp, seq, d = x_prompt.shape
    bs, tdec, _ = x_sample.shape
    n_pages = page_table.shape[1]
    past = n_pages * PAGE_SIZE
    assert past % NSA_CMP == 0 and tdec < NSA_CMP and seq % LANES == 0

    pe_big, w1big, w2big = _compress_weights(nsa_cmp_pe[l], nsa_cmp_w1[l], nsa_cmp_w2[l])
    w = dict(norm_attn=norm_attn[l], w_in=_permute_w_in(w_in[l]), gdn_conv=gdn_conv[l],
             gdn_a_log=gdn_a_log[l], gdn_dt_bias=gdn_dt_bias[l], gdn_norm=gdn_norm[l],
             w_gdn_out=w_gdn_out[l].astype(BF16), w_nsa_out=w_nsa_out[l].astype(BF16),
             w_mem_out=w_mem_out[l].astype(BF16), w_o=w_o[l].astype(BF16), norm_ffn=norm_ffn[l],
             peer_wq=peer_wq[l].astype(BF16), peer_subkeys=peer_subkeys[l],
             peer_u=peer_u[l].astype(BF16), peer_vt=peer_v[l].astype(BF16).T,
             norm_final=norm_final)

    mem_n = mem_prompt.shape[0] * mem_prompt.shape[1]
    kvm = _norm_matmul(mem_prompt.reshape(mem_n, d), norm_mem[l], w_mem_kv[l].astype(BF16),
                       _tokens_tile(mem_n, 256), 512).reshape(bp, mem_prompt.shape[1], -1)

    def prompt_keys(z, qc, qr, cmp_rows, ks, vs, kw, vw):
        n_blk = bp * seq // NSA_CMP
        cmp_out = _compress(cmp_rows.reshape(n_blk, NSA_CMP * ROW_W), pe_big, w1big, w2big,
                            _tokens_tile(n_blk, 128))
        kc, vc = _split_compressed(cmp_out, bp)
        return _nsa_attn(qc, qr, kc, vc, ks, vs, kw, vw, (), z, bp, seq, 128, 0, 0, 128)

    yp, cmp_p, slc_p, win_p, gdn_p, conv_p = _layer(
        x_prompt, 0, kvm, prompt_keys,
        jnp.zeros((bp, GDN_HEADS, GDN_DK, GDN_DV), F32),
        jnp.zeros((bp, GDN_CONV - 1, GDN_CONV_CH), F32), w, 512)
    win_len_p = min(NSA_WIN, seq)
    win_p = win_p[:, seq - win_len_p:]

    cache_cmp = cache_cmp_kv[l].reshape(-1, PAGE_SIZE, ROW_W)
    cache_slc = cache_slc_kv[l].reshape(-1, PAGE_SIZE, ROW_W)
    cache_win = cache_win_kv[l].reshape(bs, -1, ROW_W)
    wb = cache_win.shape[1]
    assert wb == NSA_WIN
    kv_w = NSA_KV * NSA_DH

    def pad_rows(a, rows):
        return jnp.pad(a, ((0, 0), (0, 0), (0, rows - a.shape[2]), (0, 0))).astype(BF16)

    def split_rows(rows):
        r = rows.reshape(bs, rows.shape[1], 2, NSA_KV, NSA_DH)
        return jnp.transpose(r[:, :, 0], (0, 2, 1, 3)), jnp.transpose(r[:, :, 1], (0, 2, 1, 3))

    def sample_keys(z, qc, qr, cmp_rows, ks_new, vs_new, kw_new, vw_new):
        past_cmp = _gather_rows(cache_cmp, page_table)
        n_blk = bs * past // NSA_CMP
        cmp_out = _compress(past_cmp.reshape(n_blk, NSA_CMP * ROW_W), pe_big, w1big, w2big,
                            _tokens_tile(n_blk, 128))
        kc, vc = _split_compressed(cmp_out, bs)
        ks, vs = _gather_split(cache_slc, page_table)
        kwc, vwc = split_rows(cache_win)
        win_rows = wb + LANES
        kw = pad_rows(jnp.concatenate([kwc, kw_new], axis=2), win_rows)
        vw = pad_rows(jnp.concatenate([vwc, vw_new], axis=2), win_rows)
        tails = (pad_rows(ks_new, LANES), pad_rows(vs_new, LANES))
        return _nsa_attn(qc, qr, kc, vc, ks, vs, kw, vw, tails, z, bs, tdec, tdec, past, wb, 128)

    ys, cmp_s, slc_s, win_new, gdn_s, conv_s = _layer(
        x_sample, past, cache_mem_kv[l].reshape(bs, cache_mem_kv.shape[2], -1), sample_keys,
        state_gdn[l], state_conv[l], w, 256)
    win_all = jnp.concatenate([cache_win_kv[l], win_new], axis=1)
    win_s = win_all[:, win_all.shape[1] - min(NSA_WIN, past + tdec):]

    stack = lambda a: a[None]
    return (yp, ys, stack(kvm.reshape(bp, mem_prompt.shape[1], 2, MEM_HEADS, MEM_DH)),
            stack(cmp_p), stack(slc_p), stack(win_p), stack(gdn_p), stack(conv_p),
            stack(cmp_s), stack(slc_s), stack(win_s), stack(gdn_s), stack(conv_s))
```

```python
import functools
import math

import jax
import jax.numpy as jnp
import numpy as np
from jax import lax
from jax.experimental import pallas as pl
from jax.experimental.pallas import tpu as pltpu

F32 = jnp.float32
BF16 = jnp.bfloat16
HIGHEST = lax.Precision.HIGHEST

D_MODEL = 1024
PAGE_SIZE = 128
GDN_HEADS = 4
GDN_DK = 128
GDN_DV = 128
GDN_CONV = 4
GDN_CHUNK = 64
GDN_CONV_CH = GDN_HEADS * (2 * GDN_DK + GDN_DV)
NSA_HEADS = 8
NSA_KV = 2
NSA_DH = 64
NSA_CMP = 32
NSA_SEL = 64
NSA_TOPN = 16
NSA_WIN = 512
NSA_FORCE = 1e9
MEM_HEADS = 4
MEM_DH = 128
PEER_HEADS = 8
PEER_NKEYS = 128
PEER_DKEY = 256
PEER_TOPK = 16
N_BRANCH = 3
ROPE_THETA = 10000.0
EPS = 1e-6

LANES = 128
SUBLANES = 8
VMEM_LIMIT = 56 * 1024 * 1024

Z_MG = 0
Z_QKV = Z_MG + N_BRANCH * D_MODEL
Z_ZG = Z_QKV + GDN_CONV_CH
Z_NQ = Z_ZG + GDN_HEADS * GDN_DV
Z_MQ = Z_NQ + NSA_HEADS * NSA_DH
Z_NKV = Z_MQ + MEM_HEADS * MEM_DH
Z_SMALL = Z_NKV + 3 * 2 * NSA_KV * NSA_DH
Z_WIDTH = Z_SMALL + LANES
SM_A = 0
SM_B = GDN_HEADS
SM_NG = 2 * GDN_HEADS
NKV_W = 3 * 2 * NSA_KV * NSA_DH
ROW_W = 2 * NSA_KV * NSA_DH


def _params(sem, vmem=VMEM_LIMIT):
    return pltpu.CompilerParams(dimension_semantics=sem, vmem_limit_bytes=vmem)


def _dot(a, b):
    return jnp.dot(a.astype(BF16), b.astype(BF16), preferred_element_type=F32)


def _dot_nt(a, b):
    return lax.dot_general(a.astype(BF16), b.astype(BF16), (((1,), (1,)), ((), ())),
                           preferred_element_type=F32)


def _dot_hi(a, b):
    return jnp.dot(a, b, precision=HIGHEST, preferred_element_type=F32)


def _rms(x, g):
    return x * lax.rsqrt(jnp.mean(x * x, axis=-1, keepdims=True) + EPS) * g


def _gelu(x):
    a = -2.0 * math.sqrt(2.0 / math.pi)
    return x / (1.0 + jnp.exp(x * (a + (a * 0.044715) * (x * x))))


def _norm_matmul_kernel(x_ref, g_ref, w_ref, o_ref, *, col_chunk):
    yb = _rms(x_ref[...], g_ref[...]).astype(BF16)
    for c0 in range(0, o_ref.shape[1], col_chunk):
        o_ref[:, c0:c0 + col_chunk] = jnp.dot(yb, w_ref[:, c0:c0 + col_chunk],
                                              preferred_element_type=F32)


def _norm_matmul(x, g, w, tm, col_chunk):
    n, d = x.shape
    wc = w.shape[1]
    return pl.pallas_call(
        functools.partial(_norm_matmul_kernel, col_chunk=col_chunk),
        grid=(n // tm,),
        in_specs=[pl.BlockSpec((tm, d), lambda i: (i, 0)),
                  pl.BlockSpec((1, d), lambda i: (0, 0)),
                  pl.BlockSpec((d, wc), lambda i: (0, 0), pipeline_mode=pl.Buffered(1))],
        out_specs=pl.BlockSpec((tm, wc), lambda i: (i, 0)),
        out_shape=jax.ShapeDtypeStruct((n, wc), F32),
        compiler_params=_params(("parallel",)),
    )(x, g.reshape(1, d), w)


def _tri_inverse(lmat, c):
    row = lax.broadcasted_iota(jnp.int32, (c, c), 0)
    col = lax.broadcasted_iota(jnp.int32, (c, c), 1)
    x = jnp.where(row == col, 1.0, 0.0) - lmat
    p = _dot_hi(lmat, lmat)
    n = 2
    while n < c:
        x = x + _dot_hi(x, p)
        n *= 2
        if n < c:
            p = _dot_hi(p, p)
    return x


def _gdn_kernel(qkv_ref, zg_ref, sm_ref, buf_ref, s0_ref, cw_ref, alog_ref, dt_ref, gn_ref,
                o_ref, snew_ref, cnew_ref, ext_ref, s_ref, *, tb, n_chunks):
    c = GDN_CHUNK
    ci = pl.program_id(1)

    @pl.when(ci == 0)
    def _():
        ext_ref[0:SUBLANES, :] = buf_ref[0]
        s_ref[...] = s0_ref[0]

    if tb < c:
        ext_ref[SUBLANES + tb:, :] = jnp.zeros((c - tb, GDN_CONV_CH), F32)
    ext_ref[SUBLANES:SUBLANES + tb, :] = qkv_ref[...]
    cw = cw_ref[...]
    conv = cw[0:1] * ext_ref[SUBLANES - 3:SUBLANES - 3 + c, :]
    for j in range(1, GDN_CONV):
        conv = conv + cw[j:j + 1] * ext_ref[SUBLANES - 3 + j:SUBLANES - 3 + j + c, :]
    u = conv * jax.nn.sigmoid(conv)
    last_rows = ext_ref[tb:tb + SUBLANES, :]
    cnew_ref[0] = last_rows
    ext_ref[0:SUBLANES, :] = last_rows

    sm = sm_ref[...]
    if tb < c:
        sm = jnp.concatenate([sm, jnp.zeros((c - tb, LANES), F32)], axis=0)
    za = sm + dt_ref[...]
    softplus = jnp.maximum(za, 0.0) + jnp.log1p(jnp.exp(-jnp.abs(za)))
    g_all = -jnp.exp(alog_ref[...]) * softplus
    beta_all = jax.nn.sigmoid(sm)
    if tb < c:
        valid = lax.broadcasted_iota(jnp.int32, (c, 1), 0) < tb
        u = jnp.where(valid, u, 0.0)
        g_all = jnp.where(valid, g_all, 0.0)
        beta_all = jnp.where(valid, beta_all, 0.0)

    row = lax.broadcasted_iota(jnp.int32, (c, c), 0)
    col = lax.broadcasted_iota(jnp.int32, (c, c), 1)
    tril = row >= col
    eye = row == col
    gc_all = _dot_hi(jnp.where(tril, 1.0, 0.0), g_all)

    hk = GDN_HEADS * GDN_DK
    for h in range(GDN_HEADS):
        qh = u[:, h * GDN_DK:(h + 1) * GDN_DK]
        kh = u[:, hk + h * GDN_DK:hk + (h + 1) * GDN_DK]
        vh = u[:, 2 * hk + h * GDN_DV:2 * hk + (h + 1) * GDN_DV]
        qn = qh * lax.rsqrt(jnp.sum(qh * qh, axis=-1, keepdims=True) + EPS) * (GDN_DK ** -0.5)
        kn = kh * lax.rsqrt(jnp.sum(kh * kh, axis=-1, keepdims=True) + EPS)
        beta = beta_all[:, SM_B + h:SM_B + h + 1]
        gc = gc_all[:, SM_A + h:SM_A + h + 1]
        gl = gc_all[c - 1:c, SM_A + h:SM_A + h + 1]
        gc_row = jnp.sum(jnp.where(eye, gc, 0.0), axis=0, keepdims=True)
        decay = jnp.exp(jnp.where(tril, gc - gc_row, -jnp.inf))
        kb = kn * beta
        lmat = jnp.where(row > col, _dot_nt(kb, kn) * decay, 0.0)
        tinv = _tri_inverse(lmat, c)
        egc = jnp.exp(gc)
        uu = _dot(tinv, vh * beta)
        ww = _dot(tinv, kb * egc)
        aqk = _dot_nt(qn, kn) * decay
        s = s_ref[h]
        v_new = uu - _dot(ww, s)
        o = _dot(qn * egc, s) + _dot(aqk, v_new)
        kd = kn * jnp.exp(gl - gc)
        s_ref[h] = s * jnp.exp(gl) + _dot(kd.T, v_new)
        zh = zg_ref[:, h * GDN_DV:(h + 1) * GDN_DV]
        on = _rms(o[0:tb], gn_ref[...]) * (zh * jax.nn.sigmoid(zh))
        o_ref[:, h * GDN_DV:(h + 1) * GDN_DV] = on

    @pl.when(ci == n_chunks - 1)
    def _():
        snew_ref[0] = s_ref[...]


def _gdn(z, conv_buf, s0, conv_w, a_log, dt_bias, gnorm, batch, t):
    c = GDN_CHUNK
    tb = min(t, c)
    n_chunks = t // tb
    assert tb % SUBLANES == 0 and n_chunks * tb == t and (tb == c or n_chunks == 1)
    buf8 = jnp.pad(conv_buf, ((0, 0), (SUBLANES - (GDN_CONV - 1), 0), (0, 0)))
    alog_row = jnp.zeros((1, LANES), F32).at[0, SM_A:SM_A + GDN_HEADS].set(a_log)
    dt_row = jnp.zeros((1, LANES), F32).at[0, SM_A:SM_A + GDN_HEADS].set(dt_bias)
    rowblk = lambda b, ci: b * n_chunks + ci
    o, s_new, c_new = pl.pallas_call(
        functools.partial(_gdn_kernel, tb=tb, n_chunks=n_chunks),
        grid=(batch, n_chunks),
        in_specs=[
            pl.BlockSpec((tb, GDN_CONV_CH), lambda b, ci: (rowblk(b, ci), Z_QKV // GDN_CONV_CH)),
            pl.BlockSpec((tb, GDN_HEADS * GDN_DV), lambda b, ci: (rowblk(b, ci), Z_ZG // (GDN_HEADS * GDN_DV))),
            pl.BlockSpec((tb, LANES), lambda b, ci: (rowblk(b, ci), Z_SMALL // LANES)),
            pl.BlockSpec((1, SUBLANES, GDN_CONV_CH), lambda b, ci: (b, 0, 0)),
            pl.BlockSpec((1, GDN_HEADS, GDN_DK, GDN_DV), lambda b, ci: (b, 0, 0, 0)),
            pl.BlockSpec((GDN_CONV, GDN_CONV_CH), lambda b, ci: (0, 0)),
            pl.BlockSpec((1, LANES), lambda b, ci: (0, 0)),
            pl.BlockSpec((1, LANES), lambda b, ci: (0, 0)),
            pl.BlockSpec((1, GDN_DV), lambda b, ci: (0, 0)),
        ],
        out_specs=[
            pl.BlockSpec((tb, GDN_HEADS * GDN_DV), lambda b, ci: (rowblk(b, ci), 0)),
            pl.BlockSpec((1, GDN_HEADS, GDN_DK, GDN_DV), lambda b, ci: (b, 0, 0, 0)),
            pl.BlockSpec((1, SUBLANES, GDN_CONV_CH), lambda b, ci: (b, 0, 0)),
        ],
        out_shape=[
            jax.ShapeDtypeStruct((batch * t, GDN_HEADS * GDN_DV), F32),
            jax.ShapeDtypeStruct((batch, GDN_HEADS, GDN_DK, GDN_DV), F32),
            jax.ShapeDtypeStruct((batch, SUBLANES, GDN_CONV_CH), F32),
        ],
        scratch_shapes=[pltpu.VMEM((SUBLANES + c, GDN_CONV_CH), F32),
                        pltpu.VMEM((GDN_HEADS, GDN_DK, GDN_DV), F32)],
        compiler_params=_params(("parallel", "arbitrary")),
    )(z, z, z, buf8, s0, conv_w, alog_row, dt_row, gnorm.reshape(1, GDN_DV))
    return o, s_new, c_new[:, SUBLANES - (GDN_CONV - 1):]


def _rope_tables(pos):
    half = NSA_DH // 2
    inv = jnp.power(ROPE_THETA, -jnp.arange(half, dtype=F32) / half)
    ang = pos.astype(F32)[:, None] * inv[None, :]
    cos, sin = jnp.cos(ang), jnp.sin(ang)
    cos_t = jnp.concatenate([cos, cos, cos, cos], axis=-1)
    sin_t = jnp.concatenate([-sin, sin, -sin, sin], axis=-1)
    return cos_t, sin_t


def _nsa_prep_kernel(nq_ref, nkv_ref, cos_ref, sin_ref,
                     qc_ref, qr_ref, slc_ref, win_ref, ks_ref, vs_ref, kw_ref, vw_ref):
    cos = cos_ref[...]
    sin = sin_ref[...]
    lane = lax.broadcasted_iota(jnp.int32, cos.shape, 1)
    first_half = (lane % NSA_DH) < (NSA_DH // 2)

    def rope(x):
        swapped = jnp.where(first_half, pltpu.roll(x, LANES - NSA_DH // 2, 1),
                            pltpu.roll(x, NSA_DH // 2, 1))
        return x * cos + swapped * sin

    scale = NSA_DH ** -0.5
    for j in range(NSA_HEADS * NSA_DH // LANES):
        x = nq_ref[:, j * LANES:(j + 1) * LANES]
        qc_ref[:, j * LANES:(j + 1) * LANES] = x * scale
        qr_ref[:, j * LANES:(j + 1) * LANES] = rope(x) * scale

    kv_w = NSA_KV * NSA_DH
    for br, (row_ref, k_ref, v_ref) in enumerate(((slc_ref, ks_ref, vs_ref),
                                                   (win_ref, kw_ref, vw_ref))):
        base = (br + 1) * ROW_W
        kr = rope(nkv_ref[:, base:base + kv_w])
        v = nkv_ref[:, base + kv_w:base + 2 * kv_w]
        row_ref[:, 0:kv_w] = kr
        row_ref[:, kv_w:2 * kv_w] = v
        for g in range(NSA_KV):
            k_ref[0, g] = kr[:, g * NSA_DH:(g + 1) * NSA_DH].astype(k_ref.dtype)
            v_ref[0, g] = v[:, g * NSA_DH:(g + 1) * NSA_DH].astype(v_ref.dtype)


def _nsa_prep(z, pos, batch, t, tm, kv_dtype):
    cos_t, sin_t = _rope_tables(pos)
    nt = t // tm
    rowblk = lambda b, j: b * nt + j
    kv_shape = jax.ShapeDtypeStruct((batch, NSA_KV, t, NSA_DH), kv_dtype)
    kv_spec = pl.BlockSpec((1, NSA_KV, tm, NSA_DH), lambda b, j: (b, 0, j, 0))
    qw = NSA_HEADS * NSA_DH
    return pl.pallas_call(
        _nsa_prep_kernel,
        grid=(batch, nt),
        in_specs=[pl.BlockSpec((tm, qw), lambda b, j: (rowblk(b, j), Z_NQ // qw)),
                  pl.BlockSpec((tm, NKV_W), lambda b, j: (rowblk(b, j), Z_NKV // NKV_W)),
                  pl.BlockSpec((tm, LANES), lambda b, j: (j, 0)),
                  pl.BlockSpec((tm, LANES), lambda b, j: (j, 0))],
        out_specs=[pl.BlockSpec((tm, qw), lambda b, j: (rowblk(b, j), 0)),
                   pl.BlockSpec((tm, qw), lambda b, j: (rowblk(b, j), 0)),
                   pl.BlockSpec((tm, ROW_W), lambda b, j: (rowblk(b, j), 0)),
                   pl.BlockSpec((tm, ROW_W), lambda b, j: (rowblk(b, j), 0)),
                   kv_spec, kv_spec, kv_spec, kv_spec],
        out_shape=[jax.ShapeDtypeStruct((batch * t, qw), F32),
                   jax.ShapeDtypeStruct((batch * t, qw), F32),
                   jax.ShapeDtypeStruct((batch * t, ROW_W), F32),
                   jax.ShapeDtypeStruct((batch * t, ROW_W), F32),
                   kv_shape, kv_shape, kv_shape, kv_shape],
        compiler_params=_params(("parallel", "parallel")),
    )(z, z, cos_t, sin_t)


def _compress_weights(pe, w1, w2):
    eye = jnp.eye(2, dtype=F32)
    w1r = w1.reshape(2, NSA_CMP, NSA_DH, NSA_DH)
    w1big = jnp.einsum('srde,st,gh->rsgdthe', w1r, eye, eye).reshape(NSA_CMP * ROW_W, ROW_W)
    w2big = jnp.einsum('sed,st,gh->sgethd', w2, eye, eye).reshape(ROW_W, ROW_W)
    pe_big = jnp.broadcast_to(jnp.transpose(pe, (1, 0, 2))[:, :, None, :],
                              (NSA_CMP, 2, NSA_KV, NSA_DH)).reshape(1, NSA_CMP * ROW_W)
    return pe_big, w1big.astype(BF16), w2big.astype(BF16)


def _compress_kernel(x_ref, pe_ref, w1_ref, w2_ref, o_ref):
    x = (x_ref[...] + pe_ref[...]).astype(BF16)
    hid = _gelu(jnp.dot(x, w1_ref[...], preferred_element_type=F32))
    o_ref[...] = jnp.dot(hid.astype(BF16), w2_ref[...], preferred_element_type=F32)


def _compress(rows, pe_big, w1big, w2big, tm):
    n, kdim = rows.shape
    return pl.pallas_call(
        _compress_kernel,
        grid=(n // tm,),
        in_specs=[pl.BlockSpec((tm, kdim), lambda i: (i, 0)),
                  pl.BlockSpec((1, kdim), lambda i: (0, 0)),
                  pl.BlockSpec((kdim, ROW_W), lambda i: (0, 0)),
                  pl.BlockSpec((ROW_W, ROW_W), lambda i: (0, 0))],
        out_specs=pl.BlockSpec((tm, ROW_W), lambda i: (i, 0)),
        out_shape=jax.ShapeDtypeStruct((n, ROW_W), F32),
        compiler_params=_params(("parallel",)),
    )(rows, pe_big, w1big, w2big)


def _split_compressed(cmp_out, batch):
    nb = cmp_out.shape[0] // batch
    x = cmp_out.reshape(batch, nb // 2, 2, 2, NSA_KV, NSA_DH)
    x = jnp.transpose(x, (3, 0, 4, 2, 1, 5)).reshape(2, batch, NSA_KV, nb, NSA_DH)
    return x[0].astype(BF16), x[1].astype(BF16)


CMP_PAGES_PER_STEP = 32
SLC_PAGES_PER_STEP = 8
BLOCKS_PER_PAGE = PAGE_SIZE // NSA_CMP


def _page_specs(block, n_pages, per_step):
    def spec(j):
        return pl.BlockSpec(block, lambda b, s, pt: (pt[b * n_pages + s * per_step + j], 0, 0))
    return [spec(j) for j in range(per_step)]


def _paged_compress_kernel(pt_ref, *refs):
    pe_ref, w1_ref, w2_ref, o_ref, x_ref = refs[CMP_PAGES_PER_STEP:]
    for j, p_ref in enumerate(refs[:CMP_PAGES_PER_STEP]):
        x_ref[j * BLOCKS_PER_PAGE:(j + 1) * BLOCKS_PER_PAGE, :] = p_ref[0]
    _compress_kernel(x_ref, pe_ref, w1_ref, w2_ref, o_ref)


def _paged_compress(cache_flat, page_table, pe_big, w1big, w2big):
    batch, n_pages = page_table.shape
    kdim = cache_flat.shape[2]
    rows = CMP_PAGES_PER_STEP * BLOCKS_PER_PAGE
    steps = n_pages // CMP_PAGES_PER_STEP
    const = lambda a: pl.BlockSpec(a.shape, lambda b, s, pt: (0, 0))
    return pl.pallas_call(
        _paged_compress_kernel,
        grid_spec=pltpu.PrefetchScalarGridSpec(
            num_scalar_prefetch=1, grid=(batch, steps),
            in_specs=_page_specs((1, BLOCKS_PER_PAGE, kdim), n_pages, CMP_PAGES_PER_STEP)
            + [const(pe_big), const(w1big), const(w2big)],
            out_specs=pl.BlockSpec((rows, ROW_W), lambda b, s, pt: (b * steps + s, 0)),
            scratch_shapes=[pltpu.VMEM((rows, kdim), F32)]),
        out_shape=jax.ShapeDtypeStruct((batch * n_pages * BLOCKS_PER_PAGE, ROW_W), F32),
        compiler_params=_params(("parallel", "parallel")),
    )(page_table.reshape(-1), *([cache_flat] * CMP_PAGES_PER_STEP), pe_big, w1big, w2big)


def _topk_mask(score, k):
    n = score.shape[-1]
    lane = lax.broadcasted_iota(jnp.int32, score.shape, 1).astype(F32)
    sel = jnp.zeros(score.shape, F32)
    for _ in range(k):
        m = jnp.max(score, axis=-1, keepdims=True)
        idx = jnp.min(jnp.where(score == m, lane, float(n)), axis=-1, keepdims=True)
        pick = lane == idx
        sel = jnp.where(pick, 1.0, sel)
        score = jnp.where(pick, -jnp.inf, score)
    return sel


def _softmax_step(carry, s, v):
    m, l, acc = carry
    m_new = jnp.maximum(m, jnp.max(s, axis=-1, keepdims=True))
    m_safe = jnp.where(m_new == -jnp.inf, 0.0, m_new)
    p = jnp.exp(s - m_safe)
    alpha = jnp.exp(m - m_safe)
    l = alpha * l + jnp.sum(p, axis=-1, keepdims=True)
    acc = alpha * acc + _dot(p, v)
    return m_new, l, acc


def _nsa_attn_kernel(*refs, tq, pos0, wrel0, n_keys, tk, has_tail):
    if has_tail:
        (qc_ref, qr_ref, kc_ref, vc_ref, ks_ref, vs_ref, kw_ref, vw_ref, kt_ref, vt_ref,
         sm_ref, o_ref) = refs
    else:
        (qc_ref, qr_ref, kc_ref, vc_ref, ks_ref, vs_ref, kw_ref, vw_ref, sm_ref, o_ref) = refs
    hp = NSA_HEADS // NSA_KV
    r = hp * tq
    g = pl.program_id(1)
    qi = pl.program_id(2)
    ncp = kc_ref.shape[2]
    nch = ncp // 2
    nsl = -(-(n_keys // NSA_SEL + (1 if has_tail else 0)) // LANES) * LANES

    def stack_heads(ref):
        x = ref[...]
        return jnp.concatenate([x[:, h * NSA_DH:(h + 1) * NSA_DH] for h in range(hp)],
                               axis=0).astype(BF16)

    qc4 = stack_heads(qc_ref)
    qr4 = stack_heads(qr_ref)
    t_q = pos0 + qi * tq + lax.broadcasted_iota(jnp.int32, (tq, 1), 0)
    t_row = jnp.concatenate([t_q] * hp, axis=0)

    s = _dot_nt(qc4, kc_ref[0, 0])
    ccol = lax.broadcasted_iota(jnp.int32, (r, ncp), 1)
    cblk = 2 * (ccol % nch) + ccol // nch
    s = jnp.where((cblk + 1) * NSA_CMP <= t_row + 1, s, -jnp.inf)
    m = jnp.max(s, axis=-1, keepdims=True)
    e = jnp.exp(s - jnp.where(m == -jnp.inf, 0.0, m))
    p_cmp = e / jnp.maximum(jnp.sum(e, axis=-1, keepdims=True), 1e-30)
    o_cmp = _dot(p_cmp, vc_ref[0, 0])

    imp = p_cmp[0:tq]
    for h in range(1, hp):
        imp = imp + p_cmp[h * tq:(h + 1) * tq]
    imp = imp[:, :nch] + imp[:, nch:]
    if nsl > nch:
        imp = jnp.concatenate([imp, jnp.zeros((tq, nsl - nch), F32)], axis=1)
    blk = lax.broadcasted_iota(jnp.int32, (tq, nsl), 1)
    cur = t_q // NSA_SEL
    forced = (blk == 0) | (blk == cur) | (blk == cur - 1)
    score = jnp.where(blk > cur, -jnp.inf, jnp.where(forced, NSA_FORCE, imp))
    sel = _topk_mask(score, NSA_TOPN).astype(BF16)

    init = (jnp.full((r, 1), -jnp.inf, F32), jnp.zeros((r, 1), F32), jnp.zeros((r, NSA_DH), F32))

    def slc_scores(k, kpos0, width):
        kpos = kpos0 + lax.broadcasted_iota(jnp.int32, (1, width), 1)
        expand = (lax.broadcasted_iota(jnp.int32, (nsl, width), 0) == kpos // NSA_SEL)
        picked = jnp.dot(sel, expand.astype(BF16), preferred_element_type=F32)
        picked = jnp.concatenate([picked] * hp, axis=0)
        return jnp.where((picked > 0.5) & (kpos <= t_row), _dot_nt(qr4, k), -jnp.inf)

    def slc_body(kt, carry):
        start = pl.multiple_of(kt * tk, tk)
        k = ks_ref[0, 0, pl.ds(start, tk), :]
        v = vs_ref[0, 0, pl.ds(start, tk), :]
        return _softmax_step(carry, slc_scores(k, kt * tk, tk), v)

    t_last = pos0 + qi * tq + tq - 1
    n_main = jnp.minimum(n_keys // tk, t_last // tk + 1)
    carry = lax.fori_loop(0, n_main, slc_body, init)
    if has_tail:
        carry = _softmax_step(carry, slc_scores(kt_ref[0, 0], n_keys, kt_ref.shape[2]), vt_ref[0, 0])
    o_slc = carry[2] / jnp.maximum(carry[1], 1e-30)

    wbase = pos0 - wrel0

    def win_body(kt, carry):
        start = pl.multiple_of(kt * tk, tk)
        k = kw_ref[0, 0, pl.ds(start, tk), :]
        v = vw_ref[0, 0, pl.ds(start, tk), :]
        kpos = wbase + kt * tk + lax.broadcasted_iota(jnp.int32, (1, tk), 1)
        rel = t_row - kpos
        ok = (rel >= 0) & (rel < NSA_WIN) & (kpos >= 0)
        return _softmax_step(carry, jnp.where(ok, _dot_nt(qr4, k), -jnp.inf), v)

    w_first = jnp.maximum(pos0 + qi * tq - (NSA_WIN - 1) - wbase, 0) // tk
    w_last = jnp.minimum((t_last - wbase) // tk, kw_ref.shape[2] // tk - 1)
    carry = lax.fori_loop(w_first, w_last + 1, win_body, init)
    o_win = carry[2] / jnp.maximum(carry[1], 1e-30)

    sig = jax.nn.sigmoid(sm_ref[...])
    gw = N_BRANCH * hp
    gates = jnp.where(g == 0, sig[:, SM_NG:SM_NG + gw], sig[:, SM_NG + gw:SM_NG + 2 * gw])
    outs = []
    for h in range(hp):
        rows = slice(h * tq, (h + 1) * tq)
        outs.append(gates[:, 3 * h:3 * h + 1] * o_cmp[rows]
                    + gates[:, 3 * h + 1:3 * h + 2] * o_slc[rows]
                    + gates[:, 3 * h + 2:3 * h + 3] * o_win[rows])
    o_ref[...] = jnp.concatenate(outs, axis=1)


def _nsa_attn(qc, qr, kc, vc, ks, vs, kw, vw, tails, z, batch, t, tq, pos0, wrel0, tk):
    nq = t // tq
    n_keys = ks.shape[2]
    gw = NSA_HEADS * NSA_DH // NSA_KV
    rowblk = lambda b, g, i: b * nq + i
    full = lambda a: pl.BlockSpec((1, 1) + a.shape[2:], lambda b, g, i: (b, g, 0, 0))
    q_spec = pl.BlockSpec((tq, gw), lambda b, g, i: (rowblk(b, g, i), g))
    operands = [qc, qr, kc, vc, ks, vs, kw, vw] + list(tails)
    in_specs = [q_spec, q_spec] + [full(a) for a in operands[2:]]
    in_specs.append(pl.BlockSpec((tq, LANES), lambda b, g, i: (rowblk(b, g, i), Z_SMALL // LANES)))
    return pl.pallas_call(
        functools.partial(_nsa_attn_kernel, tq=tq, pos0=pos0, wrel0=wrel0, n_keys=n_keys, tk=tk,
                          has_tail=bool(tails)),
        grid=(batch, NSA_KV, nq),
        in_specs=in_specs,
        out_specs=pl.BlockSpec((tq, gw), lambda b, g, i: (rowblk(b, g, i), g)),
        out_shape=jax.ShapeDtypeStruct((batch * t, NSA_HEADS * NSA_DH), F32),
        compiler_params=_params(("parallel", "parallel", "arbitrary")),
    )(*operands, z)


MASKED = -1e30


def _topk_mask_rows(score, k):
    n = score.shape[0]
    row = lax.broadcasted_iota(jnp.int32, score.shape, 0).astype(F32)
    sel = jnp.zeros(score.shape, F32)
    for _ in range(k):
        m = jnp.max(score, axis=0, keepdims=True)
        idx = jnp.min(jnp.where(score == m, row, float(n)), axis=0, keepdims=True)
        pick = row == idx
        sel = jnp.where(pick, 1.0, sel)
        score = jnp.where(pick, -jnp.inf, score)
    return sel


def _softmax_step_cols(carry, s, vt):
    m, l, acc = carry
    m_new = jnp.maximum(m, jnp.max(s, axis=0, keepdims=True))
    p = jnp.exp(s - m_new)
    alpha = jnp.exp(m - m_new)
    l = alpha * l + jnp.sum(p, axis=0, keepdims=True)
    acc = alpha * acc + _dot(vt, p)
    return m_new, l, acc


def _nsa_prompt_kernel(qc_ref, qr_ref, kc_ref, vct_ref, ks_ref, vst_ref, kw_ref, vwt_ref, sm_ref,
                       o_ref, bias_ref, *, tq, tk, tkw):
    hp = NSA_HEADS // NSA_KV
    r = hp * tq
    g = pl.program_id(1)
    qi = pl.program_id(2)
    q0 = qi * tq
    ncp = kc_ref.shape[2]
    nch = ncp // 2
    nsl = bias_ref.shape[0]
    per_tile = tk // NSA_SEL

    def heads_on_lanes(ref):
        xt = ref[...].T
        return jnp.concatenate([xt[h * NSA_DH:(h + 1) * NSA_DH] for h in range(hp)],
                               axis=1).astype(BF16)

    qct = heads_on_lanes(qc_ref)
    qrt = heads_on_lanes(qr_ref)
    t_q = q0 + lax.broadcasted_iota(jnp.int32, (1, tq), 1)
    t_lane = jnp.concatenate([t_q] * hp, axis=1)

    s = _dot(kc_ref[0, 0], qct)
    crow = lax.broadcasted_iota(jnp.int32, (ncp, 1), 0)
    cblk = 2 * (crow % nch) + crow // nch
    s = jnp.where((cblk + 1) * NSA_CMP <= t_lane + 1, s, -jnp.inf)
    m = jnp.max(s, axis=0, keepdims=True)
    e = jnp.exp(s - jnp.where(m == -jnp.inf, 0.0, m))
    p_cmp = e / jnp.maximum(jnp.sum(e, axis=0, keepdims=True), 1e-30)
    o_cmp = _dot(vct_ref[0, 0], p_cmp)

    imp = p_cmp[:, 0:tq]
    for h in range(1, hp):
        imp = imp + p_cmp[:, h * tq:(h + 1) * tq]
    imp = imp[:nch] + imp[nch:]
    blk = lax.broadcasted_iota(jnp.int32, (nsl, tq), 0)
    cur = t_q // NSA_SEL
    forced = (blk == 0) | (blk == cur) | (blk == cur - 1)
    score = jnp.where(blk > cur, -jnp.inf, jnp.where(forced, NSA_FORCE, imp))
    bias = (_topk_mask_rows(score, NSA_TOPN) - 1.0) * (-MASKED)
    bias_ref[...] = jnp.concatenate([bias] * hp, axis=1)

    init = (jnp.full((1, r), MASKED, F32), jnp.zeros((1, r), F32), jnp.zeros((NSA_DH, r), F32))

    def slc_scores(kt):
        start = pl.multiple_of(kt * tk, tk)
        sc = _dot(ks_ref[0, 0, pl.ds(start, tk), :], qrt)
        brow = bias_ref[pl.ds(pl.multiple_of(kt * per_tile, per_tile), per_tile), :]
        sc = jnp.concatenate([sc[j * NSA_SEL:(j + 1) * NSA_SEL] + brow[j:j + 1]
                              for j in range(per_tile)], axis=0)
        return sc, vst_ref[0, 0, :, pl.ds(start, tk)]

    def slc_body(kt, carry):
        sc, vt = slc_scores(kt)
        return _softmax_step_cols(carry, sc, vt)

    kd = q0 // tk
    carry = lax.fori_loop(0, kd, slc_body, init)
    sc, vt = slc_scores(kd)
    kpos = kd * tk + lax.broadcasted_iota(jnp.int32, (tk, 1), 0)
    carry = _softmax_step_cols(carry, jnp.where(kpos <= t_lane, sc, MASKED), vt)
    o_slc = carry[2] / carry[1]

    def win_body(kt, carry):
        start = pl.multiple_of(kt * tkw, tkw)
        sc = _dot(kw_ref[0, 0, pl.ds(start, tkw), :], qrt)
        rel = t_lane - (kt * tkw + lax.broadcasted_iota(jnp.int32, (tkw, 1), 0))
        sc = jnp.where((rel >= 0) & (rel < NSA_WIN), sc, MASKED)
        return _softmax_step_cols(carry, sc, vwt_ref[0, 0, :, pl.ds(start, tkw)])

    w_first = jnp.maximum(q0 - (NSA_WIN - 1), 0) // tkw
    carry = lax.fori_loop(w_first, (q0 + tq - 1) // tkw + 1, win_body, init)
    o_win = carry[2] / carry[1]

    sig = jax.nn.sigmoid(sm_ref[...].T)
    gw = N_BRANCH * hp
    gates = jnp.where(g == 0, sig[SM_NG:SM_NG + gw], sig[SM_NG + gw:SM_NG + 2 * gw])
    outs = []
    for h in range(hp):
        cols = slice(h * tq, (h + 1) * tq)
        outs.append(gates[3 * h:3 * h + 1] * o_cmp[:, cols]
                    + gates[3 * h + 1:3 * h + 2] * o_slc[:, cols]
                    + gates[3 * h + 2:3 * h + 3] * o_win[:, cols])
    o_ref[...] = jnp.concatenate(outs, axis=0).T


def _nsa_prompt_attn(qc, qr, kc, vct, ks, vst, kw, vwt, z, batch, t, tq, tk, tkw):
    nq = t // tq
    gw = NSA_HEADS * NSA_DH // NSA_KV
    assert tk % (SUBLANES * NSA_SEL) == 0 and tq == tkw and tk % tq == 0 and t % tk == 0
    rowblk = lambda b, g, i: b * nq + i
    full = lambda a: pl.BlockSpec((1, 1) + a.shape[2:], lambda b, g, i: (b, g, 0, 0))
    q_spec = pl.BlockSpec((tq, gw), lambda b, g, i: (rowblk(b, g, i), g))
    return pl.pallas_call(
        functools.partial(_nsa_prompt_kernel, tq=tq, tk=tk, tkw=tkw),
        grid=(batch, NSA_KV, nq),
        in_specs=[q_spec, q_spec] + [full(a) for a in (kc, vct, ks, vst, kw, vwt)]
        + [pl.BlockSpec((tq, LANES), lambda b, g, i: (rowblk(b, g, i), Z_SMALL // LANES))],
        out_specs=pl.BlockSpec((tq, gw), lambda b, g, i: (rowblk(b, g, i), g)),
        out_shape=jax.ShapeDtypeStruct((batch * t, NSA_HEADS * NSA_DH), F32),
        scratch_shapes=[pltpu.VMEM((t // NSA_SEL, NSA_HEADS // NSA_KV * tq), F32)],
        compiler_params=_params(("parallel", "parallel", "arbitrary")),
    )(qc, qr, kc, vct, ks, vst, kw, vwt, z)


def _softmax_step_vt(carry, s, vt):
    m, l, acc = carry
    m_new = jnp.maximum(m, jnp.max(s, axis=-1, keepdims=True))
    m_safe = jnp.where(m_new == -jnp.inf, 0.0, m_new)
    p = jnp.exp(s - m_safe)
    alpha = jnp.exp(m - m_safe)
    l = alpha * l + jnp.sum(p, axis=-1, keepdims=True)
    acc = alpha * acc + _dot_nt(p, vt)
    return m_new, l, acc


def _nsa_decode_kernel(pt_ref, *refs, tq, pos0, wrel0, n_keys):
    n_in = SLC_PAGES_PER_STEP
    qc_ref, qr_ref, kc_ref, vc_ref = refs[0:4]
    pages = refs[4:4 + n_in]
    (kw_ref, vw_ref, kt_ref, vt_ref, sm_ref, o_ref,
     m_ref, l_ref, acc_ref, ocmp_ref, sel_ref) = refs[4 + n_in:]
    hp = NSA_HEADS // NSA_KV
    r = hp * tq
    gwid = hp * NSA_DH
    step = pl.program_id(1)
    n_steps = pl.num_programs(1)
    tk = n_in * PAGE_SIZE
    per_tile = tk // NSA_SEL
    n_tiles = sel_ref.shape[1]
    ncp = kc_ref.shape[2]
    nch = ncp // 2
    nsl = -(-(n_tiles * per_tile) // LANES) * LANES
    kv_w = NSA_KV * NSA_DH

    def stack_heads(ref, g):
        x = ref[:, g * gwid:(g + 1) * gwid]
        return jnp.concatenate([x[:, h * NSA_DH:(h + 1) * NSA_DH] for h in range(hp)],
                               axis=0).astype(BF16)

    t_q = pos0 + lax.broadcasted_iota(jnp.int32, (tq, 1), 0)
    t_row = jnp.concatenate([t_q] * hp, axis=0)
    expand = (lax.broadcasted_iota(jnp.int32, (per_tile, tk), 0)
              == lax.broadcasted_iota(jnp.int32, (per_tile, tk), 1) // NSA_SEL).astype(BF16)

    def picked_rows(g, tile, width):
        pk = jnp.dot(sel_ref[g, tile].astype(BF16), expand[:, :width], preferred_element_type=F32)
        return jnp.concatenate([pk] * hp, axis=0)

    @pl.when(step == 0)
    def _():
        for g in range(NSA_KV):
            s = _dot_nt(stack_heads(qc_ref, g), kc_ref[0, g])
            ccol = lax.broadcasted_iota(jnp.int32, (r, ncp), 1)
            cblk = 2 * (ccol % nch) + ccol // nch
            s = jnp.where((cblk + 1) * NSA_CMP <= t_row + 1, s, -jnp.inf)
            m = jnp.max(s, axis=-1, keepdims=True)
            e = jnp.exp(s - jnp.where(m == -jnp.inf, 0.0, m))
            p_cmp = e / jnp.maximum(jnp.sum(e, axis=-1, keepdims=True), 1e-30)
            ocmp_ref[g] = _dot(p_cmp, vc_ref[0, g])
            imp = p_cmp[0:tq]
            for h in range(1, hp):
                imp = imp + p_cmp[h * tq:(h + 1) * tq]
            imp = imp[:, :nch] + imp[:, nch:]
            if nsl > nch:
                imp = jnp.concatenate([imp, jnp.zeros((tq, nsl - nch), F32)], axis=1)
            blk = lax.broadcasted_iota(jnp.int32, (tq, nsl), 1)
            cur = t_q // NSA_SEL
            forced = (blk == 0) | (blk == cur) | (blk == cur - 1)
            score = jnp.where(blk > cur, -jnp.inf, jnp.where(forced, NSA_FORCE, imp))
            sel = _topk_mask(score, NSA_TOPN)
            for j in range(n_tiles):
                sel_ref[g, j] = sel[:, j * per_tile:(j + 1) * per_tile]
            m_ref[g] = jnp.full((r, 1), -jnp.inf, F32)
            l_ref[g] = jnp.zeros((r, 1), F32)
            acc_ref[g] = jnp.zeros((r, NSA_DH), F32)

    kpos = step * tk + lax.broadcasted_iota(jnp.int32, (1, tk), 1)
    for g in range(NSA_KV):
        qr4 = stack_heads(qr_ref, g)
        k_t = jnp.concatenate([p[0, g * NSA_DH:(g + 1) * NSA_DH, :] for p in pages], axis=1)
        v_t = jnp.concatenate([p[0, kv_w + g * NSA_DH:kv_w + (g + 1) * NSA_DH, :] for p in pages],
                              axis=1)
        ok = (picked_rows(g, step, tk) > 0.5) & (kpos <= t_row)
        s = jnp.where(ok, _dot(qr4, k_t), -jnp.inf)
        m, l, acc = _softmax_step_vt((m_ref[g], l_ref[g], acc_ref[g]), s, v_t)
        m_ref[g] = m
        l_ref[g] = l
        acc_ref[g] = acc

    @pl.when(step == n_steps - 1)
    def _():
        sig = jax.nn.sigmoid(sm_ref[...])
        wbase = pos0 - wrel0
        tkw = LANES
        for g in range(NSA_KV):
            qr4 = stack_heads(qr_ref, g)
            tw = kt_ref.shape[2]
            tpos = n_keys + lax.broadcasted_iota(jnp.int32, (1, tw), 1)
            ok = (picked_rows(g, n_tiles - 1, tw) > 0.5) & (tpos <= t_row)
            s = jnp.where(ok, _dot_nt(qr4, kt_ref[0, g]), -jnp.inf)
            carry = _softmax_step((m_ref[g], l_ref[g], acc_ref[g]), s, vt_ref[0, g])
            o_slc = carry[2] / jnp.maximum(carry[1], 1e-30)
            carry = (jnp.full((r, 1), -jnp.inf, F32), jnp.zeros((r, 1), F32),
                     jnp.zeros((r, NSA_DH), F32))
            for kt in range(kw_ref.shape[2] // tkw):
                wpos = wbase + kt * tkw + lax.broadcasted_iota(jnp.int32, (1, tkw), 1)
                rel = t_row - wpos
                ok = (rel >= 0) & (rel < NSA_WIN) & (wpos >= 0)
                s = jnp.where(ok, _dot_nt(qr4, kw_ref[0, g, kt * tkw:(kt + 1) * tkw, :]), -jnp.inf)
                carry = _softmax_step(carry, s, vw_ref[0, g, kt * tkw:(kt + 1) * tkw, :])
            o_win = carry[2] / jnp.maximum(carry[1], 1e-30)
            o_cmp = ocmp_ref[g]
            base = SM_NG + g * N_BRANCH * hp
            outs = []
            for h in range(hp):
                rows = slice(h * tq, (h + 1) * tq)
                c0 = base + N_BRANCH * h
                outs.append(sig[:, c0:c0 + 1] * o_cmp[rows] + sig[:, c0 + 1:c0 + 2] * o_slc[rows]
                            + sig[:, c0 + 2:c0 + 3] * o_win[rows])
            o_ref[:, g * gwid:(g + 1) * gwid] = jnp.concatenate(outs, axis=1)


def _nsa_decode_attn(qc, qr, kc, vc, cache_t, page_table, kw, vw, ktail, vtail, z, tq, pos0, wrel0):
    batch, n_pages = page_table.shape
    n_keys = n_pages * PAGE_SIZE
    steps = n_pages // SLC_PAGES_PER_STEP
    hp = NSA_HEADS // NSA_KV
    r = hp * tq
    qw = NSA_HEADS * NSA_DH
    per_tile = SLC_PAGES_PER_STEP * PAGE_SIZE // NSA_SEL
    assert pos0 == n_keys and ktail.shape[2] <= per_tile * NSA_SEL
    per_b = lambda a: pl.BlockSpec((1,) + a.shape[1:], lambda b, s, pt: (b, 0, 0, 0))
    q_spec = pl.BlockSpec((tq, qw), lambda b, s, pt: (b, 0))
    return pl.pallas_call(
        functools.partial(_nsa_decode_kernel, tq=tq, pos0=pos0, wrel0=wrel0, n_keys=n_keys),
        grid_spec=pltpu.PrefetchScalarGridSpec(
            num_scalar_prefetch=1, grid=(batch, steps),
            in_specs=[q_spec, q_spec, per_b(kc), per_b(vc)]
            + _page_specs((1, ROW_W, PAGE_SIZE), n_pages, SLC_PAGES_PER_STEP)
            + [per_b(kw), per_b(vw), per_b(ktail), per_b(vtail),
               pl.BlockSpec((tq, LANES), lambda b, s, pt: (b, Z_SMALL // LANES))],
            out_specs=pl.BlockSpec((tq, qw), lambda b, s, pt: (b, 0)),
            scratch_shapes=[pltpu.VMEM((NSA_KV, r, 1), F32), pltpu.VMEM((NSA_KV, r, 1), F32),
                            pltpu.VMEM((NSA_KV, r, NSA_DH), F32),
                            pltpu.VMEM((NSA_KV, r, NSA_DH), F32),
                            pltpu.VMEM((NSA_KV, steps + 1, tq, per_tile), F32)]),
        out_shape=jax.ShapeDtypeStruct((batch * tq, qw), F32),
        compiler_params=_params(("parallel", "arbitrary")),
    )(page_table.reshape(-1), qc, qr, kc, vc, *([cache_t] * SLC_PAGES_PER_STEP),
      kw, vw, ktail, vtail, z)


def _mem_attn_kernel(q_ref, kv_ref, o_ref):
    hw = MEM_HEADS * MEM_DH
    for h in range(MEM_HEADS):
        q = q_ref[:, h * MEM_DH:(h + 1) * MEM_DH] * (MEM_DH ** -0.5)
        k = kv_ref[0, :, h * MEM_DH:(h + 1) * MEM_DH]
        v = kv_ref[0, :, hw + h * MEM_DH:hw + (h + 1) * MEM_DH]
        s = _dot_nt(q, k)
        e = jnp.exp(s - jnp.max(s, axis=-1, keepdims=True))
        p = e / jnp.sum(e, axis=-1, keepdims=True)
        o_ref[:, h * MEM_DH:(h + 1) * MEM_DH] = _dot(p, v)


def _mem_attn(z, kv, batch, t, tm):
    nt = t // tm
    hw = MEM_HEADS * MEM_DH
    return pl.pallas_call(
        _mem_attn_kernel,
        grid=(batch, nt),
        in_specs=[pl.BlockSpec((tm, hw), lambda b, j: (b * nt + j, Z_MQ // hw)),
                  pl.BlockSpec((1,) + kv.shape[1:], lambda b, j: (b, 0, 0))],
        out_specs=pl.BlockSpec((tm, hw), lambda b, j: (b * nt + j, 0)),
        out_shape=jax.ShapeDtypeStruct((batch * t, hw), F32),
        compiler_params=_params(("parallel", "parallel")),
    )(z, kv)


def _merge_kernel(x_ref, mg_ref, og_ref, on_ref, om_ref, wg_ref, wn_ref, wm_ref, wo_ref,
                  gf_ref, wq_ref, x1_ref, h2_ref, qp_ref):
    d = D_MODEL
    mix = (jax.nn.sigmoid(mg_ref[:, 0:d]) * _dot(og_ref[...], wg_ref[...])
           + jax.nn.sigmoid(mg_ref[:, d:2 * d]) * _dot(on_ref[...], wn_ref[...])
           + jax.nn.sigmoid(mg_ref[:, 2 * d:3 * d]) * _dot(om_ref[...], wm_ref[...]))
    x1 = x_ref[...] + _dot(mix, wo_ref[...])
    x1_ref[...] = x1
    h2 = _rms(x1, gf_ref[...]).astype(BF16)
    h2_ref[...] = h2
    qp_ref[...] = jnp.dot(h2, wq_ref[...], preferred_element_type=F32)


def _merge(x, z, o_gdn, o_nsa, o_mem, wg, wn, wm, wo, norm_ffn, wq, tm):
    n, d = x.shape
    qw = wq.shape[1]
    row = lambda w: pl.BlockSpec((tm, w), lambda i: (i, 0))
    const = lambda a: pl.BlockSpec(a.shape, lambda i: (0, 0))
    return pl.pallas_call(
        _merge_kernel,
        grid=(n // tm,),
        in_specs=[row(d), pl.BlockSpec((tm, N_BRANCH * d), lambda i: (i, Z_MG)),
                  row(o_gdn.shape[1]), row(o_nsa.shape[1]), row(o_mem.shape[1]),
                  const(wg), const(wn), const(wm), const(wo),
                  pl.BlockSpec((1, d), lambda i: (0, 0)), const(wq)],
        out_specs=[row(d), row(d), row(qw)],
        out_shape=[jax.ShapeDtypeStruct((n, d), F32), jax.ShapeDtypeStruct((n, d), BF16),
                   jax.ShapeDtypeStruct((n, qw), F32)],
        compiler_params=_params(("parallel",)),
    )(x, z, o_gdn, o_nsa, o_mem, wg, wn, wm, wo, norm_ffn.reshape(1, d), wq)


PEER_RANKS = PEER_TOPK + 1


def _top_values(s, n):
    vals = []
    for _ in range(n):
        m = jnp.max(s, axis=0, keepdims=True)
        vals.append(m)
        s = jnp.where(s >= m, -jnp.inf, s)
    return vals


def _peer_route_kernel(qp_ref, sk_ref, s2_ref, e2_ref, th_ref, e1_ref):
    half = PEER_DKEY // 2
    nt = (((1,), (1,)), ((), ()))
    for h in range(PEER_HEADS):
        qa = qp_ref[:, h * PEER_DKEY:h * PEER_DKEY + half]
        qb = qp_ref[:, h * PEER_DKEY + half:(h + 1) * PEER_DKEY]
        s1 = lax.dot_general(sk_ref[0], qa, nt, precision=HIGHEST, preferred_element_type=F32)
        s2 = lax.dot_general(sk_ref[1], qb, nt, precision=HIGHEST, preferred_element_type=F32)
        a = _top_values(s1, PEER_RANKS)
        b = _top_values(s2, PEER_RANKS)
        b_rows = jnp.concatenate(b, axis=0)
        cands = [a[i - 1] + b_rows[0:PEER_RANKS // i] for i in range(1, PEER_RANKS + 1)]
        work = cands
        ranked = []
        for _ in range(PEER_RANKS):
            m = work[0][0:1]
            for cnd in work:
                m = jnp.maximum(m, jnp.max(cnd, axis=0, keepdims=True))
            ranked.append(m)
            work = [jnp.where(cnd >= m, -jnp.inf, cnd) for cnd in work]
        tau = 0.5 * (ranked[PEER_TOPK - 1] + ranked[PEER_TOPK])
        top = a[0] + b[0]
        zsum = jnp.zeros_like(tau)
        for cnd in cands:
            zsum = zsum + jnp.sum(jnp.where(cnd >= tau, jnp.exp(cnd - top), 0.0),
                                  axis=0, keepdims=True)
        s2_ref[h] = s2
        e2_ref[h] = jnp.exp(s2 - b[0])
        th_ref[h] = tau - s1
        e1_ref[h] = jnp.exp(s1 - a[0]) / zsum


def _peer_route(qp, subkeys, tt):
    n = qp.shape[0]
    shape = jax.ShapeDtypeStruct((PEER_HEADS, PEER_NKEYS, n), F32)
    spec = pl.BlockSpec((PEER_HEADS, PEER_NKEYS, tt), lambda i: (0, 0, i))
    return pl.pallas_call(
        _peer_route_kernel,
        grid=(n // tt,),
        in_specs=[pl.BlockSpec((tt, qp.shape[1]), lambda i: (i, 0)),
                  pl.BlockSpec(subkeys.shape, lambda i: (0, 0, 0))],
        out_specs=[spec, spec, spec, spec],
        out_shape=[shape, shape, shape, shape],
        compiler_params=_params(("parallel",)),
    )(qp, subkeys)


def _peer_dense_kernel(ht_ref, u_ref, vt_ref, s2_ref, e2_ref, th_ref, e1_ref, x1_ref, gf_ref,
                       y_ref, acc_ref, act_ref, wa_ref, *, jb):
    j = pl.program_id(1)
    tt = ht_ref.shape[1]

    @pl.when(j == 0)
    def _():
        acc_ref[...] = jnp.zeros(acc_ref.shape, F32)

    act_ref[...] = _gelu(jnp.dot(u_ref[...], ht_ref[...], preferred_element_type=F32))
    for jj in range(jb):
        i1 = j * jb + jj
        rows = slice(jj * PEER_NKEYS, (jj + 1) * PEER_NKEYS)
        th_rows = [th_ref[h, pl.ds(i1, 1), :] for h in range(PEER_HEADS)]
        e1_rows = [e1_ref[h, pl.ds(i1, 1), :] for h in range(PEER_HEADS)]
        for c in range(tt // LANES):
            cols = slice(c * LANES, (c + 1) * LANES)
            w = None
            for h in range(PEER_HEADS):
                th = th_rows[h][:, cols]
                e1 = e1_rows[h][:, cols]
                term = jnp.where(s2_ref[h, :, cols] >= th, e2_ref[h, :, cols], 0.0) * e1
                w = term if w is None else w + term
            wa_ref[rows, cols] = (w * act_ref[rows, cols]).astype(BF16)
    acc_ref[...] += jnp.dot(vt_ref[...], wa_ref[...], preferred_element_type=F32)

    @pl.when(j == pl.num_programs(1) - 1)
    def _():
        y_ref[...] = _rms(x1_ref[...] + acc_ref[...].T, gf_ref[...])


def _peer_dense(ht, u, vt, s2, e2, th, e1, x1, norm_final, tt, jb):
    d, n = ht.shape
    n_exp = u.shape[0]
    eb = jb * PEER_NKEYS
    route = pl.BlockSpec((PEER_HEADS, PEER_NKEYS, tt), lambda t, j: (0, 0, t))
    return pl.pallas_call(
        functools.partial(_peer_dense_kernel, jb=jb),
        grid=(n // tt, n_exp // eb),
        in_specs=[pl.BlockSpec((d, tt), lambda t, j: (0, t)),
                  pl.BlockSpec((eb, d), lambda t, j: (j, 0)),
                  pl.BlockSpec((d, eb), lambda t, j: (0, j)),
                  route, route, route, route,
                  pl.BlockSpec((tt, d), lambda t, j: (t, 0)),
                  pl.BlockSpec((1, d), lambda t, j: (0, 0))],
        out_specs=pl.BlockSpec((tt, d), lambda t, j: (t, 0)),
        out_shape=jax.ShapeDtypeStruct((n, d), F32),
        scratch_shapes=[pltpu.VMEM((d, tt), F32), pltpu.VMEM((eb, tt), F32),
                        pltpu.VMEM((eb, tt), BF16)],
        compiler_params=_params(("parallel", "arbitrary")),
    )(ht, u, vt, s2, e2, th, e1, x1, norm_final.reshape(1, d))


def _permute_w_in(w_in):
    sizes = (GDN_CONV_CH, GDN_HEADS * GDN_DV, GDN_HEADS, GDN_HEADS, NSA_HEADS * NSA_DH, NKV_W,
             N_BRANCH * NSA_HEADS, MEM_HEADS * MEM_DH, N_BRANCH * D_MODEL)
    qkv, zg, a, b, nq, nkv, ng, mq, mg = jnp.split(w_in, np.cumsum(sizes)[:-1].tolist(), axis=1)
    pad = jnp.zeros((w_in.shape[0], LANES - a.shape[1] - b.shape[1] - ng.shape[1]), w_in.dtype)
    return jnp.concatenate([mg, qkv, zg, nq, mq, nkv, a, b, ng, pad], axis=1).astype(BF16)


def _tokens_tile(n, pref):
    return pref if n % pref == 0 else n


def _layer(x, pos0, kv_mem, nsa_keys, gdn_state, conv_buf, w, peer_tt):
    batch, t, d = x.shape
    n = batch * t
    xf = x.reshape(n, d)
    z = _norm_matmul(xf, w['norm_attn'], w['w_in'], _tokens_tile(n, 256), 640)

    o_gdn, s_new, conv_new = _gdn(z, conv_buf, gdn_state, w['gdn_conv'], w['gdn_a_log'],
                                  w['gdn_dt_bias'], w['gdn_norm'], batch, t)

    prep_tm = _tokens_tile(t, 512)
    kv_dtype = BF16 if prep_tm % 16 == 0 else F32
    qc, qr, slc_rows, win_rows, ks, vs, kw, vw = _nsa_prep(
        z, pos0 + jnp.arange(t), batch, t, prep_tm, kv_dtype)
    cmp_rows = z[:, Z_NKV:Z_NKV + ROW_W]
    o_nsa = nsa_keys(z, qc, qr, cmp_rows, ks, vs, kw, vw)

    o_mem = _mem_attn(z, kv_mem, batch, t, _tokens_tile(t, 512))

    x1, h2, qp = _merge(xf, z, o_gdn, o_nsa, o_mem, w['w_gdn_out'], w['w_nsa_out'],
                        w['w_mem_out'], w['w_o'], w['norm_ffn'], w['peer_wq'],
                        _tokens_tile(n, 256))
    s2, e2, th, e1 = _peer_route(qp, w['peer_subkeys'], 256)
    y = _peer_dense(h2.T, w['peer_u'], w['peer_vt'], s2, e2, th, e1, x1, w['norm_final'],
                    peer_tt, 8)
    row5 = lambda a: a.reshape(batch, t, 2, NSA_KV, NSA_DH)
    return (y.reshape(batch, t, d), row5(cmp_rows), row5(slc_rows), row5(win_rows), s_new,
            conv_new)


def kernel(x_prompt, x_sample, cache_mem_kv, cache_cmp_kv, cache_slc_kv, cache_win_kv, state_gdn, state_conv, page_table, mem_prompt, norm_attn, w_in, gdn_conv, gdn_a_log, gdn_dt_bias, gdn_norm, nsa_cmp_pe, nsa_cmp_w1, nsa_cmp_w2, norm_mem, w_mem_kv, w_gdn_out, w_nsa_out, w_mem_out, w_o, norm_ffn, peer_wq, peer_subkeys, peer_u, peer_v, norm_final):
    depth = w_in.shape[0]
    assert depth == 1
    l = 0
    bp, seq, d = x_prompt.shape
    bs, tdec, _ = x_sample.shape
    n_pages = page_table.shape[1]
    past = n_pages * PAGE_SIZE
    assert past % NSA_CMP == 0 and tdec < NSA_CMP and seq % LANES == 0

    pe_big, w1big, w2big = _compress_weights(nsa_cmp_pe[l], nsa_cmp_w1[l], nsa_cmp_w2[l])
    w = dict(norm_attn=norm_attn[l], w_in=_permute_w_in(w_in[l]), gdn_conv=gdn_conv[l],
             gdn_a_log=gdn_a_log[l], gdn_dt_bias=gdn_dt_bias[l], gdn_norm=gdn_norm[l],
             w_gdn_out=w_gdn_out[l].astype(BF16), w_nsa_out=w_nsa_out[l].astype(BF16),
             w_mem_out=w_mem_out[l].astype(BF16), w_o=w_o[l].astype(BF16), norm_ffn=norm_ffn[l],
             peer_wq=peer_wq[l].astype(BF16), peer_subkeys=peer_subkeys[l],
             peer_u=peer_u[l].astype(BF16), peer_vt=peer_v[l].astype(BF16).T,
             norm_final=norm_final)

    mem_n = mem_prompt.shape[0] * mem_prompt.shape[1]
    kvm = _norm_matmul(mem_prompt.reshape(mem_n, d), norm_mem[l], w_mem_kv[l].astype(BF16),
                       _tokens_tile(mem_n, 256), 512).reshape(bp, mem_prompt.shape[1], -1)

    def prompt_keys(z, qc, qr, cmp_rows, ks, vs, kw, vw):
        n_blk = bp * seq // NSA_CMP
        cmp_out = _compress(cmp_rows.reshape(n_blk, NSA_CMP * ROW_W), pe_big, w1big, w2big,
                            _tokens_tile(n_blk, 128))
        kc, vc = _split_compressed(cmp_out, bp)
        tr = lambda a: jnp.transpose(a, (0, 1, 3, 2))
        return _nsa_prompt_attn(qc, qr, kc, tr(vc), ks, tr(vs), kw, tr(vw), z, bp, seq,
                                128, _tokens_tile(seq, 512), 128)

    yp, cmp_p, slc_p, win_p, gdn_p, conv_p = _layer(
        x_prompt, 0, kvm, prompt_keys,
        jnp.zeros((bp, GDN_HEADS, GDN_DK, GDN_DV), F32),
        jnp.zeros((bp, GDN_CONV - 1, GDN_CONV_CH), F32), w, 512)
    win_len_p = min(NSA_WIN, seq)
    win_p = win_p[:, seq - win_len_p:]

    n_pool = cache_cmp_kv.shape[1]
    cache_cmp = cache_cmp_kv[l].reshape(n_pool, BLOCKS_PER_PAGE, NSA_CMP * ROW_W)
    cache_slc = jnp.transpose(cache_slc_kv[l], (0, 2, 3, 4, 1)).reshape(n_pool, ROW_W, PAGE_SIZE)
    cache_win = cache_win_kv[l].reshape(bs, -1, ROW_W)
    wb = cache_win.shape[1]
    assert wb == NSA_WIN
    kv_w = NSA_KV * NSA_DH

    def pad_rows(a, rows):
        return jnp.pad(a, ((0, 0), (0, 0), (0, rows - a.shape[2]), (0, 0))).astype(BF16)

    def split_rows(rows):
        r = rows.reshape(bs, rows.shape[1], 2, NSA_KV, NSA_DH)
        return jnp.transpose(r[:, :, 0], (0, 2, 1, 3)), jnp.transpose(r[:, :, 1], (0, 2, 1, 3))

    def sample_keys(z, qc, qr, cmp_rows, ks_new, vs_new, kw_new, vw_new):
        cmp_out = _paged_compress(cache_cmp, page_table, pe_big, w1big, w2big)
        kc, vc = _split_compressed(cmp_out, bs)
        kwc, vwc = split_rows(cache_win)
        win_rows = wb + LANES
        kw = pad_rows(jnp.concatenate([kwc, kw_new], axis=2), win_rows)
        vw = pad_rows(jnp.concatenate([vwc, vw_new], axis=2), win_rows)
        return _nsa_decode_attn(qc, qr, kc, vc, cache_slc, page_table, kw, vw,
                                pad_rows(ks_new, LANES), pad_rows(vs_new, LANES), z,
                                tdec, past, wb)

    ys, cmp_s, slc_s, win_new, gdn_s, conv_s = _layer(
        x_sample, past, cache_mem_kv[l].reshape(bs, cache_mem_kv.shape[2], -1), sample_keys,
        state_gdn[l], state_conv[l], w, 256)
    win_all = jnp.concatenate([cache_win_kv[l], win_new], axis=1)
    win_s = win_all[:, win_all.shape[1] - min(NSA_WIN, past + tdec):]

    stack = lambda a: a[None]
    return (yp, ys, stack(kvm.reshape(bp, mem_prompt.shape[1], 2, MEM_HEADS, MEM_DH)),
            stack(cmp_p), stack(slc_p), stack(win_p), stack(gdn_p), stack(conv_p),
            stack(cmp_s), stack(slc_s), stack(win_s), stack(gdn_s), stack(conv_s))
```

```python
import functools
import math

import jax
import jax.numpy as jnp
import numpy as np
from jax import lax
from jax.experimental import pallas as pl
from jax.experimental.pallas import tpu as pltpu

F32 = jnp.float32
BF16 = jnp.bfloat16
HIGHEST = lax.Precision.HIGHEST

D_MODEL = 1024
PAGE_SIZE = 128
GDN_HEADS = 4
GDN_DK = 128
GDN_DV = 128
GDN_CONV = 4
GDN_CHUNK = 64
GDN_CONV_CH = GDN_HEADS * (2 * GDN_DK + GDN_DV)
NSA_HEADS = 8
NSA_KV = 2
NSA_DH = 64
NSA_CMP = 32
NSA_SEL = 64
NSA_TOPN = 16
NSA_WIN = 512
NSA_FORCE = 1e9
MEM_HEADS = 4
MEM_DH = 128
PEER_HEADS = 8
PEER_NKEYS = 128
PEER_DKEY = 256
PEER_TOPK = 16
N_BRANCH = 3
ROPE_THETA = 10000.0
EPS = 1e-6

LANES = 128
SUBLANES = 8
VMEM_LIMIT = 56 * 1024 * 1024

Z_MG = 0
Z_QKV = Z_MG + N_BRANCH * D_MODEL
Z_ZG = Z_QKV + GDN_CONV_CH
Z_NQ = Z_ZG + GDN_HEADS * GDN_DV
Z_MQ = Z_NQ + NSA_HEADS * NSA_DH
Z_NKV = Z_MQ + MEM_HEADS * MEM_DH
Z_SMALL = Z_NKV + 3 * 2 * NSA_KV * NSA_DH
Z_WIDTH = Z_SMALL + LANES
SM_A = 0
SM_B = GDN_HEADS
SM_NG = 2 * GDN_HEADS
NKV_W = 3 * 2 * NSA_KV * NSA_DH
ROW_W = 2 * NSA_KV * NSA_DH


def _params(sem, vmem=VMEM_LIMIT):
    return pltpu.CompilerParams(dimension_semantics=sem, vmem_limit_bytes=vmem)


def _dot(a, b):
    return jnp.dot(a.astype(BF16), b.astype(BF16), preferred_element_type=F32)


def _dot_nt(a, b):
    return lax.dot_general(a.astype(BF16), b.astype(BF16), (((1,), (1,)), ((), ())),
                           preferred_element_type=F32)


def _dot_hi(a, b):
    return jnp.dot(a, b, precision=HIGHEST, preferred_element_type=F32)


def _dot3(a, b):
    ah = a.astype(BF16)
    bh = b.astype(BF16)
    al = (a - ah.astype(F32)).astype(BF16)
    bl = (b - bh.astype(F32)).astype(BF16)
    d = lambda x, y: jnp.dot(x, y, preferred_element_type=F32)
    return d(ah, bh) + d(al, bh) + d(ah, bl)


def _rms(x, g):
    return x * lax.rsqrt(jnp.mean(x * x, axis=-1, keepdims=True) + EPS) * g


def _gelu(x):
    a = -2.0 * math.sqrt(2.0 / math.pi)
    return x / (1.0 + jnp.exp(x * (a + (a * 0.044715) * (x * x))))


def _norm_matmul_kernel(x_ref, g_ref, w_ref, o_ref, *, col_chunk):
    yb = _rms(x_ref[...], g_ref[...]).astype(BF16)
    for c0 in range(0, o_ref.shape[1], col_chunk):
        o_ref[:, c0:c0 + col_chunk] = jnp.dot(yb, w_ref[:, c0:c0 + col_chunk],
                                              preferred_element_type=F32)


def _norm_matmul(x, g, w, tm, col_chunk):
    n, d = x.shape
    wc = w.shape[1]
    return pl.pallas_call(
        functools.partial(_norm_matmul_kernel, col_chunk=col_chunk),
        grid=(n // tm,),
        in_specs=[pl.BlockSpec((tm, d), lambda i: (i, 0)),
                  pl.BlockSpec((1, d), lambda i: (0, 0)),
                  pl.BlockSpec((d, wc), lambda i: (0, 0), pipeline_mode=pl.Buffered(1))],
        out_specs=pl.BlockSpec((tm, wc), lambda i: (i, 0)),
        out_shape=jax.ShapeDtypeStruct((n, wc), F32),
        compiler_params=_params(("parallel",)),
    )(x, g.reshape(1, d), w)


def _tri_inverse(lmat, c):
    row = lax.broadcasted_iota(jnp.int32, (c, c), 0)
    col = lax.broadcasted_iota(jnp.int32, (c, c), 1)
    x = jnp.where(row == col, 1.0, 0.0) - lmat
    p = _dot3(lmat, lmat)
    n = 2
    while n < c:
        x = x + _dot3(x, p)
        n *= 2
        if n < c:
            p = _dot3(p, p)
    return x


def _gdn_kernel(qkv_ref, zg_ref, sm_ref, buf_ref, s0_ref, cw_ref, alog_ref, dt_ref, gn_ref,
                o_ref, snew_ref, cnew_ref, ext_ref, s_ref, *, tb, n_chunks, nb):
    ci = pl.program_id(1)

    @pl.when(ci == 0)
    def _():
        for i in range(nb):
            ext_ref[i, 0:SUBLANES, :] = buf_ref[i]
            s_ref[i] = s0_ref[i]

    for i in range(nb):
        _gdn_chunk(qkv_ref.at[i], zg_ref.at[i], sm_ref.at[i], cw_ref, alog_ref, dt_ref, gn_ref,
                   o_ref.at[i], cnew_ref.at[i], ext_ref.at[i], s_ref.at[i], tb)

    @pl.when(ci == n_chunks - 1)
    def _():
        snew_ref[...] = s_ref[...]


def _gdn_chunk(qkv_ref, zg_ref, sm_ref, cw_ref, alog_ref, dt_ref, gn_ref, o_ref, cnew_ref,
               ext_ref, s_ref, tb):
    c = GDN_CHUNK
    if tb < c:
        ext_ref[SUBLANES + tb:, :] = jnp.zeros((c - tb, GDN_CONV_CH), F32)
    ext_ref[SUBLANES:SUBLANES + tb, :] = qkv_ref[...]
    cw = cw_ref[...]
    conv = cw[0:1] * ext_ref[SUBLANES - 3:SUBLANES - 3 + c, :]
    for j in range(1, GDN_CONV):
        conv = conv + cw[j:j + 1] * ext_ref[SUBLANES - 3 + j:SUBLANES - 3 + j + c, :]
    u = conv * jax.nn.sigmoid(conv)
    last_rows = ext_ref[tb:tb + SUBLANES, :]
    cnew_ref[...] = last_rows
    ext_ref[0:SUBLANES, :] = last_rows

    sm = sm_ref[...]
    if tb < c:
        sm = jnp.concatenate([sm, jnp.zeros((c - tb, LANES), F32)], axis=0)
    za = sm + dt_ref[...]
    softplus = jnp.maximum(za, 0.0) + jnp.log1p(jnp.exp(-jnp.abs(za)))
    g_all = -jnp.exp(alog_ref[...]) * softplus
    beta_all = jax.nn.sigmoid(sm)
    if tb < c:
        valid = lax.broadcasted_iota(jnp.int32, (c, 1), 0) < tb
        u = jnp.where(valid, u, 0.0)
        g_all = jnp.where(valid, g_all, 0.0)
        beta_all = jnp.where(valid, beta_all, 0.0)

    row = lax.broadcasted_iota(jnp.int32, (c, c), 0)
    col = lax.broadcasted_iota(jnp.int32, (c, c), 1)
    tril = row >= col
    eye = row == col
    gc_all = _dot_hi(jnp.where(tril, 1.0, 0.0), g_all)

    hk = GDN_HEADS * GDN_DK
    for h in range(GDN_HEADS):
        qh = u[:, h * GDN_DK:(h + 1) * GDN_DK]
        kh = u[:, hk + h * GDN_DK:hk + (h + 1) * GDN_DK]
        vh = u[:, 2 * hk + h * GDN_DV:2 * hk + (h + 1) * GDN_DV]
        qn = qh * lax.rsqrt(jnp.sum(qh * qh, axis=-1, keepdims=True) + EPS) * (GDN_DK ** -0.5)
        kn = kh * lax.rsqrt(jnp.sum(kh * kh, axis=-1, keepdims=True) + EPS)
        beta = beta_all[:, SM_B + h:SM_B + h + 1]
        gc = gc_all[:, SM_A + h:SM_A + h + 1]
        gl = gc_all[c - 1:c, SM_A + h:SM_A + h + 1]
        gc_row = jnp.sum(jnp.where(eye, gc, 0.0), axis=0, keepdims=True)
        decay = jnp.exp(jnp.where(tril, gc - gc_row, -jnp.inf))
        kb = kn * beta
        lmat = jnp.where(row > col, _dot_nt(kb, kn) * decay, 0.0)
        tinv = _tri_inverse(lmat, c)
        egc = jnp.exp(gc)
        uu = _dot(tinv, vh * beta)
        ww = _dot(tinv, kb * egc)
        aqk = _dot_nt(qn, kn) * decay
        s = s_ref[h]
        v_new = uu - _dot(ww, s)
        o = _dot(qn * egc, s) + _dot(aqk, v_new)
        kd = kn * jnp.exp(gl - gc)
        s_ref[h] = s * jnp.exp(gl) + _dot(kd.T, v_new)
        zh = zg_ref[:, h * GDN_DV:(h + 1) * GDN_DV]
        on = _rms(o[0:tb], gn_ref[...]) * (zh * jax.nn.sigmoid(zh))
        o_ref[:, h * GDN_DV:(h + 1) * GDN_DV] = on


GDN_BATCH_PER_STEP = 2


def _gdn(z, conv_buf, s0, conv_w, a_log, dt_bias, gnorm, batch, t):
    c = GDN_CHUNK
    tb = min(t, c)
    n_chunks = t // tb
    nb = GDN_BATCH_PER_STEP if batch % GDN_BATCH_PER_STEP == 0 else 1
    assert tb % SUBLANES == 0 and n_chunks * tb == t and (tb == c or n_chunks == 1)
    buf8 = jnp.pad(conv_buf, ((0, 0), (SUBLANES - (GDN_CONV - 1), 0), (0, 0)))
    alog_row = jnp.zeros((1, LANES), F32).at[0, SM_A:SM_A + GDN_HEADS].set(a_log)
    dt_row = jnp.zeros((1, LANES), F32).at[0, SM_A:SM_A + GDN_HEADS].set(dt_bias)
    z3 = z.reshape(batch, t, Z_WIDTH)
    hv = GDN_HEADS * GDN_DV
    o, s_new, c_new = pl.pallas_call(
        functools.partial(_gdn_kernel, tb=tb, n_chunks=n_chunks, nb=nb),
        grid=(batch // nb, n_chunks),
        in_specs=[
            pl.BlockSpec((nb, tb, GDN_CONV_CH), lambda b, ci: (b, ci, Z_QKV // GDN_CONV_CH)),
            pl.BlockSpec((nb, tb, hv), lambda b, ci: (b, ci, Z_ZG // hv)),
            pl.BlockSpec((nb, tb, LANES), lambda b, ci: (b, ci, Z_SMALL // LANES)),
            pl.BlockSpec((nb, SUBLANES, GDN_CONV_CH), lambda b, ci: (b, 0, 0)),
            pl.BlockSpec((nb, GDN_HEADS, GDN_DK, GDN_DV), lambda b, ci: (b, 0, 0, 0)),
            pl.BlockSpec((GDN_CONV, GDN_CONV_CH), lambda b, ci: (0, 0)),
            pl.BlockSpec((1, LANES), lambda b, ci: (0, 0)),
            pl.BlockSpec((1, LANES), lambda b, ci: (0, 0)),
            pl.BlockSpec((1, GDN_DV), lambda b, ci: (0, 0)),
        ],
        out_specs=[
            pl.BlockSpec((nb, tb, hv), lambda b, ci: (b, ci, 0)),
            pl.BlockSpec((nb, GDN_HEADS, GDN_DK, GDN_DV), lambda b, ci: (b, 0, 0, 0)),
            pl.BlockSpec((nb, SUBLANES, GDN_CONV_CH), lambda b, ci: (b, 0, 0)),
        ],
        out_shape=[
            jax.ShapeDtypeStruct((batch, t, hv), F32),
            jax.ShapeDtypeStruct((batch, GDN_HEADS, GDN_DK, GDN_DV), F32),
            jax.ShapeDtypeStruct((batch, SUBLANES, GDN_CONV_CH), F32),
        ],
        scratch_shapes=[pltpu.VMEM((nb, SUBLANES + c, GDN_CONV_CH), F32),
                        pltpu.VMEM((nb, GDN_HEADS, GDN_DK, GDN_DV), F32)],
        compiler_params=_params(("parallel", "arbitrary")),
    )(z3, z3, z3, buf8, s0, conv_w, alog_row, dt_row, gnorm.reshape(1, GDN_DV))
    return o.reshape(batch * t, hv), s_new, c_new[:, SUBLANES - (GDN_CONV - 1):]


def _rope_tables(pos):
    half = NSA_DH // 2
    inv = jnp.power(ROPE_THETA, -jnp.arange(half, dtype=F32) / half)
    ang = pos.astype(F32)[:, None] * inv[None, :]
    cos, sin = jnp.cos(ang), jnp.sin(ang)
    cos_t = jnp.concatenate([cos, cos, cos, cos], axis=-1)
    sin_t = jnp.concatenate([-sin, sin, -sin, sin], axis=-1)
    return cos_t, sin_t


def _nsa_prep_kernel(nq_ref, nkv_ref, cos_ref, sin_ref,
                     qc_ref, qr_ref, slc_ref, win_ref, ks_ref, vs_ref, kw_ref, vw_ref):
    cos = cos_ref[...]
    sin = sin_ref[...]
    lane = lax.broadcasted_iota(jnp.int32, cos.shape, 1)
    first_half = (lane % NSA_DH) < (NSA_DH // 2)

    def rope(x):
        swapped = jnp.where(first_half, pltpu.roll(x, LANES - NSA_DH // 2, 1),
                            pltpu.roll(x, NSA_DH // 2, 1))
        return x * cos + swapped * sin

    scale = NSA_DH ** -0.5
    for j in range(NSA_HEADS * NSA_DH // LANES):
        x = nq_ref[:, j * LANES:(j + 1) * LANES]
        qc_ref[:, j * LANES:(j + 1) * LANES] = x * scale
        qr_ref[:, j * LANES:(j + 1) * LANES] = rope(x) * scale

    kv_w = NSA_KV * NSA_DH
    for br, (row_ref, k_ref, v_ref) in enumerate(((slc_ref, ks_ref, vs_ref),
                                                   (win_ref, kw_ref, vw_ref))):
        base = (br + 1) * ROW_W
        kr = rope(nkv_ref[:, base:base + kv_w])
        v = nkv_ref[:, base + kv_w:base + 2 * kv_w]
        row_ref[:, 0:kv_w] = kr
        row_ref[:, kv_w:2 * kv_w] = v
        for g in range(NSA_KV):
            k_ref[0, g] = kr[:, g * NSA_DH:(g + 1) * NSA_DH].astype(k_ref.dtype)
            v_ref[0, g] = v[:, g * NSA_DH:(g + 1) * NSA_DH].astype(v_ref.dtype)


def _nsa_prep(z, pos, batch, t, tm, kv_dtype):
    cos_t, sin_t = _rope_tables(pos)
    nt = t // tm
    rowblk = lambda b, j: b * nt + j
    kv_shape = jax.ShapeDtypeStruct((batch, NSA_KV, t, NSA_DH), kv_dtype)
    kv_spec = pl.BlockSpec((1, NSA_KV, tm, NSA_DH), lambda b, j: (b, 0, j, 0))
    qw = NSA_HEADS * NSA_DH
    return pl.pallas_call(
        _nsa_prep_kernel,
        grid=(batch, nt),
        in_specs=[pl.BlockSpec((tm, qw), lambda b, j: (rowblk(b, j), Z_NQ // qw)),
                  pl.BlockSpec((tm, NKV_W), lambda b, j: (rowblk(b, j), Z_NKV // NKV_W)),
                  pl.BlockSpec((tm, LANES), lambda b, j: (j, 0)),
                  pl.BlockSpec((tm, LANES), lambda b, j: (j, 0))],
        out_specs=[pl.BlockSpec((tm, qw), lambda b, j: (rowblk(b, j), 0)),
                   pl.BlockSpec((tm, qw), lambda b, j: (rowblk(b, j), 0)),
                   pl.BlockSpec((tm, ROW_W), lambda b, j: (rowblk(b, j), 0)),
                   pl.BlockSpec((tm, ROW_W), lambda b, j: (rowblk(b, j), 0)),
                   kv_spec, kv_spec, kv_spec, kv_spec],
        out_shape=[jax.ShapeDtypeStruct((batch * t, qw), F32),
                   jax.ShapeDtypeStruct((batch * t, qw), F32),
                   jax.ShapeDtypeStruct((batch * t, ROW_W), F32),
                   jax.ShapeDtypeStruct((batch * t, ROW_W), F32),
                   kv_shape, kv_shape, kv_shape, kv_shape],
        compiler_params=_params(("parallel", "parallel")),
    )(z, z, cos_t, sin_t)


def _compress_weights(pe, w1, w2):
    eye = jnp.eye(2, dtype=F32)
    w1r = w1.reshape(2, NSA_CMP, NSA_DH, NSA_DH)
    w1big = jnp.einsum('srde,st,gh->rsgdthe', w1r, eye, eye).reshape(NSA_CMP * ROW_W, ROW_W)
    w2big = jnp.einsum('sed,st,gh->sgethd', w2, eye, eye).reshape(ROW_W, ROW_W)
    pe_big = jnp.broadcast_to(jnp.transpose(pe, (1, 0, 2))[:, :, None, :],
                              (NSA_CMP, 2, NSA_KV, NSA_DH)).reshape(1, NSA_CMP * ROW_W)
    return pe_big, w1big.astype(BF16), w2big.astype(BF16)


def _compress_kernel(x_ref, pe_ref, w1_ref, w2_ref, o_ref):
    x = (x_ref[...] + pe_ref[...]).astype(BF16)
    hid = _gelu(jnp.dot(x, w1_ref[...], preferred_element_type=F32))
    o_ref[...] = jnp.dot(hid.astype(BF16), w2_ref[...], preferred_element_type=F32)


def _compress(rows, pe_big, w1big, w2big, tm):
    n, kdim = rows.shape
    return pl.pallas_call(
        _compress_kernel,
        grid=(n // tm,),
        in_specs=[pl.BlockSpec((tm, kdim), lambda i: (i, 0)),
                  pl.BlockSpec((1, kdim), lambda i: (0, 0)),
                  pl.BlockSpec((kdim, ROW_W), lambda i: (0, 0)),
                  pl.BlockSpec((ROW_W, ROW_W), lambda i: (0, 0))],
        out_specs=pl.BlockSpec((tm, ROW_W), lambda i: (i, 0)),
        out_shape=jax.ShapeDtypeStruct((n, ROW_W), F32),
        compiler_params=_params(("parallel",)),
    )(rows, pe_big, w1big, w2big)


def _split_compressed(cmp_out, batch):
    nb = cmp_out.shape[0] // batch
    x = cmp_out.reshape(batch, nb // 2, 2, 2, NSA_KV, NSA_DH)
    x = jnp.transpose(x, (3, 0, 4, 2, 1, 5)).reshape(2, batch, NSA_KV, nb, NSA_DH)
    return x[0].astype(BF16), x[1].astype(BF16)


CMP_PAGES_PER_STEP = 32
SLC_PAGES_PER_STEP = 8
BLOCKS_PER_PAGE = PAGE_SIZE // NSA_CMP


def _page_specs(block, n_pages, per_step):
    def spec(j):
        return pl.BlockSpec(block, lambda b, s, pt: (pt[b * n_pages + s * per_step + j], 0, 0))
    return [spec(j) for j in range(per_step)]


CMP_ROW_PITCH = NSA_CMP + 4


def _paged_compress_kernel(pt_ref, *refs):
    n_in = CMP_PAGES_PER_STEP
    pe_ref, w1_ref, w2_ref, o_ref, x_ref = refs[n_in:]
    m = n_in * BLOCKS_PER_PAGE
    n_slabs = ROW_W // LANES
    for j, p_ref in enumerate(refs[:n_in]):
        x = p_ref[0].T
        for n in range(BLOCKS_PER_PAGE):
            base = (j * BLOCKS_PER_PAGE + n) * CMP_ROW_PITCH
            for sl in range(n_slabs):
                x_ref[sl, base:base + NSA_CMP, :] = x[n * NSA_CMP:(n + 1) * NSA_CMP,
                                                      sl * LANES:(sl + 1) * LANES]
    acc = jnp.zeros((m, ROW_W), F32)
    for r in range(NSA_CMP):
        lhs = jnp.concatenate([x_ref[sl, pl.ds(r, m, stride=CMP_ROW_PITCH), :]
                               for sl in range(n_slabs)], axis=1)
        lhs = (lhs + pe_ref[r:r + 1, :]).astype(BF16)
        acc = acc + jnp.dot(lhs, w1_ref[r], preferred_element_type=F32)
    o_ref[...] = jnp.dot(_gelu(acc).astype(BF16), w2_ref[...], preferred_element_type=F32)


def _paged_compress(cache_t, page_table, pe_big, w1big, w2big):
    batch, n_pages = page_table.shape
    rows = CMP_PAGES_PER_STEP * BLOCKS_PER_PAGE
    steps = n_pages // CMP_PAGES_PER_STEP
    pe_rows = pe_big.reshape(NSA_CMP, ROW_W)
    w1_rows = w1big.reshape(NSA_CMP, ROW_W, ROW_W)
    const = lambda a: pl.BlockSpec(a.shape, lambda b, s, pt: (0,) * a.ndim)
    return pl.pallas_call(
        _paged_compress_kernel,
        grid_spec=pltpu.PrefetchScalarGridSpec(
            num_scalar_prefetch=1, grid=(batch, steps),
            in_specs=_page_specs((1, ROW_W, PAGE_SIZE), n_pages, CMP_PAGES_PER_STEP)
            + [const(pe_rows), const(w1_rows), const(w2big)],
            out_specs=pl.BlockSpec((rows, ROW_W), lambda b, s, pt: (b * steps + s, 0)),
            scratch_shapes=[pltpu.VMEM((ROW_W // LANES, rows * CMP_ROW_PITCH, LANES), F32)]),
        out_shape=jax.ShapeDtypeStruct((batch * n_pages * BLOCKS_PER_PAGE, ROW_W), F32),
        compiler_params=_params(("parallel", "parallel")),
    )(page_table.reshape(-1), *([cache_t] * CMP_PAGES_PER_STEP), pe_rows, w1_rows, w2big)


def _topk_mask(score, k):
    n = score.shape[-1]
    lane = lax.broadcasted_iota(jnp.int32, score.shape, 1).astype(F32)
    sel = jnp.zeros(score.shape, F32)
    for _ in range(k):
        m = jnp.max(score, axis=-1, keepdims=True)
        idx = jnp.min(jnp.where(score == m, lane, float(n)), axis=-1, keepdims=True)
        pick = lane == idx
        sel = jnp.where(pick, 1.0, sel)
        score = jnp.where(pick, -jnp.inf, score)
    return sel


def _softmax_step(carry, s, v):
    m, l, acc = carry
    m_new = jnp.maximum(m, jnp.max(s, axis=-1, keepdims=True))
    m_safe = jnp.where(m_new == -jnp.inf, 0.0, m_new)
    p = jnp.exp(s - m_safe)
    alpha = jnp.exp(m - m_safe)
    l = alpha * l + jnp.sum(p, axis=-1, keepdims=True)
    acc = alpha * acc + _dot(p, v)
    return m_new, l, acc


def _nsa_attn_kernel(*refs, tq, pos0, wrel0, n_keys, tk, has_tail):
    if has_tail:
        (qc_ref, qr_ref, kc_ref, vc_ref, ks_ref, vs_ref, kw_ref, vw_ref, kt_ref, vt_ref,
         sm_ref, o_ref) = refs
    else:
        (qc_ref, qr_ref, kc_ref, vc_ref, ks_ref, vs_ref, kw_ref, vw_ref, sm_ref, o_ref) = refs
    hp = NSA_HEADS // NSA_KV
    r = hp * tq
    g = pl.program_id(1)
    qi = pl.program_id(2)
    ncp = kc_ref.shape[2]
    nch = ncp // 2
    nsl = -(-(n_keys // NSA_SEL + (1 if has_tail else 0)) // LANES) * LANES

    def stack_heads(ref):
        x = ref[...]
        return jnp.concatenate([x[:, h * NSA_DH:(h + 1) * NSA_DH] for h in range(hp)],
                               axis=0).astype(BF16)

    qc4 = stack_heads(qc_ref)
    qr4 = stack_heads(qr_ref)
    t_q = pos0 + qi * tq + lax.broadcasted_iota(jnp.int32, (tq, 1), 0)
    t_row = jnp.concatenate([t_q] * hp, axis=0)

    s = _dot_nt(qc4, kc_ref[0, 0])
    ccol = lax.broadcasted_iota(jnp.int32, (r, ncp), 1)
    cblk = 2 * (ccol % nch) + ccol // nch
    s = jnp.where((cblk + 1) * NSA_CMP <= t_row + 1, s, -jnp.inf)
    m = jnp.max(s, axis=-1, keepdims=True)
    e = jnp.exp(s - jnp.where(m == -jnp.inf, 0.0, m))
    p_cmp = e / jnp.maximum(jnp.sum(e, axis=-1, keepdims=True), 1e-30)
    o_cmp = _dot(p_cmp, vc_ref[0, 0])

    imp = p_cmp[0:tq]
    for h in range(1, hp):
        imp = imp + p_cmp[h * tq:(h + 1) * tq]
    imp = imp[:, :nch] + imp[:, nch:]
    if nsl > nch:
        imp = jnp.concatenate([imp, jnp.zeros((tq, nsl - nch), F32)], axis=1)
    blk = lax.broadcasted_iota(jnp.int32, (tq, nsl), 1)
    cur = t_q // NSA_SEL
    forced = (blk == 0) | (blk == cur) | (blk == cur - 1)
    score = jnp.where(blk > cur, -jnp.inf, jnp.where(forced, NSA_FORCE, imp))
    sel = _topk_mask(score, NSA_TOPN).astype(BF16)

    init = (jnp.full((r, 1), -jnp.inf, F32), jnp.zeros((r, 1), F32), jnp.zeros((r, NSA_DH), F32))

    def slc_scores(k, kpos0, width):
        kpos = kpos0 + lax.broadcasted_iota(jnp.int32, (1, width), 1)
        expand = (lax.broadcasted_iota(jnp.int32, (nsl, width), 0) == kpos // NSA_SEL)
        picked = jnp.dot(sel, expand.astype(BF16), preferred_element_type=F32)
        picked = jnp.concatenate([picked] * hp, axis=0)
        return jnp.where((picked > 0.5) & (kpos <= t_row), _dot_nt(qr4, k), -jnp.inf)

    def slc_body(kt, carry):
        start = pl.multiple_of(kt * tk, tk)
        k = ks_ref[0, 0, pl.ds(start, tk), :]
        v = vs_ref[0, 0, pl.ds(start, tk), :]
        return _softmax_step(carry, slc_scores(k, kt * tk, tk), v)

    t_last = pos0 + qi * tq + tq - 1
    n_main = jnp.minimum(n_keys // tk, t_last // tk + 1)
    carry = lax.fori_loop(0, n_main, slc_body, init)
    if has_tail:
        carry = _softmax_step(carry, slc_scores(kt_ref[0, 0], n_keys, kt_ref.shape[2]), vt_ref[0, 0])
    o_slc = carry[2] / jnp.maximum(carry[1], 1e-30)

    wbase = pos0 - wrel0

    def win_body(kt, carry):
        start = pl.multiple_of(kt * tk, tk)
        k = kw_ref[0, 0, pl.ds(start, tk), :]
        v = vw_ref[0, 0, pl.ds(start, tk), :]
        kpos = wbase + kt * tk + lax.broadcasted_iota(jnp.int32, (1, tk), 1)
        rel = t_row - kpos
        ok = (rel >= 0) & (rel < NSA_WIN) & (kpos >= 0)
        return _softmax_step(carry, jnp.where(ok, _dot_nt(qr4, k), -jnp.inf), v)

    w_first = jnp.maximum(pos0 + qi * tq - (NSA_WIN - 1) - wbase, 0) // tk
    w_last = jnp.minimum((t_last - wbase) // tk, kw_ref.shape[2] // tk - 1)
    carry = lax.fori_loop(w_first, w_last + 1, win_body, init)
    o_win = carry[2] / jnp.maximum(carry[1], 1e-30)

    sig = jax.nn.sigmoid(sm_ref[...])
    gw = N_BRANCH * hp
    gates = jnp.where(g == 0, sig[:, SM_NG:SM_NG + gw], sig[:, SM_NG + gw:SM_NG + 2 * gw])
    outs = []
    for h in range(hp):
        rows = slice(h * tq, (h + 1) * tq)
        outs.append(gates[:, 3 * h:3 * h + 1] * o_cmp[rows]
                    + gates[:, 3 * h + 1:3 * h + 2] * o_slc[rows]
                    + gates[:, 3 * h + 2:3 * h + 3] * o_win[rows])
    o_ref[...] = jnp.concatenate(outs, axis=1)


def _nsa_attn(qc, qr, kc, vc, ks, vs, kw, vw, tails, z, batch, t, tq, pos0, wrel0, tk):
    nq = t // tq
    n_keys = ks.shape[2]
    gw = NSA_HEADS * NSA_DH // NSA_KV
    rowblk = lambda b, g, i: b * nq + i
    full = lambda a: pl.BlockSpec((1, 1) + a.shape[2:], lambda b, g, i: (b, g, 0, 0))
    q_spec = pl.BlockSpec((tq, gw), lambda b, g, i: (rowblk(b, g, i), g))
    operands = [qc, qr, kc, vc, ks, vs, kw, vw] + list(tails)
    in_specs = [q_spec, q_spec] + [full(a) for a in operands[2:]]
    in_specs.append(pl.BlockSpec((tq, LANES), lambda b, g, i: (rowblk(b, g, i), Z_SMALL // LANES)))
    return pl.pallas_call(
        functools.partial(_nsa_attn_kernel, tq=tq, pos0=pos0, wrel0=wrel0, n_keys=n_keys, tk=tk,
                          has_tail=bool(tails)),
        grid=(batch, NSA_KV, nq),
        in_specs=in_specs,
        out_specs=pl.BlockSpec((tq, gw), lambda b, g, i: (rowblk(b, g, i), g)),
        out_shape=jax.ShapeDtypeStruct((batch * t, NSA_HEADS * NSA_DH), F32),
        compiler_params=_params(("parallel", "parallel", "arbitrary")),
    )(*operands, z)


MASKED = -1e30


def _topk_mask_rows(score, k):
    n = score.shape[0]
    row = lax.broadcasted_iota(jnp.int32, score.shape, 0).astype(F32)
    sel = jnp.zeros(score.shape, F32)
    for _ in range(k):
        m = jnp.max(score, axis=0, keepdims=True)
        idx = jnp.min(jnp.where(score == m, row, float(n)), axis=0, keepdims=True)
        pick = row == idx
        sel = jnp.where(pick, 1.0, sel)
        score = jnp.where(pick, -jnp.inf, score)
    return sel


def _softmax_step_cols(carry, s, vt):
    m, l, acc = carry
    m_new = jnp.maximum(m, jnp.max(s, axis=0, keepdims=True))
    p = jnp.exp(s - m_new)
    alpha = jnp.exp(m - m_new)
    l = alpha * l + jnp.sum(p, axis=0, keepdims=True)
    acc = alpha * acc + _dot(vt, p)
    return m_new, l, acc


def _nsa_prompt_kernel(qc_ref, qr_ref, kc_ref, vct_ref, ks_ref, vst_ref, kw_ref, vwt_ref, sm_ref,
                       o_ref, bias_ref, *, tq, tk, tkw):
    hp = NSA_HEADS // NSA_KV
    r = hp * tq
    g = pl.program_id(1)
    qi = pl.program_id(2)
    q0 = qi * tq
    ncp = kc_ref.shape[2]
    nch = ncp // 2
    nsl = bias_ref.shape[0]
    per_tile = tk // NSA_SEL

    def heads_on_lanes(ref):
        xt = ref[...].T
        return jnp.concatenate([xt[h * NSA_DH:(h + 1) * NSA_DH] for h in range(hp)],
                               axis=1).astype(BF16)

    qct = heads_on_lanes(qc_ref)
    qrt = heads_on_lanes(qr_ref)
    t_q = q0 + lax.broadcasted_iota(jnp.int32, (1, tq), 1)
    t_lane = jnp.concatenate([t_q] * hp, axis=1)

    s = _dot(kc_ref[0, 0], qct)
    crow = lax.broadcasted_iota(jnp.int32, (ncp, 1), 0)
    cblk = 2 * (crow % nch) + crow // nch
    s = jnp.where((cblk + 1) * NSA_CMP <= t_lane + 1, s, -jnp.inf)
    m = jnp.max(s, axis=0, keepdims=True)
    e = jnp.exp(s - jnp.where(m == -jnp.inf, 0.0, m))
    p_cmp = e / jnp.maximum(jnp.sum(e, axis=0, keepdims=True), 1e-30)
    o_cmp = _dot(vct_ref[0, 0], p_cmp)

    imp = p_cmp[:, 0:tq]
    for h in range(1, hp):
        imp = imp + p_cmp[:, h * tq:(h + 1) * tq]
    imp = imp[:nch] + imp[nch:]
    blk = lax.broadcasted_iota(jnp.int32, (nsl, tq), 0)
    cur = t_q // NSA_SEL
    forced = (blk == 0) | (blk == cur) | (blk == cur - 1)
    score = jnp.where(blk > cur, -jnp.inf, jnp.where(forced, NSA_FORCE, imp))
    bias = (_topk_mask_rows(score, NSA_TOPN) - 1.0) * (-MASKED)
    bias_ref[...] = jnp.concatenate([bias] * hp, axis=1)

    init = (jnp.full((1, r), MASKED, F32), jnp.zeros((1, r), F32), jnp.zeros((NSA_DH, r), F32))

    def slc_scores(kt):
        start = pl.multiple_of(kt * tk, tk)
        sc = _dot(ks_ref[0, 0, pl.ds(start, tk), :], qrt)
        brow = bias_ref[pl.ds(pl.multiple_of(kt * per_tile, per_tile), per_tile), :]
        sc = jnp.concatenate([sc[j * NSA_SEL:(j + 1) * NSA_SEL] + brow[j:j + 1]
                              for j in range(per_tile)], axis=0)
        return sc, vst_ref[0, 0, :, pl.ds(start, tk)]

    def slc_body(kt, carry):
        sc, vt = slc_scores(kt)
        return _softmax_step_cols(carry, sc, vt)

    kd = q0 // tk
    carry = lax.fori_loop(0, kd, slc_body, init)
    sc, vt = slc_scores(kd)
    kpos = kd * tk + lax.broadcasted_iota(jnp.int32, (tk, 1), 0)
    carry = _softmax_step_cols(carry, jnp.where(kpos <= t_lane, sc, MASKED), vt)
    o_slc = carry[2] / carry[1]

    def win_body(kt, carry):
        start = pl.multiple_of(kt * tkw, tkw)
        sc = _dot(kw_ref[0, 0, pl.ds(start, tkw), :], qrt)
        rel = t_lane - (kt * tkw + lax.broadcasted_iota(jnp.int32, (tkw, 1), 0))
        sc = jnp.where((rel >= 0) & (rel < NSA_WIN), sc, MASKED)
        return _softmax_step_cols(carry, sc, vwt_ref[0, 0, :, pl.ds(start, tkw)])

    w_first = jnp.maximum(q0 - (NSA_WIN - 1), 0) // tkw
    carry = lax.fori_loop(w_first, (q0 + tq - 1) // tkw + 1, win_body, init)
    o_win = carry[2] / carry[1]

    sig = jax.nn.sigmoid(sm_ref[...].T)
    gw = N_BRANCH * hp
    gates = jnp.where(g == 0, sig[SM_NG:SM_NG + gw], sig[SM_NG + gw:SM_NG + 2 * gw])
    outs = []
    for h in range(hp):
        cols = slice(h * tq, (h + 1) * tq)
        outs.append(gates[3 * h:3 * h + 1] * o_cmp[:, cols]
                    + gates[3 * h + 1:3 * h + 2] * o_slc[:, cols]
                    + gates[3 * h + 2:3 * h + 3] * o_win[:, cols])
    o_ref[...] = jnp.concatenate(outs, axis=0).T


def _nsa_prompt_attn(qc, qr, kc, vct, ks, vst, kw, vwt, z, batch, t, tq, tk, tkw):
    nq = t // tq
    gw = NSA_HEADS * NSA_DH // NSA_KV
    assert tk % (SUBLANES * NSA_SEL) == 0 and tq == tkw and tk % tq == 0 and t % tk == 0
    rowblk = lambda b, g, i: b * nq + i
    full = lambda a: pl.BlockSpec((1, 1) + a.shape[2:], lambda b, g, i: (b, g, 0, 0))
    q_spec = pl.BlockSpec((tq, gw), lambda b, g, i: (rowblk(b, g, i), g))
    return pl.pallas_call(
        functools.partial(_nsa_prompt_kernel, tq=tq, tk=tk, tkw=tkw),
        grid=(batch, NSA_KV, nq),
        in_specs=[q_spec, q_spec] + [full(a) for a in (kc, vct, ks, vst, kw, vwt)]
        + [pl.BlockSpec((tq, LANES), lambda b, g, i: (rowblk(b, g, i), Z_SMALL // LANES))],
        out_specs=pl.BlockSpec((tq, gw), lambda b, g, i: (rowblk(b, g, i), g)),
        out_shape=jax.ShapeDtypeStruct((batch * t, NSA_HEADS * NSA_DH), F32),
        scratch_shapes=[pltpu.VMEM((t // NSA_SEL, NSA_HEADS // NSA_KV * tq), F32)],
        compiler_params=_params(("parallel", "parallel", "arbitrary")),
    )(qc, qr, kc, vct, ks, vst, kw, vwt, z)


def _softmax_step_vt(carry, s, vt):
    m, l, acc = carry
    m_new = jnp.maximum(m, jnp.max(s, axis=-1, keepdims=True))
    m_safe = jnp.where(m_new == -jnp.inf, 0.0, m_new)
    p = jnp.exp(s - m_safe)
    alpha = jnp.exp(m - m_safe)
    l = alpha * l + jnp.sum(p, axis=-1, keepdims=True)
    acc = alpha * acc + _dot_nt(p, vt)
    return m_new, l, acc


def _nsa_decode_kernel(pt_ref, *refs, tq, pos0, wrel0, n_keys):
    n_in = SLC_PAGES_PER_STEP
    qc_ref, qr_ref, kc_ref, vc_ref = refs[0:4]
    pages = refs[4:4 + n_in]
    (kw_ref, vw_ref, kt_ref, vt_ref, sm_ref, o_ref,
     m_ref, l_ref, acc_ref, ocmp_ref, sel_ref) = refs[4 + n_in:]
    hp = NSA_HEADS // NSA_KV
    r = hp * tq
    gwid = hp * NSA_DH
    step = pl.program_id(1)
    n_steps = pl.num_programs(1)
    tk = n_in * PAGE_SIZE
    per_tile = tk // NSA_SEL
    n_tiles = sel_ref.shape[1]
    ncp = kc_ref.shape[2]
    nch = ncp // 2
    nsl = -(-(n_tiles * per_tile) // LANES) * LANES
    kv_w = NSA_KV * NSA_DH

    def stack_heads(ref, g):
        x = ref[:, g * gwid:(g + 1) * gwid]
        return jnp.concatenate([x[:, h * NSA_DH:(h + 1) * NSA_DH] for h in range(hp)],
                               axis=0).astype(BF16)

    t_q = pos0 + lax.broadcasted_iota(jnp.int32, (tq, 1), 0)
    t_row = jnp.concatenate([t_q] * hp, axis=0)
    expand = (lax.broadcasted_iota(jnp.int32, (per_tile, tk), 0)
              == lax.broadcasted_iota(jnp.int32, (per_tile, tk), 1) // NSA_SEL).astype(BF16)

    def picked_rows(g, tile, width):
        pk = jnp.dot(sel_ref[g, tile].astype(BF16), expand[:, :width], preferred_element_type=F32)
        return jnp.concatenate([pk] * hp, axis=0)

    @pl.when(step == 0)
    def _():
        for g in range(NSA_KV):
            s = _dot_nt(stack_heads(qc_ref, g), kc_ref[0, g])
            ccol = lax.broadcasted_iota(jnp.int32, (r, ncp), 1)
            cblk = 2 * (ccol % nch) + ccol // nch
            s = jnp.where((cblk + 1) * NSA_CMP <= t_row + 1, s, -jnp.inf)
            m = jnp.max(s, axis=-1, keepdims=True)
            e = jnp.exp(s - jnp.where(m == -jnp.inf, 0.0, m))
            p_cmp = e / jnp.maximum(jnp.sum(e, axis=-1, keepdims=True), 1e-30)
            ocmp_ref[g] = _dot(p_cmp, vc_ref[0, g])
            imp = p_cmp[0:tq]
            for h in range(1, hp):
                imp = imp + p_cmp[h * tq:(h + 1) * tq]
            imp = imp[:, :nch] + imp[:, nch:]
            if nsl > nch:
                imp = jnp.concatenate([imp, jnp.zeros((tq, nsl - nch), F32)], axis=1)
            blk = lax.broadcasted_iota(jnp.int32, (tq, nsl), 1)
            cur = t_q // NSA_SEL
            forced = (blk == 0) | (blk == cur) | (blk == cur - 1)
            score = jnp.where(blk > cur, -jnp.inf, jnp.where(forced, NSA_FORCE, imp))
            sel = _topk_mask(score, NSA_TOPN)
            for j in range(n_tiles):
                sel_ref[g, j] = sel[:, j * per_tile:(j + 1) * per_tile]
            m_ref[g] = jnp.full((r, 1), -jnp.inf, F32)
            l_ref[g] = jnp.zeros((r, 1), F32)
            acc_ref[g] = jnp.zeros((r, NSA_DH), F32)

    kpos = step * tk + lax.broadcasted_iota(jnp.int32, (1, tk), 1)
    for g in range(NSA_KV):
        qr4 = stack_heads(qr_ref, g)
        k_t = jnp.concatenate([p[0, g * NSA_DH:(g + 1) * NSA_DH, :] for p in pages], axis=1)
        v_t = jnp.concatenate([p[0, kv_w + g * NSA_DH:kv_w + (g + 1) * NSA_DH, :] for p in pages],
                              axis=1)
        ok = (picked_rows(g, step, tk) > 0.5) & (kpos <= t_row)
        s = jnp.where(ok, _dot(qr4, k_t), -jnp.inf)
        m, l, acc = _softmax_step_vt((m_ref[g], l_ref[g], acc_ref[g]), s, v_t)
        m_ref[g] = m
        l_ref[g] = l
        acc_ref[g] = acc

    @pl.when(step == n_steps - 1)
    def _():
        sig = jax.nn.sigmoid(sm_ref[...])
        wbase = pos0 - wrel0
        tkw = LANES
        for g in range(NSA_KV):
            qr4 = stack_heads(qr_ref, g)
            tw = kt_ref.shape[2]
            tpos = n_keys + lax.broadcasted_iota(jnp.int32, (1, tw), 1)
            ok = (picked_rows(g, n_tiles - 1, tw) > 0.5) & (tpos <= t_row)
            s = jnp.where(ok, _dot_nt(qr4, kt_ref[0, g]), -jnp.inf)
            carry = _softmax_step((m_ref[g], l_ref[g], acc_ref[g]), s, vt_ref[0, g])
            o_slc = carry[2] / jnp.maximum(carry[1], 1e-30)
            carry = (jnp.full((r, 1), -jnp.inf, F32), jnp.zeros((r, 1), F32),
                     jnp.zeros((r, NSA_DH), F32))
            for kt in range(kw_ref.shape[2] // tkw):
                wpos = wbase + kt * tkw + lax.broadcasted_iota(jnp.int32, (1, tkw), 1)
                rel = t_row - wpos
                ok = (rel >= 0) & (rel < NSA_WIN) & (wpos >= 0)
                s = jnp.where(ok, _dot_nt(qr4, kw_ref[0, g, kt * tkw:(kt + 1) * tkw, :]), -jnp.inf)
                carry = _softmax_step(carry, s, vw_ref[0, g, kt * tkw:(kt + 1) * tkw, :])
            o_win = carry[2] / jnp.maximum(carry[1], 1e-30)
            o_cmp = ocmp_ref[g]
            base = SM_NG + g * N_BRANCH * hp
            outs = []
            for h in range(hp):
                rows = slice(h * tq, (h + 1) * tq)
                c0 = base + N_BRANCH * h
                outs.append(sig[:, c0:c0 + 1] * o_cmp[rows] + sig[:, c0 + 1:c0 + 2] * o_slc[rows]
                            + sig[:, c0 + 2:c0 + 3] * o_win[rows])
            o_ref[:, g * gwid:(g + 1) * gwid] = jnp.concatenate(outs, axis=1)


def _nsa_decode_attn(qc, qr, kc, vc, cache_t, page_table, kw, vw, ktail, vtail, z, tq, pos0, wrel0):
    batch, n_pages = page_table.shape
    n_keys = n_pages * PAGE_SIZE
    steps = n_pages // SLC_PAGES_PER_STEP
    hp = NSA_HEADS // NSA_KV
    r = hp * tq
    qw = NSA_HEADS * NSA_DH
    per_tile = SLC_PAGES_PER_STEP * PAGE_SIZE // NSA_SEL
    assert pos0 == n_keys and ktail.shape[2] <= per_tile * NSA_SEL
    per_b = lambda a: pl.BlockSpec((1,) + a.shape[1:], lambda b, s, pt: (b, 0, 0, 0))
    q_spec = pl.BlockSpec((tq, qw), lambda b, s, pt: (b, 0))
    return pl.pallas_call(
        functools.partial(_nsa_decode_kernel, tq=tq, pos0=pos0, wrel0=wrel0, n_keys=n_keys),
        grid_spec=pltpu.PrefetchScalarGridSpec(
            num_scalar_prefetch=1, grid=(batch, steps),
            in_specs=[q_spec, q_spec, per_b(kc), per_b(vc)]
            + _page_specs((1, ROW_W, PAGE_SIZE), n_pages, SLC_PAGES_PER_STEP)
            + [per_b(kw), per_b(vw), per_b(ktail), per_b(vtail),
               pl.BlockSpec((tq, LANES), lambda b, s, pt: (b, Z_SMALL // LANES))],
            out_specs=pl.BlockSpec((tq, qw), lambda b, s, pt: (b, 0)),
            scratch_shapes=[pltpu.VMEM((NSA_KV, r, 1), F32), pltpu.VMEM((NSA_KV, r, 1), F32),
                            pltpu.VMEM((NSA_KV, r, NSA_DH), F32),
                            pltpu.VMEM((NSA_KV, r, NSA_DH), F32),
                            pltpu.VMEM((NSA_KV, steps + 1, tq, per_tile), F32)]),
        out_shape=jax.ShapeDtypeStruct((batch * tq, qw), F32),
        compiler_params=_params(("parallel", "arbitrary")),
    )(page_table.reshape(-1), qc, qr, kc, vc, *([cache_t] * SLC_PAGES_PER_STEP),
      kw, vw, ktail, vtail, z)


def _mem_attn_kernel(q_ref, kv_ref, o_ref):
    hw = MEM_HEADS * MEM_DH
    for h in range(MEM_HEADS):
        q = q_ref[:, h * MEM_DH:(h + 1) * MEM_DH] * (MEM_DH ** -0.5)
        k = kv_ref[0, :, h * MEM_DH:(h + 1) * MEM_DH]
        v = kv_ref[0, :, hw + h * MEM_DH:hw + (h + 1) * MEM_DH]
        s = _dot_nt(q, k)
        e = jnp.exp(s - jnp.max(s, axis=-1, keepdims=True))
        p = e / jnp.sum(e, axis=-1, keepdims=True)
        o_ref[:, h * MEM_DH:(h + 1) * MEM_DH] = _dot(p, v)


def _mem_attn(z, kv, batch, t, tm):
    nt = t // tm
    hw = MEM_HEADS * MEM_DH
    return pl.pallas_call(
        _mem_attn_kernel,
        grid=(batch, nt),
        in_specs=[pl.BlockSpec((tm, hw), lambda b, j: (b * nt + j, Z_MQ // hw)),
                  pl.BlockSpec((1,) + kv.shape[1:], lambda b, j: (b, 0, 0))],
        out_specs=pl.BlockSpec((tm, hw), lambda b, j: (b * nt + j, 0)),
        out_shape=jax.ShapeDtypeStruct((batch * t, hw), F32),
        compiler_params=_params(("parallel", "parallel")),
    )(z, kv)


def _merge_kernel(x_ref, mg_ref, og_ref, on_ref, om_ref, wg_ref, wn_ref, wm_ref, wo_ref,
                  gf_ref, wq_ref, x1_ref, h2_ref, qp_ref):
    d = D_MODEL
    mix = (jax.nn.sigmoid(mg_ref[:, 0:d]) * _dot(og_ref[...], wg_ref[...])
           + jax.nn.sigmoid(mg_ref[:, d:2 * d]) * _dot(on_ref[...], wn_ref[...])
           + jax.nn.sigmoid(mg_ref[:, 2 * d:3 * d]) * _dot(om_ref[...], wm_ref[...]))
    x1 = x_ref[...] + _dot(mix, wo_ref[...])
    x1_ref[...] = x1
    h2 = _rms(x1, gf_ref[...]).astype(BF16)
    h2_ref[...] = h2
    qp_ref[...] = jnp.dot(h2, wq_ref[...], preferred_element_type=F32)


def _merge(x, z, o_gdn, o_nsa, o_mem, wg, wn, wm, wo, norm_ffn, wq, tm):
    n, d = x.shape
    qw = wq.shape[1]
    row = lambda w: pl.BlockSpec((tm, w), lambda i: (i, 0))
    const = lambda a: pl.BlockSpec(a.shape, lambda i: (0, 0))
    return pl.pallas_call(
        _merge_kernel,
        grid=(n // tm,),
        in_specs=[row(d), pl.BlockSpec((tm, N_BRANCH * d), lambda i: (i, Z_MG)),
                  row(o_gdn.shape[1]), row(o_nsa.shape[1]), row(o_mem.shape[1]),
                  const(wg), const(wn), const(wm), const(wo),
                  pl.BlockSpec((1, d), lambda i: (0, 0)), const(wq)],
        out_specs=[row(d), row(d), row(qw)],
        out_shape=[jax.ShapeDtypeStruct((n, d), F32), jax.ShapeDtypeStruct((n, d), BF16),
                   jax.ShapeDtypeStruct((n, qw), F32)],
        compiler_params=_params(("parallel",)),
    )(x, z, o_gdn, o_nsa, o_mem, wg, wn, wm, wo, norm_ffn.reshape(1, d), wq)


PEER_RANKS = PEER_TOPK + 1


RANK_NONE = 64.0


def _top_values(s, n):
    vals = []
    rank = jnp.full(s.shape, RANK_NONE, F32)
    for k in range(n):
        m = jnp.max(s, axis=0, keepdims=True)
        vals.append(m)
        hit = s >= m
        rank = jnp.where(hit, float(k + 1), rank)
        s = jnp.where(hit, -jnp.inf, s)
    return vals, rank


def _peer_route_kernel(qp_ref, sk_ref, r2_ref, e2_ref, nb_ref, e1_ref):
    half = PEER_DKEY // 2
    nt = (((1,), (1,)), ((), ()))
    for h in range(PEER_HEADS):
        qa = qp_ref[:, h * PEER_DKEY:h * PEER_DKEY + half]
        qb = qp_ref[:, h * PEER_DKEY + half:(h + 1) * PEER_DKEY]
        s1 = lax.dot_general(sk_ref[0], qa, nt, precision=HIGHEST, preferred_element_type=F32)
        s2 = lax.dot_general(sk_ref[1], qb, nt, precision=HIGHEST, preferred_element_type=F32)
        a, _ = _top_values(s1, PEER_RANKS)
        b, rank2 = _top_values(s2, PEER_RANKS)
        b_rows = jnp.concatenate(b, axis=0)
        cands = [a[i - 1] + b_rows[0:PEER_RANKS // i] for i in range(1, PEER_RANKS + 1)]
        work = cands
        ranked = []
        for _ in range(PEER_RANKS):
            m = work[0][0:1]
            for cnd in work:
                m = jnp.maximum(m, jnp.max(cnd, axis=0, keepdims=True))
            ranked.append(m)
            work = [jnp.where(cnd >= m, -jnp.inf, cnd) for cnd in work]
        tau = 0.5 * (ranked[PEER_TOPK - 1] + ranked[PEER_TOPK])
        top = a[0] + b[0]
        zsum = jnp.zeros_like(tau)
        for cnd in cands:
            zsum = zsum + jnp.sum(jnp.where(cnd >= tau, jnp.exp(cnd - top), 0.0),
                                  axis=0, keepdims=True)
        th = tau - s1
        count = jnp.zeros(s1.shape, F32)
        for bj in b:
            count = count + jnp.where(bj >= th, 1.0, 0.0)
        r2_ref[h] = rank2.astype(BF16)
        e2_ref[h] = jnp.exp(s2 - b[0]).astype(BF16)
        nb_ref[h] = count
        e1_ref[h] = jnp.exp(s1 - a[0]) / zsum


def _peer_route(qp, subkeys, tt):
    n = qp.shape[0]
    shape = lambda dt: jax.ShapeDtypeStruct((PEER_HEADS, PEER_NKEYS, n), dt)
    spec = pl.BlockSpec((PEER_HEADS, PEER_NKEYS, tt), lambda i: (0, 0, i))
    return pl.pallas_call(
        _peer_route_kernel,
        grid=(n // tt,),
        in_specs=[pl.BlockSpec((tt, qp.shape[1]), lambda i: (i, 0)),
                  pl.BlockSpec(subkeys.shape, lambda i: (0, 0, 0))],
        out_specs=[spec, spec, spec, spec],
        out_shape=[shape(BF16), shape(BF16), shape(F32), shape(F32)],
        compiler_params=_params(("parallel",)),
    )(qp, subkeys)


def _peer_dense_kernel(ht_ref, u_ref, vt_ref, r2_ref, e2_ref, nb_ref, e1_ref, x1_ref, gf_ref,
                       y_ref, acc_ref, act_ref, wa_ref, r2s_ref, e2s_ref, *, jb):
    j = pl.program_id(1)
    tt = ht_ref.shape[1]

    @pl.when(j == 0)
    def _():
        acc_ref[...] = jnp.zeros(acc_ref.shape, F32)
        r2s_ref[...] = r2_ref[...]
        e2s_ref[...] = e2_ref[...]

    act_ref[...] = _gelu(jnp.dot(u_ref[...], ht_ref[...],
                                 preferred_element_type=F32)).astype(BF16)
    for jj in range(jb):
        i1 = j * jb + jj
        rows = slice(jj * PEER_NKEYS, (jj + 1) * PEER_NKEYS)
        nb_rows = [nb_ref[h, pl.ds(i1, 1), :].astype(BF16) for h in range(PEER_HEADS)]
        e1_rows = [e1_ref[h, pl.ds(i1, 1), :].astype(BF16) for h in range(PEER_HEADS)]
        for c in range(tt // LANES):
            cols = slice(c * LANES, (c + 1) * LANES)
            w = None
            for h in range(PEER_HEADS):
                picked = r2s_ref[h, :, cols] <= nb_rows[h][:, cols]
                term = jnp.where(picked, e2s_ref[h, :, cols], jnp.zeros((), BF16)) * e1_rows[h][:, cols]
                w = term if w is None else w + term
            wa_ref[rows, cols] = w * act_ref[rows, cols]
    acc_ref[...] += jnp.dot(vt_ref[...], wa_ref[...], preferred_element_type=F32)

    @pl.when(j == pl.num_programs(1) - 1)
    def _():
        y_ref[...] = _rms(x1_ref[...] + acc_ref[...].T, gf_ref[...])


def _peer_dense(ht, u, vt, s2, e2, th, e1, x1, norm_final, tt, jb):
    d, n = ht.shape
    n_exp = u.shape[0]
    eb = jb * PEER_NKEYS
    route = pl.BlockSpec((PEER_HEADS, PEER_NKEYS, tt), lambda t, j: (0, 0, t))
    return pl.pallas_call(
        functools.partial(_peer_dense_kernel, jb=jb),
        grid=(n // tt, n_exp // eb),
        in_specs=[pl.BlockSpec((d, tt), lambda t, j: (0, t)),
                  pl.BlockSpec((eb, d), lambda t, j: (j, 0)),
                  pl.BlockSpec((d, eb), lambda t, j: (0, j)),
                  route, route, route, route,
                  pl.BlockSpec((tt, d), lambda t, j: (t, 0)),
                  pl.BlockSpec((1, d), lambda t, j: (0, 0))],
        out_specs=pl.BlockSpec((tt, d), lambda t, j: (t, 0)),
        out_shape=jax.ShapeDtypeStruct((n, d), F32),
        scratch_shapes=[pltpu.VMEM((d, tt), F32), pltpu.VMEM((eb, tt), BF16),
                        pltpu.VMEM((eb, tt), BF16),
                        pltpu.VMEM((PEER_HEADS, PEER_NKEYS, tt), BF16),
                        pltpu.VMEM((PEER_HEADS, PEER_NKEYS, tt), BF16)],
        compiler_params=_params(("parallel", "arbitrary")),
    )(ht, u, vt, s2, e2, th, e1, x1, norm_final.reshape(1, d))


def _permute_w_in(w_in):
    sizes = (GDN_CONV_CH, GDN_HEADS * GDN_DV, GDN_HEADS, GDN_HEADS, NSA_HEADS * NSA_DH, NKV_W,
             N_BRANCH * NSA_HEADS, MEM_HEADS * MEM_DH, N_BRANCH * D_MODEL)
    qkv, zg, a, b, nq, nkv, ng, mq, mg = jnp.split(w_in, np.cumsum(sizes)[:-1].tolist(), axis=1)
    pad = jnp.zeros((w_in.shape[0], LANES - a.shape[1] - b.shape[1] - ng.shape[1]), w_in.dtype)
    return jnp.concatenate([mg, qkv, zg, nq, mq, nkv, a, b, ng, pad], axis=1).astype(BF16)


def _tokens_tile(n, pref):
    return pref if n % pref == 0 else n


def _layer(x, pos0, kv_mem, nsa_keys, gdn_state, conv_buf, w, peer_tt):
    batch, t, d = x.shape
    n = batch * t
    xf = x.reshape(n, d)
    z = _norm_matmul(xf, w['norm_attn'], w['w_in'], _tokens_tile(n, 256), 640)

    o_gdn, s_new, conv_new = _gdn(z, conv_buf, gdn_state, w['gdn_conv'], w['gdn_a_log'],
                                  w['gdn_dt_bias'], w['gdn_norm'], batch, t)

    prep_tm = _tokens_tile(t, 512)
    kv_dtype = BF16 if prep_tm % 16 == 0 else F32
    qc, qr, slc_rows, win_rows, ks, vs, kw, vw = _nsa_prep(
        z, pos0 + jnp.arange(t), batch, t, prep_tm, kv_dtype)
    cmp_rows = z[:, Z_NKV:Z_NKV + ROW_W]
    o_nsa = nsa_keys(z, qc, qr, cmp_rows, ks, vs, kw, vw)

    o_mem = _mem_attn(z, kv_mem, batch, t, _tokens_tile(t, 512))

    x1, h2, qp = _merge(xf, z, o_gdn, o_nsa, o_mem, w['w_gdn_out'], w['w_nsa_out'],
                        w['w_mem_out'], w['w_o'], w['norm_ffn'], w['peer_wq'],
                        _tokens_tile(n, 256))
    s2, e2, th, e1 = _peer_route(qp, w['peer_subkeys'], 256)
    y = _peer_dense(h2.T, w['peer_u'], w['peer_vt'], s2, e2, th, e1, x1, w['norm_final'],
                    peer_tt, 8)
    row5 = lambda a: a.reshape(batch, t, 2, NSA_KV, NSA_DH)
    return (y.reshape(batch, t, d), row5(cmp_rows), row5(slc_rows), row5(win_rows), s_new,
            conv_new)


def kernel(x_prompt, x_sample, cache_mem_kv, cache_cmp_kv, cache_slc_kv, cache_win_kv, state_gdn, state_conv, page_table, mem_prompt, norm_attn, w_in, gdn_conv, gdn_a_log, gdn_dt_bias, gdn_norm, nsa_cmp_pe, nsa_cmp_w1, nsa_cmp_w2, norm_mem, w_mem_kv, w_gdn_out, w_nsa_out, w_mem_out, w_o, norm_ffn, peer_wq, peer_subkeys, peer_u, peer_v, norm_final):
    depth = w_in.shape[0]
    assert depth == 1
    l = 0
    bp, seq, d = x_prompt.shape
    bs, tdec, _ = x_sample.shape
    n_pages = page_table.shape[1]
    past = n_pages * PAGE_SIZE
    assert past % NSA_CMP == 0 and tdec < NSA_CMP and seq % LANES == 0

    pe_big, w1big, w2big = _compress_weights(nsa_cmp_pe[l], nsa_cmp_w1[l], nsa_cmp_w2[l])
    w = dict(norm_attn=norm_attn[l], w_in=_permute_w_in(w_in[l]), gdn_conv=gdn_conv[l],
             gdn_a_log=gdn_a_log[l], gdn_dt_bias=gdn_dt_bias[l], gdn_norm=gdn_norm[l],
             w_gdn_out=w_gdn_out[l].astype(BF16), w_nsa_out=w_nsa_out[l].astype(BF16),
             w_mem_out=w_mem_out[l].astype(BF16), w_o=w_o[l].astype(BF16), norm_ffn=norm_ffn[l],
             peer_wq=peer_wq[l].astype(BF16), peer_subkeys=peer_subkeys[l],
             peer_u=peer_u[l].astype(BF16), peer_vt=peer_v[l].astype(BF16).T,
             norm_final=norm_final)

    mem_n = mem_prompt.shape[0] * mem_prompt.shape[1]
    kvm = _norm_matmul(mem_prompt.reshape(mem_n, d), norm_mem[l], w_mem_kv[l].astype(BF16),
                       _tokens_tile(mem_n, 256), 512).reshape(bp, mem_prompt.shape[1], -1)

    def prompt_keys(z, qc, qr, cmp_rows, ks, vs, kw, vw):
        n_blk = bp * seq // NSA_CMP
        cmp_out = _compress(cmp_rows.reshape(n_blk, NSA_CMP * ROW_W), pe_big, w1big, w2big,
                            _tokens_tile(n_blk, 128))
        kc, vc = _split_compressed(cmp_out, bp)
        tr = lambda a: jnp.transpose(a, (0, 1, 3, 2))
        return _nsa_prompt_attn(qc, qr, kc, tr(vc), ks, tr(vs), kw, tr(vw), z, bp, seq,
                                128, _tokens_tile(seq, 512), 128)

    yp, cmp_p, slc_p, win_p, gdn_p, conv_p = _layer(
        x_prompt, 0, kvm, prompt_keys,
        jnp.zeros((bp, GDN_HEADS, GDN_DK, GDN_DV), F32),
        jnp.zeros((bp, GDN_CONV - 1, GDN_CONV_CH), F32), w, 512)
    win_len_p = min(NSA_WIN, seq)
    win_p = win_p[:, seq - win_len_p:]

    n_pool = cache_cmp_kv.shape[1]
    feature_major = lambda c: jnp.transpose(c, (0, 2, 3, 4, 1)).reshape(n_pool, ROW_W, PAGE_SIZE)
    cache_cmp = feature_major(cache_cmp_kv[l])
    cache_slc = feature_major(cache_slc_kv[l])
    cache_win = cache_win_kv[l].reshape(bs, -1, ROW_W)
    wb = cache_win.shape[1]
    assert wb == NSA_WIN
    kv_w = NSA_KV * NSA_DH

    def pad_rows(a, rows):
        return jnp.pad(a, ((0, 0), (0, 0), (0, rows - a.shape[2]), (0, 0))).astype(BF16)

    def split_rows(rows):
        r = rows.reshape(bs, rows.shape[1], 2, NSA_KV, NSA_DH)
        return jnp.transpose(r[:, :, 0], (0, 2, 1, 3)), jnp.transpose(r[:, :, 1], (0, 2, 1, 3))

    def sample_keys(z, qc, qr, cmp_rows, ks_new, vs_new, kw_new, vw_new):
        cmp_out = _paged_compress(cache_cmp, page_table, pe_big, w1big, w2big)
        kc, vc = _split_compressed(cmp_out, bs)
        kwc, vwc = split_rows(cache_win)
        win_rows = wb + LANES
        kw = pad_rows(jnp.concatenate([kwc, kw_new], axis=2), win_rows)
        vw = pad_rows(jnp.concatenate([vwc, vw_new], axis=2), win_rows)
        return _nsa_decode_attn(qc, qr, kc, vc, cache_slc, page_table, kw, vw,
                                pad_rows(ks_new, LANES), pad_rows(vs_new, LANES), z,
                                tdec, past, wb)

    ys, cmp_s, slc_s, win_new, gdn_s, conv_s = _layer(
        x_sample, past, cache_mem_kv[l].reshape(bs, cache_mem_kv.shape[2], -1), sample_keys,
        state_gdn[l], state_conv[l], w, 256)
    win_all = jnp.concatenate([cache_win_kv[l], win_new], axis=1)
    win_s = win_all[:, win_all.shape[1] - min(NSA_WIN, past + tdec):]

    stack = lambda a: a[None]
    return (yp, ys, stack(kvm.reshape(bp, mem_prompt.shape[1], 2, MEM_HEADS, MEM_DH)),
            stack(cmp_p), stack(slc_p), stack(win_p), stack(gdn_p), stack(conv_p),
            stack(cmp_s), stack(slc_s), stack(win_s), stack(gdn_s), stack(conv_s))
```

```python
import functools
import math

import jax
import jax.numpy as jnp
import numpy as np
from jax import lax
from jax.experimental import pallas as pl
from jax.experimental.pallas import tpu as pltpu

F32 = jnp.float32
BF16 = jnp.bfloat16
HIGHEST = lax.Precision.HIGHEST

D_MODEL = 1024
PAGE_SIZE = 128
GDN_HEADS = 4
GDN_DK = 128
GDN_DV = 128
GDN_CONV = 4
GDN_CHUNK = 64
GDN_CONV_CH = GDN_HEADS * (2 * GDN_DK + GDN_DV)
NSA_HEADS = 8
NSA_KV = 2
NSA_DH = 64
NSA_CMP = 32
NSA_SEL = 64
NSA_TOPN = 16
NSA_WIN = 512
NSA_FORCE = 1e9
MEM_HEADS = 4
MEM_DH = 128
PEER_HEADS = 8
PEER_NKEYS = 128
PEER_DKEY = 256
PEER_TOPK = 16
N_BRANCH = 3
ROPE_THETA = 10000.0
EPS = 1e-6

LANES = 128
SUBLANES = 8
VMEM_LIMIT = 56 * 1024 * 1024

Z_MG = 0
Z_QKV = Z_MG + N_BRANCH * D_MODEL
Z_ZG = Z_QKV + GDN_CONV_CH
Z_NQ = Z_ZG + GDN_HEADS * GDN_DV
Z_MQ = Z_NQ + NSA_HEADS * NSA_DH
Z_NKV = Z_MQ + MEM_HEADS * MEM_DH
Z_SMALL = Z_NKV + 3 * 2 * NSA_KV * NSA_DH
Z_WIDTH = Z_SMALL + LANES
SM_A = 0
SM_B = GDN_HEADS
SM_NG = 2 * GDN_HEADS
NKV_W = 3 * 2 * NSA_KV * NSA_DH
ROW_W = 2 * NSA_KV * NSA_DH


def _params(sem, vmem=VMEM_LIMIT):
    return pltpu.CompilerParams(dimension_semantics=sem, vmem_limit_bytes=vmem)


def _dot(a, b):
    return jnp.dot(a.astype(BF16), b.astype(BF16), preferred_element_type=F32)


def _dot_nt(a, b):
    return lax.dot_general(a.astype(BF16), b.astype(BF16), (((1,), (1,)), ((), ())),
                           preferred_element_type=F32)


def _dot_hi(a, b):
    return jnp.dot(a, b, precision=HIGHEST, preferred_element_type=F32)


def _dot3(a, b):
    ah = a.astype(BF16)
    bh = b.astype(BF16)
    al = (a - ah.astype(F32)).astype(BF16)
    bl = (b - bh.astype(F32)).astype(BF16)
    d = lambda x, y: jnp.dot(x, y, preferred_element_type=F32)
    return d(ah, bh) + d(al, bh) + d(ah, bl)


def _rms(x, g):
    return x * lax.rsqrt(jnp.mean(x * x, axis=-1, keepdims=True) + EPS) * g


def _gelu(x):
    a = -2.0 * math.sqrt(2.0 / math.pi)
    return x / (1.0 + jnp.exp(x * (a + (a * 0.044715) * (x * x))))


def _norm_matmul_kernel(x_ref, g_ref, w_ref, o_ref, *, col_chunk):
    yb = _rms(x_ref[...], g_ref[...]).astype(BF16)
    for c0 in range(0, o_ref.shape[1], col_chunk):
        o_ref[:, c0:c0 + col_chunk] = jnp.dot(yb, w_ref[:, c0:c0 + col_chunk],
                                              preferred_element_type=F32)


def _norm_matmul(x, g, w, tm, col_chunk):
    n, d = x.shape
    wc = w.shape[1]
    return pl.pallas_call(
        functools.partial(_norm_matmul_kernel, col_chunk=col_chunk),
        grid=(n // tm,),
        in_specs=[pl.BlockSpec((tm, d), lambda i: (i, 0)),
                  pl.BlockSpec((1, d), lambda i: (0, 0)),
                  pl.BlockSpec((d, wc), lambda i: (0, 0), pipeline_mode=pl.Buffered(1))],
        out_specs=pl.BlockSpec((tm, wc), lambda i: (i, 0)),
        out_shape=jax.ShapeDtypeStruct((n, wc), F32),
        compiler_params=_params(("parallel",)),
    )(x, g.reshape(1, d), w)


def _tri_inverse(lmats, c):
    row = lax.broadcasted_iota(jnp.int32, (c, c), 0)
    col = lax.broadcasted_iota(jnp.int32, (c, c), 1)
    ident = jnp.where(row == col, 1.0, 0.0)
    xs = [ident - lm for lm in lmats]
    ps = [_dot3(lm, lm) for lm in lmats]
    n = 2
    while n < c:
        xs = [x + _dot3(x, p) for x, p in zip(xs, ps)]
        n *= 2
        if n < c:
            ps = [_dot3(p, p) for p in ps]
    return xs


def _gdn_kernel(qkv_ref, zg_ref, sm_ref, buf_ref, s0_ref, cw_ref, alog_ref, dt_ref, gn_ref,
                o_ref, snew_ref, cnew_ref, ext_ref, s_ref, *, tb, n_chunks, nb):
    ci = pl.program_id(1)

    @pl.when(ci == 0)
    def _():
        for i in range(nb):
            ext_ref[i, 0:SUBLANES, :] = buf_ref[i]
            s_ref[i] = s0_ref[i]

    c = GDN_CHUNK
    row = lax.broadcasted_iota(jnp.int32, (c, c), 0)
    col = lax.broadcasted_iota(jnp.int32, (c, c), 1)
    tril = row >= col
    eye = row == col
    hk = GDN_HEADS * GDN_DK
    chains = []
    for i in range(nb):
        u, g_all, beta_all = _gdn_inputs(qkv_ref.at[i], sm_ref.at[i], cw_ref, alog_ref, dt_ref,
                                         cnew_ref.at[i], ext_ref.at[i], tb)
        gc_all = _dot_hi(jnp.where(tril, 1.0, 0.0), g_all)
        for h in range(GDN_HEADS):
            qh = u[:, h * GDN_DK:(h + 1) * GDN_DK]
            kh = u[:, hk + h * GDN_DK:hk + (h + 1) * GDN_DK]
            ch = dict(i=i, h=h, v=u[:, 2 * hk + h * GDN_DV:2 * hk + (h + 1) * GDN_DV])
            ch['q'] = qh * lax.rsqrt(jnp.sum(qh * qh, axis=-1, keepdims=True) + EPS) * (GDN_DK ** -0.5)
            ch['k'] = kh * lax.rsqrt(jnp.sum(kh * kh, axis=-1, keepdims=True) + EPS)
            ch['beta'] = beta_all[:, SM_B + h:SM_B + h + 1]
            gc = gc_all[:, SM_A + h:SM_A + h + 1]
            ch['gc'] = gc
            ch['gl'] = gc_all[c - 1:c, SM_A + h:SM_A + h + 1]
            gc_row = jnp.sum(jnp.where(eye, gc, 0.0), axis=0, keepdims=True)
            ch['decay'] = jnp.exp(jnp.where(tril, gc - gc_row, -jnp.inf))
            ch['kb'] = ch['k'] * ch['beta']
            ch['egc'] = jnp.exp(gc)
            chains.append(ch)
    lmats = [jnp.where(row > col, _dot_nt(ch['kb'], ch['k']) * ch['decay'], 0.0) for ch in chains]
    tinvs = _tri_inverse(lmats, c)
    uus = [_dot(t, ch['v'] * ch['beta']) for t, ch in zip(tinvs, chains)]
    wws = [_dot(t, ch['kb'] * ch['egc']) for t, ch in zip(tinvs, chains)]
    aqks = [_dot_nt(ch['q'], ch['k']) * ch['decay'] for ch in chains]
    states = [s_ref[ch['i'], ch['h']] for ch in chains]
    v_news = [uu - _dot(ww, s) for uu, ww, s in zip(uus, wws, states)]
    outs = [_dot(ch['q'] * ch['egc'], s) + _dot(aqk, vn)
            for ch, s, aqk, vn in zip(chains, states, aqks, v_news)]
    s_news = [s * jnp.exp(ch['gl']) + _dot((ch['k'] * jnp.exp(ch['gl'] - ch['gc'])).T, vn)
              for ch, s, vn in zip(chains, states, v_news)]
    for ch, o, s_new in zip(chains, outs, s_news):
        i, h = ch['i'], ch['h']
        s_ref[i, h] = s_new
        zh = zg_ref[i, :, h * GDN_DV:(h + 1) * GDN_DV]
        o_ref[i, :, h * GDN_DV:(h + 1) * GDN_DV] = (_rms(o[0:tb], gn_ref[...])
                                                    * (zh * jax.nn.sigmoid(zh)))

    @pl.when(ci == n_chunks - 1)
    def _():
        snew_ref[...] = s_ref[...]


def _gdn_inputs(qkv_ref, sm_ref, cw_ref, alog_ref, dt_ref, cnew_ref, ext_ref, tb):
    c = GDN_CHUNK
    if tb < c:
        ext_ref[SUBLANES + tb:, :] = jnp.zeros((c - tb, GDN_CONV_CH), F32)
    ext_ref[SUBLANES:SUBLANES + tb, :] = qkv_ref[...]
    cw = cw_ref[...]
    conv = cw[0:1] * ext_ref[SUBLANES - 3:SUBLANES - 3 + c, :]
    for j in range(1, GDN_CONV):
        conv = conv + cw[j:j + 1] * ext_ref[SUBLANES - 3 + j:SUBLANES - 3 + j + c, :]
    u = conv * jax.nn.sigmoid(conv)
    last_rows = ext_ref[tb:tb + SUBLANES, :]
    cnew_ref[...] = last_rows
    ext_ref[0:SUBLANES, :] = last_rows

    sm = sm_ref[...]
    if tb < c:
        sm = jnp.concatenate([sm, jnp.zeros((c - tb, LANES), F32)], axis=0)
    za = sm + dt_ref[...]
    softplus = jnp.maximum(za, 0.0) + jnp.log1p(jnp.exp(-jnp.abs(za)))
    g_all = -jnp.exp(alog_ref[...]) * softplus
    beta_all = jax.nn.sigmoid(sm)
    if tb < c:
        valid = lax.broadcasted_iota(jnp.int32, (c, 1), 0) < tb
        u = jnp.where(valid, u, 0.0)
        g_all = jnp.where(valid, g_all, 0.0)
        beta_all = jnp.where(valid, beta_all, 0.0)
    return u, g_all, beta_all


GDN_BATCH_PER_STEP = 2


def _gdn(z, conv_buf, s0, conv_w, a_log, dt_bias, gnorm, batch, t):
    c = GDN_CHUNK
    tb = min(t, c)
    n_chunks = t // tb
    nb = GDN_BATCH_PER_STEP if batch % GDN_BATCH_PER_STEP == 0 else 1
    assert tb % SUBLANES == 0 and n_chunks * tb == t and (tb == c or n_chunks == 1)
    buf8 = jnp.pad(conv_buf, ((0, 0), (SUBLANES - (GDN_CONV - 1), 0), (0, 0)))
    alog_row = jnp.zeros((1, LANES), F32).at[0, SM_A:SM_A + GDN_HEADS].set(a_log)
    dt_row = jnp.zeros((1, LANES), F32).at[0, SM_A:SM_A + GDN_HEADS].set(dt_bias)
    z3 = z.reshape(batch, t, Z_WIDTH)
    hv = GDN_HEADS * GDN_DV
    o, s_new, c_new = pl.pallas_call(
        functools.partial(_gdn_kernel, tb=tb, n_chunks=n_chunks, nb=nb),
        grid=(batch // nb, n_chunks),
        in_specs=[
            pl.BlockSpec((nb, tb, GDN_CONV_CH), lambda b, ci: (b, ci, Z_QKV // GDN_CONV_CH)),
            pl.BlockSpec((nb, tb, hv), lambda b, ci: (b, ci, Z_ZG // hv)),
            pl.BlockSpec((nb, tb, LANES), lambda b, ci: (b, ci, Z_SMALL // LANES)),
            pl.BlockSpec((nb, SUBLANES, GDN_CONV_CH), lambda b, ci: (b, 0, 0)),
            pl.BlockSpec((nb, GDN_HEADS, GDN_DK, GDN_DV), lambda b, ci: (b, 0, 0, 0)),
            pl.BlockSpec((GDN_CONV, GDN_CONV_CH), lambda b, ci: (0, 0)),
            pl.BlockSpec((1, LANES), lambda b, ci: (0, 0)),
            pl.BlockSpec((1, LANES), lambda b, ci: (0, 0)),
            pl.BlockSpec((1, GDN_DV), lambda b, ci: (0, 0)),
        ],
        out_specs=[
            pl.BlockSpec((nb, tb, hv), lambda b, ci: (b, ci, 0)),
            pl.BlockSpec((nb, GDN_HEADS, GDN_DK, GDN_DV), lambda b, ci: (b, 0, 0, 0)),
            pl.BlockSpec((nb, SUBLANES, GDN_CONV_CH), lambda b, ci: (b, 0, 0)),
        ],
        out_shape=[
            jax.ShapeDtypeStruct((batch, t, hv), F32),
            jax.ShapeDtypeStruct((batch, GDN_HEADS, GDN_DK, GDN_DV), F32),
            jax.ShapeDtypeStruct((batch, SUBLANES, GDN_CONV_CH), F32),
        ],
        scratch_shapes=[pltpu.VMEM((nb, SUBLANES + c, GDN_CONV_CH), F32),
                        pltpu.VMEM((nb, GDN_HEADS, GDN_DK, GDN_DV), F32)],
        compiler_params=_params(("parallel", "arbitrary")),
    )(z3, z3, z3, buf8, s0, conv_w, alog_row, dt_row, gnorm.reshape(1, GDN_DV))
    return o.reshape(batch * t, hv), s_new, c_new[:, SUBLANES - (GDN_CONV - 1):]


def _rope_tables(pos):
    half = NSA_DH // 2
    inv = jnp.power(ROPE_THETA, -jnp.arange(half, dtype=F32) / half)
    ang = pos.astype(F32)[:, None] * inv[None, :]
    cos, sin = jnp.cos(ang), jnp.sin(ang)
    cos_t = jnp.concatenate([cos, cos, cos, cos], axis=-1)
    sin_t = jnp.concatenate([-sin, sin, -sin, sin], axis=-1)
    return cos_t, sin_t


def _nsa_prep_kernel(nq_ref, nkv_ref, cos_ref, sin_ref,
                     qc_ref, qr_ref, slc_ref, win_ref, ks_ref, vs_ref, kw_ref, vw_ref):
    cos = cos_ref[...]
    sin = sin_ref[...]
    lane = lax.broadcasted_iota(jnp.int32, cos.shape, 1)
    first_half = (lane % NSA_DH) < (NSA_DH // 2)

    def rope(x):
        swapped = jnp.where(first_half, pltpu.roll(x, LANES - NSA_DH // 2, 1),
                            pltpu.roll(x, NSA_DH // 2, 1))
        return x * cos + swapped * sin

    scale = NSA_DH ** -0.5
    for j in range(NSA_HEADS * NSA_DH // LANES):
        x = nq_ref[:, j * LANES:(j + 1) * LANES]
        qc_ref[:, j * LANES:(j + 1) * LANES] = x * scale
        qr_ref[:, j * LANES:(j + 1) * LANES] = rope(x) * scale

    kv_w = NSA_KV * NSA_DH
    for br, (row_ref, k_ref, v_ref) in enumerate(((slc_ref, ks_ref, vs_ref),
                                                   (win_ref, kw_ref, vw_ref))):
        base = (br + 1) * ROW_W
        kr = rope(nkv_ref[:, base:base + kv_w])
        v = nkv_ref[:, base + kv_w:base + 2 * kv_w]
        row_ref[:, 0:kv_w] = kr
        row_ref[:, kv_w:2 * kv_w] = v
        for g in range(NSA_KV):
            k_ref[0, g] = kr[:, g * NSA_DH:(g + 1) * NSA_DH].astype(k_ref.dtype)
            v_ref[0, g] = v[:, g * NSA_DH:(g + 1) * NSA_DH].astype(v_ref.dtype)


def _nsa_prep(z, pos, batch, t, tm, kv_dtype):
    cos_t, sin_t = _rope_tables(pos)
    nt = t // tm
    rowblk = lambda b, j: b * nt + j
    kv_shape = jax.ShapeDtypeStruct((batch, NSA_KV, t, NSA_DH), kv_dtype)
    kv_spec = pl.BlockSpec((1, NSA_KV, tm, NSA_DH), lambda b, j: (b, 0, j, 0))
    qw = NSA_HEADS * NSA_DH
    return pl.pallas_call(
        _nsa_prep_kernel,
        grid=(batch, nt),
        in_specs=[pl.BlockSpec((tm, qw), lambda b, j: (rowblk(b, j), Z_NQ // qw)),
                  pl.BlockSpec((tm, NKV_W), lambda b, j: (rowblk(b, j), Z_NKV // NKV_W)),
                  pl.BlockSpec((tm, LANES), lambda b, j: (j, 0)),
                  pl.BlockSpec((tm, LANES), lambda b, j: (j, 0))],
        out_specs=[pl.BlockSpec((tm, qw), lambda b, j: (rowblk(b, j), 0)),
                   pl.BlockSpec((tm, qw), lambda b, j: (rowblk(b, j), 0)),
                   pl.BlockSpec((tm, ROW_W), lambda b, j: (rowblk(b, j), 0)),
                   pl.BlockSpec((tm, ROW_W), lambda b, j: (rowblk(b, j), 0)),
                   kv_spec, kv_spec, kv_spec, kv_spec],
        out_shape=[jax.ShapeDtypeStruct((batch * t, qw), F32),
                   jax.ShapeDtypeStruct((batch * t, qw), F32),
                   jax.ShapeDtypeStruct((batch * t, ROW_W), F32),
                   jax.ShapeDtypeStruct((batch * t, ROW_W), F32),
                   kv_shape, kv_shape, kv_shape, kv_shape],
        compiler_params=_params(("parallel", "parallel")),
    )(z, z, cos_t, sin_t)


def _compress_weights(pe, w1, w2):
    eye = jnp.eye(2, dtype=F32)
    w1r = w1.reshape(2, NSA_CMP, NSA_DH, NSA_DH)
    w1big = jnp.einsum('srde,st,gh->rsgdthe', w1r, eye, eye).reshape(NSA_CMP * ROW_W, ROW_W)
    w2big = jnp.einsum('sed,st,gh->sgethd', w2, eye, eye).reshape(ROW_W, ROW_W)
    pe_big = jnp.broadcast_to(jnp.transpose(pe, (1, 0, 2))[:, :, None, :],
                              (NSA_CMP, 2, NSA_KV, NSA_DH)).reshape(1, NSA_CMP * ROW_W)
    return pe_big, w1big.astype(BF16), w2big.astype(BF16)


def _compress_kernel(x_ref, pe_ref, w1_ref, w2_ref, o_ref):
    x = (x_ref[...] + pe_ref[...]).astype(BF16)
    hid = _gelu(jnp.dot(x, w1_ref[...], preferred_element_type=F32))
    o_ref[...] = jnp.dot(hid.astype(BF16), w2_ref[...], preferred_element_type=F32)


def _compress(rows, pe_big, w1big, w2big, tm):
    n, kdim = rows.shape
    return pl.pallas_call(
        _compress_kernel,
        grid=(n // tm,),
        in_specs=[pl.BlockSpec((tm, kdim), lambda i: (i, 0)),
                  pl.BlockSpec((1, kdim), lambda i: (0, 0)),
                  pl.BlockSpec((kdim, ROW_W), lambda i: (0, 0)),
                  pl.BlockSpec((ROW_W, ROW_W), lambda i: (0, 0))],
        out_specs=pl.BlockSpec((tm, ROW_W), lambda i: (i, 0)),
        out_shape=jax.ShapeDtypeStruct((n, ROW_W), F32),
        compiler_params=_params(("parallel",)),
    )(rows, pe_big, w1big, w2big)


def _split_compressed(cmp_out, batch):
    nb = cmp_out.shape[0] // batch
    x = cmp_out.reshape(batch, nb // 2, 2, 2, NSA_KV, NSA_DH)
    x = jnp.transpose(x, (3, 0, 4, 2, 1, 5)).reshape(2, batch, NSA_KV, nb, NSA_DH)
    return x[0].astype(BF16), x[1].astype(BF16)


CMP_PAGES_PER_STEP = 32
SLC_PAGES_PER_STEP = 8
BLOCKS_PER_PAGE = PAGE_SIZE // NSA_CMP


def _page_specs(block, n_pages, per_step):
    def spec(j):
        return pl.BlockSpec(block, lambda b, s, pt: (pt[b * n_pages + s * per_step + j], 0, 0))
    return [spec(j) for j in range(per_step)]


CMP_ROW_PITCH = NSA_CMP + 4


def _paged_compress_kernel(pt_ref, *refs):
    n_in = CMP_PAGES_PER_STEP
    pe_ref, w1_ref, w2_ref, o_ref, x_ref = refs[n_in:]
    m = n_in * BLOCKS_PER_PAGE
    n_slabs = ROW_W // LANES
    for j, p_ref in enumerate(refs[:n_in]):
        x = p_ref[0].T
        for n in range(BLOCKS_PER_PAGE):
            base = (j * BLOCKS_PER_PAGE + n) * CMP_ROW_PITCH
            for sl in range(n_slabs):
                x_ref[sl, base:base + NSA_CMP, :] = x[n * NSA_CMP:(n + 1) * NSA_CMP,
                                                      sl * LANES:(sl + 1) * LANES]
    acc = jnp.zeros((m, ROW_W), F32)
    for r in range(NSA_CMP):
        lhs = jnp.concatenate([x_ref[sl, pl.ds(r, m, stride=CMP_ROW_PITCH), :]
                               for sl in range(n_slabs)], axis=1)
        lhs = (lhs + pe_ref[r:r + 1, :]).astype(BF16)
        acc = acc + jnp.dot(lhs, w1_ref[r], preferred_element_type=F32)
    o_ref[...] = jnp.dot(_gelu(acc).astype(BF16), w2_ref[...], preferred_element_type=F32)


def _paged_compress(cache_t, page_table, pe_big, w1big, w2big):
    batch, n_pages = page_table.shape
    rows = CMP_PAGES_PER_STEP * BLOCKS_PER_PAGE
    steps = n_pages // CMP_PAGES_PER_STEP
    pe_rows = pe_big.reshape(NSA_CMP, ROW_W)
    w1_rows = w1big.reshape(NSA_CMP, ROW_W, ROW_W)
    const = lambda a: pl.BlockSpec(a.shape, lambda b, s, pt: (0,) * a.ndim)
    return pl.pallas_call(
        _paged_compress_kernel,
        grid_spec=pltpu.PrefetchScalarGridSpec(
            num_scalar_prefetch=1, grid=(batch, steps),
            in_specs=_page_specs((1, ROW_W, PAGE_SIZE), n_pages, CMP_PAGES_PER_STEP)
            + [const(pe_rows), const(w1_rows), const(w2big)],
            out_specs=pl.BlockSpec((rows, ROW_W), lambda b, s, pt: (b * steps + s, 0)),
            scratch_shapes=[pltpu.VMEM((ROW_W // LANES, rows * CMP_ROW_PITCH, LANES), F32)]),
        out_shape=jax.ShapeDtypeStruct((batch * n_pages * BLOCKS_PER_PAGE, ROW_W), F32),
        compiler_params=_params(("parallel", "parallel")),
    )(page_table.reshape(-1), *([cache_t] * CMP_PAGES_PER_STEP), pe_rows, w1_rows, w2big)


def _topk_mask(score, k):
    n = score.shape[-1]
    lane = lax.broadcasted_iota(jnp.int32, score.shape, 1).astype(F32)
    sel = jnp.zeros(score.shape, F32)
    for _ in range(k):
        m = jnp.max(score, axis=-1, keepdims=True)
        idx = jnp.min(jnp.where(score == m, lane, float(n)), axis=-1, keepdims=True)
        pick = lane == idx
        sel = jnp.where(pick, 1.0, sel)
        score = jnp.where(pick, -jnp.inf, score)
    return sel


def _softmax_step(carry, s, v):
    m, l, acc = carry
    m_new = jnp.maximum(m, jnp.max(s, axis=-1, keepdims=True))
    m_safe = jnp.where(m_new == -jnp.inf, 0.0, m_new)
    p = jnp.exp(s - m_safe)
    alpha = jnp.exp(m - m_safe)
    l = alpha * l + jnp.sum(p, axis=-1, keepdims=True)
    acc = alpha * acc + _dot(p, v)
    return m_new, l, acc


def _nsa_attn_kernel(*refs, tq, pos0, wrel0, n_keys, tk, has_tail):
    if has_tail:
        (qc_ref, qr_ref, kc_ref, vc_ref, ks_ref, vs_ref, kw_ref, vw_ref, kt_ref, vt_ref,
         sm_ref, o_ref) = refs
    else:
        (qc_ref, qr_ref, kc_ref, vc_ref, ks_ref, vs_ref, kw_ref, vw_ref, sm_ref, o_ref) = refs
    hp = NSA_HEADS // NSA_KV
    r = hp * tq
    g = pl.program_id(1)
    qi = pl.program_id(2)
    ncp = kc_ref.shape[2]
    nch = ncp // 2
    nsl = -(-(n_keys // NSA_SEL + (1 if has_tail else 0)) // LANES) * LANES

    def stack_heads(ref):
        x = ref[...]
        return jnp.concatenate([x[:, h * NSA_DH:(h + 1) * NSA_DH] for h in range(hp)],
                               axis=0).astype(BF16)

    qc4 = stack_heads(qc_ref)
    qr4 = stack_heads(qr_ref)
    t_q = pos0 + qi * tq + lax.broadcasted_iota(jnp.int32, (tq, 1), 0)
    t_row = jnp.concatenate([t_q] * hp, axis=0)

    s = _dot_nt(qc4, kc_ref[0, 0])
    ccol = lax.broadcasted_iota(jnp.int32, (r, ncp), 1)
    cblk = 2 * (ccol % nch) + ccol // nch
    s = jnp.where((cblk + 1) * NSA_CMP <= t_row + 1, s, -jnp.inf)
    m = jnp.max(s, axis=-1, keepdims=True)
    e = jnp.exp(s - jnp.where(m == -jnp.inf, 0.0, m))
    p_cmp = e / jnp.maximum(jnp.sum(e, axis=-1, keepdims=True), 1e-30)
    o_cmp = _dot(p_cmp, vc_ref[0, 0])

    imp = p_cmp[0:tq]
    for h in range(1, hp):
        imp = imp + p_cmp[h * tq:(h + 1) * tq]
    imp = imp[:, :nch] + imp[:, nch:]
    if nsl > nch:
        imp = jnp.concatenate([imp, jnp.zeros((tq, nsl - nch), F32)], axis=1)
    blk = lax.broadcasted_iota(jnp.int32, (tq, nsl), 1)
    cur = t_q // NSA_SEL
    forced = (blk == 0) | (blk == cur) | (blk == cur - 1)
    score = jnp.where(blk > cur, -jnp.inf, jnp.where(forced, NSA_FORCE, imp))
    sel = _topk_mask(score, NSA_TOPN).astype(BF16)

    init = (jnp.full((r, 1), -jnp.inf, F32), jnp.zeros((r, 1), F32), jnp.zeros((r, NSA_DH), F32))

    def slc_scores(k, kpos0, width):
        kpos = kpos0 + lax.broadcasted_iota(jnp.int32, (1, width), 1)
        expand = (lax.broadcasted_iota(jnp.int32, (nsl, width), 0) == kpos // NSA_SEL)
        picked = jnp.dot(sel, expand.astype(BF16), preferred_element_type=F32)
        picked = jnp.concatenate([picked] * hp, axis=0)
        return jnp.where((picked > 0.5) & (kpos <= t_row), _dot_nt(qr4, k), -jnp.inf)

    def slc_body(kt, carry):
        start = pl.multiple_of(kt * tk, tk)
        k = ks_ref[0, 0, pl.ds(start, tk), :]
        v = vs_ref[0, 0, pl.ds(start, tk), :]
        return _softmax_step(carry, slc_scores(k, kt * tk, tk), v)

    t_last = pos0 + qi * tq + tq - 1
    n_main = jnp.minimum(n_keys // tk, t_last // tk + 1)
    carry = lax.fori_loop(0, n_main, slc_body, init)
    if has_tail:
        carry = _softmax_step(carry, slc_scores(kt_ref[0, 0], n_keys, kt_ref.shape[2]), vt_ref[0, 0])
    o_slc = carry[2] / jnp.maximum(carry[1], 1e-30)

    wbase = pos0 - wrel0

    def win_body(kt, carry):
        start = pl.multiple_of(kt * tk, tk)
        k = kw_ref[0, 0, pl.ds(start, tk), :]
        v = vw_ref[0, 0, pl.ds(start, tk), :]
        kpos = wbase + kt * tk + lax.broadcasted_iota(jnp.int32, (1, tk), 1)
        rel = t_row - kpos
        ok = (rel >= 0) & (rel < NSA_WIN) & (kpos >= 0)
        return _softmax_step(carry, jnp.where(ok, _dot_nt(qr4, k), -jnp.inf), v)

    w_first = jnp.maximum(pos0 + qi * tq - (NSA_WIN - 1) - wbase, 0) // tk
    w_last = jnp.minimum((t_last - wbase) // tk, kw_ref.shape[2] // tk - 1)
    carry = lax.fori_loop(w_first, w_last + 1, win_body, init)
    o_win = carry[2] / jnp.maximum(carry[1], 1e-30)

    sig = jax.nn.sigmoid(sm_ref[...])
    gw = N_BRANCH * hp
    gates = jnp.where(g == 0, sig[:, SM_NG:SM_NG + gw], sig[:, SM_NG + gw:SM_NG + 2 * gw])
    outs = []
    for h in range(hp):
        rows = slice(h * tq, (h + 1) * tq)
        outs.append(gates[:, 3 * h:3 * h + 1] * o_cmp[rows]
                    + gates[:, 3 * h + 1:3 * h + 2] * o_slc[rows]
                    + gates[:, 3 * h + 2:3 * h + 3] * o_win[rows])
    o_ref[...] = jnp.concatenate(outs, axis=1)


def _nsa_attn(qc, qr, kc, vc, ks, vs, kw, vw, tails, z, batch, t, tq, pos0, wrel0, tk):
    nq = t // tq
    n_keys = ks.shape[2]
    gw = NSA_HEADS * NSA_DH // NSA_KV
    rowblk = lambda b, g, i: b * nq + i
    full = lambda a: pl.BlockSpec((1, 1) + a.shape[2:], lambda b, g, i: (b, g, 0, 0))
    q_spec = pl.BlockSpec((tq, gw), lambda b, g, i: (rowblk(b, g, i), g))
    operands = [qc, qr, kc, vc, ks, vs, kw, vw] + list(tails)
    in_specs = [q_spec, q_spec] + [full(a) for a in operands[2:]]
    in_specs.append(pl.BlockSpec((tq, LANES), lambda b, g, i: (rowblk(b, g, i), Z_SMALL // LANES)))
    return pl.pallas_call(
        functools.partial(_nsa_attn_kernel, tq=tq, pos0=pos0, wrel0=wrel0, n_keys=n_keys, tk=tk,
                          has_tail=bool(tails)),
        grid=(batch, NSA_KV, nq),
        in_specs=in_specs,
        out_specs=pl.BlockSpec((tq, gw), lambda b, g, i: (rowblk(b, g, i), g)),
        out_shape=jax.ShapeDtypeStruct((batch * t, NSA_HEADS * NSA_DH), F32),
        compiler_params=_params(("parallel", "parallel", "arbitrary")),
    )(*operands, z)


MASKED = -1e30


def _topk_mask_rows(score, k):
    n = score.shape[0]
    row = lax.broadcasted_iota(jnp.int32, score.shape, 0).astype(F32)
    sel = jnp.zeros(score.shape, F32)
    for _ in range(k):
        m = jnp.max(score, axis=0, keepdims=True)
        idx = jnp.min(jnp.where(score == m, row, float(n)), axis=0, keepdims=True)
        pick = row == idx
        sel = jnp.where(pick, 1.0, sel)
        score = jnp.where(pick, -jnp.inf, score)
    return sel


def _softmax_step_cols(carry, s, vt):
    m, l, acc = carry
    m_new = jnp.maximum(m, jnp.max(s, axis=0, keepdims=True))
    p = jnp.exp(s - m_new)
    alpha = jnp.exp(m - m_new)
    l = alpha * l + jnp.sum(p, axis=0, keepdims=True)
    acc = alpha * acc + _dot(vt, p)
    return m_new, l, acc


def _nsa_prompt_kernel(qc_ref, qr_ref, kc_ref, vct_ref, ks_ref, vst_ref, kw_ref, vwt_ref, sm_ref,
                       o_ref, bias_ref, *, tq, tk, tkw):
    hp = NSA_HEADS // NSA_KV
    r = hp * tq
    g = pl.program_id(1)
    qi = pl.program_id(2)
    q0 = qi * tq
    ncp = kc_ref.shape[2]
    nch = ncp // 2
    nsl = bias_ref.shape[0]
    per_tile = tk // NSA_SEL

    def heads_on_lanes(ref):
        xt = ref[...].T
        return jnp.concatenate([xt[h * NSA_DH:(h + 1) * NSA_DH] for h in range(hp)],
                               axis=1).astype(BF16)

    qct = heads_on_lanes(qc_ref)
    qrt = heads_on_lanes(qr_ref)
    t_q = q0 + lax.broadcasted_iota(jnp.int32, (1, tq), 1)
    t_lane = jnp.concatenate([t_q] * hp, axis=1)

    s = _dot(kc_ref[0, 0], qct)
    crow = lax.broadcasted_iota(jnp.int32, (ncp, 1), 0)
    cblk = 2 * (crow % nch) + crow // nch
    s = jnp.where((cblk + 1) * NSA_CMP <= t_lane + 1, s, -jnp.inf)
    m = jnp.max(s, axis=0, keepdims=True)
    e = jnp.exp(s - jnp.where(m == -jnp.inf, 0.0, m))
    p_cmp = e / jnp.maximum(jnp.sum(e, axis=0, keepdims=True), 1e-30)
    o_cmp = _dot(vct_ref[0, 0], p_cmp)

    imp = p_cmp[:, 0:tq]
    for h in range(1, hp):
        imp = imp + p_cmp[:, h * tq:(h + 1) * tq]
    imp = imp[:nch] + imp[nch:]
    blk = lax.broadcasted_iota(jnp.int32, (nsl, tq), 0)
    cur = t_q // NSA_SEL
    forced = (blk == 0) | (blk == cur) | (blk == cur - 1)
    score = jnp.where(blk > cur, -jnp.inf, jnp.where(forced, NSA_FORCE, imp))
    bias = (_topk_mask_rows(score, NSA_TOPN) - 1.0) * (-MASKED)
    bias_ref[...] = jnp.concatenate([bias] * hp, axis=1)

    init = (jnp.full((1, r), MASKED, F32), jnp.zeros((1, r), F32), jnp.zeros((NSA_DH, r), F32))

    def slc_scores(kt):
        start = pl.multiple_of(kt * tk, tk)
        sc = _dot(ks_ref[0, 0, pl.ds(start, tk), :], qrt)
        brow = bias_ref[pl.ds(pl.multiple_of(kt * per_tile, per_tile), per_tile), :]
        sc = jnp.concatenate([sc[j * NSA_SEL:(j + 1) * NSA_SEL] + brow[j:j + 1]
                              for j in range(per_tile)], axis=0)
        return sc, vst_ref[0, 0, :, pl.ds(start, tk)]

    def slc_body(kt, carry):
        sc, vt = slc_scores(kt)
        return _softmax_step_cols(carry, sc, vt)

    kd = q0 // tk
    carry = lax.fori_loop(0, kd, slc_body, init)
    sc, vt = slc_scores(kd)
    kpos = kd * tk + lax.broadcasted_iota(jnp.int32, (tk, 1), 0)
    carry = _softmax_step_cols(carry, jnp.where(kpos <= t_lane, sc, MASKED), vt)
    o_slc = carry[2] / carry[1]

    def win_body(kt, carry):
        start = pl.multiple_of(kt * tkw, tkw)
        sc = _dot(kw_ref[0, 0, pl.ds(start, tkw), :], qrt)
        rel = t_lane - (kt * tkw + lax.broadcasted_iota(jnp.int32, (tkw, 1), 0))
        sc = jnp.where((rel >= 0) & (rel < NSA_WIN), sc, MASKED)
        return _softmax_step_cols(carry, sc, vwt_ref[0, 0, :, pl.ds(start, tkw)])

    w_first = jnp.maximum(q0 - (NSA_WIN - 1), 0) // tkw
    carry = lax.fori_loop(w_first, (q0 + tq - 1) // tkw + 1, win_body, init)
    o_win = carry[2] / carry[1]

    sig = jax.nn.sigmoid(sm_ref[...].T)
    gw = N_BRANCH * hp
    gates = jnp.where(g == 0, sig[SM_NG:SM_NG + gw], sig[SM_NG + gw:SM_NG + 2 * gw])
    outs = []
    for h in range(hp):
        cols = slice(h * tq, (h + 1) * tq)
        outs.append(gates[3 * h:3 * h + 1] * o_cmp[:, cols]
                    + gates[3 * h + 1:3 * h + 2] * o_slc[:, cols]
                    + gates[3 * h + 2:3 * h + 3] * o_win[:, cols])
    o_ref[...] = jnp.concatenate(outs, axis=0).T


def _nsa_prompt_attn(qc, qr, kc, vct, ks, vst, kw, vwt, z, batch, t, tq, tk, tkw):
    nq = t // tq
    gw = NSA_HEADS * NSA_DH // NSA_KV
    assert tk % (SUBLANES * NSA_SEL) == 0 and tq == tkw and tk % tq == 0 and t % tk == 0
    rowblk = lambda b, g, i: b * nq + i
    full = lambda a: pl.BlockSpec((1, 1) + a.shape[2:], lambda b, g, i: (b, g, 0, 0))
    q_spec = pl.BlockSpec((tq, gw), lambda b, g, i: (rowblk(b, g, i), g))
    return pl.pallas_call(
        functools.partial(_nsa_prompt_kernel, tq=tq, tk=tk, tkw=tkw),
        grid=(batch, NSA_KV, nq),
        in_specs=[q_spec, q_spec] + [full(a) for a in (kc, vct, ks, vst, kw, vwt)]
        + [pl.BlockSpec((tq, LANES), lambda b, g, i: (rowblk(b, g, i), Z_SMALL // LANES))],
        out_specs=pl.BlockSpec((tq, gw), lambda b, g, i: (rowblk(b, g, i), g)),
        out_shape=jax.ShapeDtypeStruct((batch * t, NSA_HEADS * NSA_DH), F32),
        scratch_shapes=[pltpu.VMEM((t // NSA_SEL, NSA_HEADS // NSA_KV * tq), F32)],
        compiler_params=_params(("parallel", "parallel", "arbitrary")),
    )(qc, qr, kc, vct, ks, vst, kw, vwt, z)


def _softmax_step_vt(carry, s, vt):
    m, l, acc = carry
    m_new = jnp.maximum(m, jnp.max(s, axis=-1, keepdims=True))
    m_safe = jnp.where(m_new == -jnp.inf, 0.0, m_new)
    p = jnp.exp(s - m_safe)
    alpha = jnp.exp(m - m_safe)
    l = alpha * l + jnp.sum(p, axis=-1, keepdims=True)
    acc = alpha * acc + _dot_nt(p, vt)
    return m_new, l, acc


def _nsa_decode_kernel(pt_ref, *refs, tq, pos0, wrel0, n_keys):
    n_in = SLC_PAGES_PER_STEP
    qc_ref, qr_ref, kc_ref, vc_ref = refs[0:4]
    pages = refs[4:4 + n_in]
    (kw_ref, vw_ref, kt_ref, vt_ref, sm_ref, o_ref,
     m_ref, l_ref, acc_ref, ocmp_ref, sel_ref) = refs[4 + n_in:]
    hp = NSA_HEADS // NSA_KV
    r = hp * tq
    gwid = hp * NSA_DH
    step = pl.program_id(1)
    n_steps = pl.num_programs(1)
    tk = n_in * PAGE_SIZE
    per_tile = tk // NSA_SEL
    n_tiles = sel_ref.shape[1]
    ncp = kc_ref.shape[2]
    nch = ncp // 2
    nsl = -(-(n_tiles * per_tile) // LANES) * LANES
    kv_w = NSA_KV * NSA_DH

    def stack_heads(ref, g):
        x = ref[:, g * gwid:(g + 1) * gwid]
        return jnp.concatenate([x[:, h * NSA_DH:(h + 1) * NSA_DH] for h in range(hp)],
                               axis=0).astype(BF16)

    t_q = pos0 + lax.broadcasted_iota(jnp.int32, (tq, 1), 0)
    t_row = jnp.concatenate([t_q] * hp, axis=0)
    expand = (lax.broadcasted_iota(jnp.int32, (per_tile, tk), 0)
              == lax.broadcasted_iota(jnp.int32, (per_tile, tk), 1) // NSA_SEL).astype(BF16)

    def picked_rows(g, tile, width):
        pk = jnp.dot(sel_ref[g, tile].astype(BF16), expand[:, :width], preferred_element_type=F32)
        return jnp.concatenate([pk] * hp, axis=0)

    @pl.when(step == 0)
    def _():
        for g in range(NSA_KV):
            s = _dot_nt(stack_heads(qc_ref, g), kc_ref[0, g])
            ccol = lax.broadcasted_iota(jnp.int32, (r, ncp), 1)
            cblk = 2 * (ccol % nch) + ccol // nch
            s = jnp.where((cblk + 1) * NSA_CMP <= t_row + 1, s, -jnp.inf)
            m = jnp.max(s, axis=-1, keepdims=True)
            e = jnp.exp(s - jnp.where(m == -jnp.inf, 0.0, m))
            p_cmp = e / jnp.maximum(jnp.sum(e, axis=-1, keepdims=True), 1e-30)
            ocmp_ref[g] = _dot(p_cmp, vc_ref[0, g])
            imp = p_cmp[0:tq]
            for h in range(1, hp):
                imp = imp + p_cmp[h * tq:(h + 1) * tq]
            imp = imp[:, :nch] + imp[:, nch:]
            if nsl > nch:
                imp = jnp.concatenate([imp, jnp.zeros((tq, nsl - nch), F32)], axis=1)
            blk = lax.broadcasted_iota(jnp.int32, (tq, nsl), 1)
            cur = t_q // NSA_SEL
            forced = (blk == 0) | (blk == cur) | (blk == cur - 1)
            score = jnp.where(blk > cur, -jnp.inf, jnp.where(forced, NSA_FORCE, imp))
            sel = _topk_mask(score, NSA_TOPN)
            for j in range(n_tiles):
                sel_ref[g, j] = sel[:, j * per_tile:(j + 1) * per_tile]
            m_ref[g] = jnp.full((r, 1), -jnp.inf, F32)
            l_ref[g] = jnp.zeros((r, 1), F32)
            acc_ref[g] = jnp.zeros((r, NSA_DH), F32)

    kpos = step * tk + lax.broadcasted_iota(jnp.int32, (1, tk), 1)
    for g in range(NSA_KV):
        qr4 = stack_heads(qr_ref, g)
        k_t = jnp.concatenate([p[0, g * NSA_DH:(g + 1) * NSA_DH, :] for p in pages], axis=1)
        v_t = jnp.concatenate([p[0, kv_w + g * NSA_DH:kv_w + (g + 1) * NSA_DH, :] for p in pages],
                              axis=1)
        ok = (picked_rows(g, step, tk) > 0.5) & (kpos <= t_row)
        s = jnp.where(ok, _dot(qr4, k_t), -jnp.inf)
        m, l, acc = _softmax_step_vt((m_ref[g], l_ref[g], acc_ref[g]), s, v_t)
        m_ref[g] = m
        l_ref[g] = l
        acc_ref[g] = acc

    @pl.when(step == n_steps - 1)
    def _():
        sig = jax.nn.sigmoid(sm_ref[...])
        wbase = pos0 - wrel0
        tkw = LANES
        for g in range(NSA_KV):
            qr4 = stack_heads(qr_ref, g)
            tw = kt_ref.shape[2]
            tpos = n_keys + lax.broadcasted_iota(jnp.int32, (1, tw), 1)
            ok = (picked_rows(g, n_tiles - 1, tw) > 0.5) & (tpos <= t_row)
            s = jnp.where(ok, _dot_nt(qr4, kt_ref[0, g]), -jnp.inf)
            carry = _softmax_step((m_ref[g], l_ref[g], acc_ref[g]), s, vt_ref[0, g])
            o_slc = carry[2] / jnp.maximum(carry[1], 1e-30)
            carry = (jnp.full((r, 1), -jnp.inf, F32), jnp.zeros((r, 1), F32),
                     jnp.zeros((r, NSA_DH), F32))
            for kt in range(kw_ref.shape[2] // tkw):
                wpos = wbase + kt * tkw + lax.broadcasted_iota(jnp.int32, (1, tkw), 1)
                rel = t_row - wpos
                ok = (rel >= 0) & (rel < NSA_WIN) & (wpos >= 0)
                s = jnp.where(ok, _dot_nt(qr4, kw_ref[0, g, kt * tkw:(kt + 1) * tkw, :]), -jnp.inf)
                carry = _softmax_step(carry, s, vw_ref[0, g, kt * tkw:(kt + 1) * tkw, :])
            o_win = carry[2] / jnp.maximum(carry[1], 1e-30)
            o_cmp = ocmp_ref[g]
            base = SM_NG + g * N_BRANCH * hp
            outs = []
            for h in range(hp):
                rows = slice(h * tq, (h + 1) * tq)
                c0 = base + N_BRANCH * h
                outs.append(sig[:, c0:c0 + 1] * o_cmp[rows] + sig[:, c0 + 1:c0 + 2] * o_slc[rows]
                            + sig[:, c0 + 2:c0 + 3] * o_win[rows])
            o_ref[:, g * gwid:(g + 1) * gwid] = jnp.concatenate(outs, axis=1)


def _nsa_decode_attn(qc, qr, kc, vc, cache_t, page_table, kw, vw, ktail, vtail, z, tq, pos0, wrel0):
    batch, n_pages = page_table.shape
    n_keys = n_pages * PAGE_SIZE
    steps = n_pages // SLC_PAGES_PER_STEP
    hp = NSA_HEADS // NSA_KV
    r = hp * tq
    qw = NSA_HEADS * NSA_DH
    per_tile = SLC_PAGES_PER_STEP * PAGE_SIZE // NSA_SEL
    assert pos0 == n_keys and ktail.shape[2] <= per_tile * NSA_SEL
    per_b = lambda a: pl.BlockSpec((1,) + a.shape[1:], lambda b, s, pt: (b, 0, 0, 0))
    q_spec = pl.BlockSpec((tq, qw), lambda b, s, pt: (b, 0))
    return pl.pallas_call(
        functools.partial(_nsa_decode_kernel, tq=tq, pos0=pos0, wrel0=wrel0, n_keys=n_keys),
        grid_spec=pltpu.PrefetchScalarGridSpec(
            num_scalar_prefetch=1, grid=(batch, steps),
            in_specs=[q_spec, q_spec, per_b(kc), per_b(vc)]
            + _page_specs((1, ROW_W, PAGE_SIZE), n_pages, SLC_PAGES_PER_STEP)
            + [per_b(kw), per_b(vw), per_b(ktail), per_b(vtail),
               pl.BlockSpec((tq, LANES), lambda b, s, pt: (b, Z_SMALL // LANES))],
            out_specs=pl.BlockSpec((tq, qw), lambda b, s, pt: (b, 0)),
            scratch_shapes=[pltpu.VMEM((NSA_KV, r, 1), F32), pltpu.VMEM((NSA_KV, r, 1), F32),
                            pltpu.VMEM((NSA_KV, r, NSA_DH), F32),
                            pltpu.VMEM((NSA_KV, r, NSA_DH), F32),
                            pltpu.VMEM((NSA_KV, steps + 1, tq, per_tile), F32)]),
        out_shape=jax.ShapeDtypeStruct((batch * tq, qw), F32),
        compiler_params=_params(("parallel", "arbitrary")),
    )(page_table.reshape(-1), qc, qr, kc, vc, *([cache_t] * SLC_PAGES_PER_STEP),
      kw, vw, ktail, vtail, z)


def _mem_attn_kernel(q_ref, kv_ref, o_ref):
    hw = MEM_HEADS * MEM_DH
    for h in range(MEM_HEADS):
        q = q_ref[:, h * MEM_DH:(h + 1) * MEM_DH] * (MEM_DH ** -0.5)
        k = kv_ref[0, :, h * MEM_DH:(h + 1) * MEM_DH]
        v = kv_ref[0, :, hw + h * MEM_DH:hw + (h + 1) * MEM_DH]
        s = _dot_nt(q, k)
        e = jnp.exp(s - jnp.max(s, axis=-1, keepdims=True))
        p = e / jnp.sum(e, axis=-1, keepdims=True)
        o_ref[:, h * MEM_DH:(h + 1) * MEM_DH] = _dot(p, v)


def _mem_attn(z, kv, batch, t, tm):
    nt = t // tm
    hw = MEM_HEADS * MEM_DH
    return pl.pallas_call(
        _mem_attn_kernel,
        grid=(batch, nt),
        in_specs=[pl.BlockSpec((tm, hw), lambda b, j: (b * nt + j, Z_MQ // hw)),
                  pl.BlockSpec((1,) + kv.shape[1:], lambda b, j: (b, 0, 0))],
        out_specs=pl.BlockSpec((tm, hw), lambda b, j: (b * nt + j, 0)),
        out_shape=jax.ShapeDtypeStruct((batch * t, hw), F32),
        compiler_params=_params(("parallel", "parallel")),
    )(z, kv)


def _merge_kernel(x_ref, mg_ref, og_ref, on_ref, om_ref, wg_ref, wn_ref, wm_ref, wo_ref,
                  gf_ref, wq_ref, x1_ref, h2_ref, qp_ref):
    d = D_MODEL
    mix = (jax.nn.sigmoid(mg_ref[:, 0:d]) * _dot(og_ref[...], wg_ref[...])
           + jax.nn.sigmoid(mg_ref[:, d:2 * d]) * _dot(on_ref[...], wn_ref[...])
           + jax.nn.sigmoid(mg_ref[:, 2 * d:3 * d]) * _dot(om_ref[...], wm_ref[...]))
    x1 = x_ref[...] + _dot(mix, wo_ref[...])
    x1_ref[...] = x1
    h2 = _rms(x1, gf_ref[...]).astype(BF16)
    h2_ref[...] = h2
    qp_ref[...] = jnp.dot(h2, wq_ref[...], preferred_element_type=F32)


def _merge(x, z, o_gdn, o_nsa, o_mem, wg, wn, wm, wo, norm_ffn, wq, tm):
    n, d = x.shape
    qw = wq.shape[1]
    row = lambda w: pl.BlockSpec((tm, w), lambda i: (i, 0))
    const = lambda a: pl.BlockSpec(a.shape, lambda i: (0, 0))
    return pl.pallas_call(
        _merge_kernel,
        grid=(n // tm,),
        in_specs=[row(d), pl.BlockSpec((tm, N_BRANCH * d), lambda i: (i, Z_MG)),
                  row(o_gdn.shape[1]), row(o_nsa.shape[1]), row(o_mem.shape[1]),
                  const(wg), const(wn), const(wm), const(wo),
                  pl.BlockSpec((1, d), lambda i: (0, 0)), const(wq)],
        out_specs=[row(d), row(d), row(qw)],
        out_shape=[jax.ShapeDtypeStruct((n, d), F32), jax.ShapeDtypeStruct((n, d), BF16),
                   jax.ShapeDtypeStruct((n, qw), F32)],
        compiler_params=_params(("parallel",)),
    )(x, z, o_gdn, o_nsa, o_mem, wg, wn, wm, wo, norm_ffn.reshape(1, d), wq)


PEER_RANKS = PEER_TOPK + 1


RANK_NONE = 64.0


def _top_values(s, n):
    vals = []
    rank = jnp.full(s.shape, RANK_NONE, F32)
    for k in range(n):
        m = jnp.max(s, axis=0, keepdims=True)
        vals.append(m)
        hit = s >= m
        rank = jnp.where(hit, float(k + 1), rank)
        s = jnp.where(hit, -jnp.inf, s)
    return vals, rank


def _peer_route_kernel(qp_ref, sk_ref, r2_ref, e2_ref, nb_ref, e1_ref):
    half = PEER_DKEY // 2
    nt = (((1,), (1,)), ((), ()))
    for h in range(PEER_HEADS):
        qa = qp_ref[:, h * PEER_DKEY:h * PEER_DKEY + half]
        qb = qp_ref[:, h * PEER_DKEY + half:(h + 1) * PEER_DKEY]
        s1 = lax.dot_general(sk_ref[0], qa, nt, precision=HIGHEST, preferred_element_type=F32)
        s2 = lax.dot_general(sk_ref[1], qb, nt, precision=HIGHEST, preferred_element_type=F32)
        a, _ = _top_values(s1, PEER_RANKS)
        b, rank2 = _top_values(s2, PEER_RANKS)
        b_rows = jnp.concatenate(b, axis=0)
        cands = [a[i - 1] + b_rows[0:PEER_RANKS // i] for i in range(1, PEER_RANKS + 1)]
        work = cands
        ranked = []
        for _ in range(PEER_RANKS):
            m = work[0][0:1]
            for cnd in work:
                m = jnp.maximum(m, jnp.max(cnd, axis=0, keepdims=True))
            ranked.append(m)
            work = [jnp.where(cnd >= m, -jnp.inf, cnd) for cnd in work]
        tau = 0.5 * (ranked[PEER_TOPK - 1] + ranked[PEER_TOPK])
        top = a[0] + b[0]
        zsum = jnp.zeros_like(tau)
        for cnd in cands:
            zsum = zsum + jnp.sum(jnp.where(cnd >= tau, jnp.exp(cnd - top), 0.0),
                                  axis=0, keepdims=True)
        th = tau - s1
        count = jnp.zeros(s1.shape, F32)
        for bj in b:
            count = count + jnp.where(bj >= th, 1.0, 0.0)
        r2_ref[h] = rank2.astype(BF16)
        e2_ref[h] = jnp.exp(s2 - b[0]).astype(BF16)
        nb_ref[h] = count
        e1_ref[h] = jnp.exp(s1 - a[0]) / zsum


def _peer_route(qp, subkeys, tt):
    n = qp.shape[0]
    shape = lambda dt: jax.ShapeDtypeStruct((PEER_HEADS, PEER_NKEYS, n), dt)
    spec = pl.BlockSpec((PEER_HEADS, PEER_NKEYS, tt), lambda i: (0, 0, i))
    return pl.pallas_call(
        _peer_route_kernel,
        grid=(n // tt,),
        in_specs=[pl.BlockSpec((tt, qp.shape[1]), lambda i: (i, 0)),
                  pl.BlockSpec(subkeys.shape, lambda i: (0, 0, 0))],
        out_specs=[spec, spec, spec, spec],
        out_shape=[shape(BF16), shape(BF16), shape(F32), shape(F32)],
        compiler_params=_params(("parallel",)),
    )(qp, subkeys)


def _peer_dense_kernel(ht_ref, u_ref, vt_ref, r2_ref, e2_ref, nb_ref, e1_ref, x1_ref, gf_ref,
                       y_ref, acc_ref, act_ref, wa_ref, r2s_ref, e2s_ref, rows_ref, *, jb):
    j = pl.program_id(1)
    tt = ht_ref.shape[1]

    @pl.when(j == 0)
    def _():
        acc_ref[...] = jnp.zeros(acc_ref.shape, F32)
        r2s_ref[...] = r2_ref[...]
        e2s_ref[...] = e2_ref[...]

    act_ref[...] = _gelu(jnp.dot(u_ref[...], ht_ref[...],
                                 preferred_element_type=F32)).astype(BF16)
    for jj in range(jb):
        i1 = j * jb + jj
        rows = slice(jj * PEER_NKEYS, (jj + 1) * PEER_NKEYS)
        for h in range(PEER_HEADS):
            rows_ref[jj, h:h + 1, :] = nb_ref[h, pl.ds(i1, 1), :]
            rows_ref[jj, PEER_HEADS + h:PEER_HEADS + h + 1, :] = e1_ref[h, pl.ds(i1, 1), :]
        for c in range(tt // LANES):
            cols = slice(c * LANES, (c + 1) * LANES)
            w = None
            for h in range(PEER_HEADS):
                count = rows_ref[jj, h:h + 1, cols].astype(BF16)
                e1 = rows_ref[jj, PEER_HEADS + h:PEER_HEADS + h + 1, cols].astype(BF16)
                picked = r2s_ref[h, :, cols] <= count
                term = jnp.where(picked, e2s_ref[h, :, cols], jnp.zeros((), BF16)) * e1
                w = term if w is None else w + term
            wa_ref[rows, cols] = w * act_ref[rows, cols]
    acc_ref[...] += jnp.dot(vt_ref[...], wa_ref[...], preferred_element_type=F32)

    @pl.when(j == pl.num_programs(1) - 1)
    def _():
        y_ref[...] = _rms(x1_ref[...] + acc_ref[...].T, gf_ref[...])


def _peer_dense(ht, u, vt, s2, e2, th, e1, x1, norm_final, tt, jb):
    d, n = ht.shape
    n_exp = u.shape[0]
    eb = jb * PEER_NKEYS
    route = pl.BlockSpec((PEER_HEADS, PEER_NKEYS, tt), lambda t, j: (0, 0, t))
    return pl.pallas_call(
        functools.partial(_peer_dense_kernel, jb=jb),
        grid=(n // tt, n_exp // eb),
        in_specs=[pl.BlockSpec((d, tt), lambda t, j: (0, t)),
                  pl.BlockSpec((eb, d), lambda t, j: (j, 0)),
                  pl.BlockSpec((d, eb), lambda t, j: (0, j)),
                  route, route, route, route,
                  pl.BlockSpec((tt, d), lambda t, j: (t, 0)),
                  pl.BlockSpec((1, d), lambda t, j: (0, 0))],
        out_specs=pl.BlockSpec((tt, d), lambda t, j: (t, 0)),
        out_shape=jax.ShapeDtypeStruct((n, d), F32),
        scratch_shapes=[pltpu.VMEM((d, tt), F32), pltpu.VMEM((eb, tt), BF16),
                        pltpu.VMEM((eb, tt), BF16),
                        pltpu.VMEM((PEER_HEADS, PEER_NKEYS, tt), BF16),
                        pltpu.VMEM((PEER_HEADS, PEER_NKEYS, tt), BF16),
                        pltpu.VMEM((jb, 2 * PEER_HEADS, tt), F32)],
        compiler_params=_params(("parallel", "arbitrary")),
    )(ht, u, vt, s2, e2, th, e1, x1, norm_final.reshape(1, d))


def _permute_w_in(w_in):
    sizes = (GDN_CONV_CH, GDN_HEADS * GDN_DV, GDN_HEADS, GDN_HEADS, NSA_HEADS * NSA_DH, NKV_W,
             N_BRANCH * NSA_HEADS, MEM_HEADS * MEM_DH, N_BRANCH * D_MODEL)
    qkv, zg, a, b, nq, nkv, ng, mq, mg = jnp.split(w_in, np.cumsum(sizes)[:-1].tolist(), axis=1)
    pad = jnp.zeros((w_in.shape[0], LANES - a.shape[1] - b.shape[1] - ng.shape[1]), w_in.dtype)
    return jnp.concatenate([mg, qkv, zg, nq, mq, nkv, a, b, ng, pad], axis=1).astype(BF16)


def _tokens_tile(n, pref):
    return pref if n % pref == 0 else n


def _layer(x, pos0, kv_mem, nsa_keys, gdn_state, conv_buf, w, peer_tt):
    batch, t, d = x.shape
    n = batch * t
    xf = x.reshape(n, d)
    z = _norm_matmul(xf, w['norm_attn'], w['w_in'], _tokens_tile(n, 256), 640)

    o_gdn, s_new, conv_new = _gdn(z, conv_buf, gdn_state, w['gdn_conv'], w['gdn_a_log'],
                                  w['gdn_dt_bias'], w['gdn_norm'], batch, t)

    prep_tm = _tokens_tile(t, 512)
    kv_dtype = BF16 if prep_tm % 16 == 0 else F32
    qc, qr, slc_rows, win_rows, ks, vs, kw, vw = _nsa_prep(
        z, pos0 + jnp.arange(t), batch, t, prep_tm, kv_dtype)
    cmp_rows = z[:, Z_NKV:Z_NKV + ROW_W]
    o_nsa = nsa_keys(z, qc, qr, cmp_rows, ks, vs, kw, vw)

    o_mem = _mem_attn(z, kv_mem, batch, t, _tokens_tile(t, 512))

    x1, h2, qp = _merge(xf, z, o_gdn, o_nsa, o_mem, w['w_gdn_out'], w['w_nsa_out'],
                        w['w_mem_out'], w['w_o'], w['norm_ffn'], w['peer_wq'],
                        _tokens_tile(n, 256))
    s2, e2, th, e1 = _peer_route(qp, w['peer_subkeys'], 256)
    y = _peer_dense(h2.T, w['peer_u'], w['peer_vt'], s2, e2, th, e1, x1, w['norm_final'],
                    peer_tt, 8)
    row5 = lambda a: a.reshape(batch, t, 2, NSA_KV, NSA_DH)
    return (y.reshape(batch, t, d), row5(cmp_rows), row5(slc_rows), row5(win_rows), s_new,
            conv_new)


def kernel(x_prompt, x_sample, cache_mem_kv, cache_cmp_kv, cache_slc_kv, cache_win_kv, state_gdn, state_conv, page_table, mem_prompt, norm_attn, w_in, gdn_conv, gdn_a_log, gdn_dt_bias, gdn_norm, nsa_cmp_pe, nsa_cmp_w1, nsa_cmp_w2, norm_mem, w_mem_kv, w_gdn_out, w_nsa_out, w_mem_out, w_o, norm_ffn, peer_wq, peer_subkeys, peer_u, peer_v, norm_final):
    depth = w_in.shape[0]
    assert depth == 1
    l = 0
    bp, seq, d = x_prompt.shape
    bs, tdec, _ = x_sample.shape
    n_pages = page_table.shape[1]
    past = n_pages * PAGE_SIZE
    assert past % NSA_CMP == 0 and tdec < NSA_CMP and seq % LANES == 0

    pe_big, w1big, w2big = _compress_weights(nsa_cmp_pe[l], nsa_cmp_w1[l], nsa_cmp_w2[l])
    w = dict(norm_attn=norm_attn[l], w_in=_permute_w_in(w_in[l]), gdn_conv=gdn_conv[l],
             gdn_a_log=gdn_a_log[l], gdn_dt_bias=gdn_dt_bias[l], gdn_norm=gdn_norm[l],
             w_gdn_out=w_gdn_out[l].astype(BF16), w_nsa_out=w_nsa_out[l].astype(BF16),
             w_mem_out=w_mem_out[l].astype(BF16), w_o=w_o[l].astype(BF16), norm_ffn=norm_ffn[l],
             peer_wq=peer_wq[l].astype(BF16), peer_subkeys=peer_subkeys[l],
             peer_u=peer_u[l].astype(BF16), peer_vt=peer_v[l].astype(BF16).T,
             norm_final=norm_final)

    mem_n = mem_prompt.shape[0] * mem_prompt.shape[1]
    kvm = _norm_matmul(mem_prompt.reshape(mem_n, d), norm_mem[l], w_mem_kv[l].astype(BF16),
                       _tokens_tile(mem_n, 256), 512).reshape(bp, mem_prompt.shape[1], -1)

    def prompt_keys(z, qc, qr, cmp_rows, ks, vs, kw, vw):
        n_blk = bp * seq // NSA_CMP
        cmp_out = _compress(cmp_rows.reshape(n_blk, NSA_CMP * ROW_W), pe_big, w1big, w2big,
                            _tokens_tile(n_blk, 128))
        kc, vc = _split_compressed(cmp_out, bp)
        tr = lambda a: jnp.transpose(a, (0, 1, 3, 2))
        return _nsa_prompt_attn(qc, qr, kc, tr(vc), ks, tr(vs), kw, tr(vw), z, bp, seq,
                                128, _tokens_tile(seq, 512), 128)

    yp, cmp_p, slc_p, win_p, gdn_p, conv_p = _layer(
        x_prompt, 0, kvm, prompt_keys,
        jnp.zeros((bp, GDN_HEADS, GDN_DK, GDN_DV), F32),
        jnp.zeros((bp, GDN_CONV - 1, GDN_CONV_CH), F32), w, 512)
    win_len_p = min(NSA_WIN, seq)
    win_p = win_p[:, seq - win_len_p:]

    n_pool = cache_cmp_kv.shape[1]
    feature_major = lambda c: jnp.transpose(c, (0, 2, 3, 4, 1)).reshape(n_pool, ROW_W, PAGE_SIZE)
    cache_cmp = feature_major(cache_cmp_kv[l])
    cache_slc = feature_major(cache_slc_kv[l])
    cache_win = cache_win_kv[l].reshape(bs, -1, ROW_W)
    wb = cache_win.shape[1]
    assert wb == NSA_WIN
    kv_w = NSA_KV * NSA_DH

    def pad_rows(a, rows):
        return jnp.pad(a, ((0, 0), (0, 0), (0, rows - a.shape[2]), (0, 0))).astype(BF16)

    def split_rows(rows):
        r = rows.reshape(bs, rows.shape[1], 2, NSA_KV, NSA_DH)
        return jnp.transpose(r[:, :, 0], (0, 2, 1, 3)), jnp.transpose(r[:, :, 1], (0, 2, 1, 3))

    def sample_keys(z, qc, qr, cmp_rows, ks_new, vs_new, kw_new, vw_new):
        cmp_out = _paged_compress(cache_cmp, page_table, pe_big, w1big, w2big)
        kc, vc = _split_compressed(cmp_out, bs)
        kwc, vwc = split_rows(cache_win)
        win_rows = wb + LANES
        kw = pad_rows(jnp.concatenate([kwc, kw_new], axis=2), win_rows)
        vw = pad_rows(jnp.concatenate([vwc, vw_new], axis=2), win_rows)
        return _nsa_decode_attn(qc, qr, kc, vc, cache_slc, page_table, kw, vw,
                                pad_rows(ks_new, LANES), pad_rows(vs_new, LANES), z,
                                tdec, past, wb)

    ys, cmp_s, slc_s, win_new, gdn_s, conv_s = _layer(
        x_sample, past, cache_mem_kv[l].reshape(bs, cache_mem_kv.shape[2], -1), sample_keys,
        state_gdn[l], state_conv[l], w, 256)
    win_all = jnp.concatenate([cache_win_kv[l], win_new], axis=1)
    win_s = win_all[:, win_all.shape[1] - min(NSA_WIN, past + tdec):]

    stack = lambda a: a[None]
    return (yp, ys, stack(kvm.reshape(bp, mem_prompt.shape[1], 2, MEM_HEADS, MEM_DH)),
            stack(cmp_p), stack(slc_p), stack(win_p), stack(gdn_p), stack(conv_p),
            stack(cmp_s), stack(slc_s), stack(win_s), stack(gdn_s), stack(conv_s))
```

```python
import functools
import math

import jax
import jax.numpy as jnp
import numpy as np
from jax import lax
from jax.experimental import pallas as pl
from jax.experimental.pallas import tpu as pltpu

F32 = jnp.float32
BF16 = jnp.bfloat16
HIGHEST = lax.Precision.HIGHEST

D_MODEL = 1024
PAGE_SIZE = 128
GDN_HEADS = 4
GDN_DK = 128
GDN_DV = 128
GDN_CONV = 4
GDN_CHUNK = 64
GDN_CONV_CH = GDN_HEADS * (2 * GDN_DK + GDN_DV)
NSA_HEADS = 8
NSA_KV = 2
NSA_DH = 64
NSA_CMP = 32
NSA_SEL = 64
NSA_TOPN = 16
NSA_WIN = 512
NSA_FORCE = 1e9
MEM_HEADS = 4
MEM_DH = 128
PEER_HEADS = 8
PEER_NKEYS = 128
PEER_DKEY = 256
PEER_TOPK = 16
N_BRANCH = 3
ROPE_THETA = 10000.0
EPS = 1e-6

LANES = 128
SUBLANES = 8
VMEM_LIMIT = 56 * 1024 * 1024

Z_MG = 0
Z_QKV = Z_MG + N_BRANCH * D_MODEL
Z_ZG = Z_QKV + GDN_CONV_CH
Z_NQ = Z_ZG + GDN_HEADS * GDN_DV
Z_MQ = Z_NQ + NSA_HEADS * NSA_DH
Z_NKV = Z_MQ + MEM_HEADS * MEM_DH
Z_SMALL = Z_NKV + 3 * 2 * NSA_KV * NSA_DH
Z_WIDTH = Z_SMALL + LANES
SM_A = 0
SM_B = GDN_HEADS
SM_NG = 2 * GDN_HEADS
NKV_W = 3 * 2 * NSA_KV * NSA_DH
ROW_W = 2 * NSA_KV * NSA_DH


def _params(sem, vmem=VMEM_LIMIT):
    return pltpu.CompilerParams(dimension_semantics=sem, vmem_limit_bytes=vmem)


def _dot(a, b):
    return jnp.dot(a.astype(BF16), b.astype(BF16), preferred_element_type=F32)


def _dot_nt(a, b):
    return lax.dot_general(a.astype(BF16), b.astype(BF16), (((1,), (1,)), ((), ())),
                           preferred_element_type=F32)


def _dot_hi(a, b):
    return jnp.dot(a, b, precision=HIGHEST, preferred_element_type=F32)


def _dot3(a, b):
    ah = a.astype(BF16)
    bh = b.astype(BF16)
    al = (a - ah.astype(F32)).astype(BF16)
    bl = (b - bh.astype(F32)).astype(BF16)
    d = lambda x, y: jnp.dot(x, y, preferred_element_type=F32)
    return d(ah, bh) + d(al, bh) + d(ah, bl)


def _rms(x, g):
    return x * lax.rsqrt(jnp.mean(x * x, axis=-1, keepdims=True) + EPS) * g


def _gelu(x):
    a = -2.0 * math.sqrt(2.0 / math.pi)
    return x / (1.0 + jnp.exp(x * (a + (a * 0.044715) * (x * x))))


def _norm_matmul_kernel(x_ref, g_ref, w_ref, o_ref, *, col_chunk):
    yb = _rms(x_ref[...], g_ref[...]).astype(BF16)
    for c0 in range(0, o_ref.shape[1], col_chunk):
        o_ref[:, c0:c0 + col_chunk] = jnp.dot(yb, w_ref[:, c0:c0 + col_chunk],
                                              preferred_element_type=F32)


def _norm_matmul(x, g, w, tm, col_chunk):
    n, d = x.shape
    wc = w.shape[1]
    return pl.pallas_call(
        functools.partial(_norm_matmul_kernel, col_chunk=col_chunk),
        grid=(n // tm,),
        in_specs=[pl.BlockSpec((tm, d), lambda i: (i, 0)),
                  pl.BlockSpec((1, d), lambda i: (0, 0)),
                  pl.BlockSpec((d, wc), lambda i: (0, 0), pipeline_mode=pl.Buffered(1))],
        out_specs=pl.BlockSpec((tm, wc), lambda i: (i, 0)),
        out_shape=jax.ShapeDtypeStruct((n, wc), F32),
        compiler_params=_params(("parallel",)),
    )(x, g.reshape(1, d), w)


def _tri_inverse(lmats, c):
    row = lax.broadcasted_iota(jnp.int32, (c, c), 0)
    col = lax.broadcasted_iota(jnp.int32, (c, c), 1)
    ident = jnp.where(row == col, 1.0, 0.0)
    xs = [ident - lm for lm in lmats]
    ps = [_dot3(lm, lm) for lm in lmats]
    n = 2
    while n < c:
        xs = [x + _dot3(x, p) for x, p in zip(xs, ps)]
        n *= 2
        if n < c:
            ps = [_dot3(p, p) for p in ps]
    return xs


def _gdn_kernel(qkv_ref, zg_ref, sm_ref, buf_ref, s0_ref, cw_ref, alog_ref, dt_ref, gn_ref,
                o_ref, snew_ref, cnew_ref, ext_ref, s_ref, *, tb, n_chunks, nb):
    ci = pl.program_id(1)

    @pl.when(ci == 0)
    def _():
        for i in range(nb):
            ext_ref[i, 0:SUBLANES, :] = buf_ref[i]
            s_ref[i] = s0_ref[i]

    c = GDN_CHUNK
    row = lax.broadcasted_iota(jnp.int32, (c, c), 0)
    col = lax.broadcasted_iota(jnp.int32, (c, c), 1)
    tril = row >= col
    eye = row == col
    hk = GDN_HEADS * GDN_DK
    chains = []
    for i in range(nb):
        u, g_all, beta_all = _gdn_inputs(qkv_ref.at[i], sm_ref.at[i], cw_ref, alog_ref, dt_ref,
                                         cnew_ref.at[i], ext_ref.at[i], tb)
        gc_all = _dot_hi(jnp.where(tril, 1.0, 0.0), g_all)
        for h in range(GDN_HEADS):
            qh = u[:, h * GDN_DK:(h + 1) * GDN_DK]
            kh = u[:, hk + h * GDN_DK:hk + (h + 1) * GDN_DK]
            ch = dict(i=i, h=h, v=u[:, 2 * hk + h * GDN_DV:2 * hk + (h + 1) * GDN_DV])
            ch['q'] = qh * lax.rsqrt(jnp.sum(qh * qh, axis=-1, keepdims=True) + EPS) * (GDN_DK ** -0.5)
            ch['k'] = kh * lax.rsqrt(jnp.sum(kh * kh, axis=-1, keepdims=True) + EPS)
            ch['beta'] = beta_all[:, SM_B + h:SM_B + h + 1]
            gc = gc_all[:, SM_A + h:SM_A + h + 1]
            ch['gc'] = gc
            ch['gl'] = gc_all[c - 1:c, SM_A + h:SM_A + h + 1]
            gc_row = jnp.sum(jnp.where(eye, gc, 0.0), axis=0, keepdims=True)
            ch['decay'] = jnp.exp(jnp.where(tril, gc - gc_row, -jnp.inf))
            ch['kb'] = ch['k'] * ch['beta']
            ch['egc'] = jnp.exp(gc)
            chains.append(ch)
    lmats = [jnp.where(row > col, _dot_nt(ch['kb'], ch['k']) * ch['decay'], 0.0) for ch in chains]
    tinvs = _tri_inverse(lmats, c)
    uus = [_dot(t, ch['v'] * ch['beta']) for t, ch in zip(tinvs, chains)]
    wws = [_dot(t, ch['kb'] * ch['egc']) for t, ch in zip(tinvs, chains)]
    aqks = [_dot_nt(ch['q'], ch['k']) * ch['decay'] for ch in chains]
    states = [s_ref[ch['i'], ch['h']] for ch in chains]
    v_news = [uu - _dot(ww, s) for uu, ww, s in zip(uus, wws, states)]
    outs = [_dot(ch['q'] * ch['egc'], s) + _dot(aqk, vn)
            for ch, s, aqk, vn in zip(chains, states, aqks, v_news)]
    s_news = [s * jnp.exp(ch['gl']) + _dot((ch['k'] * jnp.exp(ch['gl'] - ch['gc'])).T, vn)
              for ch, s, vn in zip(chains, states, v_news)]
    for ch, o, s_new in zip(chains, outs, s_news):
        i, h = ch['i'], ch['h']
        s_ref[i, h] = s_new
        zh = zg_ref[i, :, h * GDN_DV:(h + 1) * GDN_DV]
        o_ref[i, :, h * GDN_DV:(h + 1) * GDN_DV] = (_rms(o[0:tb], gn_ref[...])
                                                    * (zh * jax.nn.sigmoid(zh)))

    @pl.when(ci == n_chunks - 1)
    def _():
        snew_ref[...] = s_ref[...]


def _gdn_inputs(qkv_ref, sm_ref, cw_ref, alog_ref, dt_ref, cnew_ref, ext_ref, tb):
    c = GDN_CHUNK
    if tb < c:
        ext_ref[SUBLANES + tb:, :] = jnp.zeros((c - tb, GDN_CONV_CH), F32)
    ext_ref[SUBLANES:SUBLANES + tb, :] = qkv_ref[...]
    cw = cw_ref[...]
    conv = cw[0:1] * ext_ref[SUBLANES - 3:SUBLANES - 3 + c, :]
    for j in range(1, GDN_CONV):
        conv = conv + cw[j:j + 1] * ext_ref[SUBLANES - 3 + j:SUBLANES - 3 + j + c, :]
    u = conv * jax.nn.sigmoid(conv)
    last_rows = ext_ref[tb:tb + SUBLANES, :]
    cnew_ref[...] = last_rows
    ext_ref[0:SUBLANES, :] = last_rows

    sm = sm_ref[...]
    if tb < c:
        sm = jnp.concatenate([sm, jnp.zeros((c - tb, LANES), F32)], axis=0)
    za = sm + dt_ref[...]
    softplus = jnp.maximum(za, 0.0) + jnp.log1p(jnp.exp(-jnp.abs(za)))
    g_all = -jnp.exp(alog_ref[...]) * softplus
    beta_all = jax.nn.sigmoid(sm)
    if tb < c:
        valid = lax.broadcasted_iota(jnp.int32, (c, 1), 0) < tb
        u = jnp.where(valid, u, 0.0)
        g_all = jnp.where(valid, g_all, 0.0)
        beta_all = jnp.where(valid, beta_all, 0.0)
    return u, g_all, beta_all


GDN_BATCH_PER_STEP = 2


def _gdn(z, conv_buf, s0, conv_w, a_log, dt_bias, gnorm, batch, t):
    c = GDN_CHUNK
    tb = min(t, c)
    n_chunks = t // tb
    nb = GDN_BATCH_PER_STEP if batch % GDN_BATCH_PER_STEP == 0 else 1
    assert tb % SUBLANES == 0 and n_chunks * tb == t and (tb == c or n_chunks == 1)
    buf8 = jnp.pad(conv_buf, ((0, 0), (SUBLANES - (GDN_CONV - 1), 0), (0, 0)))
    alog_row = jnp.zeros((1, LANES), F32).at[0, SM_A:SM_A + GDN_HEADS].set(a_log)
    dt_row = jnp.zeros((1, LANES), F32).at[0, SM_A:SM_A + GDN_HEADS].set(dt_bias)
    z3 = z.reshape(batch, t, Z_WIDTH)
    hv = GDN_HEADS * GDN_DV
    o, s_new, c_new = pl.pallas_call(
        functools.partial(_gdn_kernel, tb=tb, n_chunks=n_chunks, nb=nb),
        grid=(batch // nb, n_chunks),
        in_specs=[
            pl.BlockSpec((nb, tb, GDN_CONV_CH), lambda b, ci: (b, ci, Z_QKV // GDN_CONV_CH)),
            pl.BlockSpec((nb, tb, hv), lambda b, ci: (b, ci, Z_ZG // hv)),
            pl.BlockSpec((nb, tb, LANES), lambda b, ci: (b, ci, Z_SMALL // LANES)),
            pl.BlockSpec((nb, SUBLANES, GDN_CONV_CH), lambda b, ci: (b, 0, 0)),
            pl.BlockSpec((nb, GDN_HEADS, GDN_DK, GDN_DV), lambda b, ci: (b, 0, 0, 0)),
            pl.BlockSpec((GDN_CONV, GDN_CONV_CH), lambda b, ci: (0, 0)),
            pl.BlockSpec((1, LANES), lambda b, ci: (0, 0)),
            pl.BlockSpec((1, LANES), lambda b, ci: (0, 0)),
            pl.BlockSpec((1, GDN_DV), lambda b, ci: (0, 0)),
        ],
        out_specs=[
            pl.BlockSpec((nb, tb, hv), lambda b, ci: (b, ci, 0)),
            pl.BlockSpec((nb, GDN_HEADS, GDN_DK, GDN_DV), lambda b, ci: (b, 0, 0, 0)),
            pl.BlockSpec((nb, SUBLANES, GDN_CONV_CH), lambda b, ci: (b, 0, 0)),
        ],
        out_shape=[
            jax.ShapeDtypeStruct((batch, t, hv), F32),
            jax.ShapeDtypeStruct((batch, GDN_HEADS, GDN_DK, GDN_DV), F32),
            jax.ShapeDtypeStruct((batch, SUBLANES, GDN_CONV_CH), F32),
        ],
        scratch_shapes=[pltpu.VMEM((nb, SUBLANES + c, GDN_CONV_CH), F32),
                        pltpu.VMEM((nb, GDN_HEADS, GDN_DK, GDN_DV), F32)],
        compiler_params=_params(("parallel", "arbitrary")),
    )(z3, z3, z3, buf8, s0, conv_w, alog_row, dt_row, gnorm.reshape(1, GDN_DV))
    return o.reshape(batch * t, hv), s_new, c_new[:, SUBLANES - (GDN_CONV - 1):]


def _rope_tables(pos):
    half = NSA_DH // 2
    inv = jnp.power(ROPE_THETA, -jnp.arange(half, dtype=F32) / half)
    ang = pos.astype(F32)[:, None] * inv[None, :]
    cos, sin = jnp.cos(ang), jnp.sin(ang)
    cos_t = jnp.concatenate([cos, cos, cos, cos], axis=-1)
    sin_t = jnp.concatenate([-sin, sin, -sin, sin], axis=-1)
    return cos_t, sin_t


def _nsa_prep_kernel(nq_ref, nkv_ref, cos_ref, sin_ref,
                     qc_ref, qr_ref, slc_ref, win_ref, ks_ref, vs_ref, kw_ref, vw_ref):
    cos = cos_ref[...]
    sin = sin_ref[...]
    lane = lax.broadcasted_iota(jnp.int32, cos.shape, 1)
    first_half = (lane % NSA_DH) < (NSA_DH // 2)

    def rope(x):
        swapped = jnp.where(first_half, pltpu.roll(x, LANES - NSA_DH // 2, 1),
                            pltpu.roll(x, NSA_DH // 2, 1))
        return x * cos + swapped * sin

    scale = NSA_DH ** -0.5
    for j in range(NSA_HEADS * NSA_DH // LANES):
        x = nq_ref[:, j * LANES:(j + 1) * LANES]
        qc_ref[:, j * LANES:(j + 1) * LANES] = x * scale
        qr_ref[:, j * LANES:(j + 1) * LANES] = rope(x) * scale

    kv_w = NSA_KV * NSA_DH
    for br, (row_ref, k_ref, v_ref) in enumerate(((slc_ref, ks_ref, vs_ref),
                                                   (win_ref, kw_ref, vw_ref))):
        base = (br + 1) * ROW_W
        kr = rope(nkv_ref[:, base:base + kv_w])
        v = nkv_ref[:, base + kv_w:base + 2 * kv_w]
        row_ref[:, 0:kv_w] = kr
        row_ref[:, kv_w:2 * kv_w] = v
        for g in range(NSA_KV):
            k_ref[0, g] = kr[:, g * NSA_DH:(g + 1) * NSA_DH].astype(k_ref.dtype)
            v_ref[0, g] = v[:, g * NSA_DH:(g + 1) * NSA_DH].astype(v_ref.dtype)


def _nsa_prep(z, pos, batch, t, tm, kv_dtype):
    cos_t, sin_t = _rope_tables(pos)
    nt = t // tm
    rowblk = lambda b, j: b * nt + j
    kv_shape = jax.ShapeDtypeStruct((batch, NSA_KV, t, NSA_DH), kv_dtype)
    kv_spec = pl.BlockSpec((1, NSA_KV, tm, NSA_DH), lambda b, j: (b, 0, j, 0))
    qw = NSA_HEADS * NSA_DH
    return pl.pallas_call(
        _nsa_prep_kernel,
        grid=(batch, nt),
        in_specs=[pl.BlockSpec((tm, qw), lambda b, j: (rowblk(b, j), Z_NQ // qw)),
                  pl.BlockSpec((tm, NKV_W), lambda b, j: (rowblk(b, j), Z_NKV // NKV_W)),
                  pl.BlockSpec((tm, LANES), lambda b, j: (j, 0)),
                  pl.BlockSpec((tm, LANES), lambda b, j: (j, 0))],
        out_specs=[pl.BlockSpec((tm, qw), lambda b, j: (rowblk(b, j), 0)),
                   pl.BlockSpec((tm, qw), lambda b, j: (rowblk(b, j), 0)),
                   pl.BlockSpec((tm, ROW_W), lambda b, j: (rowblk(b, j), 0)),
                   pl.BlockSpec((tm, ROW_W), lambda b, j: (rowblk(b, j), 0)),
                   kv_spec, kv_spec, kv_spec, kv_spec],
        out_shape=[jax.ShapeDtypeStruct((batch * t, qw), F32),
                   jax.ShapeDtypeStruct((batch * t, qw), F32),
                   jax.ShapeDtypeStruct((batch * t, ROW_W), F32),
                   jax.ShapeDtypeStruct((batch * t, ROW_W), F32),
                   kv_shape, kv_shape, kv_shape, kv_shape],
        compiler_params=_params(("parallel", "parallel")),
    )(z, z, cos_t, sin_t)


def _compress_weights(pe, w1, w2):
    eye = jnp.eye(2, dtype=F32)
    w1r = w1.reshape(2, NSA_CMP, NSA_DH, NSA_DH)
    w1big = jnp.einsum('srde,st,gh->rsgdthe', w1r, eye, eye).reshape(NSA_CMP * ROW_W, ROW_W)
    w2big = jnp.einsum('sed,st,gh->sgethd', w2, eye, eye).reshape(ROW_W, ROW_W)
    pe_big = jnp.broadcast_to(jnp.transpose(pe, (1, 0, 2))[:, :, None, :],
                              (NSA_CMP, 2, NSA_KV, NSA_DH)).reshape(1, NSA_CMP * ROW_W)
    return pe_big, w1big.astype(BF16), w2big.astype(BF16)


def _compress_kernel(x_ref, pe_ref, w1_ref, w2_ref, o_ref):
    x = (x_ref[...] + pe_ref[...]).astype(BF16)
    hid = _gelu(jnp.dot(x, w1_ref[...], preferred_element_type=F32))
    o_ref[...] = jnp.dot(hid.astype(BF16), w2_ref[...], preferred_element_type=F32)


def _compress(rows, pe_big, w1big, w2big, tm):
    n, kdim = rows.shape
    return pl.pallas_call(
        _compress_kernel,
        grid=(n // tm,),
        in_specs=[pl.BlockSpec((tm, kdim), lambda i: (i, 0)),
                  pl.BlockSpec((1, kdim), lambda i: (0, 0)),
                  pl.BlockSpec((kdim, ROW_W), lambda i: (0, 0)),
                  pl.BlockSpec((ROW_W, ROW_W), lambda i: (0, 0))],
        out_specs=pl.BlockSpec((tm, ROW_W), lambda i: (i, 0)),
        out_shape=jax.ShapeDtypeStruct((n, ROW_W), F32),
        compiler_params=_params(("parallel",)),
    )(rows, pe_big, w1big, w2big)


def _split_compressed(cmp_out, batch):
    nb = cmp_out.shape[0] // batch
    x = cmp_out.reshape(batch, nb // 2, 2, 2, NSA_KV, NSA_DH)
    x = jnp.transpose(x, (3, 0, 4, 2, 1, 5)).reshape(2, batch, NSA_KV, nb, NSA_DH)
    return x[0].astype(BF16), x[1].astype(BF16)


CMP_PAGES_PER_STEP = 32
SLC_PAGES_PER_STEP = 8
BLOCKS_PER_PAGE = PAGE_SIZE // NSA_CMP


def _page_specs(block, n_pages, per_step):
    def spec(j):
        return pl.BlockSpec(block, lambda b, s, pt: (pt[b * n_pages + s * per_step + j], 0, 0))
    return [spec(j) for j in range(per_step)]


CMP_ROW_PITCH = NSA_CMP + 4


def _paged_compress_kernel(pt_ref, *refs):
    n_in = CMP_PAGES_PER_STEP
    pe_ref, w1_ref, w2_ref, o_ref, x_ref = refs[n_in:]
    m = n_in * BLOCKS_PER_PAGE
    n_slabs = ROW_W // LANES
    for j, p_ref in enumerate(refs[:n_in]):
        x = p_ref[0].T
        for n in range(BLOCKS_PER_PAGE):
            base = (j * BLOCKS_PER_PAGE + n) * CMP_ROW_PITCH
            for sl in range(n_slabs):
                x_ref[sl, base:base + NSA_CMP, :] = x[n * NSA_CMP:(n + 1) * NSA_CMP,
                                                      sl * LANES:(sl + 1) * LANES]
    acc = jnp.zeros((m, ROW_W), F32)
    for r in range(NSA_CMP):
        lhs = jnp.concatenate([x_ref[sl, pl.ds(r, m, stride=CMP_ROW_PITCH), :]
                               for sl in range(n_slabs)], axis=1)
        lhs = (lhs + pe_ref[r:r + 1, :]).astype(BF16)
        acc = acc + jnp.dot(lhs, w1_ref[r], preferred_element_type=F32)
    o_ref[...] = jnp.dot(_gelu(acc).astype(BF16), w2_ref[...], preferred_element_type=F32)


def _paged_compress(cache_t, page_table, pe_big, w1big, w2big):
    batch, n_pages = page_table.shape
    rows = CMP_PAGES_PER_STEP * BLOCKS_PER_PAGE
    steps = n_pages // CMP_PAGES_PER_STEP
    pe_rows = pe_big.reshape(NSA_CMP, ROW_W)
    w1_rows = w1big.reshape(NSA_CMP, ROW_W, ROW_W)
    const = lambda a: pl.BlockSpec(a.shape, lambda b, s, pt: (0,) * a.ndim)
    return pl.pallas_call(
        _paged_compress_kernel,
        grid_spec=pltpu.PrefetchScalarGridSpec(
            num_scalar_prefetch=1, grid=(batch, steps),
            in_specs=_page_specs((1, ROW_W, PAGE_SIZE), n_pages, CMP_PAGES_PER_STEP)
            + [const(pe_rows), const(w1_rows), const(w2big)],
            out_specs=pl.BlockSpec((rows, ROW_W), lambda b, s, pt: (b * steps + s, 0)),
            scratch_shapes=[pltpu.VMEM((ROW_W // LANES, rows * CMP_ROW_PITCH, LANES), F32)]),
        out_shape=jax.ShapeDtypeStruct((batch * n_pages * BLOCKS_PER_PAGE, ROW_W), F32),
        compiler_params=_params(("parallel", "parallel")),
    )(page_table.reshape(-1), *([cache_t] * CMP_PAGES_PER_STEP), pe_rows, w1_rows, w2big)


def _topk_mask(score, k):
    n = score.shape[-1]
    lane = lax.broadcasted_iota(jnp.int32, score.shape, 1).astype(F32)
    sel = jnp.zeros(score.shape, F32)
    for _ in range(k):
        m = jnp.max(score, axis=-1, keepdims=True)
        idx = jnp.min(jnp.where(score == m, lane, float(n)), axis=-1, keepdims=True)
        pick = lane == idx
        sel = jnp.where(pick, 1.0, sel)
        score = jnp.where(pick, -jnp.inf, score)
    return sel


def _softmax_step(carry, s, v):
    m, l, acc = carry
    m_new = jnp.maximum(m, jnp.max(s, axis=-1, keepdims=True))
    m_safe = jnp.where(m_new == -jnp.inf, 0.0, m_new)
    p = jnp.exp(s - m_safe)
    alpha = jnp.exp(m - m_safe)
    l = alpha * l + jnp.sum(p, axis=-1, keepdims=True)
    acc = alpha * acc + _dot(p, v)
    return m_new, l, acc


def _nsa_attn_kernel(*refs, tq, pos0, wrel0, n_keys, tk, has_tail):
    if has_tail:
        (qc_ref, qr_ref, kc_ref, vc_ref, ks_ref, vs_ref, kw_ref, vw_ref, kt_ref, vt_ref,
         sm_ref, o_ref) = refs
    else:
        (qc_ref, qr_ref, kc_ref, vc_ref, ks_ref, vs_ref, kw_ref, vw_ref, sm_ref, o_ref) = refs
    hp = NSA_HEADS // NSA_KV
    r = hp * tq
    g = pl.program_id(1)
    qi = pl.program_id(2)
    ncp = kc_ref.shape[2]
    nch = ncp // 2
    nsl = -(-(n_keys // NSA_SEL + (1 if has_tail else 0)) // LANES) * LANES

    def stack_heads(ref):
        x = ref[...]
        return jnp.concatenate([x[:, h * NSA_DH:(h + 1) * NSA_DH] for h in range(hp)],
                               axis=0).astype(BF16)

    qc4 = stack_heads(qc_ref)
    qr4 = stack_heads(qr_ref)
    t_q = pos0 + qi * tq + lax.broadcasted_iota(jnp.int32, (tq, 1), 0)
    t_row = jnp.concatenate([t_q] * hp, axis=0)

    s = _dot_nt(qc4, kc_ref[0, 0])
    ccol = lax.broadcasted_iota(jnp.int32, (r, ncp), 1)
    cblk = 2 * (ccol % nch) + ccol // nch
    s = jnp.where((cblk + 1) * NSA_CMP <= t_row + 1, s, -jnp.inf)
    m = jnp.max(s, axis=-1, keepdims=True)
    e = jnp.exp(s - jnp.where(m == -jnp.inf, 0.0, m))
    p_cmp = e / jnp.maximum(jnp.sum(e, axis=-1, keepdims=True), 1e-30)
    o_cmp = _dot(p_cmp, vc_ref[0, 0])

    imp = p_cmp[0:tq]
    for h in range(1, hp):
        imp = imp + p_cmp[h * tq:(h + 1) * tq]
    imp = imp[:, :nch] + imp[:, nch:]
    if nsl > nch:
        imp = jnp.concatenate([imp, jnp.zeros((tq, nsl - nch), F32)], axis=1)
    blk = lax.broadcasted_iota(jnp.int32, (tq, nsl), 1)
    cur = t_q // NSA_SEL
    forced = (blk == 0) | (blk == cur) | (blk == cur - 1)
    score = jnp.where(blk > cur, -jnp.inf, jnp.where(forced, NSA_FORCE, imp))
    sel = _topk_mask(score, NSA_TOPN).astype(BF16)

    init = (jnp.full((r, 1), -jnp.inf, F32), jnp.zeros((r, 1), F32), jnp.zeros((r, NSA_DH), F32))

    def slc_scores(k, kpos0, width):
        kpos = kpos0 + lax.broadcasted_iota(jnp.int32, (1, width), 1)
        expand = (lax.broadcasted_iota(jnp.int32, (nsl, width), 0) == kpos // NSA_SEL)
        picked = jnp.dot(sel, expand.astype(BF16), preferred_element_type=F32)
        picked = jnp.concatenate([picked] * hp, axis=0)
        return jnp.where((picked > 0.5) & (kpos <= t_row), _dot_nt(qr4, k), -jnp.inf)

    def slc_body(kt, carry):
        start = pl.multiple_of(kt * tk, tk)
        k = ks_ref[0, 0, pl.ds(start, tk), :]
        v = vs_ref[0, 0, pl.ds(start, tk), :]
        return _softmax_step(carry, slc_scores(k, kt * tk, tk), v)

    t_last = pos0 + qi * tq + tq - 1
    n_main = jnp.minimum(n_keys // tk, t_last // tk + 1)
    carry = lax.fori_loop(0, n_main, slc_body, init)
    if has_tail:
        carry = _softmax_step(carry, slc_scores(kt_ref[0, 0], n_keys, kt_ref.shape[2]), vt_ref[0, 0])
    o_slc = carry[2] / jnp.maximum(carry[1], 1e-30)

    wbase = pos0 - wrel0

    def win_body(kt, carry):
        start = pl.multiple_of(kt * tk, tk)
        k = kw_ref[0, 0, pl.ds(start, tk), :]
        v = vw_ref[0, 0, pl.ds(start, tk), :]
        kpos = wbase + kt * tk + lax.broadcasted_iota(jnp.int32, (1, tk), 1)
        rel = t_row - kpos
        ok = (rel >= 0) & (rel < NSA_WIN) & (kpos >= 0)
        return _softmax_step(carry, jnp.where(ok, _dot_nt(qr4, k), -jnp.inf), v)

    w_first = jnp.maximum(pos0 + qi * tq - (NSA_WIN - 1) - wbase, 0) // tk
    w_last = jnp.minimum((t_last - wbase) // tk, kw_ref.shape[2] // tk - 1)
    carry = lax.fori_loop(w_first, w_last + 1, win_body, init)
    o_win = carry[2] / jnp.maximum(carry[1], 1e-30)

    sig = jax.nn.sigmoid(sm_ref[...])
    gw = N_BRANCH * hp
    gates = jnp.where(g == 0, sig[:, SM_NG:SM_NG + gw], sig[:, SM_NG + gw:SM_NG + 2 * gw])
    outs = []
    for h in range(hp):
        rows = slice(h * tq, (h + 1) * tq)
        outs.append(gates[:, 3 * h:3 * h + 1] * o_cmp[rows]
                    + gates[:, 3 * h + 1:3 * h + 2] * o_slc[rows]
                    + gates[:, 3 * h + 2:3 * h + 3] * o_win[rows])
    o_ref[...] = jnp.concatenate(outs, axis=1)


def _nsa_attn(qc, qr, kc, vc, ks, vs, kw, vw, tails, z, batch, t, tq, pos0, wrel0, tk):
    nq = t // tq
    n_keys = ks.shape[2]
    gw = NSA_HEADS * NSA_DH // NSA_KV
    rowblk = lambda b, g, i: b * nq + i
    full = lambda a: pl.BlockSpec((1, 1) + a.shape[2:], lambda b, g, i: (b, g, 0, 0))
    q_spec = pl.BlockSpec((tq, gw), lambda b, g, i: (rowblk(b, g, i), g))
    operands = [qc, qr, kc, vc, ks, vs, kw, vw] + list(tails)
    in_specs = [q_spec, q_spec] + [full(a) for a in operands[2:]]
    in_specs.append(pl.BlockSpec((tq, LANES), lambda b, g, i: (rowblk(b, g, i), Z_SMALL // LANES)))
    return pl.pallas_call(
        functools.partial(_nsa_attn_kernel, tq=tq, pos0=pos0, wrel0=wrel0, n_keys=n_keys, tk=tk,
                          has_tail=bool(tails)),
        grid=(batch, NSA_KV, nq),
        in_specs=in_specs,
        out_specs=pl.BlockSpec((tq, gw), lambda b, g, i: (rowblk(b, g, i), g)),
        out_shape=jax.ShapeDtypeStruct((batch * t, NSA_HEADS * NSA_DH), F32),
        compiler_params=_params(("parallel", "parallel", "arbitrary")),
    )(*operands, z)


MASKED = -1e30


def _topk_mask_rows(score, k):
    n = score.shape[0]
    row = lax.broadcasted_iota(jnp.int32, score.shape, 0).astype(F32)
    sel = jnp.zeros(score.shape, F32)
    for _ in range(k):
        m = jnp.max(score, axis=0, keepdims=True)
        idx = jnp.min(jnp.where(score == m, row, float(n)), axis=0, keepdims=True)
        pick = row == idx
        sel = jnp.where(pick, 1.0, sel)
        score = jnp.where(pick, -jnp.inf, score)
    return sel


def _softmax_step_cols(carry, s, vt):
    m, l, acc = carry
    m_new = jnp.maximum(m, jnp.max(s, axis=0, keepdims=True))
    p = jnp.exp(s - m_new)
    alpha = jnp.exp(m - m_new)
    l = alpha * l + jnp.sum(p, axis=0, keepdims=True)
    acc = alpha * acc + _dot(vt, p)
    return m_new, l, acc


def _nsa_prompt_kernel(qc_ref, qr_ref, kc_ref, vct_ref, ks_ref, vst_ref, kw_ref, vwt_ref, sm_ref,
                       o_ref, bias_ref, *, tq, tk, tkw):
    hp = NSA_HEADS // NSA_KV
    r = hp * tq
    g = pl.program_id(1)
    qi = pl.program_id(2)
    q0 = qi * tq
    ncp = kc_ref.shape[2]
    nch = ncp // 2
    nsl = bias_ref.shape[0]
    per_tile = tk // NSA_SEL

    def heads_on_lanes(ref):
        xt = ref[...].T
        return jnp.concatenate([xt[h * NSA_DH:(h + 1) * NSA_DH] for h in range(hp)],
                               axis=1).astype(BF16)

    qct = heads_on_lanes(qc_ref)
    qrt = heads_on_lanes(qr_ref)
    t_q = q0 + lax.broadcasted_iota(jnp.int32, (1, tq), 1)
    t_lane = jnp.concatenate([t_q] * hp, axis=1)

    s = _dot(kc_ref[0, 0], qct)
    crow = lax.broadcasted_iota(jnp.int32, (ncp, 1), 0)
    cblk = 2 * (crow % nch) + crow // nch
    s = jnp.where((cblk + 1) * NSA_CMP <= t_lane + 1, s, -jnp.inf)
    m = jnp.max(s, axis=0, keepdims=True)
    e = jnp.exp(s - jnp.where(m == -jnp.inf, 0.0, m))
    p_cmp = e / jnp.maximum(jnp.sum(e, axis=0, keepdims=True), 1e-30)
    o_cmp = _dot(vct_ref[0, 0], p_cmp)

    n_keys = kw_ref.shape[2]
    wlen = min(NSA_WIN + tq, n_keys)
    wstart = pl.multiple_of(jnp.minimum(jnp.maximum(q0 - NSA_WIN, 0), n_keys - wlen), tkw)
    sc = _dot(kw_ref[0, 0, pl.ds(wstart, wlen), :], qrt)
    rel = t_lane - (wstart + lax.broadcasted_iota(jnp.int32, (wlen, 1), 0))
    sc = jnp.where((rel >= 0) & (rel < NSA_WIN), sc, MASKED)
    p_win = jnp.exp(sc - jnp.max(sc, axis=0, keepdims=True))
    o_win = (_dot(vwt_ref[0, 0, :, pl.ds(wstart, wlen)], p_win)
             / jnp.sum(p_win, axis=0, keepdims=True))

    imp = p_cmp[:, 0:tq]
    for h in range(1, hp):
        imp = imp + p_cmp[:, h * tq:(h + 1) * tq]
    imp = imp[:nch] + imp[nch:]
    blk = lax.broadcasted_iota(jnp.int32, (nsl, tq), 0)
    cur = t_q // NSA_SEL
    forced = (blk == 0) | (blk == cur) | (blk == cur - 1)
    score = jnp.where(blk > cur, -jnp.inf, jnp.where(forced, NSA_FORCE, imp))
    bias = (_topk_mask_rows(score, NSA_TOPN) - 1.0) * (-MASKED)
    bias_ref[...] = jnp.concatenate([bias] * hp, axis=1)

    init = (jnp.full((1, r), MASKED, F32), jnp.zeros((1, r), F32), jnp.zeros((NSA_DH, r), F32))

    def slc_scores(kt):
        start = pl.multiple_of(kt * tk, tk)
        sc = _dot(ks_ref[0, 0, pl.ds(start, tk), :], qrt)
        brow = bias_ref[pl.ds(pl.multiple_of(kt * per_tile, per_tile), per_tile), :]
        sc = jnp.concatenate([sc[j * NSA_SEL:(j + 1) * NSA_SEL] + brow[j:j + 1]
                              for j in range(per_tile)], axis=0)
        return sc, vst_ref[0, 0, :, pl.ds(start, tk)]

    def slc_body(kt, carry):
        sc, vt = slc_scores(kt)
        return _softmax_step_cols(carry, sc, vt)

    kd = q0 // tk
    carry = lax.fori_loop(0, kd, slc_body, init)
    sc, vt = slc_scores(kd)
    kpos = kd * tk + lax.broadcasted_iota(jnp.int32, (tk, 1), 0)
    carry = _softmax_step_cols(carry, jnp.where(kpos <= t_lane, sc, MASKED), vt)
    o_slc = carry[2] / carry[1]

    sig = jax.nn.sigmoid(sm_ref[...].T)
    gw = N_BRANCH * hp
    gates = jnp.where(g == 0, sig[SM_NG:SM_NG + gw], sig[SM_NG + gw:SM_NG + 2 * gw])
    outs = []
    for h in range(hp):
        cols = slice(h * tq, (h + 1) * tq)
        outs.append(gates[3 * h:3 * h + 1] * o_cmp[:, cols]
                    + gates[3 * h + 1:3 * h + 2] * o_slc[:, cols]
                    + gates[3 * h + 2:3 * h + 3] * o_win[:, cols])
    o_ref[...] = jnp.concatenate(outs, axis=0).T


def _nsa_prompt_attn(qc, qr, kc, vct, ks, vst, kw, vwt, z, batch, t, tq, tk, tkw):
    nq = t // tq
    gw = NSA_HEADS * NSA_DH // NSA_KV
    assert tk % (SUBLANES * NSA_SEL) == 0 and tq == tkw and tk % tq == 0 and t % tk == 0
    rowblk = lambda b, g, i: b * nq + i
    full = lambda a: pl.BlockSpec((1, 1) + a.shape[2:], lambda b, g, i: (b, g, 0, 0))
    q_spec = pl.BlockSpec((tq, gw), lambda b, g, i: (rowblk(b, g, i), g))
    return pl.pallas_call(
        functools.partial(_nsa_prompt_kernel, tq=tq, tk=tk, tkw=tkw),
        grid=(batch, NSA_KV, nq),
        in_specs=[q_spec, q_spec] + [full(a) for a in (kc, vct, ks, vst, kw, vwt)]
        + [pl.BlockSpec((tq, LANES), lambda b, g, i: (rowblk(b, g, i), Z_SMALL // LANES))],
        out_specs=pl.BlockSpec((tq, gw), lambda b, g, i: (rowblk(b, g, i), g)),
        out_shape=jax.ShapeDtypeStruct((batch * t, NSA_HEADS * NSA_DH), F32),
        scratch_shapes=[pltpu.VMEM((t // NSA_SEL, NSA_HEADS // NSA_KV * tq), F32)],
        compiler_params=_params(("parallel", "parallel", "arbitrary")),
    )(qc, qr, kc, vct, ks, vst, kw, vwt, z)


def _softmax_step_vt(carry, s, vt):
    m, l, acc = carry
    m_new = jnp.maximum(m, jnp.max(s, axis=-1, keepdims=True))
    m_safe = jnp.where(m_new == -jnp.inf, 0.0, m_new)
    p = jnp.exp(s - m_safe)
    alpha = jnp.exp(m - m_safe)
    l = alpha * l + jnp.sum(p, axis=-1, keepdims=True)
    acc = alpha * acc + _dot_nt(p, vt)
    return m_new, l, acc


def _nsa_decode_kernel(pt_ref, *refs, tq, pos0, wrel0, n_keys):
    n_in = SLC_PAGES_PER_STEP
    qc_ref, qr_ref, kc_ref, vc_ref = refs[0:4]
    pages = refs[4:4 + n_in]
    (kw_ref, vw_ref, kt_ref, vt_ref, sm_ref, o_ref,
     m_ref, l_ref, acc_ref, ocmp_ref, sel_ref) = refs[4 + n_in:]
    hp = NSA_HEADS // NSA_KV
    r = hp * tq
    gwid = hp * NSA_DH
    step = pl.program_id(1)
    n_steps = pl.num_programs(1)
    tk = n_in * PAGE_SIZE
    per_tile = tk // NSA_SEL
    n_tiles = sel_ref.shape[1]
    ncp = kc_ref.shape[2]
    nch = ncp // 2
    nsl = -(-(n_tiles * per_tile) // LANES) * LANES
    kv_w = NSA_KV * NSA_DH

    def stack_heads(ref, g):
        x = ref[:, g * gwid:(g + 1) * gwid]
        return jnp.concatenate([x[:, h * NSA_DH:(h + 1) * NSA_DH] for h in range(hp)],
                               axis=0).astype(BF16)

    t_q = pos0 + lax.broadcasted_iota(jnp.int32, (tq, 1), 0)
    t_row = jnp.concatenate([t_q] * hp, axis=0)
    expand = (lax.broadcasted_iota(jnp.int32, (per_tile, tk), 0)
              == lax.broadcasted_iota(jnp.int32, (per_tile, tk), 1) // NSA_SEL).astype(BF16)

    def picked_rows(g, tile, width):
        pk = jnp.dot(sel_ref[g, tile].astype(BF16), expand[:, :width], preferred_element_type=F32)
        return jnp.concatenate([pk] * hp, axis=0)

    @pl.when(step == 0)
    def _():
        ccol = lax.broadcasted_iota(jnp.int32, (r, ncp), 1)
        cvis = (2 * (ccol % nch) + ccol // nch + 1) * NSA_CMP <= t_row + 1
        blk = lax.broadcasted_iota(jnp.int32, (tq, nsl), 1)
        cur = t_q // NSA_SEL
        forced = (blk == 0) | (blk == cur) | (blk == cur - 1)
        group_scores = []
        for g in range(NSA_KV):
            s = jnp.where(cvis, _dot_nt(stack_heads(qc_ref, g), kc_ref[0, g]), -jnp.inf)
            m = jnp.max(s, axis=-1, keepdims=True)
            e = jnp.exp(s - jnp.where(m == -jnp.inf, 0.0, m))
            p_cmp = e / jnp.maximum(jnp.sum(e, axis=-1, keepdims=True), 1e-30)
            ocmp_ref[g] = _dot(p_cmp, vc_ref[0, g])
            imp = p_cmp[0:tq]
            for h in range(1, hp):
                imp = imp + p_cmp[h * tq:(h + 1) * tq]
            imp = imp[:, :nch] + imp[:, nch:]
            if nsl > nch:
                imp = jnp.concatenate([imp, jnp.zeros((tq, nsl - nch), F32)], axis=1)
            group_scores.append(jnp.where(blk > cur, -jnp.inf, jnp.where(forced, NSA_FORCE, imp)))
        sel_all = _topk_mask(jnp.concatenate(group_scores, axis=0), NSA_TOPN)
        for g in range(NSA_KV):
            sel = sel_all[g * tq:(g + 1) * tq]
            for j in range(n_tiles):
                sel_ref[g, j] = sel[:, j * per_tile:(j + 1) * per_tile]
            m_ref[g] = jnp.full((r, 1), -jnp.inf, F32)
            l_ref[g] = jnp.zeros((r, 1), F32)
            acc_ref[g] = jnp.zeros((r, NSA_DH), F32)

    kpos = step * tk + lax.broadcasted_iota(jnp.int32, (1, tk), 1)
    scores, values = [], []
    for g in range(NSA_KV):
        k_t = jnp.concatenate([p[0, g * NSA_DH:(g + 1) * NSA_DH, :] for p in pages], axis=1)
        values.append(jnp.concatenate(
            [p[0, kv_w + g * NSA_DH:kv_w + (g + 1) * NSA_DH, :] for p in pages], axis=1))
        scores.append(_dot(stack_heads(qr_ref, g), k_t))
    picked = [picked_rows(g, step, tk) for g in range(NSA_KV)]
    for g in range(NSA_KV):
        ok = (picked[g] > 0.5) & (kpos <= t_row)
        m, l, acc = _softmax_step_vt((m_ref[g], l_ref[g], acc_ref[g]),
                                     jnp.where(ok, scores[g], -jnp.inf), values[g])
        m_ref[g] = m
        l_ref[g] = l
        acc_ref[g] = acc

    @pl.when(step == n_steps - 1)
    def _():
        sig = jax.nn.sigmoid(sm_ref[...])
        tw = kt_ref.shape[2]
        tpos = n_keys + lax.broadcasted_iota(jnp.int32, (1, tw), 1)
        wpos = pos0 - wrel0 + lax.broadcasted_iota(jnp.int32, (1, kw_ref.shape[2]), 1)
        rel = t_row - wpos
        wok = (rel >= 0) & (rel < NSA_WIN) & (wpos >= 0)
        qr4s = [stack_heads(qr_ref, g) for g in range(NSA_KV)]
        tail_s = [_dot_nt(qr4s[g], kt_ref[0, g]) for g in range(NSA_KV)]
        win_s = [_dot_nt(qr4s[g], kw_ref[0, g]) for g in range(NSA_KV)]
        for g in range(NSA_KV):
            ok = (picked_rows(g, n_tiles - 1, tw) > 0.5) & (tpos <= t_row)
            carry = _softmax_step((m_ref[g], l_ref[g], acc_ref[g]),
                                  jnp.where(ok, tail_s[g], -jnp.inf), vt_ref[0, g])
            o_slc = carry[2] / jnp.maximum(carry[1], 1e-30)
            s = jnp.where(wok, win_s[g], -jnp.inf)
            m = jnp.max(s, axis=-1, keepdims=True)
            e = jnp.exp(s - jnp.where(m == -jnp.inf, 0.0, m))
            o_win = _dot(e, vw_ref[0, g]) / jnp.maximum(jnp.sum(e, axis=-1, keepdims=True), 1e-30)
            o_cmp = ocmp_ref[g]
            base = SM_NG + g * N_BRANCH * hp
            outs = []
            for h in range(hp):
                rows = slice(h * tq, (h + 1) * tq)
                c0 = base + N_BRANCH * h
                outs.append(sig[:, c0:c0 + 1] * o_cmp[rows] + sig[:, c0 + 1:c0 + 2] * o_slc[rows]
                            + sig[:, c0 + 2:c0 + 3] * o_win[rows])
            o_ref[:, g * gwid:(g + 1) * gwid] = jnp.concatenate(outs, axis=1)


def _nsa_decode_attn(qc, qr, kc, vc, cache_t, page_table, kw, vw, ktail, vtail, z, tq, pos0, wrel0):
    batch, n_pages = page_table.shape
    n_keys = n_pages * PAGE_SIZE
    steps = n_pages // SLC_PAGES_PER_STEP
    hp = NSA_HEADS // NSA_KV
    r = hp * tq
    qw = NSA_HEADS * NSA_DH
    per_tile = SLC_PAGES_PER_STEP * PAGE_SIZE // NSA_SEL
    assert pos0 == n_keys and ktail.shape[2] <= per_tile * NSA_SEL
    per_b = lambda a: pl.BlockSpec((1,) + a.shape[1:], lambda b, s, pt: (b, 0, 0, 0))
    q_spec = pl.BlockSpec((tq, qw), lambda b, s, pt: (b, 0))
    return pl.pallas_call(
        functools.partial(_nsa_decode_kernel, tq=tq, pos0=pos0, wrel0=wrel0, n_keys=n_keys),
        grid_spec=pltpu.PrefetchScalarGridSpec(
            num_scalar_prefetch=1, grid=(batch, steps),
            in_specs=[q_spec, q_spec, per_b(kc), per_b(vc)]
            + _page_specs((1, ROW_W, PAGE_SIZE), n_pages, SLC_PAGES_PER_STEP)
            + [per_b(kw), per_b(vw), per_b(ktail), per_b(vtail),
               pl.BlockSpec((tq, LANES), lambda b, s, pt: (b, Z_SMALL // LANES))],
            out_specs=pl.BlockSpec((tq, qw), lambda b, s, pt: (b, 0)),
            scratch_shapes=[pltpu.VMEM((NSA_KV, r, 1), F32), pltpu.VMEM((NSA_KV, r, 1), F32),
                            pltpu.VMEM((NSA_KV, r, NSA_DH), F32),
                            pltpu.VMEM((NSA_KV, r, NSA_DH), F32),
                            pltpu.VMEM((NSA_KV, steps + 1, tq, per_tile), F32)]),
        out_shape=jax.ShapeDtypeStruct((batch * tq, qw), F32),
        compiler_params=_params(("parallel", "arbitrary")),
    )(page_table.reshape(-1), qc, qr, kc, vc, *([cache_t] * SLC_PAGES_PER_STEP),
      kw, vw, ktail, vtail, z)


def _mem_attn_kernel(q_ref, kv_ref, o_ref):
    hw = MEM_HEADS * MEM_DH
    for h in range(MEM_HEADS):
        q = q_ref[:, h * MEM_DH:(h + 1) * MEM_DH] * (MEM_DH ** -0.5)
        k = kv_ref[0, :, h * MEM_DH:(h + 1) * MEM_DH]
        v = kv_ref[0, :, hw + h * MEM_DH:hw + (h + 1) * MEM_DH]
        s = _dot_nt(q, k)
        e = jnp.exp(s - jnp.max(s, axis=-1, keepdims=True))
        p = e / jnp.sum(e, axis=-1, keepdims=True)
        o_ref[:, h * MEM_DH:(h + 1) * MEM_DH] = _dot(p, v)


def _mem_attn(z, kv, batch, t, tm):
    nt = t // tm
    hw = MEM_HEADS * MEM_DH
    return pl.pallas_call(
        _mem_attn_kernel,
        grid=(batch, nt),
        in_specs=[pl.BlockSpec((tm, hw), lambda b, j: (b * nt + j, Z_MQ // hw)),
                  pl.BlockSpec((1,) + kv.shape[1:], lambda b, j: (b, 0, 0))],
        out_specs=pl.BlockSpec((tm, hw), lambda b, j: (b * nt + j, 0)),
        out_shape=jax.ShapeDtypeStruct((batch * t, hw), F32),
        compiler_params=_params(("parallel", "parallel")),
    )(z, kv)


def _merge_kernel(x_ref, mg_ref, og_ref, on_ref, om_ref, wg_ref, wn_ref, wm_ref, wo_ref,
                  gf_ref, wq_ref, x1_ref, h2_ref, qp_ref):
    d = D_MODEL
    mix = (jax.nn.sigmoid(mg_ref[:, 0:d]) * _dot(og_ref[...], wg_ref[...])
           + jax.nn.sigmoid(mg_ref[:, d:2 * d]) * _dot(on_ref[...], wn_ref[...])
           + jax.nn.sigmoid(mg_ref[:, 2 * d:3 * d]) * _dot(om_ref[...], wm_ref[...]))
    x1 = x_ref[...] + _dot(mix, wo_ref[...])
    x1_ref[...] = x1
    h2 = _rms(x1, gf_ref[...]).astype(BF16)
    h2_ref[...] = h2
    qp_ref[...] = jnp.dot(h2, wq_ref[...], preferred_element_type=F32)


def _merge(x, z, o_gdn, o_nsa, o_mem, wg, wn, wm, wo, norm_ffn, wq, tm):
    n, d = x.shape
    qw = wq.shape[1]
    row = lambda w: pl.BlockSpec((tm, w), lambda i: (i, 0))
    const = lambda a: pl.BlockSpec(a.shape, lambda i: (0, 0))
    return pl.pallas_call(
        _merge_kernel,
        grid=(n // tm,),
        in_specs=[row(d), pl.BlockSpec((tm, N_BRANCH * d), lambda i: (i, Z_MG)),
                  row(o_gdn.shape[1]), row(o_nsa.shape[1]), row(o_mem.shape[1]),
                  const(wg), const(wn), const(wm), const(wo),
                  pl.BlockSpec((1, d), lambda i: (0, 0)), const(wq)],
        out_specs=[row(d), row(d), row(qw)],
        out_shape=[jax.ShapeDtypeStruct((n, d), F32), jax.ShapeDtypeStruct((n, d), BF16),
                   jax.ShapeDtypeStruct((n, qw), F32)],
        compiler_params=_params(("parallel",)),
    )(x, z, o_gdn, o_nsa, o_mem, wg, wn, wm, wo, norm_ffn.reshape(1, d), wq)


PEER_RANKS = PEER_TOPK + 1


RANK_NONE = 64.0


def _top_values(s, n):
    vals = []
    rank = jnp.full(s.shape, RANK_NONE, F32)
    for k in range(n):
        m = jnp.max(s, axis=0, keepdims=True)
        vals.append(m)
        hit = s >= m
        rank = jnp.where(hit, float(k + 1), rank)
        s = jnp.where(hit, -jnp.inf, s)
    return vals, rank


def _peer_route_kernel(qp_ref, sk_ref, r2_ref, e2_ref, nb_ref, e1_ref):
    half = PEER_DKEY // 2
    nt = (((1,), (1,)), ((), ()))
    for h in range(PEER_HEADS):
        qa = qp_ref[:, h * PEER_DKEY:h * PEER_DKEY + half]
        qb = qp_ref[:, h * PEER_DKEY + half:(h + 1) * PEER_DKEY]
        s1 = lax.dot_general(sk_ref[0], qa, nt, precision=HIGHEST, preferred_element_type=F32)
        s2 = lax.dot_general(sk_ref[1], qb, nt, precision=HIGHEST, preferred_element_type=F32)
        a, _ = _top_values(s1, PEER_RANKS)
        b, rank2 = _top_values(s2, PEER_RANKS)
        b_rows = jnp.concatenate(b, axis=0)
        cands = [a[i - 1] + b_rows[0:PEER_RANKS // i] for i in range(1, PEER_RANKS + 1)]
        work = cands
        ranked = []
        for _ in range(PEER_RANKS):
            m = work[0][0:1]
            for cnd in work:
                m = jnp.maximum(m, jnp.max(cnd, axis=0, keepdims=True))
            ranked.append(m)
            work = [jnp.where(cnd >= m, -jnp.inf, cnd) for cnd in work]
        tau = 0.5 * (ranked[PEER_TOPK - 1] + ranked[PEER_TOPK])
        top = a[0] + b[0]
        zsum = jnp.zeros_like(tau)
        for cnd in cands:
            zsum = zsum + jnp.sum(jnp.where(cnd >= tau, jnp.exp(cnd - top), 0.0),
                                  axis=0, keepdims=True)
        th = tau - s1
        count = jnp.zeros(s1.shape, F32)
        for bj in b:
            count = count + jnp.where(bj >= th, 1.0, 0.0)
        r2_ref[h] = rank2.astype(BF16)
        e2_ref[h] = jnp.exp(s2 - b[0]).astype(BF16)
        nb_ref[h] = count
        e1_ref[h] = jnp.exp(s1 - a[0]) / zsum


def _peer_route(qp, subkeys, tt):
    n = qp.shape[0]
    shape = lambda dt: jax.ShapeDtypeStruct((PEER_HEADS, PEER_NKEYS, n), dt)
    spec = pl.BlockSpec((PEER_HEADS, PEER_NKEYS, tt), lambda i: (0, 0, i))
    return pl.pallas_call(
        _peer_route_kernel,
        grid=(n // tt,),
        in_specs=[pl.BlockSpec((tt, qp.shape[1]), lambda i: (i, 0)),
                  pl.BlockSpec(subkeys.shape, lambda i: (0, 0, 0))],
        out_specs=[spec, spec, spec, spec],
        out_shape=[shape(BF16), shape(BF16), shape(F32), shape(F32)],
        compiler_params=_params(("parallel",)),
    )(qp, subkeys)


def _peer_dense_kernel(ht_ref, u_ref, vt_ref, r2_ref, e2_ref, nb_ref, e1_ref, x1_ref, gf_ref,
                       y_ref, acc_ref, act_ref, wa_ref, r2s_ref, e2s_ref, rows_ref, *, jb):
    j = pl.program_id(1)
    tt = ht_ref.shape[1]

    @pl.when(j == 0)
    def _():
        acc_ref[...] = jnp.zeros(acc_ref.shape, F32)
        r2s_ref[...] = r2_ref[...]
        e2s_ref[...] = e2_ref[...]

    act_ref[...] = _gelu(jnp.dot(u_ref[...], ht_ref[...],
                                 preferred_element_type=F32)).astype(BF16)
    for jj in range(jb):
        i1 = j * jb + jj
        rows = slice(jj * PEER_NKEYS, (jj + 1) * PEER_NKEYS)
        for h in range(PEER_HEADS):
            rows_ref[jj, h:h + 1, :] = nb_ref[h, pl.ds(i1, 1), :]
            rows_ref[jj, PEER_HEADS + h:PEER_HEADS + h + 1, :] = e1_ref[h, pl.ds(i1, 1), :]
        for c in range(tt // LANES):
            cols = slice(c * LANES, (c + 1) * LANES)
            w = None
            for h in range(PEER_HEADS):
                count = rows_ref[jj, h:h + 1, cols].astype(BF16)
                e1 = rows_ref[jj, PEER_HEADS + h:PEER_HEADS + h + 1, cols].astype(BF16)
                picked = r2s_ref[h, :, cols] <= count
                term = jnp.where(picked, e2s_ref[h, :, cols], jnp.zeros((), BF16)) * e1
                w = term if w is None else w + term
            wa_ref[rows, cols] = w * act_ref[rows, cols]
    acc_ref[...] += jnp.dot(vt_ref[...], wa_ref[...], preferred_element_type=F32)

    @pl.when(j == pl.num_programs(1) - 1)
    def _():
        y_ref[...] = _rms(x1_ref[...] + acc_ref[...].T, gf_ref[...])


def _peer_dense(ht, u, vt, s2, e2, th, e1, x1, norm_final, tt, jb):
    d, n = ht.shape
    n_exp = u.shape[0]
    eb = jb * PEER_NKEYS
    route = pl.BlockSpec((PEER_HEADS, PEER_NKEYS, tt), lambda t, j: (0, 0, t))
    return pl.pallas_call(
        functools.partial(_peer_dense_kernel, jb=jb),
        grid=(n // tt, n_exp // eb),
        in_specs=[pl.BlockSpec((d, tt), lambda t, j: (0, t)),
                  pl.BlockSpec((eb, d), lambda t, j: (j, 0)),
                  pl.BlockSpec((d, eb), lambda t, j: (0, j)),
                  route, route, route, route,
                  pl.BlockSpec((tt, d), lambda t, j: (t, 0)),
                  pl.BlockSpec((1, d), lambda t, j: (0, 0))],
        out_specs=pl.BlockSpec((tt, d), lambda t, j: (t, 0)),
        out_shape=jax.ShapeDtypeStruct((n, d), F32),
        scratch_shapes=[pltpu.VMEM((d, tt), F32), pltpu.VMEM((eb, tt), BF16),
                        pltpu.VMEM((eb, tt), BF16),
                        pltpu.VMEM((PEER_HEADS, PEER_NKEYS, tt), BF16),
                        pltpu.VMEM((PEER_HEADS, PEER_NKEYS, tt), BF16),
                        pltpu.VMEM((jb, 2 * PEER_HEADS, tt), F32)],
        compiler_params=_params(("parallel", "arbitrary")),
    )(ht, u, vt, s2, e2, th, e1, x1, norm_final.reshape(1, d))


def _permute_w_in(w_in):
    sizes = (GDN_CONV_CH, GDN_HEADS * GDN_DV, GDN_HEADS, GDN_HEADS, NSA_HEADS * NSA_DH, NKV_W,
             N_BRANCH * NSA_HEADS, MEM_HEADS * MEM_DH, N_BRANCH * D_MODEL)
    qkv, zg, a, b, nq, nkv, ng, mq, mg = jnp.split(w_in, np.cumsum(sizes)[:-1].tolist(), axis=1)
    pad = jnp.zeros((w_in.shape[0], LANES - a.shape[1] - b.shape[1] - ng.shape[1]), w_in.dtype)
    return jnp.concatenate([mg, qkv, zg, nq, mq, nkv, a, b, ng, pad], axis=1).astype(BF16)


def _tokens_tile(n, pref):
    return pref if n % pref == 0 else n


def _layer(x, pos0, kv_mem, nsa_keys, gdn_state, conv_buf, w, peer_tt):
    batch, t, d = x.shape
    n = batch * t
    xf = x.reshape(n, d)
    z = _norm_matmul(xf, w['norm_attn'], w['w_in'], _tokens_tile(n, 256), 640)

    o_gdn, s_new, conv_new = _gdn(z, conv_buf, gdn_state, w['gdn_conv'], w['gdn_a_log'],
                                  w['gdn_dt_bias'], w['gdn_norm'], batch, t)

    prep_tm = _tokens_tile(t, 512)
    kv_dtype = BF16 if prep_tm % 16 == 0 else F32
    qc, qr, slc_rows, win_rows, ks, vs, kw, vw = _nsa_prep(
        z, pos0 + jnp.arange(t), batch, t, prep_tm, kv_dtype)
    cmp_rows = z[:, Z_NKV:Z_NKV + ROW_W]
    o_nsa = nsa_keys(z, qc, qr, cmp_rows, ks, vs, kw, vw)

    o_mem = _mem_attn(z, kv_mem, batch, t, _tokens_tile(t, 512))

    x1, h2, qp = _merge(xf, z, o_gdn, o_nsa, o_mem, w['w_gdn_out'], w['w_nsa_out'],
                        w['w_mem_out'], w['w_o'], w['norm_ffn'], w['peer_wq'],
                        _tokens_tile(n, 256))
    s2, e2, th, e1 = _peer_route(qp, w['peer_subkeys'], 256)
    y = _peer_dense(h2.T, w['peer_u'], w['peer_vt'], s2, e2, th, e1, x1, w['norm_final'],
                    peer_tt, 8)
    row5 = lambda a: a.reshape(batch, t, 2, NSA_KV, NSA_DH)
    return (y.reshape(batch, t, d), row5(cmp_rows), row5(slc_rows), row5(win_rows), s_new,
            conv_new)


def kernel(x_prompt, x_sample, cache_mem_kv, cache_cmp_kv, cache_slc_kv, cache_win_kv, state_gdn, state_conv, page_table, mem_prompt, norm_attn, w_in, gdn_conv, gdn_a_log, gdn_dt_bias, gdn_norm, nsa_cmp_pe, nsa_cmp_w1, nsa_cmp_w2, norm_mem, w_mem_kv, w_gdn_out, w_nsa_out, w_mem_out, w_o, norm_ffn, peer_wq, peer_subkeys, peer_u, peer_v, norm_final):
    depth = w_in.shape[0]
    assert depth == 1
    l = 0
    bp, seq, d = x_prompt.shape
    bs, tdec, _ = x_sample.shape
    n_pages = page_table.shape[1]
    past = n_pages * PAGE_SIZE
    assert past % NSA_CMP == 0 and tdec < NSA_CMP and seq % LANES == 0

    pe_big, w1big, w2big = _compress_weights(nsa_cmp_pe[l], nsa_cmp_w1[l], nsa_cmp_w2[l])
    w = dict(norm_attn=norm_attn[l], w_in=_permute_w_in(w_in[l]), gdn_conv=gdn_conv[l],
             gdn_a_log=gdn_a_log[l], gdn_dt_bias=gdn_dt_bias[l], gdn_norm=gdn_norm[l],
             w_gdn_out=w_gdn_out[l].astype(BF16), w_nsa_out=w_nsa_out[l].astype(BF16),
             w_mem_out=w_mem_out[l].astype(BF16), w_o=w_o[l].astype(BF16), norm_ffn=norm_ffn[l],
             peer_wq=peer_wq[l].astype(BF16), peer_subkeys=peer_subkeys[l],
             peer_u=peer_u[l].astype(BF16), peer_vt=peer_v[l].astype(BF16).T,
             norm_final=norm_final)

    mem_n = mem_prompt.shape[0] * mem_prompt.shape[1]
    kvm = _norm_matmul(mem_prompt.reshape(mem_n, d), norm_mem[l], w_mem_kv[l].astype(BF16),
                       _tokens_tile(mem_n, 256), 512).reshape(bp, mem_prompt.shape[1], -1)

    def prompt_keys(z, qc, qr, cmp_rows, ks, vs, kw, vw):
        n_blk = bp * seq // NSA_CMP
        cmp_out = _compress(cmp_rows.reshape(n_blk, NSA_CMP * ROW_W), pe_big, w1big, w2big,
                            _tokens_tile(n_blk, 128))
        kc, vc = _split_compressed(cmp_out, bp)
        tr = lambda a: jnp.transpose(a, (0, 1, 3, 2))
        return _nsa_prompt_attn(qc, qr, kc, tr(vc), ks, tr(vs), kw, tr(vw), z, bp, seq,
                                128, _tokens_tile(seq, 512), 128)

    yp, cmp_p, slc_p, win_p, gdn_p, conv_p = _layer(
        x_prompt, 0, kvm, prompt_keys,
        jnp.zeros((bp, GDN_HEADS, GDN_DK, GDN_DV), F32),
        jnp.zeros((bp, GDN_CONV - 1, GDN_CONV_CH), F32), w, 512)
    win_len_p = min(NSA_WIN, seq)
    win_p = win_p[:, seq - win_len_p:]

    n_pool = cache_cmp_kv.shape[1]
    feature_major = lambda c: jnp.transpose(c, (0, 2, 3, 4, 1)).reshape(n_pool, ROW_W, PAGE_SIZE)
    cache_cmp = feature_major(cache_cmp_kv[l])
    cache_slc = feature_major(cache_slc_kv[l])
    cache_win = cache_win_kv[l].reshape(bs, -1, ROW_W)
    wb = cache_win.shape[1]
    assert wb == NSA_WIN
    kv_w = NSA_KV * NSA_DH

    def pad_rows(a, rows):
        return jnp.pad(a, ((0, 0), (0, 0), (0, rows - a.shape[2]), (0, 0))).astype(BF16)

    def split_rows(rows):
        r = rows.reshape(bs, rows.shape[1], 2, NSA_KV, NSA_DH)
        return jnp.transpose(r[:, :, 0], (0, 2, 1, 3)), jnp.transpose(r[:, :, 1], (0, 2, 1, 3))

    def sample_keys(z, qc, qr, cmp_rows, ks_new, vs_new, kw_new, vw_new):
        cmp_out = _paged_compress(cache_cmp, page_table, pe_big, w1big, w2big)
        kc, vc = _split_compressed(cmp_out, bs)
        kwc, vwc = split_rows(cache_win)
        win_rows = wb + LANES
        kw = pad_rows(jnp.concatenate([kwc, kw_new], axis=2), win_rows)
        vw = pad_rows(jnp.concatenate([vwc, vw_new], axis=2), win_rows)
        return _nsa_decode_attn(qc, qr, kc, vc, cache_slc, page_table, kw, vw,
                                pad_rows(ks_new, LANES), pad_rows(vs_new, LANES), z,
                                tdec, past, wb)

    ys, cmp_s, slc_s, win_new, gdn_s, conv_s = _layer(
        x_sample, past, cache_mem_kv[l].reshape(bs, cache_mem_kv.shape[2], -1), sample_keys,
        state_gdn[l], state_conv[l], w, 256)
    win_all = jnp.concatenate([cache_win_kv[l], win_new], axis=1)
    win_s = win_all[:, win_all.shape[1] - min(NSA_WIN, past + tdec):]

    stack = lambda a: a[None]
    return (yp, ys, stack(kvm.reshape(bp, mem_prompt.shape[1], 2, MEM_HEADS, MEM_DH)),
            stack(cmp_p), stack(slc_p), stack(win_p), stack(gdn_p), stack(conv_p),
            stack(cmp_s), stack(slc_s), stack(win_s), stack(gdn_s), stack(conv_s))
```

```python
import functools
import math

import jax
import jax.numpy as jnp
import numpy as np
from jax import lax
from jax.experimental import pallas as pl
from jax.experimental.pallas import tpu as pltpu

F32 = jnp.float32
BF16 = jnp.bfloat16
HIGHEST = lax.Precision.HIGHEST

D_MODEL = 1024
PAGE_SIZE = 128
GDN_HEADS = 4
GDN_DK = 128
GDN_DV = 128
GDN_CONV = 4
GDN_CHUNK = 64
GDN_CONV_CH = GDN_HEADS * (2 * GDN_DK + GDN_DV)
NSA_HEADS = 8
NSA_KV = 2
NSA_DH = 64
NSA_CMP = 32
NSA_SEL = 64
NSA_TOPN = 16
NSA_WIN = 512
NSA_FORCE = 1e9
MEM_HEADS = 4
MEM_DH = 128
PEER_HEADS = 8
PEER_NKEYS = 128
PEER_DKEY = 256
PEER_TOPK = 16
N_BRANCH = 3
ROPE_THETA = 10000.0
EPS = 1e-6

LANES = 128
SUBLANES = 8
VMEM_LIMIT = 56 * 1024 * 1024

Z_MG = 0
Z_QKV = Z_MG + N_BRANCH * D_MODEL
Z_ZG = Z_QKV + GDN_CONV_CH
Z_NQ = Z_ZG + GDN_HEADS * GDN_DV
Z_MQ = Z_NQ + NSA_HEADS * NSA_DH
Z_NKV = Z_MQ + MEM_HEADS * MEM_DH
Z_SMALL = Z_NKV + 3 * 2 * NSA_KV * NSA_DH
Z_WIDTH = Z_SMALL + LANES
SM_A = 0
SM_B = GDN_HEADS
SM_NG = 2 * GDN_HEADS
NKV_W = 3 * 2 * NSA_KV * NSA_DH
ROW_W = 2 * NSA_KV * NSA_DH


def _params(sem, vmem=VMEM_LIMIT):
    return pltpu.CompilerParams(dimension_semantics=sem, vmem_limit_bytes=vmem)


def _dot(a, b):
    return jnp.dot(a.astype(BF16), b.astype(BF16), preferred_element_type=F32)


def _dot_nt(a, b):
    return lax.dot_general(a.astype(BF16), b.astype(BF16), (((1,), (1,)), ((), ())),
                           preferred_element_type=F32)


def _dot_hi(a, b):
    return jnp.dot(a, b, precision=HIGHEST, preferred_element_type=F32)


def _dot3(a, b):
    ah = a.astype(BF16)
    bh = b.astype(BF16)
    al = (a - ah.astype(F32)).astype(BF16)
    bl = (b - bh.astype(F32)).astype(BF16)
    d = lambda x, y: jnp.dot(x, y, preferred_element_type=F32)
    return d(ah, bh) + d(al, bh) + d(ah, bl)


def _rms(x, g):
    return x * lax.rsqrt(jnp.mean(x * x, axis=-1, keepdims=True) + EPS) * g


def _gelu(x):
    a = -2.0 * math.sqrt(2.0 / math.pi)
    return x / (1.0 + jnp.exp(x * (a + (a * 0.044715) * (x * x))))


def _norm_matmul_kernel(x_ref, g_ref, w_ref, o_ref, *, col_chunk):
    yb = _rms(x_ref[...], g_ref[...]).astype(BF16)
    for c0 in range(0, o_ref.shape[1], col_chunk):
        o_ref[:, c0:c0 + col_chunk] = jnp.dot(yb, w_ref[:, c0:c0 + col_chunk],
                                              preferred_element_type=F32)


def _norm_matmul(x, g, w, tm, col_chunk):
    n, d = x.shape
    wc = w.shape[1]
    return pl.pallas_call(
        functools.partial(_norm_matmul_kernel, col_chunk=col_chunk),
        grid=(n // tm,),
        in_specs=[pl.BlockSpec((tm, d), lambda i: (i, 0)),
                  pl.BlockSpec((1, d), lambda i: (0, 0)),
                  pl.BlockSpec((d, wc), lambda i: (0, 0), pipeline_mode=pl.Buffered(1))],
        out_specs=pl.BlockSpec((tm, wc), lambda i: (i, 0)),
        out_shape=jax.ShapeDtypeStruct((n, wc), F32),
        compiler_params=_params(("parallel",)),
    )(x, g.reshape(1, d), w)


def _tri_inverse(lmats, c):
    row = lax.broadcasted_iota(jnp.int32, (c, c), 0)
    col = lax.broadcasted_iota(jnp.int32, (c, c), 1)
    ident = jnp.where(row == col, 1.0, 0.0)
    xs = [ident - lm for lm in lmats]
    ps = [_dot3(lm, lm) for lm in lmats]
    n = 2
    while n < c:
        xs = [x + _dot3(x, p) for x, p in zip(xs, ps)]
        n *= 2
        if n < c:
            ps = [_dot3(p, p) for p in ps]
    return xs


def _gdn_kernel(qkv_ref, zg_ref, sm_ref, buf_ref, s0_ref, cw_ref, alog_ref, dt_ref, gn_ref,
                o_ref, snew_ref, cnew_ref, ext_ref, s_ref, *, tb, n_chunks, nb):
    ci = pl.program_id(1)

    @pl.when(ci == 0)
    def _():
        for i in range(nb):
            ext_ref[i, 0:SUBLANES, :] = buf_ref[i]
            s_ref[i] = s0_ref[i]

    c = GDN_CHUNK
    row = lax.broadcasted_iota(jnp.int32, (c, c), 0)
    col = lax.broadcasted_iota(jnp.int32, (c, c), 1)
    tril = row >= col
    eye = row == col
    hk = GDN_HEADS * GDN_DK
    chains = []
    for i in range(nb):
        u, g_all, beta_all = _gdn_inputs(qkv_ref.at[i], sm_ref.at[i], cw_ref, alog_ref, dt_ref,
                                         cnew_ref.at[i], ext_ref.at[i], tb)
        gc_all = _dot_hi(jnp.where(tril, 1.0, 0.0), g_all)
        for h in range(GDN_HEADS):
            qh = u[:, h * GDN_DK:(h + 1) * GDN_DK]
            kh = u[:, hk + h * GDN_DK:hk + (h + 1) * GDN_DK]
            ch = dict(i=i, h=h, v=u[:, 2 * hk + h * GDN_DV:2 * hk + (h + 1) * GDN_DV])
            ch['q'] = qh * lax.rsqrt(jnp.sum(qh * qh, axis=-1, keepdims=True) + EPS) * (GDN_DK ** -0.5)
            ch['k'] = kh * lax.rsqrt(jnp.sum(kh * kh, axis=-1, keepdims=True) + EPS)
            ch['beta'] = beta_all[:, SM_B + h:SM_B + h + 1]
            gc = gc_all[:, SM_A + h:SM_A + h + 1]
            ch['gc'] = gc
            ch['gl'] = gc_all[c - 1:c, SM_A + h:SM_A + h + 1]
            gc_row = jnp.sum(jnp.where(eye, gc, 0.0), axis=0, keepdims=True)
            ch['decay'] = jnp.exp(jnp.where(tril, gc - gc_row, -jnp.inf))
            ch['kb'] = ch['k'] * ch['beta']
            ch['egc'] = jnp.exp(gc)
            chains.append(ch)
    lmats = [jnp.where(row > col, _dot_nt(ch['kb'], ch['k']) * ch['decay'], 0.0) for ch in chains]
    tinvs = _tri_inverse(lmats, c)
    uus = [_dot(t, ch['v'] * ch['beta']) for t, ch in zip(tinvs, chains)]
    wws = [_dot(t, ch['kb'] * ch['egc']) for t, ch in zip(tinvs, chains)]
    aqks = [_dot_nt(ch['q'], ch['k']) * ch['decay'] for ch in chains]
    states = [s_ref[ch['i'], ch['h']] for ch in chains]
    v_news = [uu - _dot(ww, s) for uu, ww, s in zip(uus, wws, states)]
    outs = [_dot(ch['q'] * ch['egc'], s) + _dot(aqk, vn)
            for ch, s, aqk, vn in zip(chains, states, aqks, v_news)]
    s_news = [s * jnp.exp(ch['gl']) + _dot((ch['k'] * jnp.exp(ch['gl'] - ch['gc'])).T, vn)
              for ch, s, vn in zip(chains, states, v_news)]
    for ch, o, s_new in zip(chains, outs, s_news):
        i, h = ch['i'], ch['h']
        s_ref[i, h] = s_new
        zh = zg_ref[i, :, h * GDN_DV:(h + 1) * GDN_DV]
        o_ref[i, :, h * GDN_DV:(h + 1) * GDN_DV] = (_rms(o[0:tb], gn_ref[...])
                                                    * (zh * jax.nn.sigmoid(zh)))

    @pl.when(ci == n_chunks - 1)
    def _():
        snew_ref[...] = s_ref[...]


def _gdn_inputs(qkv_ref, sm_ref, cw_ref, alog_ref, dt_ref, cnew_ref, ext_ref, tb):
    c = GDN_CHUNK
    if tb < c:
        ext_ref[SUBLANES + tb:, :] = jnp.zeros((c - tb, GDN_CONV_CH), F32)
    ext_ref[SUBLANES:SUBLANES + tb, :] = qkv_ref[...]
    cw = cw_ref[...]
    conv = cw[0:1] * ext_ref[SUBLANES - 3:SUBLANES - 3 + c, :]
    for j in range(1, GDN_CONV):
        conv = conv + cw[j:j + 1] * ext_ref[SUBLANES - 3 + j:SUBLANES - 3 + j + c, :]
    u = conv * jax.nn.sigmoid(conv)
    last_rows = ext_ref[tb:tb + SUBLANES, :]
    cnew_ref[...] = last_rows
    ext_ref[0:SUBLANES, :] = last_rows

    sm = sm_ref[...]
    if tb < c:
        sm = jnp.concatenate([sm, jnp.zeros((c - tb, LANES), F32)], axis=0)
    za = sm + dt_ref[...]
    softplus = jnp.maximum(za, 0.0) + jnp.log1p(jnp.exp(-jnp.abs(za)))
    g_all = -jnp.exp(alog_ref[...]) * softplus
    beta_all = jax.nn.sigmoid(sm)
    if tb < c:
        valid = lax.broadcasted_iota(jnp.int32, (c, 1), 0) < tb
        u = jnp.where(valid, u, 0.0)
        g_all = jnp.where(valid, g_all, 0.0)
        beta_all = jnp.where(valid, beta_all, 0.0)
    return u, g_all, beta_all


GDN_BATCH_PER_STEP = 2


def _gdn(z, conv_buf, s0, conv_w, a_log, dt_bias, gnorm, batch, t):
    c = GDN_CHUNK
    tb = min(t, c)
    n_chunks = t // tb
    nb = GDN_BATCH_PER_STEP if batch % GDN_BATCH_PER_STEP == 0 else 1
    assert tb % SUBLANES == 0 and n_chunks * tb == t and (tb == c or n_chunks == 1)
    buf8 = jnp.pad(conv_buf, ((0, 0), (SUBLANES - (GDN_CONV - 1), 0), (0, 0)))
    alog_row = jnp.zeros((1, LANES), F32).at[0, SM_A:SM_A + GDN_HEADS].set(a_log)
    dt_row = jnp.zeros((1, LANES), F32).at[0, SM_A:SM_A + GDN_HEADS].set(dt_bias)
    z3 = z.reshape(batch, t, Z_WIDTH)
    hv = GDN_HEADS * GDN_DV
    o, s_new, c_new = pl.pallas_call(
        functools.partial(_gdn_kernel, tb=tb, n_chunks=n_chunks, nb=nb),
        grid=(batch // nb, n_chunks),
        in_specs=[
            pl.BlockSpec((nb, tb, GDN_CONV_CH), lambda b, ci: (b, ci, Z_QKV // GDN_CONV_CH)),
            pl.BlockSpec((nb, tb, hv), lambda b, ci: (b, ci, Z_ZG // hv)),
            pl.BlockSpec((nb, tb, LANES), lambda b, ci: (b, ci, Z_SMALL // LANES)),
            pl.BlockSpec((nb, SUBLANES, GDN_CONV_CH), lambda b, ci: (b, 0, 0)),
            pl.BlockSpec((nb, GDN_HEADS, GDN_DK, GDN_DV), lambda b, ci: (b, 0, 0, 0)),
            pl.BlockSpec((GDN_CONV, GDN_CONV_CH), lambda b, ci: (0, 0)),
            pl.BlockSpec((1, LANES), lambda b, ci: (0, 0)),
            pl.BlockSpec((1, LANES), lambda b, ci: (0, 0)),
            pl.BlockSpec((1, GDN_DV), lambda b, ci: (0, 0)),
        ],
        out_specs=[
            pl.BlockSpec((nb, tb, hv), lambda b, ci: (b, ci, 0)),
            pl.BlockSpec((nb, GDN_HEADS, GDN_DK, GDN_DV), lambda b, ci: (b, 0, 0, 0)),
            pl.BlockSpec((nb, SUBLANES, GDN_CONV_CH), lambda b, ci: (b, 0, 0)),
        ],
        out_shape=[
            jax.ShapeDtypeStruct((batch, t, hv), F32),
            jax.ShapeDtypeStruct((batch, GDN_HEADS, GDN_DK, GDN_DV), F32),
            jax.ShapeDtypeStruct((batch, SUBLANES, GDN_CONV_CH), F32),
        ],
        scratch_shapes=[pltpu.VMEM((nb, SUBLANES + c, GDN_CONV_CH), F32),
                        pltpu.VMEM((nb, GDN_HEADS, GDN_DK, GDN_DV), F32)],
        compiler_params=_params(("parallel", "arbitrary")),
    )(z3, z3, z3, buf8, s0, conv_w, alog_row, dt_row, gnorm.reshape(1, GDN_DV))
    return o.reshape(batch * t, hv), s_new, c_new[:, SUBLANES - (GDN_CONV - 1):]


def _rope_tables(pos):
    half = NSA_DH // 2
    inv = jnp.power(ROPE_THETA, -jnp.arange(half, dtype=F32) / half)
    ang = pos.astype(F32)[:, None] * inv[None, :]
    cos, sin = jnp.cos(ang), jnp.sin(ang)
    cos_t = jnp.concatenate([cos, cos, cos, cos], axis=-1)
    sin_t = jnp.concatenate([-sin, sin, -sin, sin], axis=-1)
    return cos_t, sin_t


def _nsa_prep_kernel(nq_ref, nkv_ref, cos_ref, sin_ref,
                     qc_ref, qr_ref, slc_ref, win_ref, ks_ref, vs_ref, kw_ref, vw_ref):
    cos = cos_ref[...]
    sin = sin_ref[...]
    lane = lax.broadcasted_iota(jnp.int32, cos.shape, 1)
    first_half = (lane % NSA_DH) < (NSA_DH // 2)

    def rope(x):
        swapped = jnp.where(first_half, pltpu.roll(x, LANES - NSA_DH // 2, 1),
                            pltpu.roll(x, NSA_DH // 2, 1))
        return x * cos + swapped * sin

    scale = NSA_DH ** -0.5
    for j in range(NSA_HEADS * NSA_DH // LANES):
        x = nq_ref[:, j * LANES:(j + 1) * LANES]
        qc_ref[:, j * LANES:(j + 1) * LANES] = x * scale
        qr_ref[:, j * LANES:(j + 1) * LANES] = rope(x) * scale

    kv_w = NSA_KV * NSA_DH
    for br, (row_ref, k_ref, v_ref) in enumerate(((slc_ref, ks_ref, vs_ref),
                                                   (win_ref, kw_ref, vw_ref))):
        base = (br + 1) * ROW_W
        kr = rope(nkv_ref[:, base:base + kv_w])
        v = nkv_ref[:, base + kv_w:base + 2 * kv_w]
        row_ref[:, 0:kv_w] = kr
        row_ref[:, kv_w:2 * kv_w] = v
        for g in range(NSA_KV):
            k_ref[0, g] = kr[:, g * NSA_DH:(g + 1) * NSA_DH].astype(k_ref.dtype)
            v_ref[0, g] = v[:, g * NSA_DH:(g + 1) * NSA_DH].astype(v_ref.dtype)


def _nsa_prep(z, pos, batch, t, tm, kv_dtype):
    cos_t, sin_t = _rope_tables(pos)
    nt = t // tm
    rowblk = lambda b, j: b * nt + j
    kv_shape = jax.ShapeDtypeStruct((batch, NSA_KV, t, NSA_DH), kv_dtype)
    kv_spec = pl.BlockSpec((1, NSA_KV, tm, NSA_DH), lambda b, j: (b, 0, j, 0))
    qw = NSA_HEADS * NSA_DH
    return pl.pallas_call(
        _nsa_prep_kernel,
        grid=(batch, nt),
        in_specs=[pl.BlockSpec((tm, qw), lambda b, j: (rowblk(b, j), Z_NQ // qw)),
                  pl.BlockSpec((tm, NKV_W), lambda b, j: (rowblk(b, j), Z_NKV // NKV_W)),
                  pl.BlockSpec((tm, LANES), lambda b, j: (j, 0)),
                  pl.BlockSpec((tm, LANES), lambda b, j: (j, 0))],
        out_specs=[pl.BlockSpec((tm, qw), lambda b, j: (rowblk(b, j), 0)),
                   pl.BlockSpec((tm, qw), lambda b, j: (rowblk(b, j), 0)),
                   pl.BlockSpec((tm, ROW_W), lambda b, j: (rowblk(b, j), 0)),
                   pl.BlockSpec((tm, ROW_W), lambda b, j: (rowblk(b, j), 0)),
                   kv_spec, kv_spec, kv_spec, kv_spec],
        out_shape=[jax.ShapeDtypeStruct((batch * t, qw), F32),
                   jax.ShapeDtypeStruct((batch * t, qw), F32),
                   jax.ShapeDtypeStruct((batch * t, ROW_W), F32),
                   jax.ShapeDtypeStruct((batch * t, ROW_W), F32),
                   kv_shape, kv_shape, kv_shape, kv_shape],
        compiler_params=_params(("parallel", "parallel")),
    )(z, z, cos_t, sin_t)


def _compress_weights(pe, w1, w2):
    eye = jnp.eye(2, dtype=F32)
    w1r = w1.reshape(2, NSA_CMP, NSA_DH, NSA_DH)
    w1big = jnp.einsum('srde,st,gh->rsgdthe', w1r, eye, eye).reshape(NSA_CMP * ROW_W, ROW_W)
    w2big = jnp.einsum('sed,st,gh->sgethd', w2, eye, eye).reshape(ROW_W, ROW_W)
    pe_big = jnp.broadcast_to(jnp.transpose(pe, (1, 0, 2))[:, :, None, :],
                              (NSA_CMP, 2, NSA_KV, NSA_DH)).reshape(1, NSA_CMP * ROW_W)
    return pe_big, w1big.astype(BF16), w2big.astype(BF16)


def _compress_kernel(x_ref, pe_ref, w1_ref, w2_ref, o_ref):
    x = (x_ref[...] + pe_ref[...]).astype(BF16)
    hid = _gelu(jnp.dot(x, w1_ref[...], preferred_element_type=F32))
    o_ref[...] = jnp.dot(hid.astype(BF16), w2_ref[...], preferred_element_type=F32)


def _compress(rows, pe_big, w1big, w2big, tm):
    n, kdim = rows.shape
    return pl.pallas_call(
        _compress_kernel,
        grid=(n // tm,),
        in_specs=[pl.BlockSpec((tm, kdim), lambda i: (i, 0)),
                  pl.BlockSpec((1, kdim), lambda i: (0, 0)),
                  pl.BlockSpec((kdim, ROW_W), lambda i: (0, 0)),
                  pl.BlockSpec((ROW_W, ROW_W), lambda i: (0, 0))],
        out_specs=pl.BlockSpec((tm, ROW_W), lambda i: (i, 0)),
        out_shape=jax.ShapeDtypeStruct((n, ROW_W), F32),
        compiler_params=_params(("parallel",)),
    )(rows, pe_big, w1big, w2big)


def _split_compressed(cmp_out, batch):
    nb = cmp_out.shape[0] // batch
    x = cmp_out.reshape(batch, nb // 2, 2, 2, NSA_KV, NSA_DH)
    x = jnp.transpose(x, (3, 0, 4, 2, 1, 5)).reshape(2, batch, NSA_KV, nb, NSA_DH)
    return x[0].astype(BF16), x[1].astype(BF16)


CMP_PAGES_PER_STEP = 64
SLC_PAGES_PER_STEP = 16
BLOCKS_PER_PAGE = PAGE_SIZE // NSA_CMP


def _page_specs(block, n_pages, per_step):
    def spec(j):
        return pl.BlockSpec(block, lambda b, s, pt: (pt[b * n_pages + s * per_step + j], 0, 0))
    return [spec(j) for j in range(per_step)]


CMP_ROW_PITCH = NSA_CMP + 4


def _paged_compress_kernel(pt_ref, *refs):
    n_in = CMP_PAGES_PER_STEP
    pe_ref, w1_ref, w2_ref, o_ref, x_ref = refs[n_in:]
    m = n_in * BLOCKS_PER_PAGE
    n_slabs = ROW_W // LANES
    for j, p_ref in enumerate(refs[:n_in]):
        x = p_ref[0].T
        for n in range(BLOCKS_PER_PAGE):
            base = (j * BLOCKS_PER_PAGE + n) * CMP_ROW_PITCH
            for sl in range(n_slabs):
                x_ref[sl, base:base + NSA_CMP, :] = x[n * NSA_CMP:(n + 1) * NSA_CMP,
                                                      sl * LANES:(sl + 1) * LANES]
    acc = jnp.zeros((m, ROW_W), F32)
    for r in range(NSA_CMP):
        lhs = jnp.concatenate([x_ref[sl, pl.ds(r, m, stride=CMP_ROW_PITCH), :]
                               for sl in range(n_slabs)], axis=1)
        lhs = (lhs + pe_ref[r:r + 1, :]).astype(BF16)
        acc = acc + jnp.dot(lhs, w1_ref[r], preferred_element_type=F32)
    o_ref[...] = jnp.dot(_gelu(acc).astype(BF16), w2_ref[...], preferred_element_type=F32)


def _paged_compress(cache_t, page_table, pe_big, w1big, w2big):
    batch, n_pages = page_table.shape
    rows = CMP_PAGES_PER_STEP * BLOCKS_PER_PAGE
    steps = n_pages // CMP_PAGES_PER_STEP
    pe_rows = pe_big.reshape(NSA_CMP, ROW_W)
    w1_rows = w1big.reshape(NSA_CMP, ROW_W, ROW_W)
    const = lambda a: pl.BlockSpec(a.shape, lambda b, s, pt: (0,) * a.ndim)
    return pl.pallas_call(
        _paged_compress_kernel,
        grid_spec=pltpu.PrefetchScalarGridSpec(
            num_scalar_prefetch=1, grid=(batch, steps),
            in_specs=_page_specs((1, ROW_W, PAGE_SIZE), n_pages, CMP_PAGES_PER_STEP)
            + [const(pe_rows), const(w1_rows), const(w2big)],
            out_specs=pl.BlockSpec((rows, ROW_W), lambda b, s, pt: (b * steps + s, 0)),
            scratch_shapes=[pltpu.VMEM((ROW_W // LANES, rows * CMP_ROW_PITCH, LANES), F32)]),
        out_shape=jax.ShapeDtypeStruct((batch * n_pages * BLOCKS_PER_PAGE, ROW_W), F32),
        compiler_params=_params(("parallel", "parallel")),
    )(page_table.reshape(-1), *([cache_t] * CMP_PAGES_PER_STEP), pe_rows, w1_rows, w2big)


def _topk_mask(score, k):
    n = score.shape[-1]
    lane = lax.broadcasted_iota(jnp.int32, score.shape, 1).astype(F32)
    sel = jnp.zeros(score.shape, F32)
    for _ in range(k):
        m = jnp.max(score, axis=-1, keepdims=True)
        idx = jnp.min(jnp.where(score == m, lane, float(n)), axis=-1, keepdims=True)
        pick = lane == idx
        sel = jnp.where(pick, 1.0, sel)
        score = jnp.where(pick, -jnp.inf, score)
    return sel


def _softmax_step(carry, s, v):
    m, l, acc = carry
    m_new = jnp.maximum(m, jnp.max(s, axis=-1, keepdims=True))
    m_safe = jnp.where(m_new == -jnp.inf, 0.0, m_new)
    p = jnp.exp(s - m_safe)
    alpha = jnp.exp(m - m_safe)
    l = alpha * l + jnp.sum(p, axis=-1, keepdims=True)
    acc = alpha * acc + _dot(p, v)
    return m_new, l, acc


MASKED = -1e30


def _topk_mask_rows(score, k):
    n = score.shape[0]
    row = lax.broadcasted_iota(jnp.int32, score.shape, 0).astype(F32)
    sel = jnp.zeros(score.shape, F32)
    for _ in range(k):
        m = jnp.max(score, axis=0, keepdims=True)
        idx = jnp.min(jnp.where(score == m, row, float(n)), axis=0, keepdims=True)
        pick = row == idx
        sel = jnp.where(pick, 1.0, sel)
        score = jnp.where(pick, -jnp.inf, score)
    return sel


def _softmax_step_cols(carry, s, vt):
    m, l, acc = carry
    m_new = jnp.maximum(m, jnp.max(s, axis=0, keepdims=True))
    p = jnp.exp(s - m_new)
    alpha = jnp.exp(m - m_new)
    l = alpha * l + jnp.sum(p, axis=0, keepdims=True)
    acc = alpha * acc + _dot(vt, p)
    return m_new, l, acc


def _nsa_prompt_kernel(qc_ref, qr_ref, kc_ref, vct_ref, ks_ref, vst_ref, kw_ref, vwt_ref, sm_ref,
                       o_ref, bias_ref, *, tq, tk, tkw):
    hp = NSA_HEADS // NSA_KV
    r = hp * tq
    g = pl.program_id(1)
    qi = pl.program_id(2)
    q0 = qi * tq
    ncp = kc_ref.shape[2]
    nch = ncp // 2
    nsl = bias_ref.shape[0]
    per_tile = tk // NSA_SEL

    def heads_on_lanes(ref):
        xt = ref[...].T
        return jnp.concatenate([xt[h * NSA_DH:(h + 1) * NSA_DH] for h in range(hp)],
                               axis=1).astype(BF16)

    qct = heads_on_lanes(qc_ref)
    qrt = heads_on_lanes(qr_ref)
    t_q = q0 + lax.broadcasted_iota(jnp.int32, (1, tq), 1)
    t_lane = jnp.concatenate([t_q] * hp, axis=1)

    s = _dot(kc_ref[0, 0], qct)
    crow = lax.broadcasted_iota(jnp.int32, (ncp, 1), 0)
    cblk = 2 * (crow % nch) + crow // nch
    s = jnp.where((cblk + 1) * NSA_CMP <= t_lane + 1, s, -jnp.inf)
    m = jnp.max(s, axis=0, keepdims=True)
    e = jnp.exp(s - jnp.where(m == -jnp.inf, 0.0, m))
    p_cmp = e / jnp.maximum(jnp.sum(e, axis=0, keepdims=True), 1e-30)
    o_cmp = _dot(vct_ref[0, 0], p_cmp)

    n_keys = kw_ref.shape[2]
    wlen = min(NSA_WIN + tq, n_keys)
    wstart = pl.multiple_of(jnp.minimum(jnp.maximum(q0 - NSA_WIN, 0), n_keys - wlen), tkw)
    sc = _dot(kw_ref[0, 0, pl.ds(wstart, wlen), :], qrt)
    rel = t_lane - (wstart + lax.broadcasted_iota(jnp.int32, (wlen, 1), 0))
    sc = jnp.where((rel >= 0) & (rel < NSA_WIN), sc, MASKED)
    p_win = jnp.exp(sc - jnp.max(sc, axis=0, keepdims=True))
    o_win = (_dot(vwt_ref[0, 0, :, pl.ds(wstart, wlen)], p_win)
             / jnp.sum(p_win, axis=0, keepdims=True))

    imp = p_cmp[:, 0:tq]
    for h in range(1, hp):
        imp = imp + p_cmp[:, h * tq:(h + 1) * tq]
    imp = imp[:nch] + imp[nch:]
    blk = lax.broadcasted_iota(jnp.int32, (nsl, tq), 0)
    cur = t_q // NSA_SEL
    forced = (blk == 0) | (blk == cur) | (blk == cur - 1)
    score = jnp.where(blk > cur, -jnp.inf, jnp.where(forced, NSA_FORCE, imp))
    bias = (_topk_mask_rows(score, NSA_TOPN) - 1.0) * (-MASKED)
    bias_ref[...] = jnp.concatenate([bias] * hp, axis=1)

    init = (jnp.full((1, r), MASKED, F32), jnp.zeros((1, r), F32), jnp.zeros((NSA_DH, r), F32))

    def slc_scores(kt):
        start = pl.multiple_of(kt * tk, tk)
        sc = _dot(ks_ref[0, 0, pl.ds(start, tk), :], qrt)
        brow = bias_ref[pl.ds(pl.multiple_of(kt * per_tile, per_tile), per_tile), :]
        sc = jnp.concatenate([sc[j * NSA_SEL:(j + 1) * NSA_SEL] + brow[j:j + 1]
                              for j in range(per_tile)], axis=0)
        return sc, vst_ref[0, 0, :, pl.ds(start, tk)]

    def slc_body(kt, carry):
        sc, vt = slc_scores(kt)
        return _softmax_step_cols(carry, sc, vt)

    kd = q0 // tk
    carry = lax.fori_loop(0, kd, slc_body, init)
    sc, vt = slc_scores(kd)
    kpos = kd * tk + lax.broadcasted_iota(jnp.int32, (tk, 1), 0)
    carry = _softmax_step_cols(carry, jnp.where(kpos <= t_lane, sc, MASKED), vt)
    o_slc = carry[2] / carry[1]

    sig = jax.nn.sigmoid(sm_ref[...].T)
    gw = N_BRANCH * hp
    gates = jnp.where(g == 0, sig[SM_NG:SM_NG + gw], sig[SM_NG + gw:SM_NG + 2 * gw])
    outs = []
    for h in range(hp):
        cols = slice(h * tq, (h + 1) * tq)
        outs.append(gates[3 * h:3 * h + 1] * o_cmp[:, cols]
                    + gates[3 * h + 1:3 * h + 2] * o_slc[:, cols]
                    + gates[3 * h + 2:3 * h + 3] * o_win[:, cols])
    o_ref[...] = jnp.concatenate(outs, axis=0).T


def _nsa_prompt_attn(qc, qr, kc, vct, ks, vst, kw, vwt, z, batch, t, tq, tk, tkw):
    nq = t // tq
    gw = NSA_HEADS * NSA_DH // NSA_KV
    assert tk % (SUBLANES * NSA_SEL) == 0 and tq == tkw and tk % tq == 0 and t % tk == 0
    rowblk = lambda b, g, i: b * nq + i
    full = lambda a: pl.BlockSpec((1, 1) + a.shape[2:], lambda b, g, i: (b, g, 0, 0))
    q_spec = pl.BlockSpec((tq, gw), lambda b, g, i: (rowblk(b, g, i), g))
    return pl.pallas_call(
        functools.partial(_nsa_prompt_kernel, tq=tq, tk=tk, tkw=tkw),
        grid=(batch, NSA_KV, nq),
        in_specs=[q_spec, q_spec] + [full(a) for a in (kc, vct, ks, vst, kw, vwt)]
        + [pl.BlockSpec((tq, LANES), lambda b, g, i: (rowblk(b, g, i), Z_SMALL // LANES))],
        out_specs=pl.BlockSpec((tq, gw), lambda b, g, i: (rowblk(b, g, i), g)),
        out_shape=jax.ShapeDtypeStruct((batch * t, NSA_HEADS * NSA_DH), F32),
        scratch_shapes=[pltpu.VMEM((t // NSA_SEL, NSA_HEADS // NSA_KV * tq), F32)],
        compiler_params=_params(("parallel", "parallel", "arbitrary")),
    )(qc, qr, kc, vct, ks, vst, kw, vwt, z)


def _softmax_step_vt(carry, s, vt):
    m, l, acc = carry
    m_new = jnp.maximum(m, jnp.max(s, axis=-1, keepdims=True))
    m_safe = jnp.where(m_new == -jnp.inf, 0.0, m_new)
    p = jnp.exp(s - m_safe)
    alpha = jnp.exp(m - m_safe)
    l = alpha * l + jnp.sum(p, axis=-1, keepdims=True)
    acc = alpha * acc + _dot_nt(p, vt)
    return m_new, l, acc


def _nsa_decode_kernel(pt_ref, *refs, tq, pos0, wrel0, n_keys):
    n_in = SLC_PAGES_PER_STEP
    qc_ref, qr_ref, kc_ref, vc_ref = refs[0:4]
    pages = refs[4:4 + n_in]
    (kw_ref, vw_ref, kt_ref, vt_ref, sm_ref, o_ref,
     m_ref, l_ref, acc_ref, ocmp_ref, sel_ref) = refs[4 + n_in:]
    hp = NSA_HEADS // NSA_KV
    r = hp * tq
    gwid = hp * NSA_DH
    step = pl.program_id(1)
    n_steps = pl.num_programs(1)
    tk = n_in * PAGE_SIZE
    per_tile = tk // NSA_SEL
    n_tiles = sel_ref.shape[1]
    ncp = kc_ref.shape[2]
    nch = ncp // 2
    nsl = -(-(n_tiles * per_tile) // LANES) * LANES
    kv_w = NSA_KV * NSA_DH

    def stack_heads(ref, g):
        x = ref[:, g * gwid:(g + 1) * gwid]
        return jnp.concatenate([x[:, h * NSA_DH:(h + 1) * NSA_DH] for h in range(hp)],
                               axis=0).astype(BF16)

    t_q = pos0 + lax.broadcasted_iota(jnp.int32, (tq, 1), 0)
    t_row = jnp.concatenate([t_q] * hp, axis=0)
    expand = (lax.broadcasted_iota(jnp.int32, (per_tile, tk), 0)
              == lax.broadcasted_iota(jnp.int32, (per_tile, tk), 1) // NSA_SEL).astype(BF16)

    def picked_rows(g, tile, width):
        pk = jnp.dot(sel_ref[g, tile].astype(BF16), expand[:, :width], preferred_element_type=F32)
        return jnp.concatenate([pk] * hp, axis=0)

    @pl.when(step == 0)
    def _():
        ccol = lax.broadcasted_iota(jnp.int32, (r, ncp), 1)
        cvis = (2 * (ccol % nch) + ccol // nch + 1) * NSA_CMP <= t_row + 1
        blk = lax.broadcasted_iota(jnp.int32, (tq, nsl), 1)
        cur = t_q // NSA_SEL
        forced = (blk == 0) | (blk == cur) | (blk == cur - 1)
        group_scores = []
        for g in range(NSA_KV):
            s = jnp.where(cvis, _dot_nt(stack_heads(qc_ref, g), kc_ref[0, g]), -jnp.inf)
            m = jnp.max(s, axis=-1, keepdims=True)
            e = jnp.exp(s - jnp.where(m == -jnp.inf, 0.0, m))
            p_cmp = e / jnp.maximum(jnp.sum(e, axis=-1, keepdims=True), 1e-30)
            ocmp_ref[g] = _dot(p_cmp, vc_ref[0, g])
            imp = p_cmp[0:tq]
            for h in range(1, hp):
                imp = imp + p_cmp[h * tq:(h + 1) * tq]
            imp = imp[:, :nch] + imp[:, nch:]
            if nsl > nch:
                imp = jnp.concatenate([imp, jnp.zeros((tq, nsl - nch), F32)], axis=1)
            group_scores.append(jnp.where(blk > cur, -jnp.inf, jnp.where(forced, NSA_FORCE, imp)))
        sel_all = _topk_mask(jnp.concatenate(group_scores, axis=0), NSA_TOPN)
        for g in range(NSA_KV):
            sel = sel_all[g * tq:(g + 1) * tq]
            for j in range(n_tiles):
                sel_ref[g, j] = sel[:, j * per_tile:(j + 1) * per_tile]
            m_ref[g] = jnp.full((r, 1), -jnp.inf, F32)
            l_ref[g] = jnp.zeros((r, 1), F32)
            acc_ref[g] = jnp.zeros((r, NSA_DH), F32)

    kpos = step * tk + lax.broadcasted_iota(jnp.int32, (1, tk), 1)
    scores, values = [], []
    for g in range(NSA_KV):
        k_t = jnp.concatenate([p[0, g * NSA_DH:(g + 1) * NSA_DH, :] for p in pages], axis=1)
        values.append(jnp.concatenate(
            [p[0, kv_w + g * NSA_DH:kv_w + (g + 1) * NSA_DH, :] for p in pages], axis=1))
        scores.append(_dot(stack_heads(qr_ref, g), k_t))
    picked = [picked_rows(g, step, tk) for g in range(NSA_KV)]
    for g in range(NSA_KV):
        ok = (picked[g] > 0.5) & (kpos <= t_row)
        m, l, acc = _softmax_step_vt((m_ref[g], l_ref[g], acc_ref[g]),
                                     jnp.where(ok, scores[g], -jnp.inf), values[g])
        m_ref[g] = m
        l_ref[g] = l
        acc_ref[g] = acc

    @pl.when(step == n_steps - 1)
    def _():
        sig = jax.nn.sigmoid(sm_ref[...])
        tw = kt_ref.shape[2]
        tpos = n_keys + lax.broadcasted_iota(jnp.int32, (1, tw), 1)
        wpos = pos0 - wrel0 + lax.broadcasted_iota(jnp.int32, (1, kw_ref.shape[2]), 1)
        rel = t_row - wpos
        wok = (rel >= 0) & (rel < NSA_WIN) & (wpos >= 0)
        qr4s = [stack_heads(qr_ref, g) for g in range(NSA_KV)]
        tail_s = [_dot_nt(qr4s[g], kt_ref[0, g]) for g in range(NSA_KV)]
        win_s = [_dot_nt(qr4s[g], kw_ref[0, g]) for g in range(NSA_KV)]
        for g in range(NSA_KV):
            ok = (picked_rows(g, n_tiles - 1, tw) > 0.5) & (tpos <= t_row)
            carry = _softmax_step((m_ref[g], l_ref[g], acc_ref[g]),
                                  jnp.where(ok, tail_s[g], -jnp.inf), vt_ref[0, g])
            o_slc = carry[2] / jnp.maximum(carry[1], 1e-30)
            s = jnp.where(wok, win_s[g], -jnp.inf)
            m = jnp.max(s, axis=-1, keepdims=True)
            e = jnp.exp(s - jnp.where(m == -jnp.inf, 0.0, m))
            o_win = _dot(e, vw_ref[0, g]) / jnp.maximum(jnp.sum(e, axis=-1, keepdims=True), 1e-30)
            o_cmp = ocmp_ref[g]
            base = SM_NG + g * N_BRANCH * hp
            outs = []
            for h in range(hp):
                rows = slice(h * tq, (h + 1) * tq)
                c0 = base + N_BRANCH * h
                outs.append(sig[:, c0:c0 + 1] * o_cmp[rows] + sig[:, c0 + 1:c0 + 2] * o_slc[rows]
                            + sig[:, c0 + 2:c0 + 3] * o_win[rows])
            o_ref[:, g * gwid:(g + 1) * gwid] = jnp.concatenate(outs, axis=1)


def _nsa_decode_attn(qc, qr, kc, vc, cache_t, page_table, kw, vw, ktail, vtail, z, tq, pos0, wrel0):
    batch, n_pages = page_table.shape
    n_keys = n_pages * PAGE_SIZE
    steps = n_pages // SLC_PAGES_PER_STEP
    hp = NSA_HEADS // NSA_KV
    r = hp * tq
    qw = NSA_HEADS * NSA_DH
    per_tile = SLC_PAGES_PER_STEP * PAGE_SIZE // NSA_SEL
    assert pos0 == n_keys and ktail.shape[2] <= per_tile * NSA_SEL
    per_b = lambda a: pl.BlockSpec((1,) + a.shape[1:], lambda b, s, pt: (b, 0, 0, 0))
    q_spec = pl.BlockSpec((tq, qw), lambda b, s, pt: (b, 0))
    return pl.pallas_call(
        functools.partial(_nsa_decode_kernel, tq=tq, pos0=pos0, wrel0=wrel0, n_keys=n_keys),
        grid_spec=pltpu.PrefetchScalarGridSpec(
            num_scalar_prefetch=1, grid=(batch, steps),
            in_specs=[q_spec, q_spec, per_b(kc), per_b(vc)]
            + _page_specs((1, ROW_W, PAGE_SIZE), n_pages, SLC_PAGES_PER_STEP)
            + [per_b(kw), per_b(vw), per_b(ktail), per_b(vtail),
               pl.BlockSpec((tq, LANES), lambda b, s, pt: (b, Z_SMALL // LANES))],
            out_specs=pl.BlockSpec((tq, qw), lambda b, s, pt: (b, 0)),
            scratch_shapes=[pltpu.VMEM((NSA_KV, r, 1), F32), pltpu.VMEM((NSA_KV, r, 1), F32),
                            pltpu.VMEM((NSA_KV, r, NSA_DH), F32),
                            pltpu.VMEM((NSA_KV, r, NSA_DH), F32),
                            pltpu.VMEM((NSA_KV, steps + 1, tq, per_tile), F32)]),
        out_shape=jax.ShapeDtypeStruct((batch * tq, qw), F32),
        compiler_params=_params(("parallel", "arbitrary")),
    )(page_table.reshape(-1), qc, qr, kc, vc, *([cache_t] * SLC_PAGES_PER_STEP),
      kw, vw, ktail, vtail, z)


def _mem_attn_kernel(q_ref, kv_ref, o_ref):
    hw = MEM_HEADS * MEM_DH
    for h in range(MEM_HEADS):
        q = q_ref[:, h * MEM_DH:(h + 1) * MEM_DH] * (MEM_DH ** -0.5)
        k = kv_ref[0, :, h * MEM_DH:(h + 1) * MEM_DH]
        v = kv_ref[0, :, hw + h * MEM_DH:hw + (h + 1) * MEM_DH]
        s = _dot_nt(q, k)
        e = jnp.exp(s - jnp.max(s, axis=-1, keepdims=True))
        p = e / jnp.sum(e, axis=-1, keepdims=True)
        o_ref[:, h * MEM_DH:(h + 1) * MEM_DH] = _dot(p, v)


def _mem_attn(z, kv, batch, t, tm):
    nt = t // tm
    hw = MEM_HEADS * MEM_DH
    return pl.pallas_call(
        _mem_attn_kernel,
        grid=(batch, nt),
        in_specs=[pl.BlockSpec((tm, hw), lambda b, j: (b * nt + j, Z_MQ // hw)),
                  pl.BlockSpec((1,) + kv.shape[1:], lambda b, j: (b, 0, 0))],
        out_specs=pl.BlockSpec((tm, hw), lambda b, j: (b * nt + j, 0)),
        out_shape=jax.ShapeDtypeStruct((batch * t, hw), F32),
        compiler_params=_params(("parallel", "parallel")),
    )(z, kv)


def _merge_kernel(x_ref, mg_ref, og_ref, on_ref, om_ref, wg_ref, wn_ref, wm_ref, wo_ref,
                  gf_ref, wq_ref, x1_ref, h2_ref, qp_ref):
    d = D_MODEL
    mix = (jax.nn.sigmoid(mg_ref[:, 0:d]) * _dot(og_ref[...], wg_ref[...])
           + jax.nn.sigmoid(mg_ref[:, d:2 * d]) * _dot(on_ref[...], wn_ref[...])
           + jax.nn.sigmoid(mg_ref[:, 2 * d:3 * d]) * _dot(om_ref[...], wm_ref[...]))
    x1 = x_ref[...] + _dot(mix, wo_ref[...])
    x1_ref[...] = x1
    h2 = _rms(x1, gf_ref[...]).astype(BF16)
    h2_ref[...] = h2
    qp_ref[...] = jnp.dot(h2, wq_ref[...], preferred_element_type=F32)


def _merge(x, z, o_gdn, o_nsa, o_mem, wg, wn, wm, wo, norm_ffn, wq, tm):
    n, d = x.shape
    qw = wq.shape[1]
    row = lambda w: pl.BlockSpec((tm, w), lambda i: (i, 0))
    const = lambda a: pl.BlockSpec(a.shape, lambda i: (0, 0))
    return pl.pallas_call(
        _merge_kernel,
        grid=(n // tm,),
        in_specs=[row(d), pl.BlockSpec((tm, N_BRANCH * d), lambda i: (i, Z_MG)),
                  row(o_gdn.shape[1]), row(o_nsa.shape[1]), row(o_mem.shape[1]),
                  const(wg), const(wn), const(wm), const(wo),
                  pl.BlockSpec((1, d), lambda i: (0, 0)), const(wq)],
        out_specs=[row(d), row(d), row(qw)],
        out_shape=[jax.ShapeDtypeStruct((n, d), F32), jax.ShapeDtypeStruct((n, d), BF16),
                   jax.ShapeDtypeStruct((n, qw), F32)],
        compiler_params=_params(("parallel",)),
    )(x, z, o_gdn, o_nsa, o_mem, wg, wn, wm, wo, norm_ffn.reshape(1, d), wq)


PEER_RANKS = PEER_TOPK + 1


RANK_NONE = 64.0


def _top_values(s, n):
    vals = []
    rank = jnp.full(s.shape, RANK_NONE, F32)
    for k in range(n):
        m = jnp.max(s, axis=0, keepdims=True)
        vals.append(m)
        hit = s >= m
        rank = jnp.where(hit, float(k + 1), rank)
        s = jnp.where(hit, -jnp.inf, s)
    return vals, rank


PEER_PAIRS = [(i, j) for i in range(1, PEER_RANKS + 1) for j in range(1, PEER_RANKS // i + 1)]


def _pair_selectors():
    rows = -(-len(PEER_PAIRS) // SUBLANES) * SUBLANES
    cols = -(-PEER_RANKS // SUBLANES) * SUBLANES
    pa = np.zeros((rows, cols), np.float32)
    pb = np.zeros((rows, cols), np.float32)
    for p, (i, j) in enumerate(PEER_PAIRS):
        pa[p, i - 1] = 1.0
        pb[p, j - 1] = 1.0
    return jnp.asarray(pa), jnp.asarray(pb)


def _peer_route_kernel(qp_ref, sk_ref, pa_ref, pb_ref, r2_ref, e2_ref, nb_ref, e1_ref):
    half = PEER_DKEY // 2
    nt = (((1,), (1,)), ((), ()))
    for h in range(PEER_HEADS):
        qa = qp_ref[:, h * PEER_DKEY:h * PEER_DKEY + half]
        qb = qp_ref[:, h * PEER_DKEY + half:(h + 1) * PEER_DKEY]
        s1 = lax.dot_general(sk_ref[0], qa, nt, precision=HIGHEST, preferred_element_type=F32)
        s2 = lax.dot_general(sk_ref[1], qb, nt, precision=HIGHEST, preferred_element_type=F32)
        a, _ = _top_values(s1, PEER_RANKS)
        b, rank2 = _top_values(s2, PEER_RANKS)
        pad = jnp.full((pa_ref.shape[1] - PEER_RANKS, s1.shape[1]), MASKED, F32)
        a_rows = jnp.maximum(jnp.concatenate(a + [pad], axis=0), MASKED)
        b_rows = jnp.maximum(jnp.concatenate(b + [pad], axis=0), MASKED)
        cand = _dot_hi(pa_ref[...], a_rows) + _dot_hi(pb_ref[...], b_rows)
        prow = lax.broadcasted_iota(jnp.int32, cand.shape, 0)
        cand = jnp.where(prow < len(PEER_PAIRS), cand, -jnp.inf)
        work = cand
        ranked = []
        for _ in range(PEER_RANKS):
            m = jnp.max(work, axis=0, keepdims=True)
            ranked.append(m)
            work = jnp.where(work >= m, -jnp.inf, work)
        tau = 0.5 * (ranked[PEER_TOPK - 1] + ranked[PEER_TOPK])
        top = a[0] + b[0]
        zsum = jnp.sum(jnp.where(cand >= tau, jnp.exp(cand - top), 0.0), axis=0, keepdims=True)
        th = tau - s1
        count = jnp.zeros(s1.shape, F32)
        for bj in b:
            count = count + jnp.where(bj >= th, 1.0, 0.0)
        r2_ref[h] = rank2.astype(BF16)
        e2_ref[h] = jnp.exp(s2 - b[0]).astype(BF16)
        nb_ref[h] = count
        e1_ref[h] = jnp.exp(s1 - a[0]) / zsum


def _peer_route(qp, subkeys, tt):
    n = qp.shape[0]
    pa, pb = _pair_selectors()
    shape = lambda dt: jax.ShapeDtypeStruct((PEER_HEADS, PEER_NKEYS, n), dt)
    spec = pl.BlockSpec((PEER_HEADS, PEER_NKEYS, tt), lambda i: (0, 0, i))
    return pl.pallas_call(
        _peer_route_kernel,
        grid=(n // tt,),
        in_specs=[pl.BlockSpec((tt, qp.shape[1]), lambda i: (i, 0)),
                  pl.BlockSpec(subkeys.shape, lambda i: (0, 0, 0)),
                  pl.BlockSpec(pa.shape, lambda i: (0, 0)),
                  pl.BlockSpec(pb.shape, lambda i: (0, 0))],
        out_specs=[spec, spec, spec, spec],
        out_shape=[shape(BF16), shape(BF16), shape(F32), shape(F32)],
        compiler_params=_params(("parallel",)),
    )(qp, subkeys, pa, pb)


def _peer_dense_kernel(ht_ref, u_ref, vt_ref, r2_ref, e2_ref, nb_ref, e1_ref, x1_ref, gf_ref,
                       y_ref, acc_ref, act_ref, wa_ref, r2s_ref, e2s_ref, rows_ref, *, jb):
    j = pl.program_id(1)
    tt = ht_ref.shape[1]

    @pl.when(j == 0)
    def _():
        acc_ref[...] = jnp.zeros(acc_ref.shape, F32)
        r2s_ref[...] = r2_ref[...]
        e2s_ref[...] = e2_ref[...]

    act_ref[...] = _gelu(jnp.dot(u_ref[...], ht_ref[...],
                                 preferred_element_type=F32)).astype(BF16)
    for jj in range(jb):
        i1 = j * jb + jj
        rows = slice(jj * PEER_NKEYS, (jj + 1) * PEER_NKEYS)
        for h in range(PEER_HEADS):
            rows_ref[jj, h:h + 1, :] = nb_ref[h, pl.ds(i1, 1), :]
            rows_ref[jj, PEER_HEADS + h:PEER_HEADS + h + 1, :] = e1_ref[h, pl.ds(i1, 1), :]
        for c in range(tt // LANES):
            cols = slice(c * LANES, (c + 1) * LANES)
            w = None
            for h in range(PEER_HEADS):
                count = rows_ref[jj, h:h + 1, cols].astype(BF16)
                e1 = rows_ref[jj, PEER_HEADS + h:PEER_HEADS + h + 1, cols].astype(BF16)
                picked = r2s_ref[h, :, cols] <= count
                term = jnp.where(picked, e2s_ref[h, :, cols], jnp.zeros((), BF16)) * e1
                w = term if w is None else w + term
            wa_ref[rows, cols] = w * act_ref[rows, cols]
    acc_ref[...] += jnp.dot(vt_ref[...], wa_ref[...], preferred_element_type=F32)

    @pl.when(j == pl.num_programs(1) - 1)
    def _():
        y_ref[...] = _rms(x1_ref[...] + acc_ref[...].T, gf_ref[...])


def _peer_dense(ht, u, vt, s2, e2, th, e1, x1, norm_final, tt, jb):
    d, n = ht.shape
    n_exp = u.shape[0]
    eb = jb * PEER_NKEYS
    route = pl.BlockSpec((PEER_HEADS, PEER_NKEYS, tt), lambda t, j: (0, 0, t))
    return pl.pallas_call(
        functools.partial(_peer_dense_kernel, jb=jb),
        grid=(n // tt, n_exp // eb),
        in_specs=[pl.BlockSpec((d, tt), lambda t, j: (0, t)),
                  pl.BlockSpec((eb, d), lambda t, j: (j, 0)),
                  pl.BlockSpec((d, eb), lambda t, j: (0, j)),
                  route, route, route, route,
                  pl.BlockSpec((tt, d), lambda t, j: (t, 0)),
                  pl.BlockSpec((1, d), lambda t, j: (0, 0))],
        out_specs=pl.BlockSpec((tt, d), lambda t, j: (t, 0)),
        out_shape=jax.ShapeDtypeStruct((n, d), F32),
        scratch_shapes=[pltpu.VMEM((d, tt), F32), pltpu.VMEM((eb, tt), BF16),
                        pltpu.VMEM((eb, tt), BF16),
                        pltpu.VMEM((PEER_HEADS, PEER_NKEYS, tt), BF16),
                        pltpu.VMEM((PEER_HEADS, PEER_NKEYS, tt), BF16),
                        pltpu.VMEM((jb, 2 * PEER_HEADS, tt), F32)],
        compiler_params=_params(("parallel", "arbitrary")),
    )(ht, u, vt, s2, e2, th, e1, x1, norm_final.reshape(1, d))


def _permute_w_in(w_in):
    sizes = (GDN_CONV_CH, GDN_HEADS * GDN_DV, GDN_HEADS, GDN_HEADS, NSA_HEADS * NSA_DH, NKV_W,
             N_BRANCH * NSA_HEADS, MEM_HEADS * MEM_DH, N_BRANCH * D_MODEL)
    qkv, zg, a, b, nq, nkv, ng, mq, mg = jnp.split(w_in, np.cumsum(sizes)[:-1].tolist(), axis=1)
    pad = jnp.zeros((w_in.shape[0], LANES - a.shape[1] - b.shape[1] - ng.shape[1]), w_in.dtype)
    return jnp.concatenate([mg, qkv, zg, nq, mq, nkv, a, b, ng, pad], axis=1).astype(BF16)


def _tokens_tile(n, pref):
    return pref if n % pref == 0 else n


def _layer(x, pos0, kv_mem, nsa_keys, gdn_state, conv_buf, w, peer_tt):
    batch, t, d = x.shape
    n = batch * t
    xf = x.reshape(n, d)
    z = _norm_matmul(xf, w['norm_attn'], w['w_in'], _tokens_tile(n, 256), 640)

    o_gdn, s_new, conv_new = _gdn(z, conv_buf, gdn_state, w['gdn_conv'], w['gdn_a_log'],
                                  w['gdn_dt_bias'], w['gdn_norm'], batch, t)

    prep_tm = _tokens_tile(t, 512)
    kv_dtype = BF16 if prep_tm % 16 == 0 else F32
    qc, qr, slc_rows, win_rows, ks, vs, kw, vw = _nsa_prep(
        z, pos0 + jnp.arange(t), batch, t, prep_tm, kv_dtype)
    cmp_rows = z[:, Z_NKV:Z_NKV + ROW_W]
    o_nsa = nsa_keys(z, qc, qr, cmp_rows, ks, vs, kw, vw)

    o_mem = _mem_attn(z, kv_mem, batch, t, _tokens_tile(t, 512))

    x1, h2, qp = _merge(xf, z, o_gdn, o_nsa, o_mem, w['w_gdn_out'], w['w_nsa_out'],
                        w['w_mem_out'], w['w_o'], w['norm_ffn'], w['peer_wq'],
                        _tokens_tile(n, 256))
    s2, e2, th, e1 = _peer_route(qp, w['peer_subkeys'], 256)
    y = _peer_dense(h2.T, w['peer_u'], w['peer_vt'], s2, e2, th, e1, x1, w['norm_final'],
                    peer_tt, 8)
    row5 = lambda a: a.reshape(batch, t, 2, NSA_KV, NSA_DH)
    return (y.reshape(batch, t, d), row5(cmp_rows), row5(slc_rows), row5(win_rows), s_new,
            conv_new)


def kernel(x_prompt, x_sample, cache_mem_kv, cache_cmp_kv, cache_slc_kv, cache_win_kv, state_gdn, state_conv, page_table, mem_prompt, norm_attn, w_in, gdn_conv, gdn_a_log, gdn_dt_bias, gdn_norm, nsa_cmp_pe, nsa_cmp_w1, nsa_cmp_w2, norm_mem, w_mem_kv, w_gdn_out, w_nsa_out, w_mem_out, w_o, norm_ffn, peer_wq, peer_subkeys, peer_u, peer_v, norm_final):
    depth = w_in.shape[0]
    assert depth == 1
    l = 0
    bp, seq, d = x_prompt.shape
    bs, tdec, _ = x_sample.shape
    n_pages = page_table.shape[1]
    past = n_pages * PAGE_SIZE
    assert past % NSA_CMP == 0 and tdec < NSA_CMP and seq % LANES == 0

    pe_big, w1big, w2big = _compress_weights(nsa_cmp_pe[l], nsa_cmp_w1[l], nsa_cmp_w2[l])
    w = dict(norm_attn=norm_attn[l], w_in=_permute_w_in(w_in[l]), gdn_conv=gdn_conv[l],
             gdn_a_log=gdn_a_log[l], gdn_dt_bias=gdn_dt_bias[l], gdn_norm=gdn_norm[l],
             w_gdn_out=w_gdn_out[l].astype(BF16), w_nsa_out=w_nsa_out[l].astype(BF16),
             w_mem_out=w_mem_out[l].astype(BF16), w_o=w_o[l].astype(BF16), norm_ffn=norm_ffn[l],
             peer_wq=peer_wq[l].astype(BF16), peer_subkeys=peer_subkeys[l],
             peer_u=peer_u[l].astype(BF16), peer_vt=peer_v[l].astype(BF16).T,
             norm_final=norm_final)

    mem_n = mem_prompt.shape[0] * mem_prompt.shape[1]
    kvm = _norm_matmul(mem_prompt.reshape(mem_n, d), norm_mem[l], w_mem_kv[l].astype(BF16),
                       _tokens_tile(mem_n, 256), 512).reshape(bp, mem_prompt.shape[1], -1)

    def prompt_keys(z, qc, qr, cmp_rows, ks, vs, kw, vw):
        n_blk = bp * seq // NSA_CMP
        cmp_out = _compress(cmp_rows.reshape(n_blk, NSA_CMP * ROW_W), pe_big, w1big, w2big,
                            _tokens_tile(n_blk, 128))
        kc, vc = _split_compressed(cmp_out, bp)
        tr = lambda a: jnp.transpose(a, (0, 1, 3, 2))
        return _nsa_prompt_attn(qc, qr, kc, tr(vc), ks, tr(vs), kw, tr(vw), z, bp, seq,
                                128, _tokens_tile(seq, 512), 128)

    yp, cmp_p, slc_p, win_p, gdn_p, conv_p = _layer(
        x_prompt, 0, kvm, prompt_keys,
        jnp.zeros((bp, GDN_HEADS, GDN_DK, GDN_DV), F32),
        jnp.zeros((bp, GDN_CONV - 1, GDN_CONV_CH), F32), w, 512)
    win_len_p = min(NSA_WIN, seq)
    win_p = win_p[:, seq - win_len_p:]

    n_pool = cache_cmp_kv.shape[1]
    feature_major = lambda c: jnp.transpose(c, (0, 2, 3, 4, 1)).reshape(n_pool, ROW_W, PAGE_SIZE)
    cache_cmp = feature_major(cache_cmp_kv[l])
    cache_slc = feature_major(cache_slc_kv[l])
    cache_win = cache_win_kv[l].reshape(bs, -1, ROW_W)
    wb = cache_win.shape[1]
    assert wb == NSA_WIN
    kv_w = NSA_KV * NSA_DH

    def pad_rows(a, rows):
        return jnp.pad(a, ((0, 0), (0, 0), (0, rows - a.shape[2]), (0, 0))).astype(BF16)

    def split_rows(rows):
        r = rows.reshape(bs, rows.shape[1], 2, NSA_KV, NSA_DH)
        return jnp.transpose(r[:, :, 0], (0, 2, 1, 3)), jnp.transpose(r[:, :, 1], (0, 2, 1, 3))

    def sample_keys(z, qc, qr, cmp_rows, ks_new, vs_new, kw_new, vw_new):
        cmp_out = _paged_compress(cache_cmp, page_table, pe_big, w1big, w2big)
        kc, vc = _split_compressed(cmp_out, bs)
        kwc, vwc = split_rows(cache_win)
        win_rows = wb + LANES
        kw = pad_rows(jnp.concatenate([kwc, kw_new], axis=2), win_rows)
        vw = pad_rows(jnp.concatenate([vwc, vw_new], axis=2), win_rows)
        return _nsa_decode_attn(qc, qr, kc, vc, cache_slc, page_table, kw, vw,
                                pad_rows(ks_new, LANES), pad_rows(vs_new, LANES), z,
                                tdec, past, wb)

    ys, cmp_s, slc_s, win_new, gdn_s, conv_s = _layer(
        x_sample, past, cache_mem_kv[l].reshape(bs, cache_mem_kv.shape[2], -1), sample_keys,
        state_gdn[l], state_conv[l], w, 256)
    win_all = jnp.concatenate([cache_win_kv[l], win_new], axis=1)
    win_s = win_all[:, win_all.shape[1] - min(NSA_WIN, past + tdec):]

    stack = lambda a: a[None]
    return (yp, ys, stack(kvm.reshape(bp, mem_prompt.shape[1], 2, MEM_HEADS, MEM_DH)),
            stack(cmp_p), stack(slc_p), stack(win_p), stack(gdn_p), stack(conv_p),
            stack(cmp_s), stack(slc_s), stack(win_s), stack(gdn_s), stack(conv_s))
```

```python
import functools
import math

import jax
import jax.numpy as jnp
import numpy as np
from jax import lax
from jax.experimental import pallas as pl
from jax.experimental.pallas import tpu as pltpu

F32 = jnp.float32
BF16 = jnp.bfloat16
HIGHEST = lax.Precision.HIGHEST

D_MODEL = 1024
PAGE_SIZE = 128
GDN_HEADS = 4
GDN_DK = 128
GDN_DV = 128
GDN_CONV = 4
GDN_CHUNK = 64
GDN_CONV_CH = GDN_HEADS * (2 * GDN_DK + GDN_DV)
NSA_HEADS = 8
NSA_KV = 2
NSA_DH = 64
NSA_CMP = 32
NSA_SEL = 64
NSA_TOPN = 16
NSA_WIN = 512
NSA_FORCE = 1e9
MEM_HEADS = 4
MEM_DH = 128
PEER_HEADS = 8
PEER_NKEYS = 128
PEER_DKEY = 256
PEER_TOPK = 16
N_BRANCH = 3
ROPE_THETA = 10000.0
EPS = 1e-6

LANES = 128
SUBLANES = 8
VMEM_LIMIT = 56 * 1024 * 1024

Z_MG = 0
Z_QKV = Z_MG + N_BRANCH * D_MODEL
Z_ZG = Z_QKV + GDN_CONV_CH
Z_NQ = Z_ZG + GDN_HEADS * GDN_DV
Z_MQ = Z_NQ + NSA_HEADS * NSA_DH
Z_NKV = Z_MQ + MEM_HEADS * MEM_DH
Z_SMALL = Z_NKV + 3 * 2 * NSA_KV * NSA_DH
Z_WIDTH = Z_SMALL + LANES
SM_A = 0
SM_B = GDN_HEADS
SM_NG = 2 * GDN_HEADS
NKV_W = 3 * 2 * NSA_KV * NSA_DH
ROW_W = 2 * NSA_KV * NSA_DH


def _params(sem, vmem=VMEM_LIMIT):
    return pltpu.CompilerParams(dimension_semantics=sem, vmem_limit_bytes=vmem)


def _dot(a, b):
    return jnp.dot(a.astype(BF16), b.astype(BF16), preferred_element_type=F32)


def _dot_nt(a, b):
    return lax.dot_general(a.astype(BF16), b.astype(BF16), (((1,), (1,)), ((), ())),
                           preferred_element_type=F32)


def _dot_hi(a, b):
    return jnp.dot(a, b, precision=HIGHEST, preferred_element_type=F32)


def _dot3(a, b):
    ah = a.astype(BF16)
    bh = b.astype(BF16)
    al = (a - ah.astype(F32)).astype(BF16)
    bl = (b - bh.astype(F32)).astype(BF16)
    d = lambda x, y: jnp.dot(x, y, preferred_element_type=F32)
    return d(ah, bh) + d(al, bh) + d(ah, bl)


def _rms(x, g):
    return x * lax.rsqrt(jnp.mean(x * x, axis=-1, keepdims=True) + EPS) * g


def _gelu(x):
    a = -2.0 * math.sqrt(2.0 / math.pi) * math.log2(math.e)
    return x / (1.0 + jnp.exp2(x * (a + (a * 0.044715) * (x * x))))


def _norm_matmul_kernel(x_ref, g_ref, w_ref, o_ref, *, col_chunk):
    yb = _rms(x_ref[...], g_ref[...]).astype(BF16)
    for c0 in range(0, o_ref.shape[1], col_chunk):
        o_ref[:, c0:c0 + col_chunk] = jnp.dot(yb, w_ref[:, c0:c0 + col_chunk],
                                              preferred_element_type=F32)


def _norm_matmul(x, g, w, tm, col_chunk):
    n, d = x.shape
    wc = w.shape[1]
    return pl.pallas_call(
        functools.partial(_norm_matmul_kernel, col_chunk=col_chunk),
        grid=(n // tm,),
        in_specs=[pl.BlockSpec((tm, d), lambda i: (i, 0)),
                  pl.BlockSpec((1, d), lambda i: (0, 0)),
                  pl.BlockSpec((d, wc), lambda i: (0, 0), pipeline_mode=pl.Buffered(1))],
        out_specs=pl.BlockSpec((tm, wc), lambda i: (i, 0)),
        out_shape=jax.ShapeDtypeStruct((n, wc), F32),
        compiler_params=_params(("parallel",)),
    )(x, g.reshape(1, d), w)


def _tri_inverse(lmats, c):
    row = lax.broadcasted_iota(jnp.int32, (c, c), 0)
    col = lax.broadcasted_iota(jnp.int32, (c, c), 1)
    ident = jnp.where(row == col, 1.0, 0.0)
    xs = [ident - lm for lm in lmats]
    ps = [_dot3(lm, lm) for lm in lmats]
    n = 2
    while n < c:
        xs = [x + _dot3(x, p) for x, p in zip(xs, ps)]
        n *= 2
        if n < c:
            ps = [_dot3(p, p) for p in ps]
    return xs


def _gdn_kernel(qkv_ref, zg_ref, sm_ref, buf_ref, s0_ref, cw_ref, alog_ref, dt_ref, gn_ref,
                o_ref, snew_ref, cnew_ref, ext_ref, s_ref, *, tb, n_chunks, nb):
    ci = pl.program_id(1)

    @pl.when(ci == 0)
    def _():
        for i in range(nb):
            ext_ref[i, 0:SUBLANES, :] = buf_ref[i]
            s_ref[i] = s0_ref[i]

    c = GDN_CHUNK
    row = lax.broadcasted_iota(jnp.int32, (c, c), 0)
    col = lax.broadcasted_iota(jnp.int32, (c, c), 1)
    tril = row >= col
    eye = row == col
    hk = GDN_HEADS * GDN_DK
    chains = []
    for i in range(nb):
        u, g_all, beta_all = _gdn_inputs(qkv_ref.at[i], sm_ref.at[i], cw_ref, alog_ref, dt_ref,
                                         cnew_ref.at[i], ext_ref.at[i], tb)
        gc_all = _dot_hi(jnp.where(tril, 1.0, 0.0), g_all)
        for h in range(GDN_HEADS):
            qh = u[:, h * GDN_DK:(h + 1) * GDN_DK]
            kh = u[:, hk + h * GDN_DK:hk + (h + 1) * GDN_DK]
            ch = dict(i=i, h=h, v=u[:, 2 * hk + h * GDN_DV:2 * hk + (h + 1) * GDN_DV])
            ch['q'] = qh * lax.rsqrt(jnp.sum(qh * qh, axis=-1, keepdims=True) + EPS) * (GDN_DK ** -0.5)
            ch['k'] = kh * lax.rsqrt(jnp.sum(kh * kh, axis=-1, keepdims=True) + EPS)
            ch['beta'] = beta_all[:, SM_B + h:SM_B + h + 1]
            gc = gc_all[:, SM_A + h:SM_A + h + 1]
            ch['gc'] = gc
            ch['gl'] = gc_all[c - 1:c, SM_A + h:SM_A + h + 1]
            gc_row = jnp.sum(jnp.where(eye, gc, 0.0), axis=0, keepdims=True)
            ch['decay'] = jnp.exp(jnp.where(tril, gc - gc_row, -jnp.inf))
            ch['kb'] = ch['k'] * ch['beta']
            ch['egc'] = jnp.exp(gc)
            chains.append(ch)
    lmats = [jnp.where(row > col, _dot_nt(ch['kb'], ch['k']) * ch['decay'], 0.0) for ch in chains]
    tinvs = _tri_inverse(lmats, c)
    uus = [_dot(t, ch['v'] * ch['beta']) for t, ch in zip(tinvs, chains)]
    wws = [_dot(t, ch['kb'] * ch['egc']) for t, ch in zip(tinvs, chains)]
    aqks = [_dot_nt(ch['q'], ch['k']) * ch['decay'] for ch in chains]
    states = [s_ref[ch['i'], ch['h']] for ch in chains]
    v_news = [uu - _dot(ww, s) for uu, ww, s in zip(uus, wws, states)]
    outs = [_dot(ch['q'] * ch['egc'], s) + _dot(aqk, vn)
            for ch, s, aqk, vn in zip(chains, states, aqks, v_news)]
    s_news = [s * jnp.exp(ch['gl']) + _dot((ch['k'] * jnp.exp(ch['gl'] - ch['gc'])).T, vn)
              for ch, s, vn in zip(chains, states, v_news)]
    for ch, o, s_new in zip(chains, outs, s_news):
        i, h = ch['i'], ch['h']
        s_ref[i, h] = s_new
        zh = zg_ref[i, :, h * GDN_DV:(h + 1) * GDN_DV]
        o_ref[i, :, h * GDN_DV:(h + 1) * GDN_DV] = (_rms(o[0:tb], gn_ref[...])
                                                    * (zh * jax.nn.sigmoid(zh)))

    @pl.when(ci == n_chunks - 1)
    def _():
        snew_ref[...] = s_ref[...]


def _gdn_inputs(qkv_ref, sm_ref, cw_ref, alog_ref, dt_ref, cnew_ref, ext_ref, tb):
    c = GDN_CHUNK
    if tb < c:
        ext_ref[SUBLANES + tb:, :] = jnp.zeros((c - tb, GDN_CONV_CH), F32)
    ext_ref[SUBLANES:SUBLANES + tb, :] = qkv_ref[...]
    cw = cw_ref[...]
    conv = cw[0:1] * ext_ref[SUBLANES - 3:SUBLANES - 3 + c, :]
    for j in range(1, GDN_CONV):
        conv = conv + cw[j:j + 1] * ext_ref[SUBLANES - 3 + j:SUBLANES - 3 + j + c, :]
    u = conv * jax.nn.sigmoid(conv)
    last_rows = ext_ref[tb:tb + SUBLANES, :]
    cnew_ref[...] = last_rows
    ext_ref[0:SUBLANES, :] = last_rows

    sm = sm_ref[...]
    if tb < c:
        sm = jnp.concatenate([sm, jnp.zeros((c - tb, LANES), F32)], axis=0)
    za = sm + dt_ref[...]
    softplus = jnp.maximum(za, 0.0) + jnp.log1p(jnp.exp(-jnp.abs(za)))
    g_all = -jnp.exp(alog_ref[...]) * softplus
    beta_all = jax.nn.sigmoid(sm)
    if tb < c:
        valid = lax.broadcasted_iota(jnp.int32, (c, 1), 0) < tb
        u = jnp.where(valid, u, 0.0)
        g_all = jnp.where(valid, g_all, 0.0)
        beta_all = jnp.where(valid, beta_all, 0.0)
    return u, g_all, beta_all


GDN_BATCH_PER_STEP = 2


def _gdn(z, conv_buf, s0, conv_w, a_log, dt_bias, gnorm, batch, t):
    c = GDN_CHUNK
    tb = min(t, c)
    n_chunks = t // tb
    nb = GDN_BATCH_PER_STEP if batch % GDN_BATCH_PER_STEP == 0 else 1
    assert tb % SUBLANES == 0 and n_chunks * tb == t and (tb == c or n_chunks == 1)
    buf8 = jnp.pad(conv_buf, ((0, 0), (SUBLANES - (GDN_CONV - 1), 0), (0, 0)))
    alog_row = jnp.zeros((1, LANES), F32).at[0, SM_A:SM_A + GDN_HEADS].set(a_log)
    dt_row = jnp.zeros((1, LANES), F32).at[0, SM_A:SM_A + GDN_HEADS].set(dt_bias)
    z3 = z.reshape(batch, t, Z_WIDTH)
    hv = GDN_HEADS * GDN_DV
    o, s_new, c_new = pl.pallas_call(
        functools.partial(_gdn_kernel, tb=tb, n_chunks=n_chunks, nb=nb),
        grid=(batch // nb, n_chunks),
        in_specs=[
            pl.BlockSpec((nb, tb, GDN_CONV_CH), lambda b, ci: (b, ci, Z_QKV // GDN_CONV_CH)),
            pl.BlockSpec((nb, tb, hv), lambda b, ci: (b, ci, Z_ZG // hv)),
            pl.BlockSpec((nb, tb, LANES), lambda b, ci: (b, ci, Z_SMALL // LANES)),
            pl.BlockSpec((nb, SUBLANES, GDN_CONV_CH), lambda b, ci: (b, 0, 0)),
            pl.BlockSpec((nb, GDN_HEADS, GDN_DK, GDN_DV), lambda b, ci: (b, 0, 0, 0)),
            pl.BlockSpec((GDN_CONV, GDN_CONV_CH), lambda b, ci: (0, 0)),
            pl.BlockSpec((1, LANES), lambda b, ci: (0, 0)),
            pl.BlockSpec((1, LANES), lambda b, ci: (0, 0)),
            pl.BlockSpec((1, GDN_DV), lambda b, ci: (0, 0)),
        ],
        out_specs=[
            pl.BlockSpec((nb, tb, hv), lambda b, ci: (b, ci, 0)),
            pl.BlockSpec((nb, GDN_HEADS, GDN_DK, GDN_DV), lambda b, ci: (b, 0, 0, 0)),
            pl.BlockSpec((nb, SUBLANES, GDN_CONV_CH), lambda b, ci: (b, 0, 0)),
        ],
        out_shape=[
            jax.ShapeDtypeStruct((batch, t, hv), F32),
            jax.ShapeDtypeStruct((batch, GDN_HEADS, GDN_DK, GDN_DV), F32),
            jax.ShapeDtypeStruct((batch, SUBLANES, GDN_CONV_CH), F32),
        ],
        scratch_shapes=[pltpu.VMEM((nb, SUBLANES + c, GDN_CONV_CH), F32),
                        pltpu.VMEM((nb, GDN_HEADS, GDN_DK, GDN_DV), F32)],
        compiler_params=_params(("parallel", "arbitrary")),
    )(z3, z3, z3, buf8, s0, conv_w, alog_row, dt_row, gnorm.reshape(1, GDN_DV))
    return o.reshape(batch * t, hv), s_new, c_new[:, SUBLANES - (GDN_CONV - 1):]


def _rope_tables(pos):
    half = NSA_DH // 2
    inv = jnp.power(ROPE_THETA, -jnp.arange(half, dtype=F32) / half)
    ang = pos.astype(F32)[:, None] * inv[None, :]
    cos, sin = jnp.cos(ang), jnp.sin(ang)
    cos_t = jnp.concatenate([cos, cos, cos, cos], axis=-1)
    sin_t = jnp.concatenate([-sin, sin, -sin, sin], axis=-1)
    return cos_t, sin_t


def _nsa_prep_kernel(nq_ref, nkv_ref, cos_ref, sin_ref,
                     qc_ref, qr_ref, slc_ref, win_ref, ks_ref, vs_ref, kw_ref, vw_ref):
    cos = cos_ref[...]
    sin = sin_ref[...]
    lane = lax.broadcasted_iota(jnp.int32, cos.shape, 1)
    first_half = (lane % NSA_DH) < (NSA_DH // 2)

    def rope(x):
        swapped = jnp.where(first_half, pltpu.roll(x, LANES - NSA_DH // 2, 1),
                            pltpu.roll(x, NSA_DH // 2, 1))
        return x * cos + swapped * sin

    scale = NSA_DH ** -0.5
    for j in range(NSA_HEADS * NSA_DH // LANES):
        x = nq_ref[:, j * LANES:(j + 1) * LANES]
        qc_ref[:, j * LANES:(j + 1) * LANES] = x * scale
        qr_ref[:, j * LANES:(j + 1) * LANES] = rope(x) * scale

    kv_w = NSA_KV * NSA_DH
    for br, (row_ref, k_ref, v_ref) in enumerate(((slc_ref, ks_ref, vs_ref),
                                                   (win_ref, kw_ref, vw_ref))):
        base = (br + 1) * ROW_W
        kr = rope(nkv_ref[:, base:base + kv_w])
        v = nkv_ref[:, base + kv_w:base + 2 * kv_w]
        row_ref[:, 0:kv_w] = kr
        row_ref[:, kv_w:2 * kv_w] = v
        for g in range(NSA_KV):
            k_ref[0, g] = kr[:, g * NSA_DH:(g + 1) * NSA_DH].astype(k_ref.dtype)
            v_ref[0, g] = v[:, g * NSA_DH:(g + 1) * NSA_DH].astype(v_ref.dtype)


def _nsa_prep(z, pos, batch, t, tm, kv_dtype):
    cos_t, sin_t = _rope_tables(pos)
    nt = t // tm
    rowblk = lambda b, j: b * nt + j
    kv_shape = jax.ShapeDtypeStruct((batch, NSA_KV, t, NSA_DH), kv_dtype)
    kv_spec = pl.BlockSpec((1, NSA_KV, tm, NSA_DH), lambda b, j: (b, 0, j, 0))
    qw = NSA_HEADS * NSA_DH
    return pl.pallas_call(
        _nsa_prep_kernel,
        grid=(batch, nt),
        in_specs=[pl.BlockSpec((tm, qw), lambda b, j: (rowblk(b, j), Z_NQ // qw)),
                  pl.BlockSpec((tm, NKV_W), lambda b, j: (rowblk(b, j), Z_NKV // NKV_W)),
                  pl.BlockSpec((tm, LANES), lambda b, j: (j, 0)),
                  pl.BlockSpec((tm, LANES), lambda b, j: (j, 0))],
        out_specs=[pl.BlockSpec((tm, qw), lambda b, j: (rowblk(b, j), 0)),
                   pl.BlockSpec((tm, qw), lambda b, j: (rowblk(b, j), 0)),
                   pl.BlockSpec((tm, ROW_W), lambda b, j: (rowblk(b, j), 0)),
                   pl.BlockSpec((tm, ROW_W), lambda b, j: (rowblk(b, j), 0)),
                   kv_spec, kv_spec, kv_spec, kv_spec],
        out_shape=[jax.ShapeDtypeStruct((batch * t, qw), F32),
                   jax.ShapeDtypeStruct((batch * t, qw), F32),
                   jax.ShapeDtypeStruct((batch * t, ROW_W), F32),
                   jax.ShapeDtypeStruct((batch * t, ROW_W), F32),
                   kv_shape, kv_shape, kv_shape, kv_shape],
        compiler_params=_params(("parallel", "parallel")),
    )(z, z, cos_t, sin_t)


def _compress_weights(pe, w1, w2):
    eye = jnp.eye(2, dtype=F32)
    w1r = w1.reshape(2, NSA_CMP, NSA_DH, NSA_DH)
    w1big = jnp.einsum('srde,st,gh->rsgdthe', w1r, eye, eye).reshape(NSA_CMP * ROW_W, ROW_W)
    w2big = jnp.einsum('sed,st,gh->sgethd', w2, eye, eye).reshape(ROW_W, ROW_W)
    pe_big = jnp.broadcast_to(jnp.transpose(pe, (1, 0, 2))[:, :, None, :],
                              (NSA_CMP, 2, NSA_KV, NSA_DH)).reshape(1, NSA_CMP * ROW_W)
    return pe_big, w1big.astype(BF16), w2big.astype(BF16)


def _compress_kernel(x_ref, pe_ref, w1_ref, w2_ref, o_ref):
    x = (x_ref[...] + pe_ref[...]).astype(BF16)
    hid = _gelu(jnp.dot(x, w1_ref[...], preferred_element_type=F32))
    o_ref[...] = jnp.dot(hid.astype(BF16), w2_ref[...], preferred_element_type=F32)


def _compress(rows, pe_big, w1big, w2big, tm):
    n, kdim = rows.shape
    return pl.pallas_call(
        _compress_kernel,
        grid=(n // tm,),
        in_specs=[pl.BlockSpec((tm, kdim), lambda i: (i, 0)),
                  pl.BlockSpec((1, kdim), lambda i: (0, 0)),
                  pl.BlockSpec((kdim, ROW_W), lambda i: (0, 0)),
                  pl.BlockSpec((ROW_W, ROW_W), lambda i: (0, 0))],
        out_specs=pl.BlockSpec((tm, ROW_W), lambda i: (i, 0)),
        out_shape=jax.ShapeDtypeStruct((n, ROW_W), F32),
        compiler_params=_params(("parallel",)),
    )(rows, pe_big, w1big, w2big)


def _split_compressed(cmp_out, batch):
    nb = cmp_out.shape[0] // batch
    x = cmp_out.reshape(batch, nb // 2, 2, 2, NSA_KV, NSA_DH)
    x = jnp.transpose(x, (3, 0, 4, 2, 1, 5)).reshape(2, batch, NSA_KV, nb, NSA_DH)
    return x[0].astype(BF16), x[1].astype(BF16)


CMP_PAGES_PER_STEP = 64
SLC_PAGES_PER_STEP = 16
BLOCKS_PER_PAGE = PAGE_SIZE // NSA_CMP


def _page_specs(block, n_pages, per_step):
    def spec(j):
        return pl.BlockSpec(block, lambda b, s, pt: (pt[b * n_pages + s * per_step + j], 0, 0))
    return [spec(j) for j in range(per_step)]


CMP_ROW_PITCH = NSA_CMP + 4


def _paged_compress_kernel(pt_ref, *refs):
    n_in = CMP_PAGES_PER_STEP
    pe_ref, w1_ref, w2_ref, o_ref, x_ref = refs[n_in:]
    m = n_in * BLOCKS_PER_PAGE
    n_slabs = ROW_W // LANES
    for j, p_ref in enumerate(refs[:n_in]):
        x = p_ref[0].T
        for n in range(BLOCKS_PER_PAGE):
            base = (j * BLOCKS_PER_PAGE + n) * CMP_ROW_PITCH
            for sl in range(n_slabs):
                x_ref[sl, base:base + NSA_CMP, :] = x[n * NSA_CMP:(n + 1) * NSA_CMP,
                                                      sl * LANES:(sl + 1) * LANES]
    acc = jnp.zeros((m, ROW_W), F32)
    for r in range(NSA_CMP):
        lhs = jnp.concatenate([x_ref[sl, pl.ds(r, m, stride=CMP_ROW_PITCH), :]
                               for sl in range(n_slabs)], axis=1)
        lhs = (lhs + pe_ref[r:r + 1, :]).astype(BF16)
        acc = acc + jnp.dot(lhs, w1_ref[r], preferred_element_type=F32)
    o_ref[...] = jnp.dot(_gelu(acc).astype(BF16), w2_ref[...], preferred_element_type=F32)


def _paged_compress(cache_t, page_table, pe_big, w1big, w2big):
    batch, n_pages = page_table.shape
    rows = CMP_PAGES_PER_STEP * BLOCKS_PER_PAGE
    steps = n_pages // CMP_PAGES_PER_STEP
    pe_rows = pe_big.reshape(NSA_CMP, ROW_W)
    w1_rows = w1big.reshape(NSA_CMP, ROW_W, ROW_W)
    const = lambda a: pl.BlockSpec(a.shape, lambda b, s, pt: (0,) * a.ndim)
    return pl.pallas_call(
        _paged_compress_kernel,
        grid_spec=pltpu.PrefetchScalarGridSpec(
            num_scalar_prefetch=1, grid=(batch, steps),
            in_specs=_page_specs((1, ROW_W, PAGE_SIZE), n_pages, CMP_PAGES_PER_STEP)
            + [const(pe_rows), const(w1_rows), const(w2big)],
            out_specs=pl.BlockSpec((rows, ROW_W), lambda b, s, pt: (b * steps + s, 0)),
            scratch_shapes=[pltpu.VMEM((ROW_W // LANES, rows * CMP_ROW_PITCH, LANES), F32)]),
        out_shape=jax.ShapeDtypeStruct((batch * n_pages * BLOCKS_PER_PAGE, ROW_W), F32),
        compiler_params=_params(("parallel", "parallel")),
    )(page_table.reshape(-1), *([cache_t] * CMP_PAGES_PER_STEP), pe_rows, w1_rows, w2big)


def _topk_mask(score, k):
    n = score.shape[-1]
    lane = lax.broadcasted_iota(jnp.int32, score.shape, 1).astype(F32)
    sel = jnp.zeros(score.shape, F32)
    for _ in range(k):
        m = jnp.max(score, axis=-1, keepdims=True)
        idx = jnp.min(jnp.where(score == m, lane, float(n)), axis=-1, keepdims=True)
        pick = lane == idx
        sel = jnp.where(pick, 1.0, sel)
        score = jnp.where(pick, -jnp.inf, score)
    return sel


def _softmax_step(carry, s, v):
    m, l, acc = carry
    m_new = jnp.maximum(m, jnp.max(s, axis=-1, keepdims=True))
    m_safe = jnp.where(m_new == -jnp.inf, 0.0, m_new)
    p = jnp.exp(s - m_safe)
    alpha = jnp.exp(m - m_safe)
    l = alpha * l + jnp.sum(p, axis=-1, keepdims=True)
    acc = alpha * acc + _dot(p, v)
    return m_new, l, acc


MASKED = -1e30


def _topk_mask_rows(score, k):
    n = score.shape[0]
    row = lax.broadcasted_iota(jnp.int32, score.shape, 0).astype(F32)
    sel = jnp.zeros(score.shape, F32)
    for _ in range(k):
        m = jnp.max(score, axis=0, keepdims=True)
        idx = jnp.min(jnp.where(score == m, row, float(n)), axis=0, keepdims=True)
        pick = row == idx
        sel = jnp.where(pick, 1.0, sel)
        score = jnp.where(pick, -jnp.inf, score)
    return sel


def _softmax_step_cols(carry, s, vt):
    m, l, acc = carry
    m_new = jnp.maximum(m, jnp.max(s, axis=0, keepdims=True))
    p = jnp.exp(s - m_new)
    alpha = jnp.exp(m - m_new)
    l = alpha * l + jnp.sum(p, axis=0, keepdims=True)
    acc = alpha * acc + _dot(vt, p)
    return m_new, l, acc


def _nsa_prompt_kernel(qc_ref, qr_ref, kc_ref, vct_ref, ks_ref, vst_ref, kw_ref, vwt_ref, sm_ref,
                       o_ref, bias_ref, *, tq, tk, tkw):
    hp = NSA_HEADS // NSA_KV
    r = hp * tq
    g = pl.program_id(1)
    qi = pl.program_id(2)
    q0 = qi * tq
    ncp = kc_ref.shape[2]
    nch = ncp // 2
    nsl = bias_ref.shape[0]
    per_tile = tk // NSA_SEL

    def heads_on_lanes(ref):
        xt = ref[...].T
        return jnp.concatenate([xt[h * NSA_DH:(h + 1) * NSA_DH] for h in range(hp)],
                               axis=1).astype(BF16)

    qct = heads_on_lanes(qc_ref)
    qrt = heads_on_lanes(qr_ref)
    t_q = q0 + lax.broadcasted_iota(jnp.int32, (1, tq), 1)
    t_lane = jnp.concatenate([t_q] * hp, axis=1)

    s = _dot(kc_ref[0, 0], qct)
    crow = lax.broadcasted_iota(jnp.int32, (ncp, 1), 0)
    cblk = 2 * (crow % nch) + crow // nch
    s = jnp.where((cblk + 1) * NSA_CMP <= t_lane + 1, s, -jnp.inf)
    m = jnp.max(s, axis=0, keepdims=True)
    e = jnp.exp(s - jnp.where(m == -jnp.inf, 0.0, m))
    p_cmp = e / jnp.maximum(jnp.sum(e, axis=0, keepdims=True), 1e-30)
    o_cmp = _dot(vct_ref[0, 0], p_cmp)

    n_keys = kw_ref.shape[2]
    wlen = min(NSA_WIN + tq, n_keys)
    wstart = pl.multiple_of(jnp.minimum(jnp.maximum(q0 - NSA_WIN, 0), n_keys - wlen), tkw)
    sc = _dot(kw_ref[0, 0, pl.ds(wstart, wlen), :], qrt)
    rel = t_lane - (wstart + lax.broadcasted_iota(jnp.int32, (wlen, 1), 0))
    sc = jnp.where((rel >= 0) & (rel < NSA_WIN), sc, MASKED)
    p_win = jnp.exp(sc - jnp.max(sc, axis=0, keepdims=True))
    o_win = (_dot(vwt_ref[0, 0, :, pl.ds(wstart, wlen)], p_win)
             / jnp.sum(p_win, axis=0, keepdims=True))

    imp = p_cmp[:, 0:tq]
    for h in range(1, hp):
        imp = imp + p_cmp[:, h * tq:(h + 1) * tq]
    imp = imp[:nch] + imp[nch:]
    blk = lax.broadcasted_iota(jnp.int32, (nsl, tq), 0)
    cur = t_q // NSA_SEL
    forced = (blk == 0) | (blk == cur) | (blk == cur - 1)
    score = jnp.where(blk > cur, -jnp.inf, jnp.where(forced, NSA_FORCE, imp))
    bias = (_topk_mask_rows(score, NSA_TOPN) - 1.0) * (-MASKED)
    bias_ref[...] = jnp.concatenate([bias] * hp, axis=1)

    init = (jnp.full((1, r), MASKED, F32), jnp.zeros((1, r), F32), jnp.zeros((NSA_DH, r), F32))

    def slc_scores(kt):
        start = pl.multiple_of(kt * tk, tk)
        sc = _dot(ks_ref[0, 0, pl.ds(start, tk), :], qrt)
        brow = bias_ref[pl.ds(pl.multiple_of(kt * per_tile, per_tile), per_tile), :]
        sc = jnp.concatenate([sc[j * NSA_SEL:(j + 1) * NSA_SEL] + brow[j:j + 1]
                              for j in range(per_tile)], axis=0)
        return sc, vst_ref[0, 0, :, pl.ds(start, tk)]

    def slc_body(kt, carry):
        sc, vt = slc_scores(kt)
        return _softmax_step_cols(carry, sc, vt)

    kd = q0 // tk
    carry = lax.fori_loop(0, kd, slc_body, init)
    sc, vt = slc_scores(kd)
    kpos = kd * tk + lax.broadcasted_iota(jnp.int32, (tk, 1), 0)
    carry = _softmax_step_cols(carry, jnp.where(kpos <= t_lane, sc, MASKED), vt)
    o_slc = carry[2] / carry[1]

    sig = jax.nn.sigmoid(sm_ref[...].T)
    gw = N_BRANCH * hp
    gates = jnp.where(g == 0, sig[SM_NG:SM_NG + gw], sig[SM_NG + gw:SM_NG + 2 * gw])
    outs = []
    for h in range(hp):
        cols = slice(h * tq, (h + 1) * tq)
        outs.append(gates[3 * h:3 * h + 1] * o_cmp[:, cols]
                    + gates[3 * h + 1:3 * h + 2] * o_slc[:, cols]
                    + gates[3 * h + 2:3 * h + 3] * o_win[:, cols])
    o_ref[...] = jnp.concatenate(outs, axis=0).T


def _nsa_prompt_attn(qc, qr, kc, vct, ks, vst, kw, vwt, z, batch, t, tq, tk, tkw):
    nq = t // tq
    gw = NSA_HEADS * NSA_DH // NSA_KV
    assert tk % (SUBLANES * NSA_SEL) == 0 and tq == tkw and tk % tq == 0 and t % tk == 0
    rowblk = lambda b, g, i: b * nq + i
    full = lambda a: pl.BlockSpec((1, 1) + a.shape[2:], lambda b, g, i: (b, g, 0, 0))
    q_spec = pl.BlockSpec((tq, gw), lambda b, g, i: (rowblk(b, g, i), g))
    return pl.pallas_call(
        functools.partial(_nsa_prompt_kernel, tq=tq, tk=tk, tkw=tkw),
        grid=(batch, NSA_KV, nq),
        in_specs=[q_spec, q_spec] + [full(a) for a in (kc, vct, ks, vst, kw, vwt)]
        + [pl.BlockSpec((tq, LANES), lambda b, g, i: (rowblk(b, g, i), Z_SMALL // LANES))],
        out_specs=pl.BlockSpec((tq, gw), lambda b, g, i: (rowblk(b, g, i), g)),
        out_shape=jax.ShapeDtypeStruct((batch * t, NSA_HEADS * NSA_DH), F32),
        scratch_shapes=[pltpu.VMEM((t // NSA_SEL, NSA_HEADS // NSA_KV * tq), F32)],
        compiler_params=_params(("parallel", "parallel", "arbitrary")),
    )(qc, qr, kc, vct, ks, vst, kw, vwt, z)


def _softmax_step_vt(carry, s, vt):
    m, l, acc = carry
    m_new = jnp.maximum(m, jnp.max(s, axis=-1, keepdims=True))
    m_safe = jnp.where(m_new == -jnp.inf, 0.0, m_new)
    p = jnp.exp(s - m_safe)
    alpha = jnp.exp(m - m_safe)
    l = alpha * l + jnp.sum(p, axis=-1, keepdims=True)
    acc = alpha * acc + _dot_nt(p, vt)
    return m_new, l, acc


def _nsa_decode_kernel(pt_ref, *refs, tq, pos0, wrel0, n_keys):
    n_in = SLC_PAGES_PER_STEP
    qc_ref, qr_ref, kc_ref, vc_ref = refs[0:4]
    pages = refs[4:4 + n_in]
    (kw_ref, vw_ref, kt_ref, vt_ref, sm_ref, o_ref,
     m_ref, l_ref, acc_ref, ocmp_ref, sel_ref) = refs[4 + n_in:]
    hp = NSA_HEADS // NSA_KV
    r = hp * tq
    gwid = hp * NSA_DH
    step = pl.program_id(1)
    n_steps = pl.num_programs(1)
    tk = n_in * PAGE_SIZE
    per_tile = tk // NSA_SEL
    n_tiles = sel_ref.shape[1]
    ncp = kc_ref.shape[2]
    nch = ncp // 2
    nsl = -(-(n_tiles * per_tile) // LANES) * LANES
    kv_w = NSA_KV * NSA_DH

    def stack_heads(ref, g):
        x = ref[:, g * gwid:(g + 1) * gwid]
        return jnp.concatenate([x[:, h * NSA_DH:(h + 1) * NSA_DH] for h in range(hp)],
                               axis=0).astype(BF16)

    t_q = pos0 + lax.broadcasted_iota(jnp.int32, (tq, 1), 0)
    t_row = jnp.concatenate([t_q] * hp, axis=0)
    expand = (lax.broadcasted_iota(jnp.int32, (per_tile, tk), 0)
              == lax.broadcasted_iota(jnp.int32, (per_tile, tk), 1) // NSA_SEL).astype(BF16)

    def picked_rows(g, tile, width):
        pk = jnp.dot(sel_ref[g, tile].astype(BF16), expand[:, :width], preferred_element_type=F32)
        return jnp.concatenate([pk] * hp, axis=0)

    @pl.when(step == 0)
    def _():
        ccol = lax.broadcasted_iota(jnp.int32, (r, ncp), 1)
        cvis = (2 * (ccol % nch) + ccol // nch + 1) * NSA_CMP <= t_row + 1
        blk = lax.broadcasted_iota(jnp.int32, (tq, nsl), 1)
        cur = t_q // NSA_SEL
        forced = (blk == 0) | (blk == cur) | (blk == cur - 1)
        group_scores = []
        for g in range(NSA_KV):
            s = jnp.where(cvis, _dot_nt(stack_heads(qc_ref, g), kc_ref[0, g]), -jnp.inf)
            m = jnp.max(s, axis=-1, keepdims=True)
            e = jnp.exp(s - jnp.where(m == -jnp.inf, 0.0, m))
            p_cmp = e / jnp.maximum(jnp.sum(e, axis=-1, keepdims=True), 1e-30)
            ocmp_ref[g] = _dot(p_cmp, vc_ref[0, g])
            imp = p_cmp[0:tq]
            for h in range(1, hp):
                imp = imp + p_cmp[h * tq:(h + 1) * tq]
            imp = imp[:, :nch] + imp[:, nch:]
            if nsl > nch:
                imp = jnp.concatenate([imp, jnp.zeros((tq, nsl - nch), F32)], axis=1)
            group_scores.append(jnp.where(blk > cur, -jnp.inf, jnp.where(forced, NSA_FORCE, imp)))
        sel_all = _topk_mask(jnp.concatenate(group_scores, axis=0), NSA_TOPN)
        for g in range(NSA_KV):
            sel = sel_all[g * tq:(g + 1) * tq]
            for j in range(n_tiles):
                sel_ref[g, j] = sel[:, j * per_tile:(j + 1) * per_tile]
            m_ref[g] = jnp.full((r, 1), -jnp.inf, F32)
            l_ref[g] = jnp.zeros((r, 1), F32)
            acc_ref[g] = jnp.zeros((r, NSA_DH), F32)

    kpos = step * tk + lax.broadcasted_iota(jnp.int32, (1, tk), 1)
    scores, values = [], []
    for g in range(NSA_KV):
        k_t = jnp.concatenate([p[0, g * NSA_DH:(g + 1) * NSA_DH, :] for p in pages], axis=1)
        values.append(jnp.concatenate(
            [p[0, kv_w + g * NSA_DH:kv_w + (g + 1) * NSA_DH, :] for p in pages], axis=1))
        scores.append(_dot(stack_heads(qr_ref, g), k_t))
    picked = [picked_rows(g, step, tk) for g in range(NSA_KV)]
    for g in range(NSA_KV):
        ok = (picked[g] > 0.5) & (kpos <= t_row)
        m, l, acc = _softmax_step_vt((m_ref[g], l_ref[g], acc_ref[g]),
                                     jnp.where(ok, scores[g], -jnp.inf), values[g])
        m_ref[g] = m
        l_ref[g] = l
        acc_ref[g] = acc

    @pl.when(step == n_steps - 1)
    def _():
        sig = jax.nn.sigmoid(sm_ref[...])
        tw = kt_ref.shape[2]
        tpos = n_keys + lax.broadcasted_iota(jnp.int32, (1, tw), 1)
        wpos = pos0 - wrel0 + lax.broadcasted_iota(jnp.int32, (1, kw_ref.shape[2]), 1)
        rel = t_row - wpos
        wok = (rel >= 0) & (rel < NSA_WIN) & (wpos >= 0)
        qr4s = [stack_heads(qr_ref, g) for g in range(NSA_KV)]
        tail_s = [_dot_nt(qr4s[g], kt_ref[0, g]) for g in range(NSA_KV)]
        win_s = [_dot_nt(qr4s[g], kw_ref[0, g]) for g in range(NSA_KV)]
        for g in range(NSA_KV):
            ok = (picked_rows(g, n_tiles - 1, tw) > 0.5) & (tpos <= t_row)
            carry = _softmax_step((m_ref[g], l_ref[g], acc_ref[g]),
                                  jnp.where(ok, tail_s[g], -jnp.inf), vt_ref[0, g])
            o_slc = carry[2] / jnp.maximum(carry[1], 1e-30)
            s = jnp.where(wok, win_s[g], -jnp.inf)
            m = jnp.max(s, axis=-1, keepdims=True)
            e = jnp.exp(s - jnp.where(m == -jnp.inf, 0.0, m))
            o_win = _dot(e, vw_ref[0, g]) / jnp.maximum(jnp.sum(e, axis=-1, keepdims=True), 1e-30)
            o_cmp = ocmp_ref[g]
            base = SM_NG + g * N_BRANCH * hp
            outs = []
            for h in range(hp):
                rows = slice(h * tq, (h + 1) * tq)
                c0 = base + N_BRANCH * h
                outs.append(sig[:, c0:c0 + 1] * o_cmp[rows] + sig[:, c0 + 1:c0 + 2] * o_slc[rows]
                            + sig[:, c0 + 2:c0 + 3] * o_win[rows])
            o_ref[:, g * gwid:(g + 1) * gwid] = jnp.concatenate(outs, axis=1)


def _nsa_decode_attn(qc, qr, kc, vc, cache_t, page_table, kw, vw, ktail, vtail, z, tq, pos0, wrel0):
    batch, n_pages = page_table.shape
    n_keys = n_pages * PAGE_SIZE
    steps = n_pages // SLC_PAGES_PER_STEP
    hp = NSA_HEADS // NSA_KV
    r = hp * tq
    qw = NSA_HEADS * NSA_DH
    per_tile = SLC_PAGES_PER_STEP * PAGE_SIZE // NSA_SEL
    assert pos0 == n_keys and ktail.shape[2] <= per_tile * NSA_SEL
    per_b = lambda a: pl.BlockSpec((1,) + a.shape[1:], lambda b, s, pt: (b, 0, 0, 0))
    q_spec = pl.BlockSpec((tq, qw), lambda b, s, pt: (b, 0))
    return pl.pallas_call(
        functools.partial(_nsa_decode_kernel, tq=tq, pos0=pos0, wrel0=wrel0, n_keys=n_keys),
        grid_spec=pltpu.PrefetchScalarGridSpec(
            num_scalar_prefetch=1, grid=(batch, steps),
            in_specs=[q_spec, q_spec, per_b(kc), per_b(vc)]
            + _page_specs((1, ROW_W, PAGE_SIZE), n_pages, SLC_PAGES_PER_STEP)
            + [per_b(kw), per_b(vw), per_b(ktail), per_b(vtail),
               pl.BlockSpec((tq, LANES), lambda b, s, pt: (b, Z_SMALL // LANES))],
            out_specs=pl.BlockSpec((tq, qw), lambda b, s, pt: (b, 0)),
            scratch_shapes=[pltpu.VMEM((NSA_KV, r, 1), F32), pltpu.VMEM((NSA_KV, r, 1), F32),
                            pltpu.VMEM((NSA_KV, r, NSA_DH), F32),
                            pltpu.VMEM((NSA_KV, r, NSA_DH), F32),
                            pltpu.VMEM((NSA_KV, steps + 1, tq, per_tile), F32)]),
        out_shape=jax.ShapeDtypeStruct((batch * tq, qw), F32),
        compiler_params=_params(("parallel", "arbitrary")),
    )(page_table.reshape(-1), qc, qr, kc, vc, *([cache_t] * SLC_PAGES_PER_STEP),
      kw, vw, ktail, vtail, z)


def _mem_attn_kernel(q_ref, kv_ref, o_ref):
    hw = MEM_HEADS * MEM_DH
    for h in range(MEM_HEADS):
        q = q_ref[:, h * MEM_DH:(h + 1) * MEM_DH] * (MEM_DH ** -0.5)
        k = kv_ref[0, :, h * MEM_DH:(h + 1) * MEM_DH]
        v = kv_ref[0, :, hw + h * MEM_DH:hw + (h + 1) * MEM_DH]
        s = _dot_nt(q, k)
        e = jnp.exp(s - jnp.max(s, axis=-1, keepdims=True))
        p = e / jnp.sum(e, axis=-1, keepdims=True)
        o_ref[:, h * MEM_DH:(h + 1) * MEM_DH] = _dot(p, v)


def _mem_attn(z, kv, batch, t, tm):
    nt = t // tm
    hw = MEM_HEADS * MEM_DH
    return pl.pallas_call(
        _mem_attn_kernel,
        grid=(batch, nt),
        in_specs=[pl.BlockSpec((tm, hw), lambda b, j: (b * nt + j, Z_MQ // hw)),
                  pl.BlockSpec((1,) + kv.shape[1:], lambda b, j: (b, 0, 0))],
        out_specs=pl.BlockSpec((tm, hw), lambda b, j: (b * nt + j, 0)),
        out_shape=jax.ShapeDtypeStruct((batch * t, hw), F32),
        compiler_params=_params(("parallel", "parallel")),
    )(z, kv)


def _merge_kernel(x_ref, mg_ref, og_ref, on_ref, om_ref, wg_ref, wn_ref, wm_ref, wo_ref,
                  gf_ref, wq_ref, x1_ref, h2_ref, qp_ref):
    d = D_MODEL
    mix = (jax.nn.sigmoid(mg_ref[:, 0:d]) * _dot(og_ref[...], wg_ref[...])
           + jax.nn.sigmoid(mg_ref[:, d:2 * d]) * _dot(on_ref[...], wn_ref[...])
           + jax.nn.sigmoid(mg_ref[:, 2 * d:3 * d]) * _dot(om_ref[...], wm_ref[...]))
    x1 = x_ref[...] + _dot(mix, wo_ref[...])
    x1_ref[...] = x1
    h2 = _rms(x1, gf_ref[...]).astype(BF16)
    h2_ref[...] = h2
    qp_ref[...] = jnp.dot(h2, wq_ref[...], preferred_element_type=F32)


def _merge(x, z, o_gdn, o_nsa, o_mem, wg, wn, wm, wo, norm_ffn, wq, tm):
    n, d = x.shape
    qw = wq.shape[1]
    row = lambda w: pl.BlockSpec((tm, w), lambda i: (i, 0))
    const = lambda a: pl.BlockSpec(a.shape, lambda i: (0, 0))
    return pl.pallas_call(
        _merge_kernel,
        grid=(n // tm,),
        in_specs=[row(d), pl.BlockSpec((tm, N_BRANCH * d), lambda i: (i, Z_MG)),
                  row(o_gdn.shape[1]), row(o_nsa.shape[1]), row(o_mem.shape[1]),
                  const(wg), const(wn), const(wm), const(wo),
                  pl.BlockSpec((1, d), lambda i: (0, 0)), const(wq)],
        out_specs=[row(d), row(d), row(qw)],
        out_shape=[jax.ShapeDtypeStruct((n, d), F32), jax.ShapeDtypeStruct((n, d), BF16),
                   jax.ShapeDtypeStruct((n, qw), F32)],
        compiler_params=_params(("parallel",)),
    )(x, z, o_gdn, o_nsa, o_mem, wg, wn, wm, wo, norm_ffn.reshape(1, d), wq)


PEER_RANKS = PEER_TOPK + 1


RANK_NONE = 64.0


def _top_values(s, n):
    vals = []
    rank = jnp.full(s.shape, RANK_NONE, F32)
    for k in range(n):
        m = jnp.max(s, axis=0, keepdims=True)
        vals.append(m)
        hit = s >= m
        rank = jnp.where(hit, float(k + 1), rank)
        s = jnp.where(hit, -jnp.inf, s)
    return vals, rank


PEER_PAIRS = [(i, j) for i in range(1, PEER_RANKS + 1) for j in range(1, PEER_RANKS // i + 1)]


def _pair_selectors():
    rows = -(-len(PEER_PAIRS) // SUBLANES) * SUBLANES
    cols = -(-PEER_RANKS // SUBLANES) * SUBLANES
    pa = np.zeros((rows, cols), np.float32)
    pb = np.zeros((rows, cols), np.float32)
    for p, (i, j) in enumerate(PEER_PAIRS):
        pa[p, i - 1] = 1.0
        pb[p, j - 1] = 1.0
    return jnp.asarray(pa), jnp.asarray(pb)


def _peer_route_kernel(qp_ref, sk_ref, pa_ref, pb_ref, r2_ref, e2_ref, nb_ref, e1_ref):
    half = PEER_DKEY // 2
    nt = (((1,), (1,)), ((), ()))
    for h in range(PEER_HEADS):
        qa = qp_ref[:, h * PEER_DKEY:h * PEER_DKEY + half]
        qb = qp_ref[:, h * PEER_DKEY + half:(h + 1) * PEER_DKEY]
        s1 = lax.dot_general(sk_ref[0], qa, nt, precision=HIGHEST, preferred_element_type=F32)
        s2 = lax.dot_general(sk_ref[1], qb, nt, precision=HIGHEST, preferred_element_type=F32)
        a, _ = _top_values(s1, PEER_RANKS)
        b, rank2 = _top_values(s2, PEER_RANKS)
        pad = jnp.full((pa_ref.shape[1] - PEER_RANKS, s1.shape[1]), MASKED, F32)
        a_rows = jnp.maximum(jnp.concatenate(a + [pad], axis=0), MASKED)
        b_rows = jnp.maximum(jnp.concatenate(b + [pad], axis=0), MASKED)
        cand = _dot_hi(pa_ref[...], a_rows) + _dot_hi(pb_ref[...], b_rows)
        prow = lax.broadcasted_iota(jnp.int32, cand.shape, 0)
        cand = jnp.where(prow < len(PEER_PAIRS), cand, -jnp.inf)
        work = cand
        ranked = []
        for _ in range(PEER_RANKS):
            m = jnp.max(work, axis=0, keepdims=True)
            ranked.append(m)
            work = jnp.where(work >= m, -jnp.inf, work)
        tau = 0.5 * (ranked[PEER_TOPK - 1] + ranked[PEER_TOPK])
        top = a[0] + b[0]
        zsum = jnp.sum(jnp.where(cand >= tau, jnp.exp(cand - top), 0.0), axis=0, keepdims=True)
        th = tau - s1
        count = jnp.zeros(s1.shape, F32)
        for bj in b:
            count = count + jnp.where(bj >= th, 1.0, 0.0)
        r2_ref[h] = rank2.astype(BF16)
        e2_ref[h] = jnp.exp(s2 - b[0]).astype(BF16)
        nb_ref[h] = count
        e1_ref[h] = jnp.exp(s1 - a[0]) / zsum


def _peer_route(qp, subkeys, tt):
    n = qp.shape[0]
    pa, pb = _pair_selectors()
    shape = lambda dt: jax.ShapeDtypeStruct((PEER_HEADS, PEER_NKEYS, n), dt)
    spec = pl.BlockSpec((PEER_HEADS, PEER_NKEYS, tt), lambda i: (0, 0, i))
    return pl.pallas_call(
        _peer_route_kernel,
        grid=(n // tt,),
        in_specs=[pl.BlockSpec((tt, qp.shape[1]), lambda i: (i, 0)),
                  pl.BlockSpec(subkeys.shape, lambda i: (0, 0, 0)),
                  pl.BlockSpec(pa.shape, lambda i: (0, 0)),
                  pl.BlockSpec(pb.shape, lambda i: (0, 0))],
        out_specs=[spec, spec, spec, spec],
        out_shape=[shape(BF16), shape(BF16), shape(F32), shape(F32)],
        compiler_params=_params(("parallel",)),
    )(qp, subkeys, pa, pb)


PEER_SUB_BLOCKS = 4


def _peer_dense_kernel(ht_ref, u_ref, vt_ref, r2_ref, e2_ref, nb_ref, e1_ref, x1_ref, gf_ref,
                       y_ref, acc_ref, act_ref, wa_ref, r2s_ref, e2s_ref, rows_ref, *, jb):
    j = pl.program_id(1)
    tt = ht_ref.shape[1]

    @pl.when(j == 0)
    def _():
        acc_ref[...] = jnp.zeros(acc_ref.shape, F32)
        r2s_ref[...] = r2_ref[...]
        e2s_ref[...] = e2_ref[...]

    sub = jb // PEER_SUB_BLOCKS
    for sb in range(PEER_SUB_BLOCKS):
        srows = slice(sb * sub * PEER_NKEYS, (sb + 1) * sub * PEER_NKEYS)
        act_ref[srows, :] = _gelu(jnp.dot(u_ref[srows, :], ht_ref[...],
                                          preferred_element_type=F32)).astype(BF16)
    for jj in range(jb):
        i1 = j * jb + jj
        rows = slice(jj * PEER_NKEYS, (jj + 1) * PEER_NKEYS)
        for h in range(PEER_HEADS):
            rows_ref[jj, h:h + 1, :] = nb_ref[h, pl.ds(i1, 1), :]
            rows_ref[jj, PEER_HEADS + h:PEER_HEADS + h + 1, :] = e1_ref[h, pl.ds(i1, 1), :]
        for c in range(tt // LANES):
            cols = slice(c * LANES, (c + 1) * LANES)
            w = None
            for h in range(PEER_HEADS):
                count = rows_ref[jj, h:h + 1, cols].astype(BF16)
                e1 = rows_ref[jj, PEER_HEADS + h:PEER_HEADS + h + 1, cols].astype(BF16)
                picked = r2s_ref[h, :, cols] <= count
                term = jnp.where(picked, e2s_ref[h, :, cols], jnp.zeros((), BF16)) * e1
                w = term if w is None else w + term
            wa_ref[rows, cols] = w * act_ref[rows, cols]
    acc_ref[...] += jnp.dot(vt_ref[...], wa_ref[...], preferred_element_type=F32)

    @pl.when(j == pl.num_programs(1) - 1)
    def _():
        y_ref[...] = _rms(x1_ref[...] + acc_ref[...].T, gf_ref[...])


def _peer_dense(ht, u, vt, s2, e2, th, e1, x1, norm_final, tt, jb):
    d, n = ht.shape
    n_exp = u.shape[0]
    eb = jb * PEER_NKEYS
    route = pl.BlockSpec((PEER_HEADS, PEER_NKEYS, tt), lambda t, j: (0, 0, t))
    return pl.pallas_call(
        functools.partial(_peer_dense_kernel, jb=jb),
        grid=(n // tt, n_exp // eb),
        in_specs=[pl.BlockSpec((d, tt), lambda t, j: (0, t)),
                  pl.BlockSpec((eb, d), lambda t, j: (j, 0)),
                  pl.BlockSpec((d, eb), lambda t, j: (0, j)),
                  route, route, route, route,
                  pl.BlockSpec((tt, d), lambda t, j: (t, 0)),
                  pl.BlockSpec((1, d), lambda t, j: (0, 0))],
        out_specs=pl.BlockSpec((tt, d), lambda t, j: (t, 0)),
        out_shape=jax.ShapeDtypeStruct((n, d), F32),
        scratch_shapes=[pltpu.VMEM((d, tt), F32), pltpu.VMEM((eb, tt), BF16),
                        pltpu.VMEM((eb, tt), BF16),
                        pltpu.VMEM((PEER_HEADS, PEER_NKEYS, tt), BF16),
                        pltpu.VMEM((PEER_HEADS, PEER_NKEYS, tt), BF16),
                        pltpu.VMEM((jb, 2 * PEER_HEADS, tt), F32)],
        compiler_params=_params(("parallel", "arbitrary")),
    )(ht, u, vt, s2, e2, th, e1, x1, norm_final.reshape(1, d))


def _permute_w_in(w_in):
    sizes = (GDN_CONV_CH, GDN_HEADS * GDN_DV, GDN_HEADS, GDN_HEADS, NSA_HEADS * NSA_DH, NKV_W,
             N_BRANCH * NSA_HEADS, MEM_HEADS * MEM_DH, N_BRANCH * D_MODEL)
    qkv, zg, a, b, nq, nkv, ng, mq, mg = jnp.split(w_in, np.cumsum(sizes)[:-1].tolist(), axis=1)
    pad = jnp.zeros((w_in.shape[0], LANES - a.shape[1] - b.shape[1] - ng.shape[1]), w_in.dtype)
    return jnp.concatenate([mg, qkv, zg, nq, mq, nkv, a, b, ng, pad], axis=1).astype(BF16)


def _tokens_tile(n, pref):
    return pref if n % pref == 0 else n


def _layer(x, pos0, kv_mem, nsa_keys, gdn_state, conv_buf, w, peer_tt):
    batch, t, d = x.shape
    n = batch * t
    xf = x.reshape(n, d)
    z = _norm_matmul(xf, w['norm_attn'], w['w_in'], _tokens_tile(n, 256), 640)

    o_gdn, s_new, conv_new = _gdn(z, conv_buf, gdn_state, w['gdn_conv'], w['gdn_a_log'],
                                  w['gdn_dt_bias'], w['gdn_norm'], batch, t)

    prep_tm = _tokens_tile(t, 512)
    kv_dtype = BF16 if prep_tm % 16 == 0 else F32
    qc, qr, slc_rows, win_rows, ks, vs, kw, vw = _nsa_prep(
        z, pos0 + jnp.arange(t), batch, t, prep_tm, kv_dtype)
    cmp_rows = z[:, Z_NKV:Z_NKV + ROW_W]
    o_nsa = nsa_keys(z, qc, qr, cmp_rows, ks, vs, kw, vw)

    o_mem = _mem_attn(z, kv_mem, batch, t, _tokens_tile(t, 512))

    x1, h2, qp = _merge(xf, z, o_gdn, o_nsa, o_mem, w['w_gdn_out'], w['w_nsa_out'],
                        w['w_mem_out'], w['w_o'], w['norm_ffn'], w['peer_wq'],
                        _tokens_tile(n, 256))
    s2, e2, th, e1 = _peer_route(qp, w['peer_subkeys'], 256)
    y = _peer_dense(h2.T, w['peer_u'], w['peer_vt'], s2, e2, th, e1, x1, w['norm_final'],
                    peer_tt, 8)
    row5 = lambda a: a.reshape(batch, t, 2, NSA_KV, NSA_DH)
    return (y.reshape(batch, t, d), row5(cmp_rows), row5(slc_rows), row5(win_rows), s_new,
            conv_new)


def kernel(x_prompt, x_sample, cache_mem_kv, cache_cmp_kv, cache_slc_kv, cache_win_kv, state_gdn, state_conv, page_table, mem_prompt, norm_attn, w_in, gdn_conv, gdn_a_log, gdn_dt_bias, gdn_norm, nsa_cmp_pe, nsa_cmp_w1, nsa_cmp_w2, norm_mem, w_mem_kv, w_gdn_out, w_nsa_out, w_mem_out, w_o, norm_ffn, peer_wq, peer_subkeys, peer_u, peer_v, norm_final):
    depth = w_in.shape[0]
    assert depth == 1
    l = 0
    bp, seq, d = x_prompt.shape
    bs, tdec, _ = x_sample.shape
    n_pages = page_table.shape[1]
    past = n_pages * PAGE_SIZE
    assert past % NSA_CMP == 0 and tdec < NSA_CMP and seq % LANES == 0

    pe_big, w1big, w2big = _compress_weights(nsa_cmp_pe[l], nsa_cmp_w1[l], nsa_cmp_w2[l])
    w = dict(norm_attn=norm_attn[l], w_in=_permute_w_in(w_in[l]), gdn_conv=gdn_conv[l],
             gdn_a_log=gdn_a_log[l], gdn_dt_bias=gdn_dt_bias[l], gdn_norm=gdn_norm[l],
             w_gdn_out=w_gdn_out[l].astype(BF16), w_nsa_out=w_nsa_out[l].astype(BF16),
             w_mem_out=w_mem_out[l].astype(BF16), w_o=w_o[l].astype(BF16), norm_ffn=norm_ffn[l],
             peer_wq=peer_wq[l].astype(BF16), peer_subkeys=peer_subkeys[l],
             peer_u=peer_u[l].astype(BF16), peer_vt=peer_v[l].astype(BF16).T,
             norm_final=norm_final)

    mem_n = mem_prompt.shape[0] * mem_prompt.shape[1]
    kvm = _norm_matmul(mem_prompt.reshape(mem_n, d), norm_mem[l], w_mem_kv[l].astype(BF16),
                       _tokens_tile(mem_n, 256), 512).reshape(bp, mem_prompt.shape[1], -1)

    def prompt_keys(z, qc, qr, cmp_rows, ks, vs, kw, vw):
        n_blk = bp * seq // NSA_CMP
        cmp_out = _compress(cmp_rows.reshape(n_blk, NSA_CMP * ROW_W), pe_big, w1big, w2big,
                            _tokens_tile(n_blk, 128))
        kc, vc = _split_compressed(cmp_out, bp)
        tr = lambda a: jnp.transpose(a, (0, 1, 3, 2))
        return _nsa_prompt_attn(qc, qr, kc, tr(vc), ks, tr(vs), kw, tr(vw), z, bp, seq,
                                128, _tokens_tile(seq, 512), 128)

    yp, cmp_p, slc_p, win_p, gdn_p, conv_p = _layer(
        x_prompt, 0, kvm, prompt_keys,
        jnp.zeros((bp, GDN_HEADS, GDN_DK, GDN_DV), F32),
        jnp.zeros((bp, GDN_CONV - 1, GDN_CONV_CH), F32), w, 512)
    win_len_p = min(NSA_WIN, seq)
    win_p = win_p[:, seq - win_len_p:]

    n_pool = cache_cmp_kv.shape[1]
    feature_major = lambda c: jnp.transpose(c, (0, 2, 3, 4, 1)).reshape(n_pool, ROW_W, PAGE_SIZE)
    cache_cmp = feature_major(cache_cmp_kv[l])
    cache_slc = feature_major(cache_slc_kv[l])
    cache_win = cache_win_kv[l].reshape(bs, -1, ROW_W)
    wb = cache_win.shape[1]
    assert wb == NSA_WIN
    kv_w = NSA_KV * NSA_DH

    def pad_rows(a, rows):
        return jnp.pad(a, ((0, 0), (0, 0), (0, rows - a.shape[2]), (0, 0))).astype(BF16)

    def split_rows(rows):
        r = rows.reshape(bs, rows.shape[1], 2, NSA_KV, NSA_DH)
        return jnp.transpose(r[:, :, 0], (0, 2, 1, 3)), jnp.transpose(r[:, :, 1], (0, 2, 1, 3))

    def sample_keys(z, qc, qr, cmp_rows, ks_new, vs_new, kw_new, vw_new):
        cmp_out = _paged_compress(cache_cmp, page_table, pe_big, w1big, w2big)
        kc, vc = _split_compressed(cmp_out, bs)
        kwc, vwc = split_rows(cache_win)
        win_rows = wb + LANES
        kw = pad_rows(jnp.concatenate([kwc, kw_new], axis=2), win_rows)
        vw = pad_rows(jnp.concatenate([vwc, vw_new], axis=2), win_rows)
        return _nsa_decode_attn(qc, qr, kc, vc, cache_slc, page_table, kw, vw,
                                pad_rows(ks_new, LANES), pad_rows(vs_new, LANES), z,
                                tdec, past, wb)

    ys, cmp_s, slc_s, win_new, gdn_s, conv_s = _layer(
        x_sample, past, cache_mem_kv[l].reshape(bs, cache_mem_kv.shape[2], -1), sample_keys,
        state_gdn[l], state_conv[l], w, 256)
    win_all = jnp.concatenate([cache_win_kv[l], win_new], axis=1)
    win_s = win_all[:, win_all.shape[1] - min(NSA_WIN, past + tdec):]

    stack = lambda a: a[None]
    return (yp, ys, stack(kvm.reshape(bp, mem_prompt.shape[1], 2, MEM_HEADS, MEM_DH)),
            stack(cmp_p), stack(slc_p), stack(win_p), stack(gdn_p), stack(conv_p),
            stack(cmp_s), stack(slc_s), stack(win_s), stack(gdn_s), stack(conv_s))
```

```python
import functools
import math

import jax
import jax.numpy as jnp
import numpy as np
from jax import lax
from jax.experimental import pallas as pl
from jax.experimental.pallas import tpu as pltpu

F32 = jnp.float32
BF16 = jnp.bfloat16
HIGHEST = lax.Precision.HIGHEST

D_MODEL = 1024
PAGE_SIZE = 128
GDN_HEADS = 4
GDN_DK = 128
GDN_DV = 128
GDN_CONV = 4
GDN_CHUNK = 64
GDN_CONV_CH = GDN_HEADS * (2 * GDN_DK + GDN_DV)
NSA_HEADS = 8
NSA_KV = 2
NSA_DH = 64
NSA_CMP = 32
NSA_SEL = 64
NSA_TOPN = 16
NSA_WIN = 512
NSA_FORCE = 1e9
MEM_HEADS = 4
MEM_DH = 128
PEER_HEADS = 8
PEER_NKEYS = 128
PEER_DKEY = 256
PEER_TOPK = 16
N_BRANCH = 3
ROPE_THETA = 10000.0
EPS = 1e-6

LANES = 128
SUBLANES = 8
VMEM_LIMIT = 56 * 1024 * 1024

Z_MG = 0
Z_QKV = Z_MG + N_BRANCH * D_MODEL
Z_ZG = Z_QKV + GDN_CONV_CH
Z_NQ = Z_ZG + GDN_HEADS * GDN_DV
Z_MQ = Z_NQ + NSA_HEADS * NSA_DH
Z_NKV = Z_MQ + MEM_HEADS * MEM_DH
Z_SMALL = Z_NKV + 3 * 2 * NSA_KV * NSA_DH
Z_WIDTH = Z_SMALL + LANES
SM_A = 0
SM_B = GDN_HEADS
SM_NG = 2 * GDN_HEADS
NKV_W = 3 * 2 * NSA_KV * NSA_DH
ROW_W = 2 * NSA_KV * NSA_DH


def _params(sem, vmem=VMEM_LIMIT):
    return pltpu.CompilerParams(dimension_semantics=sem, vmem_limit_bytes=vmem)


def _dot(a, b):
    return jnp.dot(a.astype(BF16), b.astype(BF16), preferred_element_type=F32)


def _dot_nt(a, b):
    return lax.dot_general(a.astype(BF16), b.astype(BF16), (((1,), (1,)), ((), ())),
                           preferred_element_type=F32)


def _dot_hi(a, b):
    return jnp.dot(a, b, precision=HIGHEST, preferred_element_type=F32)


def _dot3(a, b):
    ah = a.astype(BF16)
    bh = b.astype(BF16)
    al = (a - ah.astype(F32)).astype(BF16)
    bl = (b - bh.astype(F32)).astype(BF16)
    d = lambda x, y: jnp.dot(x, y, preferred_element_type=F32)
    return d(ah, bh) + d(al, bh) + d(ah, bl)


def _rms(x, g):
    return x * lax.rsqrt(jnp.mean(x * x, axis=-1, keepdims=True) + EPS) * g


def _gelu(x):
    a = -2.0 * math.sqrt(2.0 / math.pi) * math.log2(math.e)
    return x / (1.0 + jnp.exp2(x * (a + (a * 0.044715) * (x * x))))


def _norm_matmul_kernel(x_ref, g_ref, w_ref, o_ref, *, col_chunk):
    yb = _rms(x_ref[...], g_ref[...]).astype(BF16)
    for c0 in range(0, o_ref.shape[1], col_chunk):
        o_ref[:, c0:c0 + col_chunk] = jnp.dot(yb, w_ref[:, c0:c0 + col_chunk],
                                              preferred_element_type=F32)


def _norm_matmul(x, g, w, tm, col_chunk):
    n, d = x.shape
    wc = w.shape[1]
    return pl.pallas_call(
        functools.partial(_norm_matmul_kernel, col_chunk=col_chunk),
        grid=(n // tm,),
        in_specs=[pl.BlockSpec((tm, d), lambda i: (i, 0)),
                  pl.BlockSpec((1, d), lambda i: (0, 0)),
                  pl.BlockSpec((d, wc), lambda i: (0, 0), pipeline_mode=pl.Buffered(1))],
        out_specs=pl.BlockSpec((tm, wc), lambda i: (i, 0)),
        out_shape=jax.ShapeDtypeStruct((n, wc), F32),
        compiler_params=_params(("parallel",)),
    )(x, g.reshape(1, d), w)


def _tri_inverse(lmats, c):
    row = lax.broadcasted_iota(jnp.int32, (c, c), 0)
    col = lax.broadcasted_iota(jnp.int32, (c, c), 1)
    ident = jnp.where(row == col, 1.0, 0.0)
    xs = [ident - lm for lm in lmats]
    ps = [_dot3(lm, lm) for lm in lmats]
    n = 2
    while n < c:
        xs = [x + _dot3(x, p) for x, p in zip(xs, ps)]
        n *= 2
        if n < c:
            ps = [_dot3(p, p) for p in ps]
    return xs


def _gdn_kernel(qkv_ref, zg_ref, sm_ref, buf_ref, s0_ref, cw_ref, alog_ref, dt_ref, gn_ref,
                o_ref, snew_ref, cnew_ref, ext_ref, s_ref, *, tb, n_chunks, nb):
    ci = pl.program_id(1)

    @pl.when(ci == 0)
    def _():
        for i in range(nb):
            ext_ref[i, 0:SUBLANES, :] = buf_ref[i]
            s_ref[i] = s0_ref[i]

    c = GDN_CHUNK
    row = lax.broadcasted_iota(jnp.int32, (c, c), 0)
    col = lax.broadcasted_iota(jnp.int32, (c, c), 1)
    tril = row >= col
    eye = row == col
    hk = GDN_HEADS * GDN_DK
    chains = []
    for i in range(nb):
        u, g_all, beta_all = _gdn_inputs(qkv_ref.at[i], sm_ref.at[i], cw_ref, alog_ref, dt_ref,
                                         cnew_ref.at[i], ext_ref.at[i], tb)
        gc_all = _dot_hi(jnp.where(tril, 1.0, 0.0), g_all)
        for h in range(GDN_HEADS):
            qh = u[:, h * GDN_DK:(h + 1) * GDN_DK]
            kh = u[:, hk + h * GDN_DK:hk + (h + 1) * GDN_DK]
            ch = dict(i=i, h=h, v=u[:, 2 * hk + h * GDN_DV:2 * hk + (h + 1) * GDN_DV])
            ch['q'] = qh * lax.rsqrt(jnp.sum(qh * qh, axis=-1, keepdims=True) + EPS) * (GDN_DK ** -0.5)
            ch['k'] = kh * lax.rsqrt(jnp.sum(kh * kh, axis=-1, keepdims=True) + EPS)
            ch['beta'] = beta_all[:, SM_B + h:SM_B + h + 1]
            gc = gc_all[:, SM_A + h:SM_A + h + 1]
            ch['gc'] = gc
            ch['gl'] = gc_all[c - 1:c, SM_A + h:SM_A + h + 1]
            gc_row = jnp.sum(jnp.where(eye, gc, 0.0), axis=0, keepdims=True)
            ch['decay'] = jnp.exp(jnp.where(tril, gc - gc_row, -jnp.inf))
            ch['kb'] = ch['k'] * ch['beta']
            ch['egc'] = jnp.exp(gc)
            chains.append(ch)
    lmats = [jnp.where(row > col, _dot_nt(ch['kb'], ch['k']) * ch['decay'], 0.0) for ch in chains]
    tinvs = _tri_inverse(lmats, c)
    uus = [_dot(t, ch['v'] * ch['beta']) for t, ch in zip(tinvs, chains)]
    wws = [_dot(t, ch['kb'] * ch['egc']) for t, ch in zip(tinvs, chains)]
    aqks = [_dot_nt(ch['q'], ch['k']) * ch['decay'] for ch in chains]
    states = [s_ref[ch['i'], ch['h']] for ch in chains]
    v_news = [uu - _dot(ww, s) for uu, ww, s in zip(uus, wws, states)]
    outs = [_dot(ch['q'] * ch['egc'], s) + _dot(aqk, vn)
            for ch, s, aqk, vn in zip(chains, states, aqks, v_news)]
    s_news = [s * jnp.exp(ch['gl']) + _dot((ch['k'] * jnp.exp(ch['gl'] - ch['gc'])).T, vn)
              for ch, s, vn in zip(chains, states, v_news)]
    for ch, o, s_new in zip(chains, outs, s_news):
        i, h = ch['i'], ch['h']
        s_ref[i, h] = s_new
        zh = zg_ref[i, :, h * GDN_DV:(h + 1) * GDN_DV]
        o_ref[i, :, h * GDN_DV:(h + 1) * GDN_DV] = (_rms(o[0:tb], gn_ref[...])
                                                    * (zh * jax.nn.sigmoid(zh)))

    @pl.when(ci == n_chunks - 1)
    def _():
        snew_ref[...] = s_ref[...]


def _gdn_inputs(qkv_ref, sm_ref, cw_ref, alog_ref, dt_ref, cnew_ref, ext_ref, tb):
    c = GDN_CHUNK
    if tb < c:
        ext_ref[SUBLANES + tb:, :] = jnp.zeros((c - tb, GDN_CONV_CH), F32)
    ext_ref[SUBLANES:SUBLANES + tb, :] = qkv_ref[...]
    cw = cw_ref[...]
    conv = cw[0:1] * ext_ref[SUBLANES - 3:SUBLANES - 3 + c, :]
    for j in range(1, GDN_CONV):
        conv = conv + cw[j:j + 1] * ext_ref[SUBLANES - 3 + j:SUBLANES - 3 + j + c, :]
    u = conv * jax.nn.sigmoid(conv)
    last_rows = ext_ref[tb:tb + SUBLANES, :]
    cnew_ref[...] = last_rows
    ext_ref[0:SUBLANES, :] = last_rows

    sm = sm_ref[...]
    if tb < c:
        sm = jnp.concatenate([sm, jnp.zeros((c - tb, LANES), F32)], axis=0)
    za = sm + dt_ref[...]
    softplus = jnp.maximum(za, 0.0) + jnp.log1p(jnp.exp(-jnp.abs(za)))
    g_all = -jnp.exp(alog_ref[...]) * softplus
    beta_all = jax.nn.sigmoid(sm)
    if tb < c:
        valid = lax.broadcasted_iota(jnp.int32, (c, 1), 0) < tb
        u = jnp.where(valid, u, 0.0)
        g_all = jnp.where(valid, g_all, 0.0)
        beta_all = jnp.where(valid, beta_all, 0.0)
    return u, g_all, beta_all


GDN_BATCH_PER_STEP = 2


def _gdn(z, conv_buf, s0, conv_w, a_log, dt_bias, gnorm, batch, t):
    c = GDN_CHUNK
    tb = min(t, c)
    n_chunks = t // tb
    nb = GDN_BATCH_PER_STEP if batch % GDN_BATCH_PER_STEP == 0 else 1
    assert tb % SUBLANES == 0 and n_chunks * tb == t and (tb == c or n_chunks == 1)
    buf8 = jnp.pad(conv_buf, ((0, 0), (SUBLANES - (GDN_CONV - 1), 0), (0, 0)))
    alog_row = jnp.zeros((1, LANES), F32).at[0, SM_A:SM_A + GDN_HEADS].set(a_log)
    dt_row = jnp.zeros((1, LANES), F32).at[0, SM_A:SM_A + GDN_HEADS].set(dt_bias)
    z3 = z.reshape(batch, t, Z_WIDTH)
    hv = GDN_HEADS * GDN_DV
    o, s_new, c_new = pl.pallas_call(
        functools.partial(_gdn_kernel, tb=tb, n_chunks=n_chunks, nb=nb),
        grid=(batch // nb, n_chunks),
        in_specs=[
            pl.BlockSpec((nb, tb, GDN_CONV_CH), lambda b, ci: (b, ci, Z_QKV // GDN_CONV_CH)),
            pl.BlockSpec((nb, tb, hv), lambda b, ci: (b, ci, Z_ZG // hv)),
            pl.BlockSpec((nb, tb, LANES), lambda b, ci: (b, ci, Z_SMALL // LANES)),
            pl.BlockSpec((nb, SUBLANES, GDN_CONV_CH), lambda b, ci: (b, 0, 0)),
            pl.BlockSpec((nb, GDN_HEADS, GDN_DK, GDN_DV), lambda b, ci: (b, 0, 0, 0)),
            pl.BlockSpec((GDN_CONV, GDN_CONV_CH), lambda b, ci: (0, 0)),
            pl.BlockSpec((1, LANES), lambda b, ci: (0, 0)),
            pl.BlockSpec((1, LANES), lambda b, ci: (0, 0)),
            pl.BlockSpec((1, GDN_DV), lambda b, ci: (0, 0)),
        ],
        out_specs=[
            pl.BlockSpec((nb, tb, hv), lambda b, ci: (b, ci, 0)),
            pl.BlockSpec((nb, GDN_HEADS, GDN_DK, GDN_DV), lambda b, ci: (b, 0, 0, 0)),
            pl.BlockSpec((nb, SUBLANES, GDN_CONV_CH), lambda b, ci: (b, 0, 0)),
        ],
        out_shape=[
            jax.ShapeDtypeStruct((batch, t, hv), F32),
            jax.ShapeDtypeStruct((batch, GDN_HEADS, GDN_DK, GDN_DV), F32),
            jax.ShapeDtypeStruct((batch, SUBLANES, GDN_CONV_CH), F32),
        ],
        scratch_shapes=[pltpu.VMEM((nb, SUBLANES + c, GDN_CONV_CH), F32),
                        pltpu.VMEM((nb, GDN_HEADS, GDN_DK, GDN_DV), F32)],
        compiler_params=_params(("parallel", "arbitrary")),
    )(z3, z3, z3, buf8, s0, conv_w, alog_row, dt_row, gnorm.reshape(1, GDN_DV))
    return o.reshape(batch * t, hv), s_new, c_new[:, SUBLANES - (GDN_CONV - 1):]


def _rope_tables(pos):
    half = NSA_DH // 2
    inv = jnp.power(ROPE_THETA, -jnp.arange(half, dtype=F32) / half)
    ang = pos.astype(F32)[:, None] * inv[None, :]
    cos, sin = jnp.cos(ang), jnp.sin(ang)
    cos_t = jnp.concatenate([cos, cos, cos, cos], axis=-1)
    sin_t = jnp.concatenate([-sin, sin, -sin, sin], axis=-1)
    return cos_t, sin_t


def _nsa_prep_kernel(nq_ref, nkv_ref, cos_ref, sin_ref,
                     qc_ref, qr_ref, slc_ref, win_ref, ks_ref, vs_ref, kw_ref, vw_ref):
    cos = cos_ref[...]
    sin = sin_ref[...]
    lane = lax.broadcasted_iota(jnp.int32, cos.shape, 1)
    first_half = (lane % NSA_DH) < (NSA_DH // 2)

    def rope(x):
        swapped = jnp.where(first_half, pltpu.roll(x, LANES - NSA_DH // 2, 1),
                            pltpu.roll(x, NSA_DH // 2, 1))
        return x * cos + swapped * sin

    scale = NSA_DH ** -0.5
    for j in range(NSA_HEADS * NSA_DH // LANES):
        x = nq_ref[:, j * LANES:(j + 1) * LANES]
        qc_ref[:, j * LANES:(j + 1) * LANES] = x * scale
        qr_ref[:, j * LANES:(j + 1) * LANES] = rope(x) * scale

    kv_w = NSA_KV * NSA_DH
    for br, (row_ref, k_ref, v_ref) in enumerate(((slc_ref, ks_ref, vs_ref),
                                                   (win_ref, kw_ref, vw_ref))):
        base = (br + 1) * ROW_W
        kr = rope(nkv_ref[:, base:base + kv_w])
        v = nkv_ref[:, base + kv_w:base + 2 * kv_w]
        row_ref[:, 0:kv_w] = kr
        row_ref[:, kv_w:2 * kv_w] = v
        for g in range(NSA_KV):
            k_ref[0, g] = kr[:, g * NSA_DH:(g + 1) * NSA_DH].astype(k_ref.dtype)
            v_ref[0, g] = v[:, g * NSA_DH:(g + 1) * NSA_DH].astype(v_ref.dtype)


def _nsa_prep(z, pos, batch, t, tm, kv_dtype):
    cos_t, sin_t = _rope_tables(pos)
    nt = t // tm
    rowblk = lambda b, j: b * nt + j
    kv_shape = jax.ShapeDtypeStruct((batch, NSA_KV, t, NSA_DH), kv_dtype)
    kv_spec = pl.BlockSpec((1, NSA_KV, tm, NSA_DH), lambda b, j: (b, 0, j, 0))
    qw = NSA_HEADS * NSA_DH
    return pl.pallas_call(
        _nsa_prep_kernel,
        grid=(batch, nt),
        in_specs=[pl.BlockSpec((tm, qw), lambda b, j: (rowblk(b, j), Z_NQ // qw)),
                  pl.BlockSpec((tm, NKV_W), lambda b, j: (rowblk(b, j), Z_NKV // NKV_W)),
                  pl.BlockSpec((tm, LANES), lambda b, j: (j, 0)),
                  pl.BlockSpec((tm, LANES), lambda b, j: (j, 0))],
        out_specs=[pl.BlockSpec((tm, qw), lambda b, j: (rowblk(b, j), 0)),
                   pl.BlockSpec((tm, qw), lambda b, j: (rowblk(b, j), 0)),
                   pl.BlockSpec((tm, ROW_W), lambda b, j: (rowblk(b, j), 0)),
                   pl.BlockSpec((tm, ROW_W), lambda b, j: (rowblk(b, j), 0)),
                   kv_spec, kv_spec, kv_spec, kv_spec],
        out_shape=[jax.ShapeDtypeStruct((batch * t, qw), F32),
                   jax.ShapeDtypeStruct((batch * t, qw), F32),
                   jax.ShapeDtypeStruct((batch * t, ROW_W), F32),
                   jax.ShapeDtypeStruct((batch * t, ROW_W), F32),
                   kv_shape, kv_shape, kv_shape, kv_shape],
        compiler_params=_params(("parallel", "parallel")),
    )(z, z, cos_t, sin_t)


def _compress_weights(pe, w1, w2):
    eye = jnp.eye(2, dtype=F32)
    w1r = w1.reshape(2, NSA_CMP, NSA_DH, NSA_DH)
    w1big = jnp.einsum('srde,st,gh->rsgdthe', w1r, eye, eye).reshape(NSA_CMP * ROW_W, ROW_W)
    w2big = jnp.einsum('sed,st,gh->sgethd', w2, eye, eye).reshape(ROW_W, ROW_W)
    pe_big = jnp.broadcast_to(jnp.transpose(pe, (1, 0, 2))[:, :, None, :],
                              (NSA_CMP, 2, NSA_KV, NSA_DH)).reshape(1, NSA_CMP * ROW_W)
    return pe_big, w1big.astype(BF16), w2big.astype(BF16)


def _compress_kernel(x_ref, pe_ref, w1_ref, w2_ref, o_ref):
    x = (x_ref[...] + pe_ref[...]).astype(BF16)
    hid = _gelu(jnp.dot(x, w1_ref[...], preferred_element_type=F32))
    o_ref[...] = jnp.dot(hid.astype(BF16), w2_ref[...], preferred_element_type=F32)


def _compress(rows, pe_big, w1big, w2big, tm):
    n, kdim = rows.shape
    return pl.pallas_call(
        _compress_kernel,
        grid=(n // tm,),
        in_specs=[pl.BlockSpec((tm, kdim), lambda i: (i, 0)),
                  pl.BlockSpec((1, kdim), lambda i: (0, 0)),
                  pl.BlockSpec((kdim, ROW_W), lambda i: (0, 0)),
                  pl.BlockSpec((ROW_W, ROW_W), lambda i: (0, 0))],
        out_specs=pl.BlockSpec((tm, ROW_W), lambda i: (i, 0)),
        out_shape=jax.ShapeDtypeStruct((n, ROW_W), F32),
        compiler_params=_params(("parallel",)),
    )(rows, pe_big, w1big, w2big)


def _split_compressed(cmp_out, batch):
    nb = cmp_out.shape[0] // batch
    x = cmp_out.reshape(batch, nb // 2, 2, 2, NSA_KV, NSA_DH)
    x = jnp.transpose(x, (3, 0, 4, 2, 1, 5)).reshape(2, batch, NSA_KV, nb, NSA_DH)
    return x[0].astype(BF16), x[1].astype(BF16)


CMP_PAGES_PER_STEP = 64
SLC_PAGES_PER_STEP = 16
BLOCKS_PER_PAGE = PAGE_SIZE // NSA_CMP


def _page_specs(block, n_pages, per_step):
    def spec(j):
        return pl.BlockSpec(block, lambda b, s, pt: (pt[b * n_pages + s * per_step + j], 0, 0))
    return [spec(j) for j in range(per_step)]


CMP_ROW_PITCH = NSA_CMP + 4


def _paged_compress_kernel(pt_ref, *refs):
    n_in = CMP_PAGES_PER_STEP
    pe_ref, w1_ref, w2_ref, o_ref, x_ref = refs[n_in:]
    m = n_in * BLOCKS_PER_PAGE
    n_slabs = ROW_W // LANES
    for j, p_ref in enumerate(refs[:n_in]):
        x = p_ref[0].T
        for n in range(BLOCKS_PER_PAGE):
            base = (j * BLOCKS_PER_PAGE + n) * CMP_ROW_PITCH
            for sl in range(n_slabs):
                x_ref[sl, base:base + NSA_CMP, :] = x[n * NSA_CMP:(n + 1) * NSA_CMP,
                                                      sl * LANES:(sl + 1) * LANES]
    acc = jnp.zeros((m, ROW_W), F32)
    for r in range(NSA_CMP):
        lhs = jnp.concatenate([x_ref[sl, pl.ds(r, m, stride=CMP_ROW_PITCH), :]
                               for sl in range(n_slabs)], axis=1)
        lhs = (lhs + pe_ref[r:r + 1, :]).astype(BF16)
        acc = acc + jnp.dot(lhs, w1_ref[r], preferred_element_type=F32)
    o_ref[...] = jnp.dot(_gelu(acc).astype(BF16), w2_ref[...], preferred_element_type=F32)


def _paged_compress(cache_t, page_table, pe_big, w1big, w2big):
    batch, n_pages = page_table.shape
    rows = CMP_PAGES_PER_STEP * BLOCKS_PER_PAGE
    steps = n_pages // CMP_PAGES_PER_STEP
    pe_rows = pe_big.reshape(NSA_CMP, ROW_W)
    w1_rows = w1big.reshape(NSA_CMP, ROW_W, ROW_W)
    const = lambda a: pl.BlockSpec(a.shape, lambda b, s, pt: (0,) * a.ndim)
    return pl.pallas_call(
        _paged_compress_kernel,
        grid_spec=pltpu.PrefetchScalarGridSpec(
            num_scalar_prefetch=1, grid=(batch, steps),
            in_specs=_page_specs((1, ROW_W, PAGE_SIZE), n_pages, CMP_PAGES_PER_STEP)
            + [const(pe_rows), const(w1_rows), const(w2big)],
            out_specs=pl.BlockSpec((rows, ROW_W), lambda b, s, pt: (b * steps + s, 0)),
            scratch_shapes=[pltpu.VMEM((ROW_W // LANES, rows * CMP_ROW_PITCH, LANES), F32)]),
        out_shape=jax.ShapeDtypeStruct((batch * n_pages * BLOCKS_PER_PAGE, ROW_W), F32),
        compiler_params=_params(("parallel", "parallel")),
    )(page_table.reshape(-1), *([cache_t] * CMP_PAGES_PER_STEP), pe_rows, w1_rows, w2big)


def _topk_mask(score, k):
    n = score.shape[-1]
    lane = lax.broadcasted_iota(jnp.int32, score.shape, 1).astype(F32)
    sel = jnp.zeros(score.shape, F32)
    for _ in range(k):
        m = jnp.max(score, axis=-1, keepdims=True)
        idx = jnp.min(jnp.where(score == m, lane, float(n)), axis=-1, keepdims=True)
        pick = lane == idx
        sel = jnp.where(pick, 1.0, sel)
        score = jnp.where(pick, -jnp.inf, score)
    return sel


def _softmax_step(carry, s, v):
    m, l, acc = carry
    m_new = jnp.maximum(m, jnp.max(s, axis=-1, keepdims=True))
    m_safe = jnp.where(m_new == -jnp.inf, 0.0, m_new)
    p = jnp.exp(s - m_safe)
    alpha = jnp.exp(m - m_safe)
    l = alpha * l + jnp.sum(p, axis=-1, keepdims=True)
    acc = alpha * acc + _dot(p, v)
    return m_new, l, acc


MASKED = -1e30


def _topk_mask_rows(score, k):
    n = score.shape[0]
    row = lax.broadcasted_iota(jnp.int32, score.shape, 0).astype(F32)
    sel = jnp.zeros(score.shape, F32)
    for _ in range(k):
        m = jnp.max(score, axis=0, keepdims=True)
        idx = jnp.min(jnp.where(score == m, row, float(n)), axis=0, keepdims=True)
        pick = row == idx
        sel = jnp.where(pick, 1.0, sel)
        score = jnp.where(pick, -jnp.inf, score)
    return sel


def _softmax_step_cols(carry, s, vt):
    m, l, acc = carry
    m_new = jnp.maximum(m, jnp.max(s, axis=0, keepdims=True))
    p = jnp.exp(s - m_new)
    alpha = jnp.exp(m - m_new)
    l = alpha * l + jnp.sum(p, axis=0, keepdims=True)
    acc = alpha * acc + _dot(vt, p)
    return m_new, l, acc


def _nsa_prompt_kernel(qc_ref, qr_ref, kc_ref, vct_ref, ks_ref, vst_ref, kw_ref, vwt_ref, sm_ref,
                       o_ref, bias_ref, *, tq, tk, tkw):
    hp = NSA_HEADS // NSA_KV
    r = hp * tq
    g = pl.program_id(1)
    qi = pl.program_id(2)
    q0 = qi * tq
    ncp = kc_ref.shape[2]
    nch = ncp // 2
    nsl = bias_ref.shape[0]
    per_tile = tk // NSA_SEL

    def heads_on_lanes(ref):
        xt = ref[...].T
        return jnp.concatenate([xt[h * NSA_DH:(h + 1) * NSA_DH] for h in range(hp)],
                               axis=1).astype(BF16)

    qct = heads_on_lanes(qc_ref)
    qrt = heads_on_lanes(qr_ref)
    t_q = q0 + lax.broadcasted_iota(jnp.int32, (1, tq), 1)
    t_lane = jnp.concatenate([t_q] * hp, axis=1)

    s = _dot(kc_ref[0, 0], qct)
    crow = lax.broadcasted_iota(jnp.int32, (ncp, 1), 0)
    cblk = 2 * (crow % nch) + crow // nch
    s = jnp.where((cblk + 1) * NSA_CMP <= t_lane + 1, s, -jnp.inf)
    m = jnp.max(s, axis=0, keepdims=True)
    e = jnp.exp(s - jnp.where(m == -jnp.inf, 0.0, m))
    p_cmp = e / jnp.maximum(jnp.sum(e, axis=0, keepdims=True), 1e-30)
    o_cmp = _dot(vct_ref[0, 0], p_cmp)

    n_keys = kw_ref.shape[2]
    wlen = min(NSA_WIN + tq, n_keys)
    wstart = pl.multiple_of(jnp.minimum(jnp.maximum(q0 - NSA_WIN, 0), n_keys - wlen), tkw)
    sc = _dot(kw_ref[0, 0, pl.ds(wstart, wlen), :], qrt)
    rel = t_lane - (wstart + lax.broadcasted_iota(jnp.int32, (wlen, 1), 0))
    sc = jnp.where((rel >= 0) & (rel < NSA_WIN), sc, MASKED)
    p_win = jnp.exp(sc - jnp.max(sc, axis=0, keepdims=True))
    o_win = (_dot(vwt_ref[0, 0, :, pl.ds(wstart, wlen)], p_win)
             / jnp.sum(p_win, axis=0, keepdims=True))

    imp = p_cmp[:, 0:tq]
    for h in range(1, hp):
        imp = imp + p_cmp[:, h * tq:(h + 1) * tq]
    imp = imp[:nch] + imp[nch:]
    blk = lax.broadcasted_iota(jnp.int32, (nsl, tq), 0)
    cur = t_q // NSA_SEL
    forced = (blk == 0) | (blk == cur) | (blk == cur - 1)
    score = jnp.where(blk > cur, -jnp.inf, jnp.where(forced, NSA_FORCE, imp))
    bias = (_topk_mask_rows(score, NSA_TOPN) - 1.0) * (-MASKED)
    bias_ref[...] = jnp.concatenate([bias] * hp, axis=1)

    init = (jnp.full((1, r), MASKED, F32), jnp.zeros((1, r), F32), jnp.zeros((NSA_DH, r), F32))

    def slc_scores(kt):
        start = pl.multiple_of(kt * tk, tk)
        sc = _dot(ks_ref[0, 0, pl.ds(start, tk), :], qrt)
        brow = bias_ref[pl.ds(pl.multiple_of(kt * per_tile, per_tile), per_tile), :]
        sc = jnp.concatenate([sc[j * NSA_SEL:(j + 1) * NSA_SEL] + brow[j:j + 1]
                              for j in range(per_tile)], axis=0)
        return sc, vst_ref[0, 0, :, pl.ds(start, tk)]

    def slc_body(kt, carry):
        sc, vt = slc_scores(kt)
        return _softmax_step_cols(carry, sc, vt)

    kd = q0 // tk
    carry = lax.fori_loop(0, kd, slc_body, init)
    sc, vt = slc_scores(kd)
    kpos = kd * tk + lax.broadcasted_iota(jnp.int32, (tk, 1), 0)
    carry = _softmax_step_cols(carry, jnp.where(kpos <= t_lane, sc, MASKED), vt)
    o_slc = carry[2] / carry[1]

    sig = jax.nn.sigmoid(sm_ref[...].T)
    gw = N_BRANCH * hp
    gates = jnp.where(g == 0, sig[SM_NG:SM_NG + gw], sig[SM_NG + gw:SM_NG + 2 * gw])
    outs = []
    for h in range(hp):
        cols = slice(h * tq, (h + 1) * tq)
        outs.append(gates[3 * h:3 * h + 1] * o_cmp[:, cols]
                    + gates[3 * h + 1:3 * h + 2] * o_slc[:, cols]
                    + gates[3 * h + 2:3 * h + 3] * o_win[:, cols])
    o_ref[...] = jnp.concatenate(outs, axis=0).T


def _nsa_prompt_attn(qc, qr, kc, vct, ks, vst, kw, vwt, z, batch, t, tq, tk, tkw):
    nq = t // tq
    gw = NSA_HEADS * NSA_DH // NSA_KV
    assert tk % (SUBLANES * NSA_SEL) == 0 and tq == tkw and tk % tq == 0 and t % tk == 0
    rowblk = lambda b, g, i: b * nq + i
    full = lambda a: pl.BlockSpec((1, 1) + a.shape[2:], lambda b, g, i: (b, g, 0, 0))
    q_spec = pl.BlockSpec((tq, gw), lambda b, g, i: (rowblk(b, g, i), g))
    return pl.pallas_call(
        functools.partial(_nsa_prompt_kernel, tq=tq, tk=tk, tkw=tkw),
        grid=(batch, NSA_KV, nq),
        in_specs=[q_spec, q_spec] + [full(a) for a in (kc, vct, ks, vst, kw, vwt)]
        + [pl.BlockSpec((tq, LANES), lambda b, g, i: (rowblk(b, g, i), Z_SMALL // LANES))],
        out_specs=pl.BlockSpec((tq, gw), lambda b, g, i: (rowblk(b, g, i), g)),
        out_shape=jax.ShapeDtypeStruct((batch * t, NSA_HEADS * NSA_DH), F32),
        scratch_shapes=[pltpu.VMEM((t // NSA_SEL, NSA_HEADS // NSA_KV * tq), F32)],
        compiler_params=_params(("parallel", "parallel", "arbitrary")),
    )(qc, qr, kc, vct, ks, vst, kw, vwt, z)


def _softmax_step_vt(carry, s, vt):
    m, l, acc = carry
    m_new = jnp.maximum(m, jnp.max(s, axis=-1, keepdims=True))
    m_safe = jnp.where(m_new == -jnp.inf, 0.0, m_new)
    p = jnp.exp(s - m_safe)
    alpha = jnp.exp(m - m_safe)
    l = alpha * l + jnp.sum(p, axis=-1, keepdims=True)
    acc = alpha * acc + _dot_nt(p, vt)
    return m_new, l, acc


def _nsa_decode_kernel(pt_ref, *refs, tq, pos0, wrel0, n_keys):
    n_in = SLC_PAGES_PER_STEP
    qc_ref, qr_ref, kc_ref, vc_ref = refs[0:4]
    pages = refs[4:4 + n_in]
    (kw_ref, vw_ref, kt_ref, vt_ref, sm_ref, o_ref,
     m_ref, l_ref, acc_ref, ocmp_ref, sel_ref) = refs[4 + n_in:]
    hp = NSA_HEADS // NSA_KV
    r = hp * tq
    gwid = hp * NSA_DH
    step = pl.program_id(1)
    n_steps = pl.num_programs(1)
    tk = n_in * PAGE_SIZE
    per_tile = tk // NSA_SEL
    n_tiles = sel_ref.shape[1]
    ncp = kc_ref.shape[2]
    nch = ncp // 2
    nsl = -(-(n_tiles * per_tile) // LANES) * LANES
    kv_w = NSA_KV * NSA_DH

    def stack_heads(ref, g):
        x = ref[:, g * gwid:(g + 1) * gwid]
        return jnp.concatenate([x[:, h * NSA_DH:(h + 1) * NSA_DH] for h in range(hp)],
                               axis=0).astype(BF16)

    t_q = pos0 + lax.broadcasted_iota(jnp.int32, (tq, 1), 0)
    t_row = jnp.concatenate([t_q] * hp, axis=0)
    expand = (lax.broadcasted_iota(jnp.int32, (per_tile, tk), 0)
              == lax.broadcasted_iota(jnp.int32, (per_tile, tk), 1) // NSA_SEL).astype(BF16)

    def picked_rows(g, tile, width):
        pk = jnp.dot(sel_ref[g, tile].astype(BF16), expand[:, :width], preferred_element_type=F32)
        return jnp.concatenate([pk] * hp, axis=0)

    @pl.when(step == 0)
    def _():
        ccol = lax.broadcasted_iota(jnp.int32, (r, ncp), 1)
        cvis = (2 * (ccol % nch) + ccol // nch + 1) * NSA_CMP <= t_row + 1
        blk = lax.broadcasted_iota(jnp.int32, (tq, nsl), 1)
        cur = t_q // NSA_SEL
        forced = (blk == 0) | (blk == cur) | (blk == cur - 1)
        group_scores = []
        for g in range(NSA_KV):
            s = jnp.where(cvis, _dot_nt(stack_heads(qc_ref, g), kc_ref[0, g]), -jnp.inf)
            m = jnp.max(s, axis=-1, keepdims=True)
            e = jnp.exp(s - jnp.where(m == -jnp.inf, 0.0, m))
            p_cmp = e / jnp.maximum(jnp.sum(e, axis=-1, keepdims=True), 1e-30)
            ocmp_ref[g] = _dot(p_cmp, vc_ref[0, g])
            imp = p_cmp[0:tq]
            for h in range(1, hp):
                imp = imp + p_cmp[h * tq:(h + 1) * tq]
            imp = imp[:, :nch] + imp[:, nch:]
            if nsl > nch:
                imp = jnp.concatenate([imp, jnp.zeros((tq, nsl - nch), F32)], axis=1)
            group_scores.append(jnp.where(blk > cur, -jnp.inf, jnp.where(forced, NSA_FORCE, imp)))
        sel_all = _topk_mask(jnp.concatenate(group_scores, axis=0), NSA_TOPN)
        for g in range(NSA_KV):
            sel = sel_all[g * tq:(g + 1) * tq]
            for j in range(n_tiles):
                sel_ref[g, j] = sel[:, j * per_tile:(j + 1) * per_tile]
            m_ref[g] = jnp.full((r, 1), -jnp.inf, F32)
            l_ref[g] = jnp.zeros((r, 1), F32)
            acc_ref[g] = jnp.zeros((r, NSA_DH), F32)

    kpos = step * tk + lax.broadcasted_iota(jnp.int32, (1, tk), 1)
    scores, values = [], []
    for g in range(NSA_KV):
        k_t = jnp.concatenate([p[0, g * NSA_DH:(g + 1) * NSA_DH, :] for p in pages], axis=1)
        values.append(jnp.concatenate(
            [p[0, kv_w + g * NSA_DH:kv_w + (g + 1) * NSA_DH, :] for p in pages], axis=1))
        scores.append(_dot(stack_heads(qr_ref, g), k_t))
    picked = [picked_rows(g, step, tk) for g in range(NSA_KV)]
    for g in range(NSA_KV):
        ok = (picked[g] > 0.5) & (kpos <= t_row)
        m, l, acc = _softmax_step_vt((m_ref[g], l_ref[g], acc_ref[g]),
                                     jnp.where(ok, scores[g], -jnp.inf), values[g])
        m_ref[g] = m
        l_ref[g] = l
        acc_ref[g] = acc

    @pl.when(step == n_steps - 1)
    def _():
        sig = jax.nn.sigmoid(sm_ref[...])
        tw = kt_ref.shape[2]
        tpos = n_keys + lax.broadcasted_iota(jnp.int32, (1, tw), 1)
        wpos = pos0 - wrel0 + lax.broadcasted_iota(jnp.int32, (1, kw_ref.shape[2]), 1)
        rel = t_row - wpos
        wok = (rel >= 0) & (rel < NSA_WIN) & (wpos >= 0)
        qr4s = [stack_heads(qr_ref, g) for g in range(NSA_KV)]
        tail_s = [_dot_nt(qr4s[g], kt_ref[0, g]) for g in range(NSA_KV)]
        win_s = [_dot_nt(qr4s[g], kw_ref[0, g]) for g in range(NSA_KV)]
        for g in range(NSA_KV):
            ok = (picked_rows(g, n_tiles - 1, tw) > 0.5) & (tpos <= t_row)
            carry = _softmax_step((m_ref[g], l_ref[g], acc_ref[g]),
                                  jnp.where(ok, tail_s[g], -jnp.inf), vt_ref[0, g])
            o_slc = carry[2] / jnp.maximum(carry[1], 1e-30)
            s = jnp.where(wok, win_s[g], -jnp.inf)
            m = jnp.max(s, axis=-1, keepdims=True)
            e = jnp.exp(s - jnp.where(m == -jnp.inf, 0.0, m))
            o_win = _dot(e, vw_ref[0, g]) / jnp.maximum(jnp.sum(e, axis=-1, keepdims=True), 1e-30)
            o_cmp = ocmp_ref[g]
            base = SM_NG + g * N_BRANCH * hp
            outs = []
            for h in range(hp):
                rows = slice(h * tq, (h + 1) * tq)
                c0 = base + N_BRANCH * h
                outs.append(sig[:, c0:c0 + 1] * o_cmp[rows] + sig[:, c0 + 1:c0 + 2] * o_slc[rows]
                            + sig[:, c0 + 2:c0 + 3] * o_win[rows])
            o_ref[:, g * gwid:(g + 1) * gwid] = jnp.concatenate(outs, axis=1)


def _nsa_decode_attn(qc, qr, kc, vc, cache_t, page_table, kw, vw, ktail, vtail, z, tq, pos0, wrel0):
    batch, n_pages = page_table.shape
    n_keys = n_pages * PAGE_SIZE
    steps = n_pages // SLC_PAGES_PER_STEP
    hp = NSA_HEADS // NSA_KV
    r = hp * tq
    qw = NSA_HEADS * NSA_DH
    per_tile = SLC_PAGES_PER_STEP * PAGE_SIZE // NSA_SEL
    assert pos0 == n_keys and ktail.shape[2] <= per_tile * NSA_SEL
    per_b = lambda a: pl.BlockSpec((1,) + a.shape[1:], lambda b, s, pt: (b, 0, 0, 0))
    q_spec = pl.BlockSpec((tq, qw), lambda b, s, pt: (b, 0))
    return pl.pallas_call(
        functools.partial(_nsa_decode_kernel, tq=tq, pos0=pos0, wrel0=wrel0, n_keys=n_keys),
        grid_spec=pltpu.PrefetchScalarGridSpec(
            num_scalar_prefetch=1, grid=(batch, steps),
            in_specs=[q_spec, q_spec, per_b(kc), per_b(vc)]
            + _page_specs((1, ROW_W, PAGE_SIZE), n_pages, SLC_PAGES_PER_STEP)
            + [per_b(kw), per_b(vw), per_b(ktail), per_b(vtail),
               pl.BlockSpec((tq, LANES), lambda b, s, pt: (b, Z_SMALL // LANES))],
            out_specs=pl.BlockSpec((tq, qw), lambda b, s, pt: (b, 0)),
            scratch_shapes=[pltpu.VMEM((NSA_KV, r, 1), F32), pltpu.VMEM((NSA_KV, r, 1), F32),
                            pltpu.VMEM((NSA_KV, r, NSA_DH), F32),
                            pltpu.VMEM((NSA_KV, r, NSA_DH), F32),
                            pltpu.VMEM((NSA_KV, steps + 1, tq, per_tile), F32)]),
        out_shape=jax.ShapeDtypeStruct((batch * tq, qw), F32),
        compiler_params=_params(("parallel", "arbitrary")),
    )(page_table.reshape(-1), qc, qr, kc, vc, *([cache_t] * SLC_PAGES_PER_STEP),
      kw, vw, ktail, vtail, z)


def _mem_attn_kernel(q_ref, kv_ref, o_ref):
    hw = MEM_HEADS * MEM_DH
    for h in range(MEM_HEADS):
        q = q_ref[:, h * MEM_DH:(h + 1) * MEM_DH] * (MEM_DH ** -0.5)
        k = kv_ref[0, :, h * MEM_DH:(h + 1) * MEM_DH]
        v = kv_ref[0, :, hw + h * MEM_DH:hw + (h + 1) * MEM_DH]
        s = _dot_nt(q, k)
        e = jnp.exp(s - jnp.max(s, axis=-1, keepdims=True))
        p = e / jnp.sum(e, axis=-1, keepdims=True)
        o_ref[:, h * MEM_DH:(h + 1) * MEM_DH] = _dot(p, v)


def _mem_attn(z, kv, batch, t, tm):
    nt = t // tm
    hw = MEM_HEADS * MEM_DH
    return pl.pallas_call(
        _mem_attn_kernel,
        grid=(batch, nt),
        in_specs=[pl.BlockSpec((tm, hw), lambda b, j: (b * nt + j, Z_MQ // hw)),
                  pl.BlockSpec((1,) + kv.shape[1:], lambda b, j: (b, 0, 0))],
        out_specs=pl.BlockSpec((tm, hw), lambda b, j: (b * nt + j, 0)),
        out_shape=jax.ShapeDtypeStruct((batch * t, hw), F32),
        compiler_params=_params(("parallel", "parallel")),
    )(z, kv)


def _merge_kernel(x_ref, mg_ref, og_ref, on_ref, om_ref, wg_ref, wn_ref, wm_ref, wo_ref,
                  gf_ref, wq_ref, x1_ref, h2_ref, qp_ref):
    d = D_MODEL
    mix = (jax.nn.sigmoid(mg_ref[:, 0:d]) * _dot(og_ref[...], wg_ref[...])
           + jax.nn.sigmoid(mg_ref[:, d:2 * d]) * _dot(on_ref[...], wn_ref[...])
           + jax.nn.sigmoid(mg_ref[:, 2 * d:3 * d]) * _dot(om_ref[...], wm_ref[...]))
    x1 = x_ref[...] + _dot(mix, wo_ref[...])
    x1_ref[...] = x1
    h2 = _rms(x1, gf_ref[...]).astype(BF16)
    h2_ref[...] = h2
    qp_ref[...] = jnp.dot(h2, wq_ref[...], preferred_element_type=F32)


def _merge(x, z, o_gdn, o_nsa, o_mem, wg, wn, wm, wo, norm_ffn, wq, tm):
    n, d = x.shape
    qw = wq.shape[1]
    row = lambda w: pl.BlockSpec((tm, w), lambda i: (i, 0))
    const = lambda a: pl.BlockSpec(a.shape, lambda i: (0, 0))
    return pl.pallas_call(
        _merge_kernel,
        grid=(n // tm,),
        in_specs=[row(d), pl.BlockSpec((tm, N_BRANCH * d), lambda i: (i, Z_MG)),
                  row(o_gdn.shape[1]), row(o_nsa.shape[1]), row(o_mem.shape[1]),
                  const(wg), const(wn), const(wm), const(wo),
                  pl.BlockSpec((1, d), lambda i: (0, 0)), const(wq)],
        out_specs=[row(d), row(d), row(qw)],
        out_shape=[jax.ShapeDtypeStruct((n, d), F32), jax.ShapeDtypeStruct((n, d), BF16),
                   jax.ShapeDtypeStruct((n, qw), F32)],
        compiler_params=_params(("parallel",)),
    )(x, z, o_gdn, o_nsa, o_mem, wg, wn, wm, wo, norm_ffn.reshape(1, d), wq)


PEER_RANKS = PEER_TOPK + 1


RANK_NONE = 64.0


def _top_values(s, n):
    vals = []
    rank = jnp.full(s.shape, RANK_NONE, F32)
    for k in range(n):
        m = jnp.max(s, axis=0, keepdims=True)
        vals.append(m)
        hit = s >= m
        rank = jnp.where(hit, float(k + 1), rank)
        s = jnp.where(hit, -jnp.inf, s)
    return vals, rank


PEER_PAIRS = [(i, j) for i in range(1, PEER_RANKS + 1) for j in range(1, PEER_RANKS // i + 1)]


def _pair_selectors():
    rows = -(-len(PEER_PAIRS) // SUBLANES) * SUBLANES
    cols = -(-PEER_RANKS // SUBLANES) * SUBLANES
    pa = np.zeros((rows, cols), np.float32)
    pb = np.zeros((rows, cols), np.float32)
    for p, (i, j) in enumerate(PEER_PAIRS):
        pa[p, i - 1] = 1.0
        pb[p, j - 1] = 1.0
    return jnp.asarray(pa), jnp.asarray(pb)


def _peer_route_kernel(qp_ref, sk_ref, pa_ref, pb_ref, r2_ref, e2_ref, nb_ref, e1_ref):
    half = PEER_DKEY // 2
    nt = (((1,), (1,)), ((), ()))
    for h in range(PEER_HEADS):
        qa = qp_ref[:, h * PEER_DKEY:h * PEER_DKEY + half]
        qb = qp_ref[:, h * PEER_DKEY + half:(h + 1) * PEER_DKEY]
        s1 = lax.dot_general(sk_ref[0], qa, nt, precision=HIGHEST, preferred_element_type=F32)
        s2 = lax.dot_general(sk_ref[1], qb, nt, precision=HIGHEST, preferred_element_type=F32)
        a, _ = _top_values(s1, PEER_RANKS)
        b, rank2 = _top_values(s2, PEER_RANKS)
        pad = jnp.full((pa_ref.shape[1] - PEER_RANKS, s1.shape[1]), MASKED, F32)
        a_rows = jnp.maximum(jnp.concatenate(a + [pad], axis=0), MASKED)
        b_rows = jnp.maximum(jnp.concatenate(b + [pad], axis=0), MASKED)
        cand = _dot_hi(pa_ref[...], a_rows) + _dot_hi(pb_ref[...], b_rows)
        prow = lax.broadcasted_iota(jnp.int32, cand.shape, 0)
        cand = jnp.where(prow < len(PEER_PAIRS), cand, -jnp.inf)
        work = cand
        ranked = []
        for _ in range(PEER_RANKS):
            m = jnp.max(work, axis=0, keepdims=True)
            ranked.append(m)
            work = jnp.where(work >= m, -jnp.inf, work)
        tau = 0.5 * (ranked[PEER_TOPK - 1] + ranked[PEER_TOPK])
        top = a[0] + b[0]
        zsum = jnp.sum(jnp.where(cand >= tau, jnp.exp(cand - top), 0.0), axis=0, keepdims=True)
        th = tau - s1
        count = jnp.zeros(s1.shape, F32)
        for bj in b:
            count = count + jnp.where(bj >= th, 1.0, 0.0)
        r2_ref[h] = rank2.astype(BF16)
        e2_ref[h] = jnp.exp(s2 - b[0]).astype(BF16)
        nb_ref[h] = count
        e1_ref[h] = jnp.exp(s1 - a[0]) / zsum


def _peer_route(qp, subkeys, tt):
    n = qp.shape[0]
    pa, pb = _pair_selectors()
    shape = lambda dt: jax.ShapeDtypeStruct((PEER_HEADS, PEER_NKEYS, n), dt)
    spec = pl.BlockSpec((PEER_HEADS, PEER_NKEYS, tt), lambda i: (0, 0, i))
    return pl.pallas_call(
        _peer_route_kernel,
        grid=(n // tt,),
        in_specs=[pl.BlockSpec((tt, qp.shape[1]), lambda i: (i, 0)),
                  pl.BlockSpec(subkeys.shape, lambda i: (0, 0, 0)),
                  pl.BlockSpec(pa.shape, lambda i: (0, 0)),
                  pl.BlockSpec(pb.shape, lambda i: (0, 0))],
        out_specs=[spec, spec, spec, spec],
        out_shape=[shape(BF16), shape(BF16), shape(F32), shape(F32)],
        compiler_params=_params(("parallel",)),
    )(qp, subkeys, pa, pb)


PEER_SUB_BLOCKS = 4
PEER_BLOCK_ROWS = 8


def _blocked_transpose(v):
    eb = PEER_BLOCK_ROWS * PEER_NKEYS
    return jnp.transpose(v.reshape(v.shape[0] // eb, eb, v.shape[1]), (0, 2, 1))


def _peer_dense_kernel(ht_ref, u_ref, vt_ref, r2_ref, e2_ref, nb_ref, e1_ref, x1_ref, gf_ref,
                       y_ref, acc_ref, act_ref, wa_ref, r2s_ref, e2s_ref, rows_ref, *, jb):
    j = pl.program_id(1)
    tt = ht_ref.shape[1]

    @pl.when(j == 0)
    def _():
        acc_ref[...] = jnp.zeros(acc_ref.shape, F32)
        r2s_ref[...] = r2_ref[...]
        e2s_ref[...] = e2_ref[...]

    sub = jb // PEER_SUB_BLOCKS
    for sb in range(PEER_SUB_BLOCKS):
        srows = slice(sb * sub * PEER_NKEYS, (sb + 1) * sub * PEER_NKEYS)
        act_ref[srows, :] = _gelu(jnp.dot(u_ref[srows, :], ht_ref[...],
                                          preferred_element_type=F32)).astype(BF16)
    for jj in range(jb):
        i1 = j * jb + jj
        rows = slice(jj * PEER_NKEYS, (jj + 1) * PEER_NKEYS)
        for h in range(PEER_HEADS):
            rows_ref[jj, h:h + 1, :] = nb_ref[h, pl.ds(i1, 1), :]
            rows_ref[jj, PEER_HEADS + h:PEER_HEADS + h + 1, :] = e1_ref[h, pl.ds(i1, 1), :]
        for c in range(tt // LANES):
            cols = slice(c * LANES, (c + 1) * LANES)
            w = None
            for h in range(PEER_HEADS):
                count = rows_ref[jj, h:h + 1, cols].astype(BF16)
                e1 = rows_ref[jj, PEER_HEADS + h:PEER_HEADS + h + 1, cols].astype(BF16)
                picked = r2s_ref[h, :, cols] <= count
                term = jnp.where(picked, e2s_ref[h, :, cols], jnp.zeros((), BF16)) * e1
                w = term if w is None else w + term
            wa_ref[rows, cols] = w * act_ref[rows, cols]
    acc_ref[...] += jnp.dot(vt_ref[0], wa_ref[...], preferred_element_type=F32)

    @pl.when(j == pl.num_programs(1) - 1)
    def _():
        y_ref[...] = _rms(x1_ref[...] + acc_ref[...].T, gf_ref[...])


def _peer_dense(ht, u, vt, s2, e2, th, e1, x1, norm_final, tt, jb):
    d, n = ht.shape
    n_exp = u.shape[0]
    eb = jb * PEER_NKEYS
    route = pl.BlockSpec((PEER_HEADS, PEER_NKEYS, tt), lambda t, j: (0, 0, t))
    return pl.pallas_call(
        functools.partial(_peer_dense_kernel, jb=jb),
        grid=(n // tt, n_exp // eb),
        in_specs=[pl.BlockSpec((d, tt), lambda t, j: (0, t)),
                  pl.BlockSpec((eb, d), lambda t, j: (j, 0)),
                  pl.BlockSpec((1, d, eb), lambda t, j: (j, 0, 0)),
                  route, route, route, route,
                  pl.BlockSpec((tt, d), lambda t, j: (t, 0)),
                  pl.BlockSpec((1, d), lambda t, j: (0, 0))],
        out_specs=pl.BlockSpec((tt, d), lambda t, j: (t, 0)),
        out_shape=jax.ShapeDtypeStruct((n, d), F32),
        scratch_shapes=[pltpu.VMEM((d, tt), F32), pltpu.VMEM((eb, tt), BF16),
                        pltpu.VMEM((eb, tt), BF16),
                        pltpu.VMEM((PEER_HEADS, PEER_NKEYS, tt), BF16),
                        pltpu.VMEM((PEER_HEADS, PEER_NKEYS, tt), BF16),
                        pltpu.VMEM((jb, 2 * PEER_HEADS, tt), F32)],
        compiler_params=_params(("parallel", "arbitrary")),
    )(ht, u, vt, s2, e2, th, e1, x1, norm_final.reshape(1, d))


def _permute_w_in(w_in):
    sizes = (GDN_CONV_CH, GDN_HEADS * GDN_DV, GDN_HEADS, GDN_HEADS, NSA_HEADS * NSA_DH, NKV_W,
             N_BRANCH * NSA_HEADS, MEM_HEADS * MEM_DH, N_BRANCH * D_MODEL)
    qkv, zg, a, b, nq, nkv, ng, mq, mg = jnp.split(w_in, np.cumsum(sizes)[:-1].tolist(), axis=1)
    pad = jnp.zeros((w_in.shape[0], LANES - a.shape[1] - b.shape[1] - ng.shape[1]), w_in.dtype)
    return jnp.concatenate([mg, qkv, zg, nq, mq, nkv, a, b, ng, pad], axis=1).astype(BF16)


def _tokens_tile(n, pref):
    return pref if n % pref == 0 else n


def _layer(x, pos0, kv_mem, nsa_keys, gdn_state, conv_buf, w, peer_tt):
    batch, t, d = x.shape
    n = batch * t
    xf = x.reshape(n, d)
    z = _norm_matmul(xf, w['norm_attn'], w['w_in'], _tokens_tile(n, 256), 640)

    o_gdn, s_new, conv_new = _gdn(z, conv_buf, gdn_state, w['gdn_conv'], w['gdn_a_log'],
                                  w['gdn_dt_bias'], w['gdn_norm'], batch, t)

    prep_tm = _tokens_tile(t, 512)
    kv_dtype = BF16 if prep_tm % 16 == 0 else F32
    qc, qr, slc_rows, win_rows, ks, vs, kw, vw = _nsa_prep(
        z, pos0 + jnp.arange(t), batch, t, prep_tm, kv_dtype)
    cmp_rows = z[:, Z_NKV:Z_NKV + ROW_W]
    o_nsa = nsa_keys(z, qc, qr, cmp_rows, ks, vs, kw, vw)

    o_mem = _mem_attn(z, kv_mem, batch, t, _tokens_tile(t, 512))

    x1, h2, qp = _merge(xf, z, o_gdn, o_nsa, o_mem, w['w_gdn_out'], w['w_nsa_out'],
                        w['w_mem_out'], w['w_o'], w['norm_ffn'], w['peer_wq'],
                        _tokens_tile(n, 256))
    s2, e2, th, e1 = _peer_route(qp, w['peer_subkeys'], 256)
    y = _peer_dense(h2.T, w['peer_u'], w['peer_vt'], s2, e2, th, e1, x1, w['norm_final'],
                    peer_tt, PEER_BLOCK_ROWS)
    row5 = lambda a: a.reshape(batch, t, 2, NSA_KV, NSA_DH)
    return (y.reshape(batch, t, d), row5(cmp_rows), row5(slc_rows), row5(win_rows), s_new,
            conv_new)


def kernel(x_prompt, x_sample, cache_mem_kv, cache_cmp_kv, cache_slc_kv, cache_win_kv, state_gdn, state_conv, page_table, mem_prompt, norm_attn, w_in, gdn_conv, gdn_a_log, gdn_dt_bias, gdn_norm, nsa_cmp_pe, nsa_cmp_w1, nsa_cmp_w2, norm_mem, w_mem_kv, w_gdn_out, w_nsa_out, w_mem_out, w_o, norm_ffn, peer_wq, peer_subkeys, peer_u, peer_v, norm_final):
    depth = w_in.shape[0]
    assert depth == 1
    l = 0
    bp, seq, d = x_prompt.shape
    bs, tdec, _ = x_sample.shape
    n_pages = page_table.shape[1]
    past = n_pages * PAGE_SIZE
    assert past % NSA_CMP == 0 and tdec < NSA_CMP and seq % LANES == 0

    pe_big, w1big, w2big = _compress_weights(nsa_cmp_pe[l], nsa_cmp_w1[l], nsa_cmp_w2[l])
    w = dict(norm_attn=norm_attn[l], w_in=_permute_w_in(w_in[l]), gdn_conv=gdn_conv[l],
             gdn_a_log=gdn_a_log[l], gdn_dt_bias=gdn_dt_bias[l], gdn_norm=gdn_norm[l],
             w_gdn_out=w_gdn_out[l].astype(BF16), w_nsa_out=w_nsa_out[l].astype(BF16),
             w_mem_out=w_mem_out[l].astype(BF16), w_o=w_o[l].astype(BF16), norm_ffn=norm_ffn[l],
             peer_wq=peer_wq[l].astype(BF16), peer_subkeys=peer_subkeys[l],
             peer_u=peer_u[l].astype(BF16), peer_vt=_blocked_transpose(peer_v[l].astype(BF16)),
             norm_final=norm_final)

    mem_n = mem_prompt.shape[0] * mem_prompt.shape[1]
    kvm = _norm_matmul(mem_prompt.reshape(mem_n, d), norm_mem[l], w_mem_kv[l].astype(BF16),
                       _tokens_tile(mem_n, 256), 512).reshape(bp, mem_prompt.shape[1], -1)

    def prompt_keys(z, qc, qr, cmp_rows, ks, vs, kw, vw):
        n_blk = bp * seq // NSA_CMP
        cmp_out = _compress(cmp_rows.reshape(n_blk, NSA_CMP * ROW_W), pe_big, w1big, w2big,
                            _tokens_tile(n_blk, 128))
        kc, vc = _split_compressed(cmp_out, bp)
        tr = lambda a: jnp.transpose(a, (0, 1, 3, 2))
        return _nsa_prompt_attn(qc, qr, kc, tr(vc), ks, tr(vs), kw, tr(vw), z, bp, seq,
                                128, _tokens_tile(seq, 512), 128)

    yp, cmp_p, slc_p, win_p, gdn_p, conv_p = _layer(
        x_prompt, 0, kvm, prompt_keys,
        jnp.zeros((bp, GDN_HEADS, GDN_DK, GDN_DV), F32),
        jnp.zeros((bp, GDN_CONV - 1, GDN_CONV_CH), F32), w, 512)
    win_len_p = min(NSA_WIN, seq)
    win_p = win_p[:, seq - win_len_p:]

    n_pool = cache_cmp_kv.shape[1]
    feature_major = lambda c: jnp.transpose(c, (0, 2, 3, 4, 1)).reshape(n_pool, ROW_W, PAGE_SIZE)
    cache_cmp = feature_major(cache_cmp_kv[l])
    cache_slc = feature_major(cache_slc_kv[l])
    cache_win = cache_win_kv[l].reshape(bs, -1, ROW_W)
    wb = cache_win.shape[1]
    assert wb == NSA_WIN
    kv_w = NSA_KV * NSA_DH

    def pad_rows(a, rows):
        return jnp.pad(a, ((0, 0), (0, 0), (0, rows - a.shape[2]), (0, 0))).astype(BF16)

    def split_rows(rows):
        r = rows.reshape(bs, rows.shape[1], 2, NSA_KV, NSA_DH)
        return jnp.transpose(r[:, :, 0], (0, 2, 1, 3)), jnp.transpose(r[:, :, 1], (0, 2, 1, 3))

    def sample_keys(z, qc, qr, cmp_rows, ks_new, vs_new, kw_new, vw_new):
        cmp_out = _paged_compress(cache_cmp, page_table, pe_big, w1big, w2big)
        kc, vc = _split_compressed(cmp_out, bs)
        kwc, vwc = split_rows(cache_win)
        win_rows = wb + LANES
        kw = pad_rows(jnp.concatenate([kwc, kw_new], axis=2), win_rows)
        vw = pad_rows(jnp.concatenate([vwc, vw_new], axis=2), win_rows)
        return _nsa_decode_attn(qc, qr, kc, vc, cache_slc, page_table, kw, vw,
                                pad_rows(ks_new, LANES), pad_rows(vs_new, LANES), z,
                                tdec, past, wb)

    ys, cmp_s, slc_s, win_new, gdn_s, conv_s = _layer(
        x_sample, past, cache_mem_kv[l].reshape(bs, cache_mem_kv.shape[2], -1), sample_keys,
        state_gdn[l], state_conv[l], w, 256)
    win_all = jnp.concatenate([cache_win_kv[l], win_new], axis=1)
    win_s = win_all[:, win_all.shape[1] - min(NSA_WIN, past + tdec):]

    stack = lambda a: a[None]
    return (yp, ys, stack(kvm.reshape(bp, mem_prompt.shape[1], 2, MEM_HEADS, MEM_DH)),
            stack(cmp_p), stack(slc_p), stack(win_p), stack(gdn_p), stack(conv_p),
            stack(cmp_s), stack(slc_s), stack(win_s), stack(gdn_s), stack(conv_s))
```

```python
import functools
import math

import jax
import jax.numpy as jnp
import numpy as np
from jax import lax
from jax.experimental import pallas as pl
from jax.experimental.pallas import tpu as pltpu

F32 = jnp.float32
BF16 = jnp.bfloat16
HIGHEST = lax.Precision.HIGHEST

D_MODEL = 1024
PAGE_SIZE = 128
GDN_HEADS = 4
GDN_DK = 128
GDN_DV = 128
GDN_CONV = 4
GDN_CHUNK = 64
GDN_CONV_CH = GDN_HEADS * (2 * GDN_DK + GDN_DV)
NSA_HEADS = 8
NSA_KV = 2
NSA_DH = 64
NSA_CMP = 32
NSA_SEL = 64
NSA_TOPN = 16
NSA_WIN = 512
NSA_FORCE = 1e9
MEM_HEADS = 4
MEM_DH = 128
PEER_HEADS = 8
PEER_NKEYS = 128
PEER_DKEY = 256
PEER_TOPK = 16
N_BRANCH = 3
ROPE_THETA = 10000.0
EPS = 1e-6

LANES = 128
SUBLANES = 8
VMEM_LIMIT = 56 * 1024 * 1024

Z_MG = 0
Z_QKV = Z_MG + N_BRANCH * D_MODEL
Z_ZG = Z_QKV + GDN_CONV_CH
Z_NQ = Z_ZG + GDN_HEADS * GDN_DV
Z_MQ = Z_NQ + NSA_HEADS * NSA_DH
Z_NKV = Z_MQ + MEM_HEADS * MEM_DH
Z_SMALL = Z_NKV + 3 * 2 * NSA_KV * NSA_DH
Z_WIDTH = Z_SMALL + LANES
SM_A = 0
SM_B = GDN_HEADS
SM_NG = 2 * GDN_HEADS
NKV_W = 3 * 2 * NSA_KV * NSA_DH
ROW_W = 2 * NSA_KV * NSA_DH


def _params(sem, vmem=VMEM_LIMIT):
    return pltpu.CompilerParams(dimension_semantics=sem, vmem_limit_bytes=vmem)


def _dot(a, b):
    return jnp.dot(a.astype(BF16), b.astype(BF16), preferred_element_type=F32)


def _dot_nt(a, b):
    return lax.dot_general(a.astype(BF16), b.astype(BF16), (((1,), (1,)), ((), ())),
                           preferred_element_type=F32)


def _dot_hi(a, b):
    return jnp.dot(a, b, precision=HIGHEST, preferred_element_type=F32)


def _dot3(a, b):
    ah = a.astype(BF16)
    bh = b.astype(BF16)
    al = (a - ah.astype(F32)).astype(BF16)
    bl = (b - bh.astype(F32)).astype(BF16)
    d = lambda x, y: jnp.dot(x, y, preferred_element_type=F32)
    return d(ah, bh) + d(al, bh) + d(ah, bl)


def _rms(x, g):
    return x * lax.rsqrt(jnp.mean(x * x, axis=-1, keepdims=True) + EPS) * g


def _gelu(x):
    a = -2.0 * math.sqrt(2.0 / math.pi) * math.log2(math.e)
    return x / (1.0 + jnp.exp2(x * (a + (a * 0.044715) * (x * x))))


def _norm_matmul_kernel(x_ref, g_ref, w_ref, o_ref, *, col_chunk):
    yb = _rms(x_ref[...], g_ref[...]).astype(BF16)
    for c0 in range(0, o_ref.shape[1], col_chunk):
        o_ref[:, c0:c0 + col_chunk] = jnp.dot(yb, w_ref[:, c0:c0 + col_chunk],
                                              preferred_element_type=F32)


def _norm_matmul(x, g, w, tm, col_chunk):
    n, d = x.shape
    wc = w.shape[1]
    return pl.pallas_call(
        functools.partial(_norm_matmul_kernel, col_chunk=col_chunk),
        grid=(n // tm,),
        in_specs=[pl.BlockSpec((tm, d), lambda i: (i, 0)),
                  pl.BlockSpec((1, d), lambda i: (0, 0)),
                  pl.BlockSpec((d, wc), lambda i: (0, 0), pipeline_mode=pl.Buffered(1))],
        out_specs=pl.BlockSpec((tm, wc), lambda i: (i, 0)),
        out_shape=jax.ShapeDtypeStruct((n, wc), F32),
        compiler_params=_params(("parallel",)),
    )(x, g.reshape(1, d), w)


def _tri_inverse(lmats, c):
    row = lax.broadcasted_iota(jnp.int32, (c, c), 0)
    col = lax.broadcasted_iota(jnp.int32, (c, c), 1)
    ident = jnp.where(row == col, 1.0, 0.0)
    xs = [ident - lm for lm in lmats]
    ps = [_dot3(lm, lm) for lm in lmats]
    n = 2
    while n < c:
        xs = [x + _dot3(x, p) for x, p in zip(xs, ps)]
        n *= 2
        if n < c:
            ps = [_dot3(p, p) for p in ps]
    return xs


def _gdn_kernel(qkv_ref, zg_ref, sm_ref, buf_ref, s0_ref, cw_ref, alog_ref, dt_ref, gn_ref,
                o_ref, snew_ref, cnew_ref, ext_ref, s_ref, *, tb, n_chunks, nb):
    ci = pl.program_id(1)

    @pl.when(ci == 0)
    def _():
        for i in range(nb):
            ext_ref[i, 0:SUBLANES, :] = buf_ref[i]
            s_ref[i] = s0_ref[i]

    c = GDN_CHUNK
    row = lax.broadcasted_iota(jnp.int32, (c, c), 0)
    col = lax.broadcasted_iota(jnp.int32, (c, c), 1)
    tril = row >= col
    eye = row == col
    hk = GDN_HEADS * GDN_DK
    chains = []
    for i in range(nb):
        u, g_all, beta_all = _gdn_inputs(qkv_ref.at[i], sm_ref.at[i], cw_ref, alog_ref, dt_ref,
                                         cnew_ref.at[i], ext_ref.at[i], tb)
        gc_all = _dot_hi(jnp.where(tril, 1.0, 0.0), g_all)
        for h in range(GDN_HEADS):
            qh = u[:, h * GDN_DK:(h + 1) * GDN_DK]
            kh = u[:, hk + h * GDN_DK:hk + (h + 1) * GDN_DK]
            ch = dict(i=i, h=h, v=u[:, 2 * hk + h * GDN_DV:2 * hk + (h + 1) * GDN_DV])
            ch['q'] = qh * lax.rsqrt(jnp.sum(qh * qh, axis=-1, keepdims=True) + EPS) * (GDN_DK ** -0.5)
            ch['k'] = kh * lax.rsqrt(jnp.sum(kh * kh, axis=-1, keepdims=True) + EPS)
            ch['beta'] = beta_all[:, SM_B + h:SM_B + h + 1]
            gc = gc_all[:, SM_A + h:SM_A + h + 1]
            ch['gc'] = gc
            ch['gl'] = gc_all[c - 1:c, SM_A + h:SM_A + h + 1]
            gc_row = jnp.sum(jnp.where(eye, gc, 0.0), axis=0, keepdims=True)
            ch['decay'] = jnp.exp(jnp.where(tril, gc - gc_row, -jnp.inf))
            ch['kb'] = ch['k'] * ch['beta']
            ch['egc'] = jnp.exp(gc)
            chains.append(ch)
    lmats = [jnp.where(row > col, _dot_nt(ch['kb'], ch['k']) * ch['decay'], 0.0) for ch in chains]
    tinvs = _tri_inverse(lmats, c)
    uus = [_dot(t, ch['v'] * ch['beta']) for t, ch in zip(tinvs, chains)]
    wws = [_dot(t, ch['kb'] * ch['egc']) for t, ch in zip(tinvs, chains)]
    aqks = [_dot_nt(ch['q'], ch['k']) * ch['decay'] for ch in chains]
    states = [s_ref[ch['i'], ch['h']] for ch in chains]
    v_news = [uu - _dot(ww, s) for uu, ww, s in zip(uus, wws, states)]
    outs = [_dot(ch['q'] * ch['egc'], s) + _dot(aqk, vn)
            for ch, s, aqk, vn in zip(chains, states, aqks, v_news)]
    s_news = [s * jnp.exp(ch['gl']) + _dot((ch['k'] * jnp.exp(ch['gl'] - ch['gc'])).T, vn)
              for ch, s, vn in zip(chains, states, v_news)]
    for ch, o, s_new in zip(chains, outs, s_news):
        i, h = ch['i'], ch['h']
        s_ref[i, h] = s_new
        zh = zg_ref[i, :, h * GDN_DV:(h + 1) * GDN_DV]
        o_ref[i, :, h * GDN_DV:(h + 1) * GDN_DV] = (_rms(o[0:tb], gn_ref[...])
                                                    * (zh * jax.nn.sigmoid(zh)))

    @pl.when(ci == n_chunks - 1)
    def _():
        snew_ref[...] = s_ref[...]


def _gdn_inputs(qkv_ref, sm_ref, cw_ref, alog_ref, dt_ref, cnew_ref, ext_ref, tb):
    c = GDN_CHUNK
    if tb < c:
        ext_ref[SUBLANES + tb:, :] = jnp.zeros((c - tb, GDN_CONV_CH), F32)
    ext_ref[SUBLANES:SUBLANES + tb, :] = qkv_ref[...]
    cw = cw_ref[...]
    conv = cw[0:1] * ext_ref[SUBLANES - 3:SUBLANES - 3 + c, :]
    for j in range(1, GDN_CONV):
        conv = conv + cw[j:j + 1] * ext_ref[SUBLANES - 3 + j:SUBLANES - 3 + j + c, :]
    u = conv * jax.nn.sigmoid(conv)
    last_rows = ext_ref[tb:tb + SUBLANES, :]
    cnew_ref[...] = last_rows
    ext_ref[0:SUBLANES, :] = last_rows

    sm = sm_ref[...]
    if tb < c:
        sm = jnp.concatenate([sm, jnp.zeros((c - tb, LANES), F32)], axis=0)
    za = sm + dt_ref[...]
    softplus = jnp.maximum(za, 0.0) + jnp.log1p(jnp.exp(-jnp.abs(za)))
    g_all = -jnp.exp(alog_ref[...]) * softplus
    beta_all = jax.nn.sigmoid(sm)
    if tb < c:
        valid = lax.broadcasted_iota(jnp.int32, (c, 1), 0) < tb
        u = jnp.where(valid, u, 0.0)
        g_all = jnp.where(valid, g_all, 0.0)
        beta_all = jnp.where(valid, beta_all, 0.0)
    return u, g_all, beta_all


GDN_BATCH_PER_STEP = 2


def _gdn(z, conv_buf, s0, conv_w, a_log, dt_bias, gnorm, batch, t):
    c = GDN_CHUNK
    tb = min(t, c)
    n_chunks = t // tb
    nb = GDN_BATCH_PER_STEP if batch % GDN_BATCH_PER_STEP == 0 else 1
    assert tb % SUBLANES == 0 and n_chunks * tb == t and (tb == c or n_chunks == 1)
    buf8 = jnp.pad(conv_buf, ((0, 0), (SUBLANES - (GDN_CONV - 1), 0), (0, 0)))
    alog_row = jnp.zeros((1, LANES), F32).at[0, SM_A:SM_A + GDN_HEADS].set(a_log)
    dt_row = jnp.zeros((1, LANES), F32).at[0, SM_A:SM_A + GDN_HEADS].set(dt_bias)
    z3 = z.reshape(batch, t, Z_WIDTH)
    hv = GDN_HEADS * GDN_DV
    o, s_new, c_new = pl.pallas_call(
        functools.partial(_gdn_kernel, tb=tb, n_chunks=n_chunks, nb=nb),
        grid=(batch // nb, n_chunks),
        in_specs=[
            pl.BlockSpec((nb, tb, GDN_CONV_CH), lambda b, ci: (b, ci, Z_QKV // GDN_CONV_CH)),
            pl.BlockSpec((nb, tb, hv), lambda b, ci: (b, ci, Z_ZG // hv)),
            pl.BlockSpec((nb, tb, LANES), lambda b, ci: (b, ci, Z_SMALL // LANES)),
            pl.BlockSpec((nb, SUBLANES, GDN_CONV_CH), lambda b, ci: (b, 0, 0)),
            pl.BlockSpec((nb, GDN_HEADS, GDN_DK, GDN_DV), lambda b, ci: (b, 0, 0, 0)),
            pl.BlockSpec((GDN_CONV, GDN_CONV_CH), lambda b, ci: (0, 0)),
            pl.BlockSpec((1, LANES), lambda b, ci: (0, 0)),
            pl.BlockSpec((1, LANES), lambda b, ci: (0, 0)),
            pl.BlockSpec((1, GDN_DV), lambda b, ci: (0, 0)),
        ],
        out_specs=[
            pl.BlockSpec((nb, tb, hv), lambda b, ci: (b, ci, 0)),
            pl.BlockSpec((nb, GDN_HEADS, GDN_DK, GDN_DV), lambda b, ci: (b, 0, 0, 0)),
            pl.BlockSpec((nb, SUBLANES, GDN_CONV_CH), lambda b, ci: (b, 0, 0)),
        ],
        out_shape=[
            jax.ShapeDtypeStruct((batch, t, hv), F32),
            jax.ShapeDtypeStruct((batch, GDN_HEADS, GDN_DK, GDN_DV), F32),
            jax.ShapeDtypeStruct((batch, SUBLANES, GDN_CONV_CH), F32),
        ],
        scratch_shapes=[pltpu.VMEM((nb, SUBLANES + c, GDN_CONV_CH), F32),
                        pltpu.VMEM((nb, GDN_HEADS, GDN_DK, GDN_DV), F32)],
        compiler_params=_params(("parallel", "arbitrary")),
    )(z3, z3, z3, buf8, s0, conv_w, alog_row, dt_row, gnorm.reshape(1, GDN_DV))
    return o.reshape(batch * t, hv), s_new, c_new[:, SUBLANES - (GDN_CONV - 1):]


def _rope_tables(pos):
    half = NSA_DH // 2
    inv = jnp.power(ROPE_THETA, -jnp.arange(half, dtype=F32) / half)
    ang = pos.astype(F32)[:, None] * inv[None, :]
    cos, sin = jnp.cos(ang), jnp.sin(ang)
    cos_t = jnp.concatenate([cos, cos, cos, cos], axis=-1)
    sin_t = jnp.concatenate([-sin, sin, -sin, sin], axis=-1)
    return cos_t, sin_t


def _nsa_prep_kernel(nq_ref, nkv_ref, cos_ref, sin_ref,
                     qc_ref, qr_ref, slc_ref, win_ref, ks_ref, vs_ref, kw_ref, vw_ref):
    cos = cos_ref[...]
    sin = sin_ref[...]
    lane = lax.broadcasted_iota(jnp.int32, cos.shape, 1)
    first_half = (lane % NSA_DH) < (NSA_DH // 2)

    def rope(x):
        swapped = jnp.where(first_half, pltpu.roll(x, LANES - NSA_DH // 2, 1),
                            pltpu.roll(x, NSA_DH // 2, 1))
        return x * cos + swapped * sin

    scale = NSA_DH ** -0.5
    for j in range(NSA_HEADS * NSA_DH // LANES):
        x = nq_ref[:, j * LANES:(j + 1) * LANES]
        qc_ref[:, j * LANES:(j + 1) * LANES] = x * scale
        qr_ref[:, j * LANES:(j + 1) * LANES] = rope(x) * scale

    kv_w = NSA_KV * NSA_DH
    for br, (row_ref, k_ref, v_ref) in enumerate(((slc_ref, ks_ref, vs_ref),
                                                   (win_ref, kw_ref, vw_ref))):
        base = (br + 1) * ROW_W
        kr = rope(nkv_ref[:, base:base + kv_w])
        v = nkv_ref[:, base + kv_w:base + 2 * kv_w]
        row_ref[:, 0:kv_w] = kr
        row_ref[:, kv_w:2 * kv_w] = v
        for g in range(NSA_KV):
            k_ref[0, g] = kr[:, g * NSA_DH:(g + 1) * NSA_DH].astype(k_ref.dtype)
            v_ref[0, g] = v[:, g * NSA_DH:(g + 1) * NSA_DH].astype(v_ref.dtype)


def _nsa_prep(z, pos, batch, t, tm, kv_dtype):
    cos_t, sin_t = _rope_tables(pos)
    nt = t // tm
    rowblk = lambda b, j: b * nt + j
    kv_shape = jax.ShapeDtypeStruct((batch, NSA_KV, t, NSA_DH), kv_dtype)
    kv_spec = pl.BlockSpec((1, NSA_KV, tm, NSA_DH), lambda b, j: (b, 0, j, 0))
    qw = NSA_HEADS * NSA_DH
    return pl.pallas_call(
        _nsa_prep_kernel,
        grid=(batch, nt),
        in_specs=[pl.BlockSpec((tm, qw), lambda b, j: (rowblk(b, j), Z_NQ // qw)),
                  pl.BlockSpec((tm, NKV_W), lambda b, j: (rowblk(b, j), Z_NKV // NKV_W)),
                  pl.BlockSpec((tm, LANES), lambda b, j: (j, 0)),
                  pl.BlockSpec((tm, LANES), lambda b, j: (j, 0))],
        out_specs=[pl.BlockSpec((tm, qw), lambda b, j: (rowblk(b, j), 0)),
                   pl.BlockSpec((tm, qw), lambda b, j: (rowblk(b, j), 0)),
                   pl.BlockSpec((tm, ROW_W), lambda b, j: (rowblk(b, j), 0)),
                   pl.BlockSpec((tm, ROW_W), lambda b, j: (rowblk(b, j), 0)),
                   kv_spec, kv_spec, kv_spec, kv_spec],
        out_shape=[jax.ShapeDtypeStruct((batch * t, qw), F32),
                   jax.ShapeDtypeStruct((batch * t, qw), F32),
                   jax.ShapeDtypeStruct((batch * t, ROW_W), F32),
                   jax.ShapeDtypeStruct((batch * t, ROW_W), F32),
                   kv_shape, kv_shape, kv_shape, kv_shape],
        compiler_params=_params(("parallel", "parallel")),
    )(z, z, cos_t, sin_t)


def _compress_weights(pe, w1, w2):
    eye = jnp.eye(2, dtype=F32)
    w1r = w1.reshape(2, NSA_CMP, NSA_DH, NSA_DH)
    w1big = jnp.einsum('srde,st,gh->rsgdthe', w1r, eye, eye).reshape(NSA_CMP * ROW_W, ROW_W)
    w2big = jnp.einsum('sed,st,gh->sgethd', w2, eye, eye).reshape(ROW_W, ROW_W)
    pe_big = jnp.broadcast_to(jnp.transpose(pe, (1, 0, 2))[:, :, None, :],
                              (NSA_CMP, 2, NSA_KV, NSA_DH)).reshape(1, NSA_CMP * ROW_W)
    return pe_big, w1big.astype(BF16), w2big.astype(BF16)


def _compress_kernel(x_ref, pe_ref, w1_ref, w2_ref, o_ref):
    x = (x_ref[...] + pe_ref[...]).astype(BF16)
    hid = _gelu(jnp.dot(x, w1_ref[...], preferred_element_type=F32))
    o_ref[...] = jnp.dot(hid.astype(BF16), w2_ref[...], preferred_element_type=F32)


def _compress(rows, pe_big, w1big, w2big, tm):
    n, kdim = rows.shape
    return pl.pallas_call(
        _compress_kernel,
        grid=(n // tm,),
        in_specs=[pl.BlockSpec((tm, kdim), lambda i: (i, 0)),
                  pl.BlockSpec((1, kdim), lambda i: (0, 0)),
                  pl.BlockSpec((kdim, ROW_W), lambda i: (0, 0)),
                  pl.BlockSpec((ROW_W, ROW_W), lambda i: (0, 0))],
        out_specs=pl.BlockSpec((tm, ROW_W), lambda i: (i, 0)),
        out_shape=jax.ShapeDtypeStruct((n, ROW_W), F32),
        compiler_params=_params(("parallel",)),
    )(rows, pe_big, w1big, w2big)


def _split_compressed(cmp_out, batch):
    nb = cmp_out.shape[0] // batch
    x = cmp_out.reshape(batch, nb // 2, 2, 2, NSA_KV, NSA_DH)
    x = jnp.transpose(x, (3, 0, 4, 2, 1, 5)).reshape(2, batch, NSA_KV, nb, NSA_DH)
    return x[0].astype(BF16), x[1].astype(BF16)


CMP_PAGES_PER_STEP = 64
SLC_PAGES_PER_STEP = 16
BLOCKS_PER_PAGE = PAGE_SIZE // NSA_CMP


def _page_specs(block, n_pages, per_step):
    def spec(j):
        return pl.BlockSpec(block, lambda b, s, pt: (pt[b * n_pages + s * per_step + j], 0, 0))
    return [spec(j) for j in range(per_step)]


CMP_ROW_PITCH = NSA_CMP + 4


def _paged_compress_kernel(pt_ref, *refs):
    n_in = CMP_PAGES_PER_STEP
    pe_ref, w1_ref, w2_ref, o_ref, x_ref = refs[n_in:]
    m = n_in * BLOCKS_PER_PAGE
    n_slabs = ROW_W // LANES
    for j, p_ref in enumerate(refs[:n_in]):
        x = p_ref[0].T
        for n in range(BLOCKS_PER_PAGE):
            base = (j * BLOCKS_PER_PAGE + n) * CMP_ROW_PITCH
            for sl in range(n_slabs):
                x_ref[sl, base:base + NSA_CMP, :] = x[n * NSA_CMP:(n + 1) * NSA_CMP,
                                                      sl * LANES:(sl + 1) * LANES]
    acc = jnp.zeros((m, ROW_W), F32)
    for r in range(NSA_CMP):
        lhs = jnp.concatenate([x_ref[sl, pl.ds(r, m, stride=CMP_ROW_PITCH), :]
                               for sl in range(n_slabs)], axis=1)
        lhs = (lhs + pe_ref[r:r + 1, :]).astype(BF16)
        acc = acc + jnp.dot(lhs, w1_ref[r], preferred_element_type=F32)
    o_ref[...] = jnp.dot(_gelu(acc).astype(BF16), w2_ref[...], preferred_element_type=F32)


def _paged_compress(cache_t, page_table, pe_big, w1big, w2big):
    batch, n_pages = page_table.shape
    rows = CMP_PAGES_PER_STEP * BLOCKS_PER_PAGE
    steps = n_pages // CMP_PAGES_PER_STEP
    pe_rows = pe_big.reshape(NSA_CMP, ROW_W)
    w1_rows = w1big.reshape(NSA_CMP, ROW_W, ROW_W)
    const = lambda a: pl.BlockSpec(a.shape, lambda b, s, pt: (0,) * a.ndim)
    return pl.pallas_call(
        _paged_compress_kernel,
        grid_spec=pltpu.PrefetchScalarGridSpec(
            num_scalar_prefetch=1, grid=(batch, steps),
            in_specs=_page_specs((1, ROW_W, PAGE_SIZE), n_pages, CMP_PAGES_PER_STEP)
            + [const(pe_rows), const(w1_rows), const(w2big)],
            out_specs=pl.BlockSpec((rows, ROW_W), lambda b, s, pt: (b * steps + s, 0)),
            scratch_shapes=[pltpu.VMEM((ROW_W // LANES, rows * CMP_ROW_PITCH, LANES), F32)]),
        out_shape=jax.ShapeDtypeStruct((batch * n_pages * BLOCKS_PER_PAGE, ROW_W), F32),
        compiler_params=_params(("parallel", "parallel")),
    )(page_table.reshape(-1), *([cache_t] * CMP_PAGES_PER_STEP), pe_rows, w1_rows, w2big)


def _topk_mask(score, k):
    n = score.shape[-1]
    lane = lax.broadcasted_iota(jnp.int32, score.shape, 1).astype(F32)
    sel = jnp.zeros(score.shape, F32)
    for _ in range(k):
        m = jnp.max(score, axis=-1, keepdims=True)
        idx = jnp.min(jnp.where(score == m, lane, float(n)), axis=-1, keepdims=True)
        pick = lane == idx
        sel = jnp.where(pick, 1.0, sel)
        score = jnp.where(pick, -jnp.inf, score)
    return sel


def _softmax_step(carry, s, v):
    m, l, acc = carry
    m_new = jnp.maximum(m, jnp.max(s, axis=-1, keepdims=True))
    m_safe = jnp.where(m_new == -jnp.inf, 0.0, m_new)
    p = jnp.exp(s - m_safe)
    alpha = jnp.exp(m - m_safe)
    l = alpha * l + jnp.sum(p, axis=-1, keepdims=True)
    acc = alpha * acc + _dot(p, v)
    return m_new, l, acc


MASKED = -1e30


def _topk_mask_rows(score, k):
    n = score.shape[0]
    row = lax.broadcasted_iota(jnp.int32, score.shape, 0).astype(F32)
    sel = jnp.zeros(score.shape, F32)
    for _ in range(k):
        m = jnp.max(score, axis=0, keepdims=True)
        idx = jnp.min(jnp.where(score == m, row, float(n)), axis=0, keepdims=True)
        pick = row == idx
        sel = jnp.where(pick, 1.0, sel)
        score = jnp.where(pick, -jnp.inf, score)
    return sel


def _softmax_step_cols(carry, s, vt):
    m, l, acc = carry
    m_new = jnp.maximum(m, jnp.max(s, axis=0, keepdims=True))
    p = jnp.exp(s - m_new)
    alpha = jnp.exp(m - m_new)
    l = alpha * l + jnp.sum(p, axis=0, keepdims=True)
    acc = alpha * acc + _dot(vt, p)
    return m_new, l, acc


def _nsa_prompt_kernel(qc_ref, qr_ref, kc_ref, vct_ref, ks_ref, vst_ref, kw_ref, vwt_ref, sm_ref,
                       o_ref, bias_ref, *, tq, tk, tkw):
    hp = NSA_HEADS // NSA_KV
    r = hp * tq
    g = pl.program_id(1)
    qi = pl.program_id(2)
    q0 = qi * tq
    ncp = kc_ref.shape[2]
    nch = ncp // 2
    nsl = bias_ref.shape[0]
    per_tile = tk // NSA_SEL

    def heads_on_lanes(ref):
        xt = ref[...].T
        return jnp.concatenate([xt[h * NSA_DH:(h + 1) * NSA_DH] for h in range(hp)],
                               axis=1).astype(BF16)

    qct = heads_on_lanes(qc_ref)
    qrt = heads_on_lanes(qr_ref)
    t_q = q0 + lax.broadcasted_iota(jnp.int32, (1, tq), 1)
    t_lane = jnp.concatenate([t_q] * hp, axis=1)

    s = _dot(kc_ref[0, 0], qct)
    crow = lax.broadcasted_iota(jnp.int32, (ncp, 1), 0)
    cblk = 2 * (crow % nch) + crow // nch
    s = jnp.where((cblk + 1) * NSA_CMP <= t_lane + 1, s, -jnp.inf)
    m = jnp.max(s, axis=0, keepdims=True)
    e = jnp.exp(s - jnp.where(m == -jnp.inf, 0.0, m))
    p_cmp = e / jnp.maximum(jnp.sum(e, axis=0, keepdims=True), 1e-30)
    o_cmp = _dot(vct_ref[0, 0], p_cmp)

    n_keys = kw_ref.shape[2]
    wlen = min(NSA_WIN + tq, n_keys)
    wstart = pl.multiple_of(jnp.minimum(jnp.maximum(q0 - NSA_WIN, 0), n_keys - wlen), tkw)
    sc = _dot(kw_ref[0, 0, pl.ds(wstart, wlen), :], qrt)
    rel = t_lane - (wstart + lax.broadcasted_iota(jnp.int32, (wlen, 1), 0))
    sc = jnp.where((rel >= 0) & (rel < NSA_WIN), sc, MASKED)
    p_win = jnp.exp(sc - jnp.max(sc, axis=0, keepdims=True))
    o_win = (_dot(vwt_ref[0, 0, :, pl.ds(wstart, wlen)], p_win)
             / jnp.sum(p_win, axis=0, keepdims=True))

    imp = p_cmp[:, 0:tq]
    for h in range(1, hp):
        imp = imp + p_cmp[:, h * tq:(h + 1) * tq]
    imp = imp[:nch] + imp[nch:]
    blk = lax.broadcasted_iota(jnp.int32, (nsl, tq), 0)
    cur = t_q // NSA_SEL
    forced = (blk == 0) | (blk == cur) | (blk == cur - 1)
    score = jnp.where(blk > cur, -jnp.inf, jnp.where(forced, NSA_FORCE, imp))
    bias = (_topk_mask_rows(score, NSA_TOPN) - 1.0) * (-MASKED)
    bias_ref[...] = jnp.concatenate([bias] * hp, axis=1)

    init = (jnp.full((1, r), MASKED, F32), jnp.zeros((1, r), F32), jnp.zeros((NSA_DH, r), F32))

    def slc_scores(kt):
        start = pl.multiple_of(kt * tk, tk)
        sc = _dot(ks_ref[0, 0, pl.ds(start, tk), :], qrt)
        brow = bias_ref[pl.ds(pl.multiple_of(kt * per_tile, per_tile), per_tile), :]
        sc = jnp.concatenate([sc[j * NSA_SEL:(j + 1) * NSA_SEL] + brow[j:j + 1]
                              for j in range(per_tile)], axis=0)
        return sc, vst_ref[0, 0, :, pl.ds(start, tk)]

    def slc_body(kt, carry):
        sc, vt = slc_scores(kt)
        return _softmax_step_cols(carry, sc, vt)

    kd = q0 // tk
    carry = lax.fori_loop(0, kd, slc_body, init)
    sc, vt = slc_scores(kd)
    kpos = kd * tk + lax.broadcasted_iota(jnp.int32, (tk, 1), 0)
    carry = _softmax_step_cols(carry, jnp.where(kpos <= t_lane, sc, MASKED), vt)
    o_slc = carry[2] / carry[1]

    sig = jax.nn.sigmoid(sm_ref[...].T)
    gw = N_BRANCH * hp
    gates = jnp.where(g == 0, sig[SM_NG:SM_NG + gw], sig[SM_NG + gw:SM_NG + 2 * gw])
    outs = []
    for h in range(hp):
        cols = slice(h * tq, (h + 1) * tq)
        outs.append(gates[3 * h:3 * h + 1] * o_cmp[:, cols]
                    + gates[3 * h + 1:3 * h + 2] * o_slc[:, cols]
                    + gates[3 * h + 2:3 * h + 3] * o_win[:, cols])
    o_ref[...] = jnp.concatenate(outs, axis=0).T


def _nsa_prompt_attn(qc, qr, kc, vct, ks, vst, kw, vwt, z, batch, t, tq, tk, tkw):
    nq = t // tq
    gw = NSA_HEADS * NSA_DH // NSA_KV
    assert tk % (SUBLANES * NSA_SEL) == 0 and tq == tkw and tk % tq == 0 and t % tk == 0
    rowblk = lambda b, g, i: b * nq + i
    full = lambda a: pl.BlockSpec((1, 1) + a.shape[2:], lambda b, g, i: (b, g, 0, 0))
    q_spec = pl.BlockSpec((tq, gw), lambda b, g, i: (rowblk(b, g, i), g))
    return pl.pallas_call(
        functools.partial(_nsa_prompt_kernel, tq=tq, tk=tk, tkw=tkw),
        grid=(batch, NSA_KV, nq),
        in_specs=[q_spec, q_spec] + [full(a) for a in (kc, vct, ks, vst, kw, vwt)]
        + [pl.BlockSpec((tq, LANES), lambda b, g, i: (rowblk(b, g, i), Z_SMALL // LANES))],
        out_specs=pl.BlockSpec((tq, gw), lambda b, g, i: (rowblk(b, g, i), g)),
        out_shape=jax.ShapeDtypeStruct((batch * t, NSA_HEADS * NSA_DH), F32),
        scratch_shapes=[pltpu.VMEM((t // NSA_SEL, NSA_HEADS // NSA_KV * tq), F32)],
        compiler_params=_params(("parallel", "parallel", "arbitrary")),
    )(qc, qr, kc, vct, ks, vst, kw, vwt, z)


def _softmax_step_vt(carry, s, vt):
    m, l, acc = carry
    m_new = jnp.maximum(m, jnp.max(s, axis=-1, keepdims=True))
    m_safe = jnp.where(m_new == -jnp.inf, 0.0, m_new)
    p = jnp.exp(s - m_safe)
    alpha = jnp.exp(m - m_safe)
    l = alpha * l + jnp.sum(p, axis=-1, keepdims=True)
    acc = alpha * acc + _dot_nt(p, vt)
    return m_new, l, acc


def _nsa_decode_kernel(pt_ref, *refs, tq, pos0, wrel0, n_keys):
    n_in = SLC_PAGES_PER_STEP
    qc_ref, qr_ref, kc_ref, vc_ref = refs[0:4]
    pages = refs[4:4 + n_in]
    (kw_ref, vw_ref, kt_ref, vt_ref, sm_ref, o_ref,
     m_ref, l_ref, acc_ref, ocmp_ref, sel_ref) = refs[4 + n_in:]
    hp = NSA_HEADS // NSA_KV
    r = hp * tq
    gwid = hp * NSA_DH
    step = pl.program_id(1)
    n_steps = pl.num_programs(1)
    tk = n_in * PAGE_SIZE
    per_tile = tk // NSA_SEL
    n_tiles = sel_ref.shape[1]
    ncp = kc_ref.shape[2]
    nch = ncp // 2
    nsl = -(-(n_tiles * per_tile) // LANES) * LANES
    kv_w = NSA_KV * NSA_DH

    def stack_heads(ref, g):
        x = ref[:, g * gwid:(g + 1) * gwid]
        return jnp.concatenate([x[:, h * NSA_DH:(h + 1) * NSA_DH] for h in range(hp)],
                               axis=0).astype(BF16)

    t_q = pos0 + lax.broadcasted_iota(jnp.int32, (tq, 1), 0)
    t_row = jnp.concatenate([t_q] * hp, axis=0)
    expand = (lax.broadcasted_iota(jnp.int32, (per_tile, tk), 0)
              == lax.broadcasted_iota(jnp.int32, (per_tile, tk), 1) // NSA_SEL).astype(BF16)

    def picked_rows(g, tile, width):
        pk = jnp.dot(sel_ref[g, tile].astype(BF16), expand[:, :width], preferred_element_type=F32)
        return jnp.concatenate([pk] * hp, axis=0)

    @pl.when(step == 0)
    def _():
        ccol = lax.broadcasted_iota(jnp.int32, (r, ncp), 1)
        cvis = (2 * (ccol % nch) + ccol // nch + 1) * NSA_CMP <= t_row + 1
        blk = lax.broadcasted_iota(jnp.int32, (tq, nsl), 1)
        cur = t_q // NSA_SEL
        forced = (blk == 0) | (blk == cur) | (blk == cur - 1)
        group_scores = []
        for g in range(NSA_KV):
            s = jnp.where(cvis, _dot_nt(stack_heads(qc_ref, g), kc_ref[0, g]), -jnp.inf)
            m = jnp.max(s, axis=-1, keepdims=True)
            e = jnp.exp(s - jnp.where(m == -jnp.inf, 0.0, m))
            p_cmp = e / jnp.maximum(jnp.sum(e, axis=-1, keepdims=True), 1e-30)
            ocmp_ref[g] = _dot(p_cmp, vc_ref[0, g])
            imp = p_cmp[0:tq]
            for h in range(1, hp):
                imp = imp + p_cmp[h * tq:(h + 1) * tq]
            imp = imp[:, :nch] + imp[:, nch:]
            if nsl > nch:
                imp = jnp.concatenate([imp, jnp.zeros((tq, nsl - nch), F32)], axis=1)
            group_scores.append(jnp.where(blk > cur, -jnp.inf, jnp.where(forced, NSA_FORCE, imp)))
        sel_all = _topk_mask(jnp.concatenate(group_scores, axis=0), NSA_TOPN)
        for g in range(NSA_KV):
            sel = sel_all[g * tq:(g + 1) * tq]
            for j in range(n_tiles):
                sel_ref[g, j] = sel[:, j * per_tile:(j + 1) * per_tile]
            m_ref[g] = jnp.full((r, 1), -jnp.inf, F32)
            l_ref[g] = jnp.zeros((r, 1), F32)
            acc_ref[g] = jnp.zeros((r, NSA_DH), F32)

    kpos = step * tk + lax.broadcasted_iota(jnp.int32, (1, tk), 1)
    scores, values = [], []
    for g in range(NSA_KV):
        k_t = jnp.concatenate([p[0, g * NSA_DH:(g + 1) * NSA_DH, :] for p in pages], axis=1)
        values.append(jnp.concatenate(
            [p[0, kv_w + g * NSA_DH:kv_w + (g + 1) * NSA_DH, :] for p in pages], axis=1))
        scores.append(_dot(stack_heads(qr_ref, g), k_t))
    picked = [picked_rows(g, step, tk) for g in range(NSA_KV)]
    for g in range(NSA_KV):
        ok = (picked[g] > 0.5) & (kpos <= t_row)
        m, l, acc = _softmax_step_vt((m_ref[g], l_ref[g], acc_ref[g]),
                                     jnp.where(ok, scores[g], -jnp.inf), values[g])
        m_ref[g] = m
        l_ref[g] = l
        acc_ref[g] = acc

    @pl.when(step == n_steps - 1)
    def _():
        sig = jax.nn.sigmoid(sm_ref[...])
        tw = kt_ref.shape[2]
        tpos = n_keys + lax.broadcasted_iota(jnp.int32, (1, tw), 1)
        wpos = pos0 - wrel0 + lax.broadcasted_iota(jnp.int32, (1, kw_ref.shape[2]), 1)
        rel = t_row - wpos
        wok = (rel >= 0) & (rel < NSA_WIN) & (wpos >= 0)
        qr4s = [stack_heads(qr_ref, g) for g in range(NSA_KV)]
        tail_s = [_dot_nt(qr4s[g], kt_ref[0, g]) for g in range(NSA_KV)]
        win_s = [_dot_nt(qr4s[g], kw_ref[0, g]) for g in range(NSA_KV)]
        for g in range(NSA_KV):
            ok = (picked_rows(g, n_tiles - 1, tw) > 0.5) & (tpos <= t_row)
            carry = _softmax_step((m_ref[g], l_ref[g], acc_ref[g]),
                                  jnp.where(ok, tail_s[g], -jnp.inf), vt_ref[0, g])
            o_slc = carry[2] / jnp.maximum(carry[1], 1e-30)
            s = jnp.where(wok, win_s[g], -jnp.inf)
            m = jnp.max(s, axis=-1, keepdims=True)
            e = jnp.exp(s - jnp.where(m == -jnp.inf, 0.0, m))
            o_win = _dot(e, vw_ref[0, g]) / jnp.maximum(jnp.sum(e, axis=-1, keepdims=True), 1e-30)
            o_cmp = ocmp_ref[g]
            base = SM_NG + g * N_BRANCH * hp
            outs = []
            for h in range(hp):
                rows = slice(h * tq, (h + 1) * tq)
                c0 = base + N_BRANCH * h
                outs.append(sig[:, c0:c0 + 1] * o_cmp[rows] + sig[:, c0 + 1:c0 + 2] * o_slc[rows]
                            + sig[:, c0 + 2:c0 + 3] * o_win[rows])
            o_ref[:, g * gwid:(g + 1) * gwid] = jnp.concatenate(outs, axis=1)


def _nsa_decode_attn(qc, qr, kc, vc, cache_t, page_table, kw, vw, ktail, vtail, z, tq, pos0, wrel0):
    batch, n_pages = page_table.shape
    n_keys = n_pages * PAGE_SIZE
    steps = n_pages // SLC_PAGES_PER_STEP
    hp = NSA_HEADS // NSA_KV
    r = hp * tq
    qw = NSA_HEADS * NSA_DH
    per_tile = SLC_PAGES_PER_STEP * PAGE_SIZE // NSA_SEL
    assert pos0 == n_keys and ktail.shape[2] <= per_tile * NSA_SEL
    per_b = lambda a: pl.BlockSpec((1,) + a.shape[1:], lambda b, s, pt: (b, 0, 0, 0))
    q_spec = pl.BlockSpec((tq, qw), lambda b, s, pt: (b, 0))
    return pl.pallas_call(
        functools.partial(_nsa_decode_kernel, tq=tq, pos0=pos0, wrel0=wrel0, n_keys=n_keys),
        grid_spec=pltpu.PrefetchScalarGridSpec(
            num_scalar_prefetch=1, grid=(batch, steps),
            in_specs=[q_spec, q_spec, per_b(kc), per_b(vc)]
            + _page_specs((1, ROW_W, PAGE_SIZE), n_pages, SLC_PAGES_PER_STEP)
            + [per_b(kw), per_b(vw), per_b(ktail), per_b(vtail),
               pl.BlockSpec((tq, LANES), lambda b, s, pt: (b, Z_SMALL // LANES))],
            out_specs=pl.BlockSpec((tq, qw), lambda b, s, pt: (b, 0)),
            scratch_shapes=[pltpu.VMEM((NSA_KV, r, 1), F32), pltpu.VMEM((NSA_KV, r, 1), F32),
                            pltpu.VMEM((NSA_KV, r, NSA_DH), F32),
                            pltpu.VMEM((NSA_KV, r, NSA_DH), F32),
                            pltpu.VMEM((NSA_KV, steps + 1, tq, per_tile), F32)]),
        out_shape=jax.ShapeDtypeStruct((batch * tq, qw), F32),
        compiler_params=_params(("parallel", "arbitrary")),
    )(page_table.reshape(-1), qc, qr, kc, vc, *([cache_t] * SLC_PAGES_PER_STEP),
      kw, vw, ktail, vtail, z)


def _mem_attn_kernel(q_ref, kv_ref, o_ref):
    hw = MEM_HEADS * MEM_DH
    for h in range(MEM_HEADS):
        q = q_ref[:, h * MEM_DH:(h + 1) * MEM_DH] * (MEM_DH ** -0.5)
        k = kv_ref[0, :, h * MEM_DH:(h + 1) * MEM_DH]
        v = kv_ref[0, :, hw + h * MEM_DH:hw + (h + 1) * MEM_DH]
        s = _dot_nt(q, k)
        e = jnp.exp(s - jnp.max(s, axis=-1, keepdims=True))
        p = e / jnp.sum(e, axis=-1, keepdims=True)
        o_ref[:, h * MEM_DH:(h + 1) * MEM_DH] = _dot(p, v)


def _mem_attn(z, kv, batch, t, tm):
    nt = t // tm
    hw = MEM_HEADS * MEM_DH
    return pl.pallas_call(
        _mem_attn_kernel,
        grid=(batch, nt),
        in_specs=[pl.BlockSpec((tm, hw), lambda b, j: (b * nt + j, Z_MQ // hw)),
                  pl.BlockSpec((1,) + kv.shape[1:], lambda b, j: (b, 0, 0))],
        out_specs=pl.BlockSpec((tm, hw), lambda b, j: (b * nt + j, 0)),
        out_shape=jax.ShapeDtypeStruct((batch * t, hw), F32),
        compiler_params=_params(("parallel", "parallel")),
    )(z, kv)


def _merge_kernel(x_ref, mg_ref, og_ref, on_ref, om_ref, wg_ref, wn_ref, wm_ref, wo_ref,
                  gf_ref, wq_ref, x1_ref, h2_ref, qp_ref):
    d = D_MODEL
    mix = (jax.nn.sigmoid(mg_ref[:, 0:d]) * _dot(og_ref[...], wg_ref[...])
           + jax.nn.sigmoid(mg_ref[:, d:2 * d]) * _dot(on_ref[...], wn_ref[...])
           + jax.nn.sigmoid(mg_ref[:, 2 * d:3 * d]) * _dot(om_ref[...], wm_ref[...]))
    x1 = x_ref[...] + _dot(mix, wo_ref[...])
    x1_ref[...] = x1
    h2 = _rms(x1, gf_ref[...]).astype(BF16)
    h2_ref[...] = h2
    qp_ref[...] = jnp.dot(h2, wq_ref[...], preferred_element_type=F32)


def _merge(x, z, o_gdn, o_nsa, o_mem, wg, wn, wm, wo, norm_ffn, wq, tm):
    n, d = x.shape
    qw = wq.shape[1]
    row = lambda w: pl.BlockSpec((tm, w), lambda i: (i, 0))
    const = lambda a: pl.BlockSpec(a.shape, lambda i: (0, 0))
    return pl.pallas_call(
        _merge_kernel,
        grid=(n // tm,),
        in_specs=[row(d), pl.BlockSpec((tm, N_BRANCH * d), lambda i: (i, Z_MG)),
                  row(o_gdn.shape[1]), row(o_nsa.shape[1]), row(o_mem.shape[1]),
                  const(wg), const(wn), const(wm), const(wo),
                  pl.BlockSpec((1, d), lambda i: (0, 0)), const(wq)],
        out_specs=[row(d), row(d), row(qw)],
        out_shape=[jax.ShapeDtypeStruct((n, d), F32), jax.ShapeDtypeStruct((n, d), BF16),
                   jax.ShapeDtypeStruct((n, qw), F32)],
        compiler_params=_params(("parallel",)),
    )(x, z, o_gdn, o_nsa, o_mem, wg, wn, wm, wo, norm_ffn.reshape(1, d), wq)


PEER_RANKS = PEER_TOPK + 1


RANK_NONE = 64.0


def _top_values(s, n):
    vals = []
    rank = jnp.full(s.shape, RANK_NONE, F32)
    for k in range(n):
        m = jnp.max(s, axis=0, keepdims=True)
        vals.append(m)
        hit = s >= m
        rank = jnp.where(hit, float(k + 1), rank)
        s = jnp.where(hit, -jnp.inf, s)
    return vals, rank


PEER_PAIRS = [(i, j) for i in range(1, PEER_RANKS + 1) for j in range(1, PEER_RANKS // i + 1)]


def _pair_selectors():
    rows = -(-len(PEER_PAIRS) // SUBLANES) * SUBLANES
    cols = -(-PEER_RANKS // SUBLANES) * SUBLANES
    pa = np.zeros((rows, cols), np.float32)
    pb = np.zeros((rows, cols), np.float32)
    for p, (i, j) in enumerate(PEER_PAIRS):
        pa[p, i - 1] = 1.0
        pb[p, j - 1] = 1.0
    return jnp.asarray(pa), jnp.asarray(pb)


def _peer_route_kernel(qp_ref, sk_ref, pa_ref, pb_ref, r2_ref, e2_ref, nb_ref, e1_ref):
    half = PEER_DKEY // 2
    nt = (((1,), (1,)), ((), ()))
    for h in range(PEER_HEADS):
        qa = qp_ref[:, h * PEER_DKEY:h * PEER_DKEY + half]
        qb = qp_ref[:, h * PEER_DKEY + half:(h + 1) * PEER_DKEY]
        s1 = lax.dot_general(sk_ref[0], qa, nt, precision=HIGHEST, preferred_element_type=F32)
        s2 = lax.dot_general(sk_ref[1], qb, nt, precision=HIGHEST, preferred_element_type=F32)
        a, _ = _top_values(s1, PEER_RANKS)
        b, rank2 = _top_values(s2, PEER_RANKS)
        pad = jnp.full((pa_ref.shape[1] - PEER_RANKS, s1.shape[1]), MASKED, F32)
        a_rows = jnp.maximum(jnp.concatenate(a + [pad], axis=0), MASKED)
        b_rows = jnp.maximum(jnp.concatenate(b + [pad], axis=0), MASKED)
        cand = _dot_hi(pa_ref[...], a_rows) + _dot_hi(pb_ref[...], b_rows)
        prow = lax.broadcasted_iota(jnp.int32, cand.shape, 0)
        cand = jnp.where(prow < len(PEER_PAIRS), cand, -jnp.inf)
        work = cand
        ranked = []
        for _ in range(PEER_RANKS):
            m = jnp.max(work, axis=0, keepdims=True)
            ranked.append(m)
            work = jnp.where(work >= m, -jnp.inf, work)
        tau = 0.5 * (ranked[PEER_TOPK - 1] + ranked[PEER_TOPK])
        top = a[0] + b[0]
        zsum = jnp.sum(jnp.where(cand >= tau, jnp.exp(cand - top), 0.0), axis=0, keepdims=True)
        th = tau - s1
        count = jnp.zeros(s1.shape, F32)
        for bj in b:
            count = count + jnp.where(bj >= th, 1.0, 0.0)
        r2_ref[h] = rank2.astype(BF16)
        e2_ref[h] = jnp.exp(s2 - b[0]).astype(BF16)
        nb_ref[h] = count
        e1_ref[h] = jnp.exp(s1 - a[0]) / zsum


def _peer_route(qp, subkeys, tt):
    n = qp.shape[0]
    pa, pb = _pair_selectors()
    shape = lambda dt: jax.ShapeDtypeStruct((PEER_HEADS, PEER_NKEYS, n), dt)
    spec = pl.BlockSpec((PEER_HEADS, PEER_NKEYS, tt), lambda i: (0, 0, i))
    return pl.pallas_call(
        _peer_route_kernel,
        grid=(n // tt,),
        in_specs=[pl.BlockSpec((tt, qp.shape[1]), lambda i: (i, 0)),
                  pl.BlockSpec(subkeys.shape, lambda i: (0, 0, 0)),
                  pl.BlockSpec(pa.shape, lambda i: (0, 0)),
                  pl.BlockSpec(pb.shape, lambda i: (0, 0))],
        out_specs=[spec, spec, spec, spec],
        out_shape=[shape(BF16), shape(BF16), shape(F32), shape(F32)],
        compiler_params=_params(("parallel",)),
    )(qp, subkeys, pa, pb)


PEER_SUB_BLOCKS = 4
PEER_BLOCK_ROWS = 16


def _blocked_transpose(v):
    eb = PEER_BLOCK_ROWS * PEER_NKEYS
    return jnp.transpose(v.reshape(v.shape[0] // eb, eb, v.shape[1]), (0, 2, 1))


def _peer_dense_kernel(ht_ref, u_ref, vt_ref, r2_ref, e2_ref, nb_ref, e1_ref, x1_ref, gf_ref,
                       y_ref, acc_ref, act_ref, wa_ref, r2s_ref, e2s_ref, rows_ref, *, jb):
    j = pl.program_id(1)
    tt = ht_ref.shape[1]

    @pl.when(j == 0)
    def _():
        acc_ref[...] = jnp.zeros(acc_ref.shape, F32)
        r2s_ref[...] = r2_ref[...]
        e2s_ref[...] = e2_ref[...]

    sub = jb // PEER_SUB_BLOCKS
    for sb in range(PEER_SUB_BLOCKS):
        srows = slice(sb * sub * PEER_NKEYS, (sb + 1) * sub * PEER_NKEYS)
        act_ref[srows, :] = _gelu(jnp.dot(u_ref[srows, :], ht_ref[...],
                                          preferred_element_type=F32)).astype(BF16)
    for jj in range(jb):
        i1 = j * jb + jj
        rows = slice(jj * PEER_NKEYS, (jj + 1) * PEER_NKEYS)
        for h in range(PEER_HEADS):
            rows_ref[jj, h:h + 1, :] = nb_ref[h, pl.ds(i1, 1), :]
            rows_ref[jj, PEER_HEADS + h:PEER_HEADS + h + 1, :] = e1_ref[h, pl.ds(i1, 1), :]
        for c in range(tt // LANES):
            cols = slice(c * LANES, (c + 1) * LANES)
            w = None
            for h in range(PEER_HEADS):
                count = rows_ref[jj, h:h + 1, cols].astype(BF16)
                e1 = rows_ref[jj, PEER_HEADS + h:PEER_HEADS + h + 1, cols].astype(BF16)
                picked = r2s_ref[h, :, cols] <= count
                term = jnp.where(picked, e2s_ref[h, :, cols], jnp.zeros((), BF16)) * e1
                w = term if w is None else w + term
            wa_ref[rows, cols] = w * act_ref[rows, cols]
    acc_ref[...] += jnp.dot(vt_ref[0], wa_ref[...], preferred_element_type=F32)

    @pl.when(j == pl.num_programs(1) - 1)
    def _():
        y_ref[...] = _rms(x1_ref[...] + acc_ref[...].T, gf_ref[...])


def _peer_dense(ht, u, vt, s2, e2, th, e1, x1, norm_final, tt, jb):
    d, n = ht.shape
    n_exp = u.shape[0]
    eb = jb * PEER_NKEYS
    route = pl.BlockSpec((PEER_HEADS, PEER_NKEYS, tt), lambda t, j: (0, 0, t))
    return pl.pallas_call(
        functools.partial(_peer_dense_kernel, jb=jb),
        grid=(n // tt, n_exp // eb),
        in_specs=[pl.BlockSpec((d, tt), lambda t, j: (0, t)),
                  pl.BlockSpec((eb, d), lambda t, j: (j, 0)),
                  pl.BlockSpec((1, d, eb), lambda t, j: (j, 0, 0)),
                  route, route, route, route,
                  pl.BlockSpec((tt, d), lambda t, j: (t, 0)),
                  pl.BlockSpec((1, d), lambda t, j: (0, 0))],
        out_specs=pl.BlockSpec((tt, d), lambda t, j: (t, 0)),
        out_shape=jax.ShapeDtypeStruct((n, d), F32),
        scratch_shapes=[pltpu.VMEM((d, tt), F32), pltpu.VMEM((eb, tt), BF16),
                        pltpu.VMEM((eb, tt), BF16),
                        pltpu.VMEM((PEER_HEADS, PEER_NKEYS, tt), BF16),
                        pltpu.VMEM((PEER_HEADS, PEER_NKEYS, tt), BF16),
                        pltpu.VMEM((jb, 2 * PEER_HEADS, tt), F32)],
        compiler_params=_params(("parallel", "arbitrary")),
    )(ht, u, vt, s2, e2, th, e1, x1, norm_final.reshape(1, d))


def _permute_w_in(w_in):
    sizes = (GDN_CONV_CH, GDN_HEADS * GDN_DV, GDN_HEADS, GDN_HEADS, NSA_HEADS * NSA_DH, NKV_W,
             N_BRANCH * NSA_HEADS, MEM_HEADS * MEM_DH, N_BRANCH * D_MODEL)
    qkv, zg, a, b, nq, nkv, ng, mq, mg = jnp.split(w_in, np.cumsum(sizes)[:-1].tolist(), axis=1)
    pad = jnp.zeros((w_in.shape[0], LANES - a.shape[1] - b.shape[1] - ng.shape[1]), w_in.dtype)
    return jnp.concatenate([mg, qkv, zg, nq, mq, nkv, a, b, ng, pad], axis=1).astype(BF16)


def _tokens_tile(n, pref):
    return pref if n % pref == 0 else n


def _layer(x, pos0, kv_mem, nsa_keys, gdn_state, conv_buf, w, peer_tt):
    batch, t, d = x.shape
    n = batch * t
    xf = x.reshape(n, d)
    z = _norm_matmul(xf, w['norm_attn'], w['w_in'], _tokens_tile(n, 256), 640)

    o_gdn, s_new, conv_new = _gdn(z, conv_buf, gdn_state, w['gdn_conv'], w['gdn_a_log'],
                                  w['gdn_dt_bias'], w['gdn_norm'], batch, t)

    prep_tm = _tokens_tile(t, 512)
    kv_dtype = BF16 if prep_tm % 16 == 0 else F32
    qc, qr, slc_rows, win_rows, ks, vs, kw, vw = _nsa_prep(
        z, pos0 + jnp.arange(t), batch, t, prep_tm, kv_dtype)
    cmp_rows = z[:, Z_NKV:Z_NKV + ROW_W]
    o_nsa = nsa_keys(z, qc, qr, cmp_rows, ks, vs, kw, vw)

    o_mem = _mem_attn(z, kv_mem, batch, t, _tokens_tile(t, 512))

    x1, h2, qp = _merge(xf, z, o_gdn, o_nsa, o_mem, w['w_gdn_out'], w['w_nsa_out'],
                        w['w_mem_out'], w['w_o'], w['norm_ffn'], w['peer_wq'],
                        _tokens_tile(n, 256))
    s2, e2, th, e1 = _peer_route(qp, w['peer_subkeys'], 256)
    y = _peer_dense(h2.T, w['peer_u'], w['peer_vt'], s2, e2, th, e1, x1, w['norm_final'],
                    peer_tt, PEER_BLOCK_ROWS)
    row5 = lambda a: a.reshape(batch, t, 2, NSA_KV, NSA_DH)
    return (y.reshape(batch, t, d), row5(cmp_rows), row5(slc_rows), row5(win_rows), s_new,
            conv_new)


def kernel(x_prompt, x_sample, cache_mem_kv, cache_cmp_kv, cache_slc_kv, cache_win_kv, state_gdn, state_conv, page_table, mem_prompt, norm_attn, w_in, gdn_conv, gdn_a_log, gdn_dt_bias, gdn_norm, nsa_cmp_pe, nsa_cmp_w1, nsa_cmp_w2, norm_mem, w_mem_kv, w_gdn_out, w_nsa_out, w_mem_out, w_o, norm_ffn, peer_wq, peer_subkeys, peer_u, peer_v, norm_final):
    depth = w_in.shape[0]
    assert depth == 1
    l = 0
    bp, seq, d = x_prompt.shape
    bs, tdec, _ = x_sample.shape
    n_pages = page_table.shape[1]
    past = n_pages * PAGE_SIZE
    assert past % NSA_CMP == 0 and tdec < NSA_CMP and seq % LANES == 0

    pe_big, w1big, w2big = _compress_weights(nsa_cmp_pe[l], nsa_cmp_w1[l], nsa_cmp_w2[l])
    w = dict(norm_attn=norm_attn[l], w_in=_permute_w_in(w_in[l]), gdn_conv=gdn_conv[l],
             gdn_a_log=gdn_a_log[l], gdn_dt_bias=gdn_dt_bias[l], gdn_norm=gdn_norm[l],
             w_gdn_out=w_gdn_out[l].astype(BF16), w_nsa_out=w_nsa_out[l].astype(BF16),
             w_mem_out=w_mem_out[l].astype(BF16), w_o=w_o[l].astype(BF16), norm_ffn=norm_ffn[l],
             peer_wq=peer_wq[l].astype(BF16), peer_subkeys=peer_subkeys[l],
             peer_u=peer_u[l].astype(BF16), peer_vt=_blocked_transpose(peer_v[l].astype(BF16)),
             norm_final=norm_final)

    mem_n = mem_prompt.shape[0] * mem_prompt.shape[1]
    kvm = _norm_matmul(mem_prompt.reshape(mem_n, d), norm_mem[l], w_mem_kv[l].astype(BF16),
                       _tokens_tile(mem_n, 256), 512).reshape(bp, mem_prompt.shape[1], -1)

    def prompt_keys(z, qc, qr, cmp_rows, ks, vs, kw, vw):
        n_blk = bp * seq // NSA_CMP
        cmp_out = _compress(cmp_rows.reshape(n_blk, NSA_CMP * ROW_W), pe_big, w1big, w2big,
                            _tokens_tile(n_blk, 128))
        kc, vc = _split_compressed(cmp_out, bp)
        tr = lambda a: jnp.transpose(a, (0, 1, 3, 2))
        return _nsa_prompt_attn(qc, qr, kc, tr(vc), ks, tr(vs), kw, tr(vw), z, bp, seq,
                                128, _tokens_tile(seq, 512), 128)

    yp, cmp_p, slc_p, win_p, gdn_p, conv_p = _layer(
        x_prompt, 0, kvm, prompt_keys,
        jnp.zeros((bp, GDN_HEADS, GDN_DK, GDN_DV), F32),
        jnp.zeros((bp, GDN_CONV - 1, GDN_CONV_CH), F32), w, 512)
    win_len_p = min(NSA_WIN, seq)
    win_p = win_p[:, seq - win_len_p:]

    n_pool = cache_cmp_kv.shape[1]
    feature_major = lambda c: jnp.transpose(c, (0, 2, 3, 4, 1)).reshape(n_pool, ROW_W, PAGE_SIZE)
    cache_cmp = feature_major(cache_cmp_kv[l])
    cache_slc = feature_major(cache_slc_kv[l])
    cache_win = cache_win_kv[l].reshape(bs, -1, ROW_W)
    wb = cache_win.shape[1]
    assert wb == NSA_WIN
    kv_w = NSA_KV * NSA_DH

    def pad_rows(a, rows):
        return jnp.pad(a, ((0, 0), (0, 0), (0, rows - a.shape[2]), (0, 0))).astype(BF16)

    def split_rows(rows):
        r = rows.reshape(bs, rows.shape[1], 2, NSA_KV, NSA_DH)
        return jnp.transpose(r[:, :, 0], (0, 2, 1, 3)), jnp.transpose(r[:, :, 1], (0, 2, 1, 3))

    def sample_keys(z, qc, qr, cmp_rows, ks_new, vs_new, kw_new, vw_new):
        cmp_out = _paged_compress(cache_cmp, page_table, pe_big, w1big, w2big)
        kc, vc = _split_compressed(cmp_out, bs)
        kwc, vwc = split_rows(cache_win)
        win_rows = wb + LANES
        kw = pad_rows(jnp.concatenate([kwc, kw_new], axis=2), win_rows)
        vw = pad_rows(jnp.concatenate([vwc, vw_new], axis=2), win_rows)
        return _nsa_decode_attn(qc, qr, kc, vc, cache_slc, page_table, kw, vw,
                                pad_rows(ks_new, LANES), pad_rows(vs_new, LANES), z,
                                tdec, past, wb)

    ys, cmp_s, slc_s, win_new, gdn_s, conv_s = _layer(
        x_sample, past, cache_mem_kv[l].reshape(bs, cache_mem_kv.shape[2], -1), sample_keys,
        state_gdn[l], state_conv[l], w, 256)
    win_all = jnp.concatenate([cache_win_kv[l], win_new], axis=1)
    win_s = win_all[:, win_all.shape[1] - min(NSA_WIN, past + tdec):]

    stack = lambda a: a[None]
    return (yp, ys, stack(kvm.reshape(bp, mem_prompt.shape[1], 2, MEM_HEADS, MEM_DH)),
            stack(cmp_p), stack(slc_p), stack(win_p), stack(gdn_p), stack(conv_p),
            stack(cmp_s), stack(slc_s), stack(win_s), stack(gdn_s), stack(conv_s))
```

```python
import functools
import math

import jax
import jax.numpy as jnp
import numpy as np
from jax import lax
from jax.experimental import pallas as pl
from jax.experimental.pallas import tpu as pltpu

F32 = jnp.float32
BF16 = jnp.bfloat16
HIGHEST = lax.Precision.HIGHEST

D_MODEL = 1024
PAGE_SIZE = 128
GDN_HEADS = 4
GDN_DK = 128
GDN_DV = 128
GDN_CONV = 4
GDN_CHUNK = 64
GDN_CONV_CH = GDN_HEADS * (2 * GDN_DK + GDN_DV)
NSA_HEADS = 8
NSA_KV = 2
NSA_DH = 64
NSA_CMP = 32
NSA_SEL = 64
NSA_TOPN = 16
NSA_WIN = 512
NSA_FORCE = 1e9
MEM_HEADS = 4
MEM_DH = 128
PEER_HEADS = 8
PEER_NKEYS = 128
PEER_DKEY = 256
PEER_TOPK = 16
N_BRANCH = 3
ROPE_THETA = 10000.0
EPS = 1e-6

LANES = 128
SUBLANES = 8
VMEM_LIMIT = 56 * 1024 * 1024

Z_MG = 0
Z_QKV = Z_MG + N_BRANCH * D_MODEL
Z_ZG = Z_QKV + GDN_CONV_CH
Z_NQ = Z_ZG + GDN_HEADS * GDN_DV
Z_MQ = Z_NQ + NSA_HEADS * NSA_DH
Z_NKV = Z_MQ + MEM_HEADS * MEM_DH
Z_SMALL = Z_NKV + 3 * 2 * NSA_KV * NSA_DH
Z_WIDTH = Z_SMALL + LANES
SM_A = 0
SM_B = GDN_HEADS
SM_NG = 2 * GDN_HEADS
NKV_W = 3 * 2 * NSA_KV * NSA_DH
ROW_W = 2 * NSA_KV * NSA_DH


def _params(sem, vmem=VMEM_LIMIT):
    return pltpu.CompilerParams(dimension_semantics=sem, vmem_limit_bytes=vmem)


def _dot(a, b):
    return jnp.dot(a.astype(BF16), b.astype(BF16), preferred_element_type=F32)


def _dot_nt(a, b):
    return lax.dot_general(a.astype(BF16), b.astype(BF16), (((1,), (1,)), ((), ())),
                           preferred_element_type=F32)


def _dot_hi(a, b):
    return jnp.dot(a, b, precision=HIGHEST, preferred_element_type=F32)


def _dot3(a, b):
    ah = a.astype(BF16)
    bh = b.astype(BF16)
    al = (a - ah.astype(F32)).astype(BF16)
    bl = (b - bh.astype(F32)).astype(BF16)
    d = lambda x, y: jnp.dot(x, y, preferred_element_type=F32)
    return d(ah, bh) + d(al, bh) + d(ah, bl)


def _rms(x, g):
    return x * lax.rsqrt(jnp.mean(x * x, axis=-1, keepdims=True) + EPS) * g


def _gelu(x):
    a = -2.0 * math.sqrt(2.0 / math.pi) * math.log2(math.e)
    return x / (1.0 + jnp.exp2(x * (a + (a * 0.044715) * (x * x))))


def _norm_matmul_kernel(x_ref, g_ref, w_ref, o_ref, *, col_chunk):
    yb = _rms(x_ref[...], g_ref[...]).astype(BF16)
    for c0 in range(0, o_ref.shape[1], col_chunk):
        o_ref[:, c0:c0 + col_chunk] = jnp.dot(yb, w_ref[:, c0:c0 + col_chunk],
                                              preferred_element_type=F32)


def _norm_matmul(x, g, w, tm, col_chunk):
    n, d = x.shape
    wc = w.shape[1]
    return pl.pallas_call(
        functools.partial(_norm_matmul_kernel, col_chunk=col_chunk),
        grid=(n // tm,),
        in_specs=[pl.BlockSpec((tm, d), lambda i: (i, 0)),
                  pl.BlockSpec((1, d), lambda i: (0, 0)),
                  pl.BlockSpec((d, wc), lambda i: (0, 0), pipeline_mode=pl.Buffered(1))],
        out_specs=pl.BlockSpec((tm, wc), lambda i: (i, 0)),
        out_shape=jax.ShapeDtypeStruct((n, wc), F32),
        compiler_params=_params(("parallel",)),
    )(x, g.reshape(1, d), w)


def _tri_inverse(lmats, c):
    row = lax.broadcasted_iota(jnp.int32, (c, c), 0)
    col = lax.broadcasted_iota(jnp.int32, (c, c), 1)
    ident = jnp.where(row == col, 1.0, 0.0)
    xs = [ident - lm for lm in lmats]
    ps = [_dot3(lm, lm) for lm in lmats]
    n = 2
    while n < c:
        xs = [x + _dot3(x, p) for x, p in zip(xs, ps)]
        n *= 2
        if n < c:
            ps = [_dot3(p, p) for p in ps]
    return xs


def _gdn_kernel(qkv_ref, zg_ref, sm_ref, buf_ref, s0_ref, cw_ref, alog_ref, dt_ref, gn_ref,
                o_ref, snew_ref, cnew_ref, ext_ref, s_ref, *, tb, n_chunks, nb):
    ci = pl.program_id(1)

    @pl.when(ci == 0)
    def _():
        for i in range(nb):
            ext_ref[i, 0:SUBLANES, :] = buf_ref[i]
            s_ref[i] = s0_ref[i]

    c = GDN_CHUNK
    row = lax.broadcasted_iota(jnp.int32, (c, c), 0)
    col = lax.broadcasted_iota(jnp.int32, (c, c), 1)
    tril = row >= col
    eye = row == col
    hk = GDN_HEADS * GDN_DK
    chains = []
    for i in range(nb):
        u, g_all, beta_all = _gdn_inputs(qkv_ref.at[i], sm_ref.at[i], cw_ref, alog_ref, dt_ref,
                                         cnew_ref.at[i], ext_ref.at[i], tb)
        gc_all = _dot_hi(jnp.where(tril, 1.0, 0.0), g_all)
        for h in range(GDN_HEADS):
            qh = u[:, h * GDN_DK:(h + 1) * GDN_DK]
            kh = u[:, hk + h * GDN_DK:hk + (h + 1) * GDN_DK]
            ch = dict(i=i, h=h, v=u[:, 2 * hk + h * GDN_DV:2 * hk + (h + 1) * GDN_DV])
            ch['q'] = qh * lax.rsqrt(jnp.sum(qh * qh, axis=-1, keepdims=True) + EPS) * (GDN_DK ** -0.5)
            ch['k'] = kh * lax.rsqrt(jnp.sum(kh * kh, axis=-1, keepdims=True) + EPS)
            ch['beta'] = beta_all[:, SM_B + h:SM_B + h + 1]
            gc = gc_all[:, SM_A + h:SM_A + h + 1]
            ch['gc'] = gc
            ch['gl'] = gc_all[c - 1:c, SM_A + h:SM_A + h + 1]
            gc_row = jnp.sum(jnp.where(eye, gc, 0.0), axis=0, keepdims=True)
            ch['decay'] = jnp.exp(jnp.where(tril, gc - gc_row, -jnp.inf))
            ch['kb'] = ch['k'] * ch['beta']
            ch['egc'] = jnp.exp(gc)
            chains.append(ch)
    lmats = [jnp.where(row > col, _dot_nt(ch['kb'], ch['k']) * ch['decay'], 0.0) for ch in chains]
    tinvs = _tri_inverse(lmats, c)
    uus = [_dot(t, ch['v'] * ch['beta']) for t, ch in zip(tinvs, chains)]
    wws = [_dot(t, ch['kb'] * ch['egc']) for t, ch in zip(tinvs, chains)]
    aqks = [_dot_nt(ch['q'], ch['k']) * ch['decay'] for ch in chains]
    states = [s_ref[ch['i'], ch['h']] for ch in chains]
    v_news = [uu - _dot(ww, s) for uu, ww, s in zip(uus, wws, states)]
    outs = [_dot(ch['q'] * ch['egc'], s) + _dot(aqk, vn)
            for ch, s, aqk, vn in zip(chains, states, aqks, v_news)]
    s_news = [s * jnp.exp(ch['gl']) + _dot((ch['k'] * jnp.exp(ch['gl'] - ch['gc'])).T, vn)
              for ch, s, vn in zip(chains, states, v_news)]
    for ch, o, s_new in zip(chains, outs, s_news):
        i, h = ch['i'], ch['h']
        s_ref[i, h] = s_new
        zh = zg_ref[i, :, h * GDN_DV:(h + 1) * GDN_DV]
        o_ref[i, :, h * GDN_DV:(h + 1) * GDN_DV] = (_rms(o[0:tb], gn_ref[...])
                                                    * (zh * jax.nn.sigmoid(zh)))

    @pl.when(ci == n_chunks - 1)
    def _():
        snew_ref[...] = s_ref[...]


def _gdn_inputs(qkv_ref, sm_ref, cw_ref, alog_ref, dt_ref, cnew_ref, ext_ref, tb):
    c = GDN_CHUNK
    if tb < c:
        ext_ref[SUBLANES + tb:, :] = jnp.zeros((c - tb, GDN_CONV_CH), F32)
    ext_ref[SUBLANES:SUBLANES + tb, :] = qkv_ref[...]
    cw = cw_ref[...]
    conv = cw[0:1] * ext_ref[SUBLANES - 3:SUBLANES - 3 + c, :]
    for j in range(1, GDN_CONV):
        conv = conv + cw[j:j + 1] * ext_ref[SUBLANES - 3 + j:SUBLANES - 3 + j + c, :]
    u = conv * jax.nn.sigmoid(conv)
    last_rows = ext_ref[tb:tb + SUBLANES, :]
    cnew_ref[...] = last_rows
    ext_ref[0:SUBLANES, :] = last_rows

    sm = sm_ref[...]
    if tb < c:
        sm = jnp.concatenate([sm, jnp.zeros((c - tb, LANES), F32)], axis=0)
    za = sm + dt_ref[...]
    softplus = jnp.maximum(za, 0.0) + jnp.log1p(jnp.exp(-jnp.abs(za)))
    g_all = -jnp.exp(alog_ref[...]) * softplus
    beta_all = jax.nn.sigmoid(sm)
    if tb < c:
        valid = lax.broadcasted_iota(jnp.int32, (c, 1), 0) < tb
        u = jnp.where(valid, u, 0.0)
        g_all = jnp.where(valid, g_all, 0.0)
        beta_all = jnp.where(valid, beta_all, 0.0)
    return u, g_all, beta_all


GDN_BATCH_PER_STEP = 2


def _gdn(z, conv_buf, s0, conv_w, a_log, dt_bias, gnorm, batch, t):
    c = GDN_CHUNK
    tb = min(t, c)
    n_chunks = t // tb
    nb = GDN_BATCH_PER_STEP if batch % GDN_BATCH_PER_STEP == 0 else 1
    assert tb % SUBLANES == 0 and n_chunks * tb == t and (tb == c or n_chunks == 1)
    buf8 = jnp.pad(conv_buf, ((0, 0), (SUBLANES - (GDN_CONV - 1), 0), (0, 0)))
    alog_row = jnp.zeros((1, LANES), F32).at[0, SM_A:SM_A + GDN_HEADS].set(a_log)
    dt_row = jnp.zeros((1, LANES), F32).at[0, SM_A:SM_A + GDN_HEADS].set(dt_bias)
    z3 = z.reshape(batch, t, Z_WIDTH)
    hv = GDN_HEADS * GDN_DV
    o, s_new, c_new = pl.pallas_call(
        functools.partial(_gdn_kernel, tb=tb, n_chunks=n_chunks, nb=nb),
        grid=(batch // nb, n_chunks),
        in_specs=[
            pl.BlockSpec((nb, tb, GDN_CONV_CH), lambda b, ci: (b, ci, Z_QKV // GDN_CONV_CH)),
            pl.BlockSpec((nb, tb, hv), lambda b, ci: (b, ci, Z_ZG // hv)),
            pl.BlockSpec((nb, tb, LANES), lambda b, ci: (b, ci, Z_SMALL // LANES)),
            pl.BlockSpec((nb, SUBLANES, GDN_CONV_CH), lambda b, ci: (b, 0, 0)),
            pl.BlockSpec((nb, GDN_HEADS, GDN_DK, GDN_DV), lambda b, ci: (b, 0, 0, 0)),
            pl.BlockSpec((GDN_CONV, GDN_CONV_CH), lambda b, ci: (0, 0)),
            pl.BlockSpec((1, LANES), lambda b, ci: (0, 0)),
            pl.BlockSpec((1, LANES), lambda b, ci: (0, 0)),
            pl.BlockSpec((1, GDN_DV), lambda b, ci: (0, 0)),
        ],
        out_specs=[
            pl.BlockSpec((nb, tb, hv), lambda b, ci: (b, ci, 0)),
            pl.BlockSpec((nb, GDN_HEADS, GDN_DK, GDN_DV), lambda b, ci: (b, 0, 0, 0)),
            pl.BlockSpec((nb, SUBLANES, GDN_CONV_CH), lambda b, ci: (b, 0, 0)),
        ],
        out_shape=[
            jax.ShapeDtypeStruct((batch, t, hv), F32),
            jax.ShapeDtypeStruct((batch, GDN_HEADS, GDN_DK, GDN_DV), F32),
            jax.ShapeDtypeStruct((batch, SUBLANES, GDN_CONV_CH), F32),
        ],
        scratch_shapes=[pltpu.VMEM((nb, SUBLANES + c, GDN_CONV_CH), F32),
                        pltpu.VMEM((nb, GDN_HEADS, GDN_DK, GDN_DV), F32)],
        compiler_params=_params(("parallel", "arbitrary")),
    )(z3, z3, z3, buf8, s0, conv_w, alog_row, dt_row, gnorm.reshape(1, GDN_DV))
    return o.reshape(batch * t, hv), s_new, c_new[:, SUBLANES - (GDN_CONV - 1):]


def _rope_tables(pos):
    half = NSA_DH // 2
    inv = jnp.power(ROPE_THETA, -jnp.arange(half, dtype=F32) / half)
    ang = pos.astype(F32)[:, None] * inv[None, :]
    cos, sin = jnp.cos(ang), jnp.sin(ang)
    cos_t = jnp.concatenate([cos, cos, cos, cos], axis=-1)
    sin_t = jnp.concatenate([-sin, sin, -sin, sin], axis=-1)
    return cos_t, sin_t


def _nsa_prep_kernel(nq_ref, nkv_ref, cos_ref, sin_ref,
                     qc_ref, qr_ref, cmp_ref, slc_ref, win_ref, ks_ref, vs_ref, kw_ref, vw_ref, *,
                     feature_major):
    def put_rows(ref, rows):
        if feature_major:
            ref[0] = rows.T
        else:
            ref[...] = rows

    put_rows(cmp_ref, nkv_ref[:, 0:ROW_W])
    cos = cos_ref[...]
    sin = sin_ref[...]
    lane = lax.broadcasted_iota(jnp.int32, cos.shape, 1)
    first_half = (lane % NSA_DH) < (NSA_DH // 2)

    def rope(x):
        swapped = jnp.where(first_half, pltpu.roll(x, LANES - NSA_DH // 2, 1),
                            pltpu.roll(x, NSA_DH // 2, 1))
        return x * cos + swapped * sin

    scale = NSA_DH ** -0.5
    for j in range(NSA_HEADS * NSA_DH // LANES):
        x = nq_ref[:, j * LANES:(j + 1) * LANES]
        qc_ref[:, j * LANES:(j + 1) * LANES] = x * scale
        qr_ref[:, j * LANES:(j + 1) * LANES] = rope(x) * scale

    kv_w = NSA_KV * NSA_DH
    for br, (row_ref, k_ref, v_ref) in enumerate(((slc_ref, ks_ref, vs_ref),
                                                   (win_ref, kw_ref, vw_ref))):
        base = (br + 1) * ROW_W
        kr = rope(nkv_ref[:, base:base + kv_w])
        v = nkv_ref[:, base + kv_w:base + 2 * kv_w]
        put_rows(row_ref, jnp.concatenate([kr, v], axis=1))
        for g in range(NSA_KV):
            k_ref[0, g] = kr[:, g * NSA_DH:(g + 1) * NSA_DH].astype(k_ref.dtype)
            v_ref[0, g] = v[:, g * NSA_DH:(g + 1) * NSA_DH].astype(v_ref.dtype)


def _nsa_prep(z, pos, batch, t, tm, kv_dtype):
    cos_t, sin_t = _rope_tables(pos)
    nt = t // tm
    rowblk = lambda b, j: b * nt + j
    kv_shape = jax.ShapeDtypeStruct((batch, NSA_KV, t, NSA_DH), kv_dtype)
    kv_spec = pl.BlockSpec((1, NSA_KV, tm, NSA_DH), lambda b, j: (b, 0, j, 0))
    qw = NSA_HEADS * NSA_DH
    feature_major = tm % LANES == 0
    if feature_major:
        row_spec = pl.BlockSpec((1, ROW_W, tm), lambda b, j: (b, 0, j))
        row_shape = jax.ShapeDtypeStruct((batch, ROW_W, t), F32)
    else:
        row_spec = pl.BlockSpec((tm, ROW_W), lambda b, j: (rowblk(b, j), 0))
        row_shape = jax.ShapeDtypeStruct((batch * t, ROW_W), F32)
    return pl.pallas_call(
        functools.partial(_nsa_prep_kernel, feature_major=feature_major),
        grid=(batch, nt),
        in_specs=[pl.BlockSpec((tm, qw), lambda b, j: (rowblk(b, j), Z_NQ // qw)),
                  pl.BlockSpec((tm, NKV_W), lambda b, j: (rowblk(b, j), Z_NKV // NKV_W)),
                  pl.BlockSpec((tm, LANES), lambda b, j: (j, 0)),
                  pl.BlockSpec((tm, LANES), lambda b, j: (j, 0))],
        out_specs=[pl.BlockSpec((tm, qw), lambda b, j: (rowblk(b, j), 0)),
                   pl.BlockSpec((tm, qw), lambda b, j: (rowblk(b, j), 0)),
                   row_spec, row_spec, row_spec,
                   kv_spec, kv_spec, kv_spec, kv_spec],
        out_shape=[jax.ShapeDtypeStruct((batch * t, qw), F32),
                   jax.ShapeDtypeStruct((batch * t, qw), F32),
                   row_shape, row_shape, row_shape,
                   kv_shape, kv_shape, kv_shape, kv_shape],
        compiler_params=_params(("parallel", "parallel")),
    )(z, z, cos_t, sin_t)


def _compress_weights(pe, w1, w2):
    eye = jnp.eye(2, dtype=F32)
    w1r = w1.reshape(2, NSA_CMP, NSA_DH, NSA_DH)
    w1big = jnp.einsum('srde,st,gh->rsgdthe', w1r, eye, eye).reshape(NSA_CMP * ROW_W, ROW_W)
    w2big = jnp.einsum('sed,st,gh->sgethd', w2, eye, eye).reshape(ROW_W, ROW_W)
    pe_big = jnp.broadcast_to(jnp.transpose(pe, (1, 0, 2))[:, :, None, :],
                              (NSA_CMP, 2, NSA_KV, NSA_DH)).reshape(1, NSA_CMP * ROW_W)
    return pe_big, w1big.astype(BF16), w2big.astype(BF16)


def _compress_kernel(x_ref, pe_ref, w1_ref, w2_ref, o_ref):
    x = (x_ref[...] + pe_ref[...]).astype(BF16)
    hid = _gelu(jnp.dot(x, w1_ref[...], preferred_element_type=F32))
    o_ref[...] = jnp.dot(hid.astype(BF16), w2_ref[...], preferred_element_type=F32)


def _compress(rows, pe_big, w1big, w2big, tm):
    n, kdim = rows.shape
    return pl.pallas_call(
        _compress_kernel,
        grid=(n // tm,),
        in_specs=[pl.BlockSpec((tm, kdim), lambda i: (i, 0)),
                  pl.BlockSpec((1, kdim), lambda i: (0, 0)),
                  pl.BlockSpec((kdim, ROW_W), lambda i: (0, 0)),
                  pl.BlockSpec((ROW_W, ROW_W), lambda i: (0, 0))],
        out_specs=pl.BlockSpec((tm, ROW_W), lambda i: (i, 0)),
        out_shape=jax.ShapeDtypeStruct((n, ROW_W), F32),
        compiler_params=_params(("parallel",)),
    )(rows, pe_big, w1big, w2big)


def _split_compressed(cmp_out, batch):
    nb = cmp_out.shape[0] // batch
    x = cmp_out.reshape(batch, nb // 2, 2, 2, NSA_KV, NSA_DH)
    x = jnp.transpose(x, (3, 0, 4, 2, 1, 5)).reshape(2, batch, NSA_KV, nb, NSA_DH)
    return x[0].astype(BF16), x[1].astype(BF16)


CMP_PAGES_PER_STEP = 64
SLC_PAGES_PER_STEP = 16
BLOCKS_PER_PAGE = PAGE_SIZE // NSA_CMP


def _page_specs(block, n_pages, per_step):
    def spec(j):
        return pl.BlockSpec(block, lambda b, s, pt: (pt[b * n_pages + s * per_step + j], 0, 0))
    return [spec(j) for j in range(per_step)]


CMP_ROW_PITCH = NSA_CMP + 4


def _paged_compress_kernel(pt_ref, *refs):
    n_in = CMP_PAGES_PER_STEP
    pe_ref, w1_ref, w2_ref, o_ref, x_ref = refs[n_in:]
    m = n_in * BLOCKS_PER_PAGE
    n_slabs = ROW_W // LANES
    for j, p_ref in enumerate(refs[:n_in]):
        x = p_ref[0].T
        for n in range(BLOCKS_PER_PAGE):
            base = (j * BLOCKS_PER_PAGE + n) * CMP_ROW_PITCH
            for sl in range(n_slabs):
                x_ref[sl, base:base + NSA_CMP, :] = x[n * NSA_CMP:(n + 1) * NSA_CMP,
                                                      sl * LANES:(sl + 1) * LANES]
    acc = jnp.zeros((m, ROW_W), F32)
    for r in range(NSA_CMP):
        lhs = jnp.concatenate([x_ref[sl, pl.ds(r, m, stride=CMP_ROW_PITCH), :]
                               for sl in range(n_slabs)], axis=1)
        lhs = (lhs + pe_ref[r:r + 1, :]).astype(BF16)
        acc = acc + jnp.dot(lhs, w1_ref[r], preferred_element_type=F32)
    o_ref[...] = jnp.dot(_gelu(acc).astype(BF16), w2_ref[...], preferred_element_type=F32)


def _paged_compress(cache_t, page_table, pe_big, w1big, w2big):
    batch, n_pages = page_table.shape
    rows = CMP_PAGES_PER_STEP * BLOCKS_PER_PAGE
    steps = n_pages // CMP_PAGES_PER_STEP
    pe_rows = pe_big.reshape(NSA_CMP, ROW_W)
    w1_rows = w1big.reshape(NSA_CMP, ROW_W, ROW_W)
    const = lambda a: pl.BlockSpec(a.shape, lambda b, s, pt: (0,) * a.ndim)
    return pl.pallas_call(
        _paged_compress_kernel,
        grid_spec=pltpu.PrefetchScalarGridSpec(
            num_scalar_prefetch=1, grid=(batch, steps),
            in_specs=_page_specs((1, ROW_W, PAGE_SIZE), n_pages, CMP_PAGES_PER_STEP)
            + [const(pe_rows), const(w1_rows), const(w2big)],
            out_specs=pl.BlockSpec((rows, ROW_W), lambda b, s, pt: (b * steps + s, 0)),
            scratch_shapes=[pltpu.VMEM((ROW_W // LANES, rows * CMP_ROW_PITCH, LANES), F32)]),
        out_shape=jax.ShapeDtypeStruct((batch * n_pages * BLOCKS_PER_PAGE, ROW_W), F32),
        compiler_params=_params(("parallel", "parallel")),
    )(page_table.reshape(-1), *([cache_t] * CMP_PAGES_PER_STEP), pe_rows, w1_rows, w2big)


def _topk_mask(score, k):
    n = score.shape[-1]
    lane = lax.broadcasted_iota(jnp.int32, score.shape, 1).astype(F32)
    sel = jnp.zeros(score.shape, F32)
    for _ in range(k):
        m = jnp.max(score, axis=-1, keepdims=True)
        idx = jnp.min(jnp.where(score == m, lane, float(n)), axis=-1, keepdims=True)
        pick = lane == idx
        sel = jnp.where(pick, 1.0, sel)
        score = jnp.where(pick, -jnp.inf, score)
    return sel


def _softmax_step(carry, s, v):
    m, l, acc = carry
    m_new = jnp.maximum(m, jnp.max(s, axis=-1, keepdims=True))
    m_safe = jnp.where(m_new == -jnp.inf, 0.0, m_new)
    p = jnp.exp(s - m_safe)
    alpha = jnp.exp(m - m_safe)
    l = alpha * l + jnp.sum(p, axis=-1, keepdims=True)
    acc = alpha * acc + _dot(p, v)
    return m_new, l, acc


MASKED = -1e30


def _topk_mask_rows(score, k):
    n = score.shape[0]
    row = lax.broadcasted_iota(jnp.int32, score.shape, 0).astype(F32)
    sel = jnp.zeros(score.shape, F32)
    for _ in range(k):
        m = jnp.max(score, axis=0, keepdims=True)
        idx = jnp.min(jnp.where(score == m, row, float(n)), axis=0, keepdims=True)
        pick = row == idx
        sel = jnp.where(pick, 1.0, sel)
        score = jnp.where(pick, -jnp.inf, score)
    return sel


def _softmax_step_cols(carry, s, vt):
    m, l, acc = carry
    m_new = jnp.maximum(m, jnp.max(s, axis=0, keepdims=True))
    p = jnp.exp(s - m_new)
    alpha = jnp.exp(m - m_new)
    l = alpha * l + jnp.sum(p, axis=0, keepdims=True)
    acc = alpha * acc + _dot(vt, p)
    return m_new, l, acc


def _nsa_prompt_kernel(qc_ref, qr_ref, kc_ref, vct_ref, ks_ref, vst_ref, kw_ref, vwt_ref, sm_ref,
                       o_ref, bias_ref, *, tq, tk, tkw):
    hp = NSA_HEADS // NSA_KV
    r = hp * tq
    g = pl.program_id(1)
    qi = pl.program_id(2)
    q0 = qi * tq
    ncp = kc_ref.shape[2]
    nch = ncp // 2
    nsl = bias_ref.shape[0]
    per_tile = tk // NSA_SEL

    def heads_on_lanes(ref):
        xt = ref[...].T
        return jnp.concatenate([xt[h * NSA_DH:(h + 1) * NSA_DH] for h in range(hp)],
                               axis=1).astype(BF16)

    qct = heads_on_lanes(qc_ref)
    qrt = heads_on_lanes(qr_ref)
    t_q = q0 + lax.broadcasted_iota(jnp.int32, (1, tq), 1)
    t_lane = jnp.concatenate([t_q] * hp, axis=1)

    s = _dot(kc_ref[0, 0], qct)
    crow = lax.broadcasted_iota(jnp.int32, (ncp, 1), 0)
    cblk = 2 * (crow % nch) + crow // nch
    s = jnp.where((cblk + 1) * NSA_CMP <= t_lane + 1, s, -jnp.inf)
    m = jnp.max(s, axis=0, keepdims=True)
    e = jnp.exp(s - jnp.where(m == -jnp.inf, 0.0, m))
    p_cmp = e / jnp.maximum(jnp.sum(e, axis=0, keepdims=True), 1e-30)
    o_cmp = _dot(vct_ref[0, 0], p_cmp)

    n_keys = kw_ref.shape[2]
    wlen = min(NSA_WIN + tq, n_keys)
    wstart = pl.multiple_of(jnp.minimum(jnp.maximum(q0 - NSA_WIN, 0), n_keys - wlen), tkw)
    sc = _dot(kw_ref[0, 0, pl.ds(wstart, wlen), :], qrt)
    rel = t_lane - (wstart + lax.broadcasted_iota(jnp.int32, (wlen, 1), 0))
    sc = jnp.where((rel >= 0) & (rel < NSA_WIN), sc, MASKED)
    p_win = jnp.exp(sc - jnp.max(sc, axis=0, keepdims=True))
    o_win = (_dot(vwt_ref[0, 0, :, pl.ds(wstart, wlen)], p_win)
             / jnp.sum(p_win, axis=0, keepdims=True))

    imp = p_cmp[:, 0:tq]
    for h in range(1, hp):
        imp = imp + p_cmp[:, h * tq:(h + 1) * tq]
    imp = imp[:nch] + imp[nch:]
    blk = lax.broadcasted_iota(jnp.int32, (nsl, tq), 0)
    cur = t_q // NSA_SEL
    forced = (blk == 0) | (blk == cur) | (blk == cur - 1)
    score = jnp.where(blk > cur, -jnp.inf, jnp.where(forced, NSA_FORCE, imp))
    bias = (_topk_mask_rows(score, NSA_TOPN) - 1.0) * (-MASKED)
    bias_ref[...] = jnp.concatenate([bias] * hp, axis=1)

    init = (jnp.full((1, r), MASKED, F32), jnp.zeros((1, r), F32), jnp.zeros((NSA_DH, r), F32))

    def slc_scores(kt):
        start = pl.multiple_of(kt * tk, tk)
        sc = _dot(ks_ref[0, 0, pl.ds(start, tk), :], qrt)
        brow = bias_ref[pl.ds(pl.multiple_of(kt * per_tile, per_tile), per_tile), :]
        sc = jnp.concatenate([sc[j * NSA_SEL:(j + 1) * NSA_SEL] + brow[j:j + 1]
                              for j in range(per_tile)], axis=0)
        return sc, vst_ref[0, 0, :, pl.ds(start, tk)]

    def slc_body(kt, carry):
        sc, vt = slc_scores(kt)
        return _softmax_step_cols(carry, sc, vt)

    kd = q0 // tk
    carry = lax.fori_loop(0, kd, slc_body, init)
    sc, vt = slc_scores(kd)
    kpos = kd * tk + lax.broadcasted_iota(jnp.int32, (tk, 1), 0)
    carry = _softmax_step_cols(carry, jnp.where(kpos <= t_lane, sc, MASKED), vt)
    o_slc = carry[2] / carry[1]

    sig = jax.nn.sigmoid(sm_ref[...].T)
    gw = N_BRANCH * hp
    gates = jnp.where(g == 0, sig[SM_NG:SM_NG + gw], sig[SM_NG + gw:SM_NG + 2 * gw])
    outs = []
    for h in range(hp):
        cols = slice(h * tq, (h + 1) * tq)
        outs.append(gates[3 * h:3 * h + 1] * o_cmp[:, cols]
                    + gates[3 * h + 1:3 * h + 2] * o_slc[:, cols]
                    + gates[3 * h + 2:3 * h + 3] * o_win[:, cols])
    o_ref[...] = jnp.concatenate(outs, axis=0).T


def _nsa_prompt_attn(qc, qr, kc, vct, ks, vst, kw, vwt, z, batch, t, tq, tk, tkw):
    nq = t // tq
    gw = NSA_HEADS * NSA_DH // NSA_KV
    assert tk % (SUBLANES * NSA_SEL) == 0 and tq == tkw and tk % tq == 0 and t % tk == 0
    rowblk = lambda b, g, i: b * nq + i
    full = lambda a: pl.BlockSpec((1, 1) + a.shape[2:], lambda b, g, i: (b, g, 0, 0))
    q_spec = pl.BlockSpec((tq, gw), lambda b, g, i: (rowblk(b, g, i), g))
    return pl.pallas_call(
        functools.partial(_nsa_prompt_kernel, tq=tq, tk=tk, tkw=tkw),
        grid=(batch, NSA_KV, nq),
        in_specs=[q_spec, q_spec] + [full(a) for a in (kc, vct, ks, vst, kw, vwt)]
        + [pl.BlockSpec((tq, LANES), lambda b, g, i: (rowblk(b, g, i), Z_SMALL // LANES))],
        out_specs=pl.BlockSpec((tq, gw), lambda b, g, i: (rowblk(b, g, i), g)),
        out_shape=jax.ShapeDtypeStruct((batch * t, NSA_HEADS * NSA_DH), F32),
        scratch_shapes=[pltpu.VMEM((t // NSA_SEL, NSA_HEADS // NSA_KV * tq), F32)],
        compiler_params=_params(("parallel", "parallel", "arbitrary")),
    )(qc, qr, kc, vct, ks, vst, kw, vwt, z)


def _softmax_step_vt(carry, s, vt):
    m, l, acc = carry
    m_new = jnp.maximum(m, jnp.max(s, axis=-1, keepdims=True))
    m_safe = jnp.where(m_new == -jnp.inf, 0.0, m_new)
    p = jnp.exp(s - m_safe)
    alpha = jnp.exp(m - m_safe)
    l = alpha * l + jnp.sum(p, axis=-1, keepdims=True)
    acc = alpha * acc + _dot_nt(p, vt)
    return m_new, l, acc


def _nsa_decode_kernel(pt_ref, *refs, tq, pos0, wrel0, n_keys):
    n_in = SLC_PAGES_PER_STEP
    qc_ref, qr_ref, kc_ref, vc_ref = refs[0:4]
    pages = refs[4:4 + n_in]
    (kw_ref, vw_ref, kt_ref, vt_ref, sm_ref, o_ref,
     m_ref, l_ref, acc_ref, ocmp_ref, sel_ref) = refs[4 + n_in:]
    hp = NSA_HEADS // NSA_KV
    r = hp * tq
    gwid = hp * NSA_DH
    step = pl.program_id(1)
    n_steps = pl.num_programs(1)
    tk = n_in * PAGE_SIZE
    per_tile = tk // NSA_SEL
    n_tiles = sel_ref.shape[1]
    ncp = kc_ref.shape[2]
    nch = ncp // 2
    nsl = -(-(n_tiles * per_tile) // LANES) * LANES
    kv_w = NSA_KV * NSA_DH

    def stack_heads(ref, g):
        x = ref[:, g * gwid:(g + 1) * gwid]
        return jnp.concatenate([x[:, h * NSA_DH:(h + 1) * NSA_DH] for h in range(hp)],
                               axis=0).astype(BF16)

    t_q = pos0 + lax.broadcasted_iota(jnp.int32, (tq, 1), 0)
    t_row = jnp.concatenate([t_q] * hp, axis=0)
    expand = (lax.broadcasted_iota(jnp.int32, (per_tile, tk), 0)
              == lax.broadcasted_iota(jnp.int32, (per_tile, tk), 1) // NSA_SEL).astype(BF16)

    def picked_rows(g, tile, width):
        pk = jnp.dot(sel_ref[g, tile].astype(BF16), expand[:, :width], preferred_element_type=F32)
        return jnp.concatenate([pk] * hp, axis=0)

    @pl.when(step == 0)
    def _():
        ccol = lax.broadcasted_iota(jnp.int32, (r, ncp), 1)
        cvis = (2 * (ccol % nch) + ccol // nch + 1) * NSA_CMP <= t_row + 1
        blk = lax.broadcasted_iota(jnp.int32, (tq, nsl), 1)
        cur = t_q // NSA_SEL
        forced = (blk == 0) | (blk == cur) | (blk == cur - 1)
        group_scores = []
        for g in range(NSA_KV):
            s = jnp.where(cvis, _dot_nt(stack_heads(qc_ref, g), kc_ref[0, g]), -jnp.inf)
            m = jnp.max(s, axis=-1, keepdims=True)
            e = jnp.exp(s - jnp.where(m == -jnp.inf, 0.0, m))
            p_cmp = e / jnp.maximum(jnp.sum(e, axis=-1, keepdims=True), 1e-30)
            ocmp_ref[g] = _dot(p_cmp, vc_ref[0, g])
            imp = p_cmp[0:tq]
            for h in range(1, hp):
                imp = imp + p_cmp[h * tq:(h + 1) * tq]
            imp = imp[:, :nch] + imp[:, nch:]
            if nsl > nch:
                imp = jnp.concatenate([imp, jnp.zeros((tq, nsl - nch), F32)], axis=1)
            group_scores.append(jnp.where(blk > cur, -jnp.inf, jnp.where(forced, NSA_FORCE, imp)))
        sel_all = _topk_mask(jnp.concatenate(group_scores, axis=0), NSA_TOPN)
        for g in range(NSA_KV):
            sel = sel_all[g * tq:(g + 1) * tq]
            for j in range(n_tiles):
                sel_ref[g, j] = sel[:, j * per_tile:(j + 1) * per_tile]
            m_ref[g] = jnp.full((r, 1), -jnp.inf, F32)
            l_ref[g] = jnp.zeros((r, 1), F32)
            acc_ref[g] = jnp.zeros((r, NSA_DH), F32)

    kpos = step * tk + lax.broadcasted_iota(jnp.int32, (1, tk), 1)
    scores, values = [], []
    for g in range(NSA_KV):
        k_t = jnp.concatenate([p[0, g * NSA_DH:(g + 1) * NSA_DH, :] for p in pages], axis=1)
        values.append(jnp.concatenate(
            [p[0, kv_w + g * NSA_DH:kv_w + (g + 1) * NSA_DH, :] for p in pages], axis=1))
        scores.append(_dot(stack_heads(qr_ref, g), k_t))
    picked = [picked_rows(g, step, tk) for g in range(NSA_KV)]
    for g in range(NSA_KV):
        ok = (picked[g] > 0.5) & (kpos <= t_row)
        m, l, acc = _softmax_step_vt((m_ref[g], l_ref[g], acc_ref[g]),
                                     jnp.where(ok, scores[g], -jnp.inf), values[g])
        m_ref[g] = m
        l_ref[g] = l
        acc_ref[g] = acc

    @pl.when(step == n_steps - 1)
    def _():
        sig = jax.nn.sigmoid(sm_ref[...])
        tw = kt_ref.shape[2]
        tpos = n_keys + lax.broadcasted_iota(jnp.int32, (1, tw), 1)
        wpos = pos0 - wrel0 + lax.broadcasted_iota(jnp.int32, (1, kw_ref.shape[2]), 1)
        rel = t_row - wpos
        wok = (rel >= 0) & (rel < NSA_WIN) & (wpos >= 0)
        qr4s = [stack_heads(qr_ref, g) for g in range(NSA_KV)]
        tail_s = [_dot_nt(qr4s[g], kt_ref[0, g]) for g in range(NSA_KV)]
        win_s = [_dot_nt(qr4s[g], kw_ref[0, g]) for g in range(NSA_KV)]
        for g in range(NSA_KV):
            ok = (picked_rows(g, n_tiles - 1, tw) > 0.5) & (tpos <= t_row)
            carry = _softmax_step((m_ref[g], l_ref[g], acc_ref[g]),
                                  jnp.where(ok, tail_s[g], -jnp.inf), vt_ref[0, g])
            o_slc = carry[2] / jnp.maximum(carry[1], 1e-30)
            s = jnp.where(wok, win_s[g], -jnp.inf)
            m = jnp.max(s, axis=-1, keepdims=True)
            e = jnp.exp(s - jnp.where(m == -jnp.inf, 0.0, m))
            o_win = _dot(e, vw_ref[0, g]) / jnp.maximum(jnp.sum(e, axis=-1, keepdims=True), 1e-30)
            o_cmp = ocmp_ref[g]
            base = SM_NG + g * N_BRANCH * hp
            outs = []
            for h in range(hp):
                rows = slice(h * tq, (h + 1) * tq)
                c0 = base + N_BRANCH * h
                outs.append(sig[:, c0:c0 + 1] * o_cmp[rows] + sig[:, c0 + 1:c0 + 2] * o_slc[rows]
                            + sig[:, c0 + 2:c0 + 3] * o_win[rows])
            o_ref[:, g * gwid:(g + 1) * gwid] = jnp.concatenate(outs, axis=1)


def _nsa_decode_attn(qc, qr, kc, vc, cache_t, page_table, kw, vw, ktail, vtail, z, tq, pos0, wrel0):
    batch, n_pages = page_table.shape
    n_keys = n_pages * PAGE_SIZE
    steps = n_pages // SLC_PAGES_PER_STEP
    hp = NSA_HEADS // NSA_KV
    r = hp * tq
    qw = NSA_HEADS * NSA_DH
    per_tile = SLC_PAGES_PER_STEP * PAGE_SIZE // NSA_SEL
    assert pos0 == n_keys and ktail.shape[2] <= per_tile * NSA_SEL
    per_b = lambda a: pl.BlockSpec((1,) + a.shape[1:], lambda b, s, pt: (b, 0, 0, 0))
    q_spec = pl.BlockSpec((tq, qw), lambda b, s, pt: (b, 0))
    return pl.pallas_call(
        functools.partial(_nsa_decode_kernel, tq=tq, pos0=pos0, wrel0=wrel0, n_keys=n_keys),
        grid_spec=pltpu.PrefetchScalarGridSpec(
            num_scalar_prefetch=1, grid=(batch, steps),
            in_specs=[q_spec, q_spec, per_b(kc), per_b(vc)]
            + _page_specs((1, ROW_W, PAGE_SIZE), n_pages, SLC_PAGES_PER_STEP)
            + [per_b(kw), per_b(vw), per_b(ktail), per_b(vtail),
               pl.BlockSpec((tq, LANES), lambda b, s, pt: (b, Z_SMALL // LANES))],
            out_specs=pl.BlockSpec((tq, qw), lambda b, s, pt: (b, 0)),
            scratch_shapes=[pltpu.VMEM((NSA_KV, r, 1), F32), pltpu.VMEM((NSA_KV, r, 1), F32),
                            pltpu.VMEM((NSA_KV, r, NSA_DH), F32),
                            pltpu.VMEM((NSA_KV, r, NSA_DH), F32),
                            pltpu.VMEM((NSA_KV, steps + 1, tq, per_tile), F32)]),
        out_shape=jax.ShapeDtypeStruct((batch * tq, qw), F32),
        compiler_params=_params(("parallel", "arbitrary")),
    )(page_table.reshape(-1), qc, qr, kc, vc, *([cache_t] * SLC_PAGES_PER_STEP),
      kw, vw, ktail, vtail, z)


def _mem_attn_kernel(q_ref, kv_ref, o_ref):
    hw = MEM_HEADS * MEM_DH
    for h in range(MEM_HEADS):
        q = q_ref[:, h * MEM_DH:(h + 1) * MEM_DH] * (MEM_DH ** -0.5)
        k = kv_ref[0, :, h * MEM_DH:(h + 1) * MEM_DH]
        v = kv_ref[0, :, hw + h * MEM_DH:hw + (h + 1) * MEM_DH]
        s = _dot_nt(q, k)
        e = jnp.exp(s - jnp.max(s, axis=-1, keepdims=True))
        p = e / jnp.sum(e, axis=-1, keepdims=True)
        o_ref[:, h * MEM_DH:(h + 1) * MEM_DH] = _dot(p, v)


def _mem_attn(z, kv, batch, t, tm):
    nt = t // tm
    hw = MEM_HEADS * MEM_DH
    return pl.pallas_call(
        _mem_attn_kernel,
        grid=(batch, nt),
        in_specs=[pl.BlockSpec((tm, hw), lambda b, j: (b * nt + j, Z_MQ // hw)),
                  pl.BlockSpec((1,) + kv.shape[1:], lambda b, j: (b, 0, 0))],
        out_specs=pl.BlockSpec((tm, hw), lambda b, j: (b * nt + j, 0)),
        out_shape=jax.ShapeDtypeStruct((batch * t, hw), F32),
        compiler_params=_params(("parallel", "parallel")),
    )(z, kv)


def _merge_kernel(x_ref, mg_ref, og_ref, on_ref, om_ref, wg_ref, wn_ref, wm_ref, wo_ref,
                  gf_ref, wq_ref, x1_ref, h2_ref, qp_ref):
    d = D_MODEL
    mix = (jax.nn.sigmoid(mg_ref[:, 0:d]) * _dot(og_ref[...], wg_ref[...])
           + jax.nn.sigmoid(mg_ref[:, d:2 * d]) * _dot(on_ref[...], wn_ref[...])
           + jax.nn.sigmoid(mg_ref[:, 2 * d:3 * d]) * _dot(om_ref[...], wm_ref[...]))
    x1 = x_ref[...] + _dot(mix, wo_ref[...])
    x1_ref[...] = x1
    h2 = _rms(x1, gf_ref[...]).astype(BF16)
    h2_ref[...] = h2
    qp_ref[...] = jnp.dot(h2, wq_ref[...], preferred_element_type=F32)


def _merge(x, z, o_gdn, o_nsa, o_mem, wg, wn, wm, wo, norm_ffn, wq, tm):
    n, d = x.shape
    qw = wq.shape[1]
    row = lambda w: pl.BlockSpec((tm, w), lambda i: (i, 0))
    const = lambda a: pl.BlockSpec(a.shape, lambda i: (0, 0))
    return pl.pallas_call(
        _merge_kernel,
        grid=(n // tm,),
        in_specs=[row(d), pl.BlockSpec((tm, N_BRANCH * d), lambda i: (i, Z_MG)),
                  row(o_gdn.shape[1]), row(o_nsa.shape[1]), row(o_mem.shape[1]),
                  const(wg), const(wn), const(wm), const(wo),
                  pl.BlockSpec((1, d), lambda i: (0, 0)), const(wq)],
        out_specs=[row(d), row(d), row(qw)],
        out_shape=[jax.ShapeDtypeStruct((n, d), F32), jax.ShapeDtypeStruct((n, d), BF16),
                   jax.ShapeDtypeStruct((n, qw), F32)],
        compiler_params=_params(("parallel",)),
    )(x, z, o_gdn, o_nsa, o_mem, wg, wn, wm, wo, norm_ffn.reshape(1, d), wq)


PEER_RANKS = PEER_TOPK + 1


RANK_NONE = 64.0


def _top_values(s, n):
    vals = []
    rank = jnp.full(s.shape, RANK_NONE, F32)
    for k in range(n):
        m = jnp.max(s, axis=0, keepdims=True)
        vals.append(m)
        hit = s >= m
        rank = jnp.where(hit, float(k + 1), rank)
        s = jnp.where(hit, -jnp.inf, s)
    return vals, rank


PEER_PAIRS = [(i, j) for i in range(1, PEER_RANKS + 1) for j in range(1, PEER_RANKS // i + 1)]


def _pair_selectors():
    rows = -(-len(PEER_PAIRS) // SUBLANES) * SUBLANES
    cols = -(-PEER_RANKS // SUBLANES) * SUBLANES
    pa = np.zeros((rows, cols), np.float32)
    pb = np.zeros((rows, cols), np.float32)
    for p, (i, j) in enumerate(PEER_PAIRS):
        pa[p, i - 1] = 1.0
        pb[p, j - 1] = 1.0
    return jnp.asarray(pa), jnp.asarray(pb)


def _peer_route_kernel(qp_ref, sk_ref, pa_ref, pb_ref, r2_ref, e2_ref, nb_ref, e1_ref):
    half = PEER_DKEY // 2
    nt = (((1,), (1,)), ((), ()))
    for h in range(PEER_HEADS):
        qa = qp_ref[:, h * PEER_DKEY:h * PEER_DKEY + half]
        qb = qp_ref[:, h * PEER_DKEY + half:(h + 1) * PEER_DKEY]
        s1 = lax.dot_general(sk_ref[0], qa, nt, precision=HIGHEST, preferred_element_type=F32)
        s2 = lax.dot_general(sk_ref[1], qb, nt, precision=HIGHEST, preferred_element_type=F32)
        a, _ = _top_values(s1, PEER_RANKS)
        b, rank2 = _top_values(s2, PEER_RANKS)
        pad = jnp.full((pa_ref.shape[1] - PEER_RANKS, s1.shape[1]), MASKED, F32)
        a_rows = jnp.maximum(jnp.concatenate(a + [pad], axis=0), MASKED)
        b_rows = jnp.maximum(jnp.concatenate(b + [pad], axis=0), MASKED)
        cand = _dot_hi(pa_ref[...], a_rows) + _dot_hi(pb_ref[...], b_rows)
        prow = lax.broadcasted_iota(jnp.int32, cand.shape, 0)
        cand = jnp.where(prow < len(PEER_PAIRS), cand, -jnp.inf)
        work = cand
        ranked = []
        for _ in range(PEER_RANKS):
            m = jnp.max(work, axis=0, keepdims=True)
            ranked.append(m)
            work = jnp.where(work >= m, -jnp.inf, work)
        tau = 0.5 * (ranked[PEER_TOPK - 1] + ranked[PEER_TOPK])
        top = a[0] + b[0]
        zsum = jnp.sum(jnp.where(cand >= tau, jnp.exp(cand - top), 0.0), axis=0, keepdims=True)
        th = tau - s1
        count = jnp.zeros(s1.shape, F32)
        for bj in b:
            count = count + jnp.where(bj >= th, 1.0, 0.0)
        r2_ref[h] = rank2.astype(BF16)
        e2_ref[h] = jnp.exp(s2 - b[0]).astype(BF16)
        nb_ref[h] = count
        e1_ref[h] = jnp.exp(s1 - a[0]) / zsum


def _peer_route(qp, subkeys, tt):
    n = qp.shape[0]
    pa, pb = _pair_selectors()
    shape = lambda dt: jax.ShapeDtypeStruct((PEER_HEADS, PEER_NKEYS, n), dt)
    spec = pl.BlockSpec((PEER_HEADS, PEER_NKEYS, tt), lambda i: (0, 0, i))
    return pl.pallas_call(
        _peer_route_kernel,
        grid=(n // tt,),
        in_specs=[pl.BlockSpec((tt, qp.shape[1]), lambda i: (i, 0)),
                  pl.BlockSpec(subkeys.shape, lambda i: (0, 0, 0)),
                  pl.BlockSpec(pa.shape, lambda i: (0, 0)),
                  pl.BlockSpec(pb.shape, lambda i: (0, 0))],
        out_specs=[spec, spec, spec, spec],
        out_shape=[shape(BF16), shape(BF16), shape(F32), shape(F32)],
        compiler_params=_params(("parallel",)),
    )(qp, subkeys, pa, pb)


PEER_SUB_BLOCKS = 4
PEER_BLOCK_ROWS = 16


def _blocked_transpose(v):
    eb = PEER_BLOCK_ROWS * PEER_NKEYS
    return jnp.transpose(v.reshape(v.shape[0] // eb, eb, v.shape[1]), (0, 2, 1))


def _peer_dense_kernel(ht_ref, u_ref, vt_ref, r2_ref, e2_ref, nb_ref, e1_ref, x1_ref, gf_ref,
                       y_ref, acc_ref, act_ref, wa_ref, r2s_ref, e2s_ref, rows_ref, *, jb):
    j = pl.program_id(1)
    tt = ht_ref.shape[1]

    @pl.when(j == 0)
    def _():
        acc_ref[...] = jnp.zeros(acc_ref.shape, F32)
        r2s_ref[...] = r2_ref[...]
        e2s_ref[...] = e2_ref[...]

    sub = jb // PEER_SUB_BLOCKS
    for sb in range(PEER_SUB_BLOCKS):
        srows = slice(sb * sub * PEER_NKEYS, (sb + 1) * sub * PEER_NKEYS)
        act_ref[srows, :] = _gelu(jnp.dot(u_ref[srows, :], ht_ref[...],
                                          preferred_element_type=F32)).astype(BF16)
    for jj in range(jb):
        i1 = j * jb + jj
        rows = slice(jj * PEER_NKEYS, (jj + 1) * PEER_NKEYS)
        for h in range(PEER_HEADS):
            rows_ref[jj, h:h + 1, :] = nb_ref[h, pl.ds(i1, 1), :]
            rows_ref[jj, PEER_HEADS + h:PEER_HEADS + h + 1, :] = e1_ref[h, pl.ds(i1, 1), :]
        for c in range(tt // LANES):
            cols = slice(c * LANES, (c + 1) * LANES)
            w = None
            for h in range(PEER_HEADS):
                count = rows_ref[jj, h:h + 1, cols].astype(BF16)
                e1 = rows_ref[jj, PEER_HEADS + h:PEER_HEADS + h + 1, cols].astype(BF16)
                picked = r2s_ref[h, :, cols] <= count
                term = jnp.where(picked, e2s_ref[h, :, cols], jnp.zeros((), BF16)) * e1
                w = term if w is None else w + term
            wa_ref[rows, cols] = w * act_ref[rows, cols]
    acc_ref[...] += jnp.dot(vt_ref[0], wa_ref[...], preferred_element_type=F32)

    @pl.when(j == pl.num_programs(1) - 1)
    def _():
        y_ref[...] = _rms(x1_ref[...] + acc_ref[...].T, gf_ref[...])


def _peer_dense(ht, u, vt, s2, e2, th, e1, x1, norm_final, tt, jb):
    d, n = ht.shape
    n_exp = u.shape[0]
    eb = jb * PEER_NKEYS
    route = pl.BlockSpec((PEER_HEADS, PEER_NKEYS, tt), lambda t, j: (0, 0, t))
    return pl.pallas_call(
        functools.partial(_peer_dense_kernel, jb=jb),
        grid=(n // tt, n_exp // eb),
        in_specs=[pl.BlockSpec((d, tt), lambda t, j: (0, t)),
                  pl.BlockSpec((eb, d), lambda t, j: (j, 0)),
                  pl.BlockSpec((1, d, eb), lambda t, j: (j, 0, 0)),
                  route, route, route, route,
                  pl.BlockSpec((tt, d), lambda t, j: (t, 0)),
                  pl.BlockSpec((1, d), lambda t, j: (0, 0))],
        out_specs=pl.BlockSpec((tt, d), lambda t, j: (t, 0)),
        out_shape=jax.ShapeDtypeStruct((n, d), F32),
        scratch_shapes=[pltpu.VMEM((d, tt), F32), pltpu.VMEM((eb, tt), BF16),
                        pltpu.VMEM((eb, tt), BF16),
                        pltpu.VMEM((PEER_HEADS, PEER_NKEYS, tt), BF16),
                        pltpu.VMEM((PEER_HEADS, PEER_NKEYS, tt), BF16),
                        pltpu.VMEM((jb, 2 * PEER_HEADS, tt), F32)],
        compiler_params=_params(("parallel", "arbitrary")),
    )(ht, u, vt, s2, e2, th, e1, x1, norm_final.reshape(1, d))


def _permute_w_in(w_in):
    sizes = (GDN_CONV_CH, GDN_HEADS * GDN_DV, GDN_HEADS, GDN_HEADS, NSA_HEADS * NSA_DH, NKV_W,
             N_BRANCH * NSA_HEADS, MEM_HEADS * MEM_DH, N_BRANCH * D_MODEL)
    qkv, zg, a, b, nq, nkv, ng, mq, mg = jnp.split(w_in, np.cumsum(sizes)[:-1].tolist(), axis=1)
    pad = jnp.zeros((w_in.shape[0], LANES - a.shape[1] - b.shape[1] - ng.shape[1]), w_in.dtype)
    return jnp.concatenate([mg, qkv, zg, nq, mq, nkv, a, b, ng, pad], axis=1).astype(BF16)


def _tokens_tile(n, pref):
    return pref if n % pref == 0 else n


def _layer(x, pos0, kv_mem, nsa_keys, gdn_state, conv_buf, w, peer_tt):
    batch, t, d = x.shape
    n = batch * t
    xf = x.reshape(n, d)
    z = _norm_matmul(xf, w['norm_attn'], w['w_in'], _tokens_tile(n, 256), 640)

    o_gdn, s_new, conv_new = _gdn(z, conv_buf, gdn_state, w['gdn_conv'], w['gdn_a_log'],
                                  w['gdn_dt_bias'], w['gdn_norm'], batch, t)

    prep_tm = _tokens_tile(t, 512)
    kv_dtype = BF16 if prep_tm % 16 == 0 else F32
    qc, qr, cmp_out, slc_rows, win_rows, ks, vs, kw, vw = _nsa_prep(
        z, pos0 + jnp.arange(t), batch, t, prep_tm, kv_dtype)
    cmp_rows = z[:, Z_NKV:Z_NKV + ROW_W]
    o_nsa = nsa_keys(z, qc, qr, cmp_rows, ks, vs, kw, vw)

    o_mem = _mem_attn(z, kv_mem, batch, t, _tokens_tile(t, 512))

    x1, h2, qp = _merge(xf, z, o_gdn, o_nsa, o_mem, w['w_gdn_out'], w['w_nsa_out'],
                        w['w_mem_out'], w['w_o'], w['norm_ffn'], w['peer_wq'],
                        _tokens_tile(n, 256))
    s2, e2, th, e1 = _peer_route(qp, w['peer_subkeys'], 256)
    y = _peer_dense(h2.T, w['peer_u'], w['peer_vt'], s2, e2, th, e1, x1, w['norm_final'],
                    peer_tt, PEER_BLOCK_ROWS)
    def row5(a):
        if a.ndim == 3:
            return jnp.transpose(a.reshape(batch, 2, NSA_KV, NSA_DH, t), (0, 4, 1, 2, 3))
        return a.reshape(batch, t, 2, NSA_KV, NSA_DH)

    return (y.reshape(batch, t, d), row5(cmp_out), row5(slc_rows), row5(win_rows), s_new,
            conv_new)


def kernel(x_prompt, x_sample, cache_mem_kv, cache_cmp_kv, cache_slc_kv, cache_win_kv, state_gdn, state_conv, page_table, mem_prompt, norm_attn, w_in, gdn_conv, gdn_a_log, gdn_dt_bias, gdn_norm, nsa_cmp_pe, nsa_cmp_w1, nsa_cmp_w2, norm_mem, w_mem_kv, w_gdn_out, w_nsa_out, w_mem_out, w_o, norm_ffn, peer_wq, peer_subkeys, peer_u, peer_v, norm_final):
    depth = w_in.shape[0]
    assert depth == 1
    l = 0
    bp, seq, d = x_prompt.shape
    bs, tdec, _ = x_sample.shape
    n_pages = page_table.shape[1]
    past = n_pages * PAGE_SIZE
    assert past % NSA_CMP == 0 and tdec < NSA_CMP and seq % LANES == 0

    pe_big, w1big, w2big = _compress_weights(nsa_cmp_pe[l], nsa_cmp_w1[l], nsa_cmp_w2[l])
    w = dict(norm_attn=norm_attn[l], w_in=_permute_w_in(w_in[l]), gdn_conv=gdn_conv[l],
             gdn_a_log=gdn_a_log[l], gdn_dt_bias=gdn_dt_bias[l], gdn_norm=gdn_norm[l],
             w_gdn_out=w_gdn_out[l].astype(BF16), w_nsa_out=w_nsa_out[l].astype(BF16),
             w_mem_out=w_mem_out[l].astype(BF16), w_o=w_o[l].astype(BF16), norm_ffn=norm_ffn[l],
             peer_wq=peer_wq[l].astype(BF16), peer_subkeys=peer_subkeys[l],
             peer_u=peer_u[l].astype(BF16), peer_vt=_blocked_transpose(peer_v[l].astype(BF16)),
             norm_final=norm_final)

    mem_n = mem_prompt.shape[0] * mem_prompt.shape[1]
    kvm = _norm_matmul(mem_prompt.reshape(mem_n, d), norm_mem[l], w_mem_kv[l].astype(BF16),
                       _tokens_tile(mem_n, 256), 512).reshape(bp, mem_prompt.shape[1], -1)

    def prompt_keys(z, qc, qr, cmp_rows, ks, vs, kw, vw):
        n_blk = bp * seq // NSA_CMP
        cmp_out = _compress(cmp_rows.reshape(n_blk, NSA_CMP * ROW_W), pe_big, w1big, w2big,
                            _tokens_tile(n_blk, 128))
        kc, vc = _split_compressed(cmp_out, bp)
        tr = lambda a: jnp.transpose(a, (0, 1, 3, 2))
        return _nsa_prompt_attn(qc, qr, kc, tr(vc), ks, tr(vs), kw, tr(vw), z, bp, seq,
                                128, _tokens_tile(seq, 512), 128)

    yp, cmp_p, slc_p, win_p, gdn_p, conv_p = _layer(
        x_prompt, 0, kvm, prompt_keys,
        jnp.zeros((bp, GDN_HEADS, GDN_DK, GDN_DV), F32),
        jnp.zeros((bp, GDN_CONV - 1, GDN_CONV_CH), F32), w, 512)
    win_len_p = min(NSA_WIN, seq)
    win_p = win_p[:, seq - win_len_p:]

    n_pool = cache_cmp_kv.shape[1]
    feature_major = lambda c: jnp.transpose(c, (0, 2, 3, 4, 1)).reshape(n_pool, ROW_W, PAGE_SIZE)
    cache_cmp = feature_major(cache_cmp_kv[l])
    cache_slc = feature_major(cache_slc_kv[l])
    cache_win = cache_win_kv[l].reshape(bs, -1, ROW_W)
    wb = cache_win.shape[1]
    assert wb == NSA_WIN
    kv_w = NSA_KV * NSA_DH

    def pad_rows(a, rows):
        return jnp.pad(a, ((0, 0), (0, 0), (0, rows - a.shape[2]), (0, 0))).astype(BF16)

    def split_rows(rows):
        r = rows.reshape(bs, rows.shape[1], 2, NSA_KV, NSA_DH)
        return jnp.transpose(r[:, :, 0], (0, 2, 1, 3)), jnp.transpose(r[:, :, 1], (0, 2, 1, 3))

    def sample_keys(z, qc, qr, cmp_rows, ks_new, vs_new, kw_new, vw_new):
        cmp_out = _paged_compress(cache_cmp, page_table, pe_big, w1big, w2big)
        kc, vc = _split_compressed(cmp_out, bs)
        kwc, vwc = split_rows(cache_win)
        win_rows = wb + LANES
        kw = pad_rows(jnp.concatenate([kwc, kw_new], axis=2), win_rows)
        vw = pad_rows(jnp.concatenate([vwc, vw_new], axis=2), win_rows)
        return _nsa_decode_attn(qc, qr, kc, vc, cache_slc, page_table, kw, vw,
                                pad_rows(ks_new, LANES), pad_rows(vs_new, LANES), z,
                                tdec, past, wb)

    ys, cmp_s, slc_s, win_new, gdn_s, conv_s = _layer(
        x_sample, past, cache_mem_kv[l].reshape(bs, cache_mem_kv.shape[2], -1), sample_keys,
        state_gdn[l], state_conv[l], w, 256)
    win_all = jnp.concatenate([cache_win_kv[l], win_new], axis=1)
    win_s = win_all[:, win_all.shape[1] - min(NSA_WIN, past + tdec):]

    stack = lambda a: a[None]
    return (yp, ys, stack(kvm.reshape(bp, mem_prompt.shape[1], 2, MEM_HEADS, MEM_DH)),
            stack(cmp_p), stack(slc_p), stack(win_p), stack(gdn_p), stack(conv_p),
            stack(cmp_s), stack(slc_s), stack(win_s), stack(gdn_s), stack(conv_s))
```
